```python
import math
import jax, jax.numpy as jnp
from jax import lax
import numpy as np

D_MODEL = 1024
BATCH = 8
SEQ = 2048
DEPTH = 4
DEC_BATCH = 128
DEC_SEQ = 8
PAST_LEN = 2048
PAGE_SIZE = 128

D_MIX = D_MODEL
W_CONV = D_MIX // 4
CONV_WIDTH = 31
W_GLA = D_MIX // 4
H_GLA = 4
DV_GLA = W_GLA // H_GLA
DK_GLA = DV_GLA // 2
GATE_RANK = 16
GATE_NORMALIZER = 16.0
GLA_CHUNK = 64
W_NSA = D_MIX - W_CONV - W_GLA
H_NSA = 8
HD = W_NSA // H_NSA
G_KV = 2
HPG = H_NSA // G_KV
CMP_BLOCK = 32
SEL_BLOCK = 64
TOP_N = 8
WINDOW = 512
Q_BLOCK = 128
N_BUCKETS = 32
MAX_EXACT = N_BUCKETS // 2
MAX_DISTANCE = 128
SCALE = HD ** -0.5
EPS = 1e-6
NEG = -1e30
FORCE = 1e4

SPLITS = (("c_a", W_CONV), ("c_b", W_CONV), ("c_gate", W_CONV),
          ("l_q", H_GLA * DK_GLA), ("l_k", H_GLA * DK_GLA), ("l_v", W_GLA),
          ("l_gk", GATE_RANK), ("l_gate", W_GLA),
          ("n_q", W_NSA), ("n_cmp", 2 * G_KV * HD), ("n_slc", 2 * G_KV * HD),
          ("n_win", 2 * G_KV * HD), ("n_g", 3 * H_NSA), ("n_gate", W_NSA))
D_IN = sum(w for _, w in SPLITS)

LAYER_PARAMS = ("norm_g", "w_in", "conv_w_dw", "conv_b_dw", "conv_ln_g", "conv_ln_b",
                "conv_w_pw", "conv_b_pw", "gla_w_gk", "gla_b_gk", "gla_norm_g",
                "nsa_q_norm_g", "nsa_k_norm_g", "nsa_pe_cmp", "nsa_w_cmp", "nsa_b_gate", "w_out")

kernel_name = "hymba_conv_gla_nsa_step"


def rmsnorm(x, g):
    xf = x.astype(jnp.float32)
    y = xf * lax.rsqrt(jnp.mean(xf * xf, axis=-1, keepdims=True) + EPS)
    return (y * g.astype(jnp.float32)).astype(x.dtype)


def layernorm(x, g, b):
    xf = x.astype(jnp.float32)
    mu = jnp.mean(xf, axis=-1, keepdims=True)
    var = jnp.mean(jnp.square(xf - mu), axis=-1, keepdims=True)
    return ((xf - mu) * lax.rsqrt(var + EPS) * g.astype(jnp.float32) + b.astype(jnp.float32)).astype(x.dtype)


def split_proj(z):
    out, off = {}, 0
    for name, w in SPLITS:
        out[name] = z[..., off:off + w]
        off += w
    return out


def rel_bucket(dist):
    n = jnp.maximum(dist, 0)
    nf = jnp.maximum(n, 1).astype(jnp.float32)
    large = MAX_EXACT + (jnp.log(nf / MAX_EXACT) / math.log(MAX_DISTANCE / MAX_EXACT)
                         * (N_BUCKETS - MAX_EXACT)).astype(jnp.int32)
    large = jnp.minimum(large, N_BUCKETS - 1)
    return jnp.where(n < MAX_EXACT, n, large)


def masked_softmax(logits, mask):
    l = jnp.where(mask, logits.astype(jnp.float32), NEG)
    m = jnp.max(l, axis=-1, keepdims=True)
    p = jnp.where(mask, jnp.exp(l - m), 0.0)
    return p / jnp.maximum(jnp.sum(p, axis=-1, keepdims=True), 1e-30)


def conv_branch(a, b, gate, hist, w_dw, b_dw, ln_g, ln_b, w_pw, b_pw):
    u = a * jax.nn.sigmoid(b)
    ext = jnp.concatenate([hist.astype(u.dtype), u], axis=1)
    y = lax.conv_general_dilated(ext, w_dw[:, None, :].astype(u.dtype), window_strides=(1,),
                                 padding="VALID", dimension_numbers=("NWC", "WIO", "NWC"),
                                 feature_group_count=W_CONV) + b_dw
    y = jax.nn.silu(layernorm(y, ln_g, ln_b))
    y = jnp.einsum("btc,cd->btd", y, w_pw) + b_pw
    return y * jax.nn.silu(gate), ext[:, -(CONV_WIDTH - 1):]


def gla_scan(q, k, v, la, s0, chunk):
    B, T, H, DK = q.shape
    DV = v.shape[-1]
    n = T // chunk

    def to_chunks(a):
        return a.astype(jnp.float32).reshape(B, n, chunk, H, a.shape[-1]).transpose(1, 0, 3, 2, 4)

    causal = jnp.tril(jnp.ones((chunk, chunk), dtype=bool))[:, :, None]

    def step(S, inp):
        qc, kc, vc, lc = inp
        bc = jnp.cumsum(lc, axis=2)
        inter = jnp.einsum("bhtd,bhde->bhte", qc * jnp.exp(bc), S)
        decay = jnp.exp(jnp.where(causal, bc[:, :, :, None, :] - bc[:, :, None, :, :], -jnp.inf))
        att = jnp.einsum("bhtsd,bhtd,bhsd->bhts", decay, qc, kc)
        intra = jnp.einsum("bhts,bhse->bhte", att, vc)
        b_last = bc[:, :, -1:]
        S_new = jnp.exp(b_last[:, :, 0])[..., None] * S + jnp.einsum(
            "bhsd,bhse->bhde", kc * jnp.exp(b_last - bc), vc)
        return S_new, inter + intra

    S, o = lax.scan(step, s0.astype(jnp.float32), (to_chunks(q), to_chunks(k), to_chunks(v), to_chunks(la)))
    o = o.transpose(1, 0, 3, 2, 4).reshape(B, T, H, DV)
    return o.astype(v.dtype), S


def gla_branch(q, k, v, gk_low, gate, s0, w_gk, b_gk, g_out, chunk):
    B, T, _ = q.shape
    q = q.reshape(B, T, H_GLA, DK_GLA) * (DK_GLA ** -0.5)
    k = k.reshape(B, T, H_GLA, DK_GLA)
    v = v.reshape(B, T, H_GLA, DV_GLA)
    la = jax.nn.log_sigmoid(jnp.einsum("btr,rk->btk", gk_low, w_gk) + b_gk).astype(jnp.float32)
    la = la.reshape(B, T, H_GLA, DK_GLA) / GATE_NORMALIZER
    o, S = gla_scan(q, k, v, la, s0, chunk)
    o = rmsnorm(o, g_out).reshape(B, T, W_GLA)
    return o * jax.nn.silu(gate), S


def compress(rows, pe, w_cmp):
    B, L = rows.shape[:2]
    blk = rows.reshape(B, L // CMP_BLOCK, CMP_BLOCK, 2, G_KV, HD) + pe.transpose(1, 0, 2)[:, :, None, :]
    return jnp.einsum("bnjsgd,sjde->bnsge", blk, w_cmp)


def nsa_compressed(q, qpos, kc, vc, table_g):
    NC = kc.shape[1]
    blk_end = jnp.arange(NC, dtype=jnp.int32) * CMP_BLOCK + CMP_BLOCK - 1
    dist = qpos[:, None] - blk_end[None, :]
    bias = table_g[rel_bucket(dist)].astype(jnp.float32)
    logits = jnp.einsum("btgjd,bngd->btgjn", q, kc).astype(jnp.float32) * SCALE \
        + bias.transpose(0, 2, 3, 1)[None]
    p = masked_softmax(logits, (dist >= 0)[None, :, None, None, :])
    o = jnp.einsum("btgjn,bngd->btgjd", p.astype(vc.dtype), vc)
    return o, p


def nsa_select(p_cmp, qpos, n_blocks):
    B, T, G, _, _ = p_cmp.shape
    imp = p_cmp.sum(axis=3).reshape(B, T, G, n_blocks, SEL_BLOCK // CMP_BLOCK).sum(axis=-1)
    j = jnp.arange(n_blocks, dtype=jnp.int32)[None, :]
    cur = (qpos // SEL_BLOCK)[:, None]
    forced = (j == 0) | (j == cur) | (j == cur - 1)
    valid = j * SEL_BLOCK <= qpos[:, None]
    score = jnp.where(forced[None, :, None, :], FORCE,
                      jnp.where(valid[None, :, None, :], imp, -1.0))
    _, idx = lax.top_k(score, min(TOP_N, n_blocks))
    return idx


def nsa_local(q, qpos, idx, ks, vs, kw, vw, kwpos, table_g):
    B, Tq = q.shape[:2]
    N = idx.shape[-1]
    bi = jnp.arange(B)[:, None, None, None]
    gi = jnp.arange(G_KV)[None, None, :, None]
    kg = ks[bi, gi, idx].reshape(B, Tq, G_KV, N * SEL_BLOCK, HD)
    vg = vs[bi, gi, idx].reshape(B, Tq, G_KV, N * SEL_BLOCK, HD)
    pos = (idx[..., None] * SEL_BLOCK + jnp.arange(SEL_BLOCK, dtype=jnp.int32)).reshape(B, Tq, G_KV, N * SEL_BLOCK)
    dist = qpos[None, :, None, None] - pos
    bias = table_g[rel_bucket(dist), gi].astype(jnp.float32)
    logits = jnp.einsum("btgjd,btgkd->btgjk", q, kg).astype(jnp.float32) * SCALE + jnp.moveaxis(bias, -1, 3)
    p = masked_softmax(logits, (dist >= 0)[:, :, :, None, :])
    o_slc = jnp.einsum("btgjk,btgkd->btgjd", p.astype(vg.dtype), vg)
    dw = qpos[:, None] - kwpos[None, :]
    mw = (dw >= 0) & (dw < WINDOW) & (kwpos >= 0)[None, :]
    bw = table_g[rel_bucket(dw)].astype(jnp.float32)
    lw = jnp.einsum("btgjd,bkgd->btgjk", q, kw).astype(jnp.float32) * SCALE + bw.transpose(0, 2, 3, 1)[None]
    pw = masked_softmax(lw, mw[None, :, None, None, :])
    o_win = jnp.einsum("btgjk,bkgd->btgjd", pw.astype(vw.dtype), vw)
    return o_slc, o_win


def nsa_branch(z, qpos, lp, table_g, past):
    B, T, _ = z["n_q"].shape
    kng = lp["nsa_k_norm_g"]
    q = rmsnorm(z["n_q"].reshape(B, T, H_NSA, HD), lp["nsa_q_norm_g"]).reshape(B, T, G_KV, HPG, HD)

    def rows(zz, g):
        r = zz.reshape(B, T, 2, G_KV, HD)
        return r if g is None else jnp.stack([rmsnorm(r[:, :, 0], g), r[:, :, 1]], axis=2)

    cmp_new = rows(z["n_cmp"], None)
    slc_new = rows(z["n_slc"], kng[1])
    win_new = rows(z["n_win"], kng[2])
    if past is None:
        cmp_rows, slc_rows = cmp_new, slc_new
    else:
        win_past, cmp_past, slc_past = past
        n_pad = -(-(PAST_LEN + T) // SEL_BLOCK) * SEL_BLOCK - PAST_LEN - T
        pad = jnp.zeros((B, n_pad, 2, G_KV, HD), cmp_new.dtype)
        cmp_rows = jnp.concatenate([cmp_past.astype(cmp_new.dtype), cmp_new, pad], axis=1)
        slc_rows = jnp.concatenate([slc_past.astype(slc_new.dtype), slc_new, pad], axis=1)
    n_blocks = cmp_rows.shape[1] // SEL_BLOCK
    kvc = compress(cmp_rows, lp["nsa_pe_cmp"], lp["nsa_w_cmp"])
    kc = rmsnorm(kvc[:, :, 0], kng[0])
    vc = kvc[:, :, 1]
    o_cmp, p_cmp = nsa_compressed(q, qpos, kc, vc, table_g)
    idx = nsa_select(p_cmp, qpos, n_blocks)
    sel = slc_rows.reshape(B, n_blocks, SEL_BLOCK, 2, G_KV, HD).transpose(3, 0, 4, 1, 2, 5)
    ks, vs = sel[0], sel[1]
    if past is None:
        kwp = jnp.pad(win_new, ((0, 0), (WINDOW, 0), (0, 0), (0, 0), (0, 0)))

        def block(i):
            s = i * Q_BLOCK
            qb = lax.dynamic_slice_in_dim(q, s, Q_BLOCK, axis=1)
            ib = lax.dynamic_slice_in_dim(idx, s, Q_BLOCK, axis=1)
            wb = lax.dynamic_slice_in_dim(kwp, s, WINDOW + Q_BLOCK, axis=1)
            qp = s + jnp.arange(Q_BLOCK, dtype=jnp.int32)
            kp = s - WINDOW + jnp.arange(WINDOW + Q_BLOCK, dtype=jnp.int32)
            return nsa_local(qb, qp, ib, ks, vs, wb[:, :, 0], wb[:, :, 1], kp, table_g)

        o_slc, o_win = lax.map(block, jnp.arange(T // Q_BLOCK, dtype=jnp.int32))
        o_slc = jnp.moveaxis(o_slc, 0, 1).reshape(B, T, G_KV, HPG, HD)
        o_win = jnp.moveaxis(o_win, 0, 1).reshape(B, T, G_KV, HPG, HD)
        win_state = win_new[:, -min(WINDOW, T):]
    else:
        wrows = jnp.concatenate([win_past.astype(win_new.dtype), win_new], axis=1)
        kp = PAST_LEN - win_past.shape[1] + jnp.arange(wrows.shape[1], dtype=jnp.int32)
        o_slc, o_win = nsa_local(q, qpos, idx, ks, vs, wrows[:, :, 0], wrows[:, :, 1], kp, table_g)
        win_state = wrows[:, -min(WINDOW, PAST_LEN + T):]
    g = jax.nn.sigmoid(z["n_g"] + lp["nsa_b_gate"]).reshape(B, T, 3, G_KV, HPG, 1)
    o = g[:, :, 0] * o_cmp + g[:, :, 1] * o_slc + g[:, :, 2] * o_win
    y = o.reshape(B, T, W_NSA) * jax.nn.silu(z["n_gate"])
    return y, win_state, cmp_new, slc_new


def mixer_layer(x, lp, table_g, past):
    B, T, _ = x.shape
    h = rmsnorm(x, lp["norm_g"])
    z = split_proj(jnp.einsum("btd,de->bte", h, lp["w_in"]))
    if past is None:
        p0 = 0
        conv_hist = jnp.zeros((B, CONV_WIDTH - 1, W_CONV), x.dtype)
        gla_s0 = jnp.zeros((B, H_GLA, DK_GLA, DV_GLA), jnp.float32)
        chunk = GLA_CHUNK
        nsa_past = None
    else:
        conv_hist, gla_s0, win_past, cmp_past, slc_past = past
        p0 = PAST_LEN
        chunk = T
        nsa_past = (win_past, cmp_past, slc_past)
    qpos = p0 + jnp.arange(T, dtype=jnp.int32)
    y_conv, conv_new = conv_branch(z["c_a"], z["c_b"], z["c_gate"], conv_hist, lp["conv_w_dw"], lp["conv_b_dw"],
                                   lp["conv_ln_g"], lp["conv_ln_b"], lp["conv_w_pw"], lp["conv_b_pw"])
    y_gla, gla_new = gla_branch(z["l_q"], z["l_k"], z["l_v"], z["l_gk"], z["l_gate"], gla_s0,
                                lp["gla_w_gk"], lp["gla_b_gk"], lp["gla_norm_g"], chunk)
    y_nsa, win_new, cmp_new, slc_new = nsa_branch(z, qpos, lp, table_g, nsa_past)
    y = jnp.concatenate([y_conv, y_gla, y_nsa], axis=-1)
    x = x + jnp.einsum("btc,cd->btd", y, lp["w_out"])
    return x, (conv_new, gla_new, win_new, cmp_new, slc_new)


def setup_inputs(seed: int = 0) -> dict:
    key = jax.random.key(seed)
    ks = jax.random.split(key, 32)

    def nrm(k, shape, s):
        return jax.random.normal(k, shape, jnp.float32) * s

    n_pages = PAST_LEN // PAGE_SIZE
    n_used = DEC_BATCH * n_pages
    n_phys = n_used + n_used // 4
    page_table = jax.random.permutation(ks[4], n_phys)[:n_used].reshape(DEC_BATCH, n_pages).astype(jnp.int32)
    w_buf = min(WINDOW, PAST_LEN)
    return {
        "x_prompt": nrm(ks[0], (BATCH, SEQ, D_MODEL), 1.0),
        "x_sample": nrm(ks[1], (DEC_BATCH, DEC_SEQ, D_MODEL), 1.0),
        "cache_cmp_kv": nrm(ks[2], (DEPTH, n_phys, PAGE_SIZE, 2, G_KV, HD), 1.0),
        "cache_slc_kv": nrm(ks[3], (DEPTH, n_phys, PAGE_SIZE, 2, G_KV, HD), 1.0),
        "page_table": page_table,
        "state_win_kv": nrm(ks[5], (DEPTH, DEC_BATCH, w_buf, 2, G_KV, HD), 1.0),
        "state_gla": nrm(ks[6], (DEPTH, DEC_BATCH, H_GLA, DK_GLA, DV_GLA), 1.0),
        "state_conv": nrm(ks[7], (DEPTH, DEC_BATCH, CONV_WIDTH - 1, W_CONV), 0.5),
        "rel_bias": nrm(ks[8], (N_BUCKETS, H_NSA), 0.5),
        "norm_g": 1.0 + nrm(ks[9], (DEPTH, D_MODEL), 0.02),
        "w_in": nrm(ks[10], (DEPTH, D_MODEL, D_IN), D_MODEL ** -0.5),
        "conv_w_dw": nrm(ks[11], (DEPTH, CONV_WIDTH, W_CONV), CONV_WIDTH ** -0.5),
        "conv_b_dw": nrm(ks[12], (DEPTH, W_CONV), 0.02),
        "conv_ln_g": 1.0 + nrm(ks[13], (DEPTH, W_CONV), 0.02),
        "conv_ln_b": nrm(ks[14], (DEPTH, W_CONV), 0.02),
        "conv_w_pw": nrm(ks[15], (DEPTH, W_CONV, W_CONV), W_CONV ** -0.5),
        "conv_b_pw": nrm(ks[16], (DEPTH, W_CONV), 0.02),
        "gla_w_gk": nrm(ks[17], (DEPTH, GATE_RANK, H_GLA * DK_GLA), GATE_RANK ** -0.5),
        "gla_b_gk": nrm(ks[18], (DEPTH, H_GLA * DK_GLA), 0.02),
        "gla_norm_g": 1.0 + nrm(ks[19], (DEPTH, DV_GLA), 0.02),
        "nsa_q_norm_g": 1.0 + nrm(ks[20], (DEPTH, HD), 0.02),
        "nsa_k_norm_g": 1.0 + nrm(ks[21], (DEPTH, 3, HD), 0.02),
        "nsa_pe_cmp": nrm(ks[22], (DEPTH, 2, CMP_BLOCK, HD), 0.1),
        "nsa_w_cmp": nrm(ks[23], (DEPTH, 2, CMP_BLOCK, HD, HD), (CMP_BLOCK * HD) ** -0.5),
        "nsa_b_gate": nrm(ks[24], (DEPTH, 3 * H_NSA), 0.02),
        "w_out": nrm(ks[25], (DEPTH, D_MIX, D_MODEL), D_MIX ** -0.5),
        "norm_f": 1.0 + nrm(ks[26], (D_MODEL,), 0.02),
    }


def reference(x_prompt, x_sample, cache_cmp_kv, cache_slc_kv, page_table, state_win_kv, state_gla, state_conv,
              rel_bias, norm_g, w_in, conv_w_dw, conv_b_dw, conv_ln_g, conv_ln_b, conv_w_pw, conv_b_pw,
              gla_w_gk, gla_b_gk, gla_norm_g, nsa_q_norm_g, nsa_k_norm_g, nsa_pe_cmp, nsa_w_cmp, nsa_b_gate,
              w_out, norm_f):
    table_g = rel_bias.reshape(N_BUCKETS, G_KV, HPG)
    stacked = (norm_g, w_in, conv_w_dw, conv_b_dw, conv_ln_g, conv_ln_b, conv_w_pw, conv_b_pw,
               gla_w_gk, gla_b_gk, gla_norm_g, nsa_q_norm_g, nsa_k_norm_g, nsa_pe_cmp, nsa_w_cmp,
               nsa_b_gate, w_out)
    n_seq = page_table.shape[0]
    yp, ys = x_prompt, x_sample
    outs_p, outs_s = [], []
    for l in range(DEPTH):
        lp = {name: arr[l] for name, arr in zip(LAYER_PARAMS, stacked)}
        yp, st_p = mixer_layer(yp, lp, table_g, None)
        cmp_past = cache_cmp_kv[l][page_table].reshape(n_seq, -1, 2, G_KV, HD)
        slc_past = cache_slc_kv[l][page_table].reshape(n_seq, -1, 2, G_KV, HD)
        ys, st_s = mixer_layer(ys, lp, table_g, (state_conv[l], state_gla[l], state_win_kv[l], cmp_past, slc_past))
        outs_p.append(st_p)
        outs_s.append(st_s)
    yp = rmsnorm(yp, norm_f)
    ys = rmsnorm(ys, norm_f)
    conv_p = jnp.stack([o[0] for o in outs_p])
    conv_s = jnp.stack([o[0] for o in outs_s])
    gla_p = jnp.stack([o[1] for o in outs_p])
    gla_s = jnp.stack([o[1] for o in outs_s])
    win_p = jnp.stack([o[2] for o in outs_p])
    win_s = jnp.stack([o[2] for o in outs_s])
    cmp_p = jnp.stack([o[3] for o in outs_p])
    cmp_s = jnp.stack([o[3] for o in outs_s])
    slc_p = jnp.stack([o[4] for o in outs_p])
    slc_s = jnp.stack([o[4] for o in outs_s])
    return (yp, ys, conv_p, conv_s, gla_p, gla_s, win_p, win_s, cmp_p, cmp_s, slc_p, slc_s)
```

```python
import functools
import math

import numpy as np
import jax
import jax.numpy as jnp
from jax import lax
from jax.experimental import pallas as pl
from jax.experimental.pallas import tpu as pltpu

F32 = jnp.float32
BF16 = jnp.bfloat16

D_MODEL = 1024
W_CONV = 256
CONV_WIDTH = 31
HIST = CONV_WIDTH - 1
H_GLA = 4
DK_GLA = 32
DV_GLA = 64
W_GLA = H_GLA * DV_GLA
GATE_RANK = 16
GATE_NORMALIZER = 16.0
H_NSA = 8
HD = 64
G_KV = 2
HPG = H_NSA // G_KV
W_NSA = H_NSA * HD
CMP_BLOCK = 32
SEL_BLOCK = 64
TOP_N = 8
WINDOW = 512
N_BUCKETS = 32
MAX_EXACT = N_BUCKETS // 2
MAX_DISTANCE = 128
PAGE_SIZE = 128
SCALE = HD ** -0.5
EPS = 1e-6
NEG = -1e30
FORCE = 1e4

LANES = 128
SEL_LANE0 = 64
SEL_SLOTS = 40
CONST_LANE = 112
ODD_SLOT0 = 64
TQ = 256
VMEM_LIMIT = 56 * 1024 * 1024

C_CONV = 0
C_GLA = 3 * W_CONV
N_GLA = 2 * H_GLA * DK_GLA + 2 * W_GLA + LANES
C_NSA = C_GLA + N_GLA
N_NSA = W_NSA + 3 * 256 + W_NSA + LANES
N_IN = C_NSA + N_NSA


def _rel_bucket_np(dist):
    n = np.maximum(dist, 0)
    nf = np.maximum(n, 1).astype(np.float32)
    large = MAX_EXACT + (np.log(nf / np.float32(MAX_EXACT)) / np.float32(math.log(MAX_DISTANCE / MAX_EXACT))
                         * np.float32(N_BUCKETS - MAX_EXACT)).astype(np.int32)
    large = np.minimum(large, N_BUCKETS - 1)
    return np.where(n < MAX_EXACT, n, large).astype(np.int32)


def _in_proj_columns():
    o = {}
    off = 0
    for name, w in (("c_a", 256), ("c_b", 256), ("c_gate", 256), ("l_q", 128), ("l_k", 128), ("l_v", 256),
                    ("l_gk", 16), ("l_gate", 256), ("n_q", 512), ("n_cmp", 256), ("n_slc", 256),
                    ("n_win", 256), ("n_g", 24), ("n_gate", 512)):
        o[name] = off
        off += w
    cols = -np.ones((N_IN,), np.int64)

    def put(dst, name, width):
        cols[dst:dst + width] = o[name] + np.arange(width)

    put(0, "c_a", 256); put(256, "c_b", 256); put(512, "c_gate", 256)
    g = C_GLA
    put(g, "l_q", 128); put(g + 128, "l_k", 128); put(g + 256, "l_v", 256); put(g + 512, "l_gate", 256)
    put(g + 768, "l_gk", 16)
    n = C_NSA
    put(n, "n_q", 512); put(n + 512, "n_cmp", 256); put(n + 768, "n_slc", 256); put(n + 1024, "n_win", 256)
    cols[n + 1280:n + 1792] = o["n_gate"] + _nsa_perm()
    put(n + 1792, "n_g", 24)
    return cols


def _nsa_perm():
    p = np.zeros((W_NSA,), np.int64)
    for j in range(HPG):
        for g in range(G_KV):
            p[128 * j + 64 * g:128 * j + 64 * g + 64] = 64 * (HPG * g + j) + np.arange(64)
    return p


def _blockdiag_ones(n, blk):
    i = np.arange(n)
    return (i[:, None] // blk == i[None, :] // blk).astype(np.float32)


def _split2(x):
    hi = x.astype(BF16)
    lo = (x - hi.astype(F32)).astype(BF16)
    return hi, lo


def _dot(a, b):
    return jnp.dot(a, b, preferred_element_type=F32)


def _dot_nt(a, b):
    return lax.dot_general(a, b, (((1,), (1,)), ((), ())), preferred_element_type=F32)


def _dot_tn(a, b):
    return lax.dot_general(a, b, (((0,), (0,)), ((), ())), preferred_element_type=F32)


def _dot_split(x, m_bf16):
    hi, lo = _split2(x)
    return _dot(hi, m_bf16) + _dot(lo, m_bf16)


def _seg_rms(x, ind_ref, seg):
    return lax.rsqrt(_dot_split(x * x, ind_ref[...]) * (1.0 / seg) + EPS)


def _sigmoid(x):
    return 1.0 / (1.0 + jnp.exp(-x))


def _silu(x):
    return x * _sigmoid(x)


def _lane(shape):
    return lax.broadcasted_iota(jnp.int32, shape, len(shape) - 1)


def _half_pad(x, odd):
    if odd:
        x = pltpu.roll(x, 64, axis=1)
    return jnp.where(_lane(x.shape) < 64, x, 0.0)


def _inproj_body(x_ref, ng_ref, w_ref, indq_ref, indk_ref, qg_ref, kg_ref, *outs, tm, seq_len, prompt):
    zc_ref, zg_ref, qn_ref, cmp_ref, slc_ref, win_ref, gate_ref, g_ref = outs[:8]
    x = x_ref[...]
    ms = jnp.mean(x * x, axis=-1, keepdims=True)
    h = (x * lax.rsqrt(ms + EPS) * ng_ref[...]).astype(BF16)

    def mm(lo, width):
        return _dot(h, w_ref[:, lo:lo + width])

    zc_ref[...] = mm(C_CONV, 3 * W_CONV)
    zg_ref[...] = mm(C_GLA, N_GLA)
    q = mm(C_NSA, W_NSA)
    qn_ref[...] = (q * _seg_rms(q, indq_ref, HD) * qg_ref[...]).astype(BF16)
    cmp_ref[...] = mm(C_NSA + 512, 256)
    slc = mm(C_NSA + 768, 256)
    win = mm(C_NSA + 1024, 256)
    ks = slc[:, 0:128]
    ks = ks * _seg_rms(ks, indk_ref, HD) * kg_ref[0:1, :]
    kw = win[:, 0:128]
    kw = kw * _seg_rms(kw, indk_ref, HD) * kg_ref[1:2, :]
    slc_ref[:, 0:128] = ks
    slc_ref[:, 128:256] = slc[:, 128:256]
    win_ref[:, 0:128] = kw
    win_ref[:, 128:256] = win[:, 128:256]
    gate_ref[...] = mm(C_NSA + 1280, W_NSA)
    g_ref[...] = mm(C_NSA + 1792, LANES)
    if prompt:
        ksa_ref, vs_ref, kwa_ref, vw_ref = outs[8:]
        lane = _lane((tm, LANES))
        row = lax.broadcasted_iota(jnp.int32, (tm, LANES), 0)
        t = lax.rem(pl.program_id(0) * tm + row, seq_len)
        ones = ((lane == CONST_LANE) | (lane == CONST_LANE + 1)).astype(F32)
        onehot = ((lane - SEL_LANE0) == lax.shift_right_logical(t, 6)).astype(F32)
        for g in range(G_KV):
            ksa_ref[g] = (_half_pad(ks, g == 1) + onehot + ones).astype(BF16)
            kwa_ref[g] = (_half_pad(kw, g == 1) + ones).astype(BF16)
        vs_ref[...] = slc[:, 128:256].astype(BF16)
        vw_ref[...] = win[:, 128:256].astype(BF16)


def _in_proj(x2d, norm_g, w, indq, indk, qg, kg, *, seq_len, prompt):
    m = x2d.shape[0]
    tm = min(256, m)
    assert m % tm == 0 and (not prompt or seq_len % tm == 0)
    row = lambda n: pl.BlockSpec((tm, n), lambda i: (i, 0))
    full = lambda a: pl.BlockSpec(a.shape, lambda i: (0,) * a.ndim)
    out_shape = [jax.ShapeDtypeStruct((m, 3 * W_CONV), F32), jax.ShapeDtypeStruct((m, N_GLA), F32),
                 jax.ShapeDtypeStruct((m, W_NSA), BF16), jax.ShapeDtypeStruct((m, 256), F32),
                 jax.ShapeDtypeStruct((m, 256), F32), jax.ShapeDtypeStruct((m, 256), F32),
                 jax.ShapeDtypeStruct((m, W_NSA), F32), jax.ShapeDtypeStruct((m, LANES), F32)]
    out_specs = [row(3 * W_CONV), row(N_GLA), row(W_NSA), row(256), row(256), row(256), row(W_NSA), row(LANES)]
    if prompt:
        aug = pl.BlockSpec((G_KV, tm, LANES), lambda i: (0, i, 0))
        out_shape += [jax.ShapeDtypeStruct((G_KV, m, LANES), BF16), jax.ShapeDtypeStruct((m, LANES), BF16),
                      jax.ShapeDtypeStruct((G_KV, m, LANES), BF16), jax.ShapeDtypeStruct((m, LANES), BF16)]
        out_specs += [aug, row(LANES), aug, row(LANES)]
    return pl.pallas_call(
        functools.partial(_inproj_body, tm=tm, seq_len=seq_len, prompt=prompt),
        grid=(m // tm,),
        in_specs=[row(D_MODEL), full(norm_g), full(w), full(indq), full(indk), full(qg), full(kg)],
        out_specs=out_specs, out_shape=out_shape,
        compiler_params=pltpu.CompilerParams(dimension_semantics=("arbitrary",), vmem_limit_bytes=VMEM_LIMIT),
        name="in_proj",
    )(x2d, norm_g, w, indq, indk, qg, kg)


def _outproj_body(x_ref, yc_ref, yg_ref, yn_ref, w_ref, nf_ref, o_ref, *, final):
    x = x_ref[...]
    x = (x + _dot(yc_ref[...], w_ref[0:256, :]) + _dot(yg_ref[...], w_ref[256:512, :])
         + _dot(yn_ref[...].astype(BF16), w_ref[512:1024, :]))
    if final:
        ms = jnp.mean(x * x, axis=-1, keepdims=True)
        x = x * lax.rsqrt(ms + EPS) * nf_ref[...]
    o_ref[...] = x


def _out_proj(x2d, yc, yg, yn, w, norm_f, *, final):
    m = x2d.shape[0]
    tm = min(512, m)
    assert m % tm == 0
    row = lambda n: pl.BlockSpec((tm, n), lambda i: (i, 0))
    full = lambda a: pl.BlockSpec(a.shape, lambda i: (0,) * a.ndim)
    return pl.pallas_call(
        functools.partial(_outproj_body, final=final),
        grid=(m // tm,),
        in_specs=[row(D_MODEL), row(256), row(256), row(512), full(w), full(norm_f)],
        out_specs=row(D_MODEL), out_shape=jax.ShapeDtypeStruct((m, D_MODEL), F32),
        compiler_params=pltpu.CompilerParams(dimension_semantics=("arbitrary",), vmem_limit_bytes=VMEM_LIMIT),
        name="out_proj",
    )(x2d, yc, yg, yn, w, norm_f)


def _conv_body(zc_ref, hist_ref, wdw_ref, bdw_ref, lng_ref, lnb_ref, wpw_ref, bpw_ref, y_ref, st_ref,
               ext_scr, act_scr, *, nb, t_len, tc, tmm):
    n_chunks = t_len // tc

    def per_batch(bi, _):
        base = bi * t_len
        ext_scr[0:32, :] = jnp.zeros((32, W_CONV), F32)
        ext_scr[2:32, :] = hist_ref[bi]

        def chunk(c, _):
            r0 = pl.multiple_of(c * tc, tc)
            g0 = pl.multiple_of(base + r0, tc)
            a = zc_ref[pl.ds(g0, tc), 0:256]
            b = zc_ref[pl.ds(g0, tc), 256:512]
            ext_scr[pl.ds(32 + r0, tc), :] = a * _sigmoid(b)
            win = ext_scr[pl.ds(r0, tc + 32), :]
            acc = jnp.zeros((tc, W_CONV), F32) + bdw_ref[...]
            for r in range(8):
                taps = [j for j in range(r, CONV_WIDTH, 8)]
                wr = win[2 + r:2 + r + tc + 8 * (len(taps) - 1)]
                for mi, j in enumerate(taps):
                    acc = acc + wr[8 * mi:8 * mi + tc] * wdw_ref[j:j + 1, :]
            mu = jnp.mean(acc, axis=-1, keepdims=True)
            d = acc - mu
            var = jnp.mean(d * d, axis=-1, keepdims=True)
            yn = d * lax.rsqrt(var + EPS) * lng_ref[...] + lnb_ref[...]
            act_scr[pl.ds(g0, tc), :] = _silu(yn)
            return 0

        lax.fori_loop(0, n_chunks, chunk, 0)
        st_ref[bi] = ext_scr[2 + t_len:32 + t_len, :]
        return 0

    lax.fori_loop(0, nb, per_batch, 0)

    def mm(c, _):
        r0 = pl.multiple_of(c * tmm, tmm)
        y = _dot(act_scr[pl.ds(r0, tmm), :].astype(BF16), wpw_ref[...]) + bpw_ref[...]
        y_ref[pl.ds(r0, tmm), :] = (y * _silu(zc_ref[pl.ds(r0, tmm), 512:768])).astype(BF16)
        return 0

    lax.fori_loop(0, nb * t_len // tmm, mm, 0)


def _conv(zc, hist, wdw, bdw, lng, lnb, wpw, bpw, *, nb, t_len):
    m = zc.shape[0]
    n_b = m // t_len
    tc = min(64, t_len)
    tmm = min(256, nb * t_len)
    full = lambda a: pl.BlockSpec(a.shape, lambda i: (0,) * a.ndim)
    return pl.pallas_call(
        functools.partial(_conv_body, nb=nb, t_len=t_len, tc=tc, tmm=tmm),
        grid=(n_b // nb,),
        in_specs=[pl.BlockSpec((nb * t_len, 3 * W_CONV), lambda i: (i, 0)),
                  pl.BlockSpec((nb, HIST, W_CONV), lambda i: (i, 0, 0)),
                  full(wdw), full(bdw), full(lng), full(lnb), full(wpw), full(bpw)],
        out_specs=[pl.BlockSpec((nb * t_len, W_CONV), lambda i: (i, 0)),
                   pl.BlockSpec((nb, HIST, W_CONV), lambda i: (i, 0, 0))],
        out_shape=[jax.ShapeDtypeStruct((m, W_CONV), BF16), jax.ShapeDtypeStruct((n_b, HIST, W_CONV), F32)],
        scratch_shapes=[pltpu.VMEM((32 + t_len, W_CONV), F32), pltpu.VMEM((nb * t_len, W_CONV), F32)],
        compiler_params=pltpu.CompilerParams(dimension_semantics=("arbitrary",), vmem_limit_bytes=VMEM_LIMIT),
        name="conv",
    )(zc, hist, wdw, bdw, lng, lnb, wpw, bpw)


def _gla_body(zg_ref, s0_ref, wgk_ref, bgk_ref, ltri_ref, lsum_ref, ind_ref, indv_ref, ng_ref, bd_ref,
              y_ref, st_ref, bc_scr, qe_scr, ke_scr, dec_scr, o_scr, s_scr, *, nb, t_len, chunk, rb):
    rows = nb * t_len
    n_rb = rows // rb
    cpb = t_len // chunk

    def phase1(i, _):
        r0 = pl.multiple_of(i * rb, rb)
        gk = zg_ref[pl.ds(r0, rb), 768:896]
        pre = _dot_split(gk, wgk_ref[0]) + _dot(gk.astype(BF16), wgk_ref[1]) + bgk_ref[...]
        la = (jnp.minimum(pre, 0.0) - jnp.log(1.0 + jnp.exp(-jnp.abs(pre)))) * (1.0 / GATE_NORMALIZER)
        h1, l1 = _split2(la)
        l2 = (la - h1.astype(F32) - l1.astype(F32)).astype(BF16)
        bc = _dot(ltri_ref[...], h1) + _dot(ltri_ref[...], l1) + _dot(ltri_ref[...], l2)
        bt = _dot(lsum_ref[...], h1) + _dot(lsum_ref[...], l1) + _dot(lsum_ref[...], l2)
        bc_scr[pl.ds(r0, rb), :] = bc
        qe_scr[pl.ds(r0, rb), :] = zg_ref[pl.ds(r0, rb), 0:128] * (DK_GLA ** -0.5) * jnp.exp(bc)
        ke_scr[pl.ds(r0, rb), :] = zg_ref[pl.ds(r0, rb), 128:256] * jnp.exp(bt - bc)
        dec_scr[pl.ds(r0, rb), :] = jnp.exp(bt)
        return 0

    lax.fori_loop(0, n_rb, phase1, 0)

    ti = lax.broadcasted_iota(jnp.int32, (chunk, chunk, LANES), 0)
    si = lax.broadcasted_iota(jnp.int32, (chunk, chunk, LANES), 1)
    causal = si <= ti

    def phase2(c, _):
        r0 = pl.multiple_of(c * chunk, chunk)
        bc = bc_scr[pl.ds(r0, chunk), :]
        q = zg_ref[pl.ds(r0, chunk), 0:128] * (DK_GLA ** -0.5)
        k = zg_ref[pl.ds(r0, chunk), 128:256]
        v = zg_ref[pl.ds(r0, chunk), 256:512]
        e = jnp.exp(jnp.where(causal, bc[:, None, :] - bc[None, :, :], NEG))
        p = (q[:, None, :] * k[None, :, :] * e).reshape(chunk * chunk, LANES)
        att = _dot_split(p, ind_ref[...]).reshape(chunk, chunk, W_GLA)
        o_scr[pl.ds(r0, chunk), :] = (att * v[None, :, :]).sum(axis=1)
        return 0

    lax.fori_loop(0, rows // chunk, phase2, 0)

    def per_batch(bi, _):
        s_scr[...] = s0_ref[bi]

        def step(c, _):
            r0 = pl.multiple_of(bi * t_len + c * chunk, chunk)
            s = s_scr[...]
            o_scr[pl.ds(r0, chunk), :] += _dot_nt(qe_scr[pl.ds(r0, chunk), :].astype(BF16), s.astype(BF16))
            upd = _dot_tn(zg_ref[pl.ds(r0, chunk), 256:512].astype(BF16), ke_scr[pl.ds(r0, chunk), :].astype(BF16))
            s_scr[...] = s * dec_scr[pl.ds(r0, 1), :] + upd * bd_ref[...]
            return 0

        lax.fori_loop(0, cpb, step, 0)
        st_ref[bi] = s_scr[...]
        return 0

    lax.fori_loop(0, nb, per_batch, 0)

    def phase4(i, _):
        r0 = pl.multiple_of(i * rb, rb)
        o = o_scr[pl.ds(r0, rb), :]
        o = o * _seg_rms(o, indv_ref, DV_GLA) * ng_ref[...]
        y_ref[pl.ds(r0, rb), :] = (o * _silu(zg_ref[pl.ds(r0, rb), 512:768])).astype(BF16)
        return 0

    lax.fori_loop(0, n_rb, phase4, 0)


def _gla(zg, s0t, wgk, bgk, ng, *, nb, t_len, chunk):
    m = zg.shape[0]
    n_b = m // t_len
    rows = nb * t_len
    rb = min(128, rows)
    assert rows % rb == 0 and rb % chunk == 0 and t_len % chunk == 0
    ltri = jnp.asarray(_blockdiag_ones(rb, chunk) * np.tril(np.ones((rb, rb), np.float32)), BF16)
    lsum = jnp.asarray(_blockdiag_ones(rb, chunk), BF16)
    i128 = np.arange(LANES)
    i256 = np.arange(W_GLA)
    ind = jnp.asarray((i128[:, None] // DK_GLA == i256[None, :] // DV_GLA).astype(np.float32), BF16)
    indv = jnp.asarray(_blockdiag_ones(W_GLA, DV_GLA), BF16)
    bd = jnp.asarray((i256[:, None] // DV_GLA == i128[None, :] // DK_GLA).astype(np.float32))
    full = lambda a: pl.BlockSpec(a.shape, lambda i: (0,) * a.ndim)
    return pl.pallas_call(
        functools.partial(_gla_body, nb=nb, t_len=t_len, chunk=chunk, rb=rb),
        grid=(n_b // nb,),
        in_specs=[pl.BlockSpec((rows, N_GLA), lambda i: (i, 0)),
                  pl.BlockSpec((nb, W_GLA, LANES), lambda i: (i, 0, 0)),
                  full(wgk), full(bgk), full(ltri), full(lsum), full(ind), full(indv), full(ng), full(bd)],
        out_specs=[pl.BlockSpec((rows, W_GLA), lambda i: (i, 0)),
                   pl.BlockSpec((nb, W_GLA, LANES), lambda i: (i, 0, 0))],
        out_shape=[jax.ShapeDtypeStruct((m, W_GLA), BF16), jax.ShapeDtypeStruct((n_b, W_GLA, LANES), F32)],
        scratch_shapes=[pltpu.VMEM((rows, LANES), F32), pltpu.VMEM((rows, LANES), F32),
                        pltpu.VMEM((rows, LANES), F32), pltpu.VMEM((rows, LANES), F32),
                        pltpu.VMEM((rows, W_GLA), F32), pltpu.VMEM((W_GLA, LANES), F32)],
        compiler_params=pltpu.CompilerParams(dimension_semantics=("arbitrary",), vmem_limit_bytes=VMEM_LIMIT),
        name="gla",
    )(zg, s0t, wgk, bgk, ltri, lsum, ind, indv, ng, bd)


KW = CMP_BLOCK * 256


def _compress_rows(x_ref, w_ref, pe_ref, kvc_ref, n_rows):
    kvc_ref[...] = jnp.zeros(kvc_ref.shape, F32)
    for par in range(2):
        x = (x_ref[:, par * KW:(par + 1) * KW] + pe_ref[...]).astype(BF16)
        kvc_ref[ODD_SLOT0 * par:ODD_SLOT0 * par + n_rows, :] = _dot(x, w_ref[...])


def _compress_prompt_body(x_ref, w_ref, pe_ref, kvc_ref, *, n_rows):
    _compress_rows(x_ref, w_ref, pe_ref, kvc_ref, n_rows)


def _compress_prompt(cmp_rows, w, pe, *, n_b, t_len):
    n_rows = t_len // SEL_BLOCK
    full = lambda a: pl.BlockSpec(a.shape, lambda i: (0,) * a.ndim)
    return pl.pallas_call(
        functools.partial(_compress_prompt_body, n_rows=n_rows),
        grid=(n_b,),
        in_specs=[pl.BlockSpec((n_rows, 2 * KW), lambda i: (i, 0)), full(w), full(pe)],
        out_specs=pl.BlockSpec((None, LANES, 256), lambda i: (i, 0, 0)),
        out_shape=jax.ShapeDtypeStruct((n_b, LANES, 256), F32),
        compiler_params=pltpu.CompilerParams(dimension_semantics=("arbitrary",), vmem_limit_bytes=VMEM_LIMIT),
        name="compress_prompt",
    )(cmp_rows.reshape(n_b * n_rows, 2 * KW), w, pe)


def _compress_paged_body(pt_ref, *refs, n_pages, t_new, n_rows):
    pages = refs[:n_pages]
    new_ref, w_ref, pe_ref, kvc_ref, x_scr = refs[n_pages:]
    per_page = PAGE_SIZE // SEL_BLOCK
    for p in range(n_pages):
        x_scr[per_page * p:per_page * (p + 1), :] = pages[p][...]
    n_past = per_page * n_pages
    x_scr[n_past:, :] = jnp.zeros((n_rows - n_past, 2 * KW), F32)
    x_scr[n_past:n_past + 1, 0:t_new * 256] = new_ref[...]
    _compress_rows(x_scr, w_ref, pe_ref, kvc_ref, n_rows)


def _paged_specs(layer, n_pages, block):
    return [pl.BlockSpec((None, None) + block, functools.partial(
        lambda b, pt, p: (layer, pt[b, p], 0, 0), p=p)) for p in range(n_pages)]


def _compress_paged(page_table, cache, layer, cmp_new, w, pe, *, t_new):
    n_b, n_pages = page_table.shape
    per_page = PAGE_SIZE // SEL_BLOCK
    n_rows = -(-(per_page * n_pages + 1) // 8) * 8
    cache = cache.reshape(cache.shape[:2] + (per_page, 2 * KW))
    full = lambda a: pl.BlockSpec(a.shape, lambda b, pt: (0,) * a.ndim)
    grid_spec = pltpu.PrefetchScalarGridSpec(
        num_scalar_prefetch=1, grid=(n_b,),
        in_specs=_paged_specs(layer, n_pages, (per_page, 2 * KW)) + [
            pl.BlockSpec((None, 1, t_new * 256), lambda b, pt: (b, 0, 0)), full(w), full(pe)],
        out_specs=pl.BlockSpec((None, LANES, 256), lambda b, pt: (b, 0, 0)),
        scratch_shapes=[pltpu.VMEM((n_rows, 2 * KW), F32)])
    return pl.pallas_call(
        functools.partial(_compress_paged_body, n_pages=n_pages, t_new=t_new, n_rows=n_rows),
        grid_spec=grid_spec, out_shape=jax.ShapeDtypeStruct((n_b, LANES, 256), F32),
        compiler_params=pltpu.CompilerParams(dimension_semantics=("arbitrary",), vmem_limit_bytes=VMEM_LIMIT),
        name="compress_paged",
    )(page_table, *([cache] * n_pages), cmp_new.reshape(n_b, 1, t_new * 256), w, pe)


def _cmpsel_body(qn_ref, kvc_ref, bias_ref, sbase_ref, useimp_ref, indk_ref, kg_ref, cvec_ref,
                 qa_ref, ocmp_ref, *, nb, tb, n_blocks):
    rows = nb * tb
    lane = _lane((rows, LANES))
    qn = qn_ref[...].astype(F32)
    qpad = [_half_pad(qn[:, 128 * (h // 2):128 * (h // 2) + 128], h % 2 == 1) for h in range(H_NSA)]
    imp_parts = []
    o_parts = [[None] * nb for _ in range(H_NSA)]
    for bi in range(nb):
        kvc = kvc_ref[bi]
        kc = kvc[:, 0:128]
        kc = kc * _seg_rms(kc, indk_ref, HD) * kg_ref[...]
        vboth = kvc[:, 128:256].astype(BF16)
        imp_b = []
        for g in range(G_KV):
            kpad = _half_pad(kc, g == 1).astype(BF16)
            qs = jnp.concatenate([qpad[HPG * g + j][bi * tb:(bi + 1) * tb] for j in range(HPG)], axis=0).astype(BF16)
            bias = bias_ref[g]
            l = _dot_nt(qs, kpad) + bias
            vis = bias > 0.5 * NEG
            mx = jnp.max(l, axis=-1, keepdims=True)
            p = jnp.where(vis, jnp.exp(l - mx), 0.0)
            p = p / jnp.maximum(jnp.sum(p, axis=-1, keepdims=True), 1e-30)
            o = _dot(p.astype(BF16), vboth)
            ig = p[0:tb]
            for j in range(1, HPG):
                ig = ig + p[j * tb:(j + 1) * tb]
            imp_b.append(ig)
            for j in range(HPG):
                o_parts[HPG * g + j][bi] = o[j * tb:(j + 1) * tb]
        imp_parts.append(imp_b)
    for j in range(HPG):
        o0 = jnp.concatenate(o_parts[j], axis=0) if nb > 1 else o_parts[j][0]
        o1 = jnp.concatenate(o_parts[HPG + j], axis=0) if nb > 1 else o_parts[HPG + j][0]
        ocmp_ref[:, 128 * j:128 * j + 128] = jnp.where(lane < 64, o0, o1)
    for g in range(G_KV):
        ig = jnp.concatenate([imp_parts[bi][g] for bi in range(nb)], axis=0) if nb > 1 else imp_parts[0][g]
        imp = ig + pltpu.roll(ig, LANES - ODD_SLOT0, axis=1)
        score = jnp.where(useimp_ref[...] > 0.5, imp, sbase_ref[...])
        st = score.T[0:SEL_SLOTS]
        jj = lax.broadcasted_iota(jnp.int32, (SEL_SLOTS, rows), 0)
        rank = jnp.zeros((SEL_SLOTS, rows), F32)
        for jp in range(n_blocks):
            r = st[jp:jp + 1, :]
            before = (r > st) | ((r == st) & (jp < jj))
            rank = rank + before.astype(F32)
        selb = jnp.where(rank < TOP_N - 0.5, 0.0, NEG)
        full_t = jnp.concatenate([jnp.zeros((SEL_LANE0, rows), F32), selb,
                                  jnp.zeros((LANES - SEL_LANE0 - SEL_SLOTS, rows), F32)], axis=0)
        extra = full_t.T
        for j in range(HPG):
            h = HPG * g + j
            qa_ref[g, j] = (qpad[h] + extra + cvec_ref[h]).astype(qa_ref.dtype)


def _cmpsel(qn, kvc, bias, sbase, useimp, indk, kg0, cvec, *, nb, tb, n_blocks, qa_dtype):
    m = qn.shape[0]
    rows = nb * tb
    assert rows == LANES and m % rows == 0
    n_pos = bias.shape[0]
    full = lambda a: pl.BlockSpec(a.shape, lambda i: (0,) * a.ndim)
    return pl.pallas_call(
        functools.partial(_cmpsel_body, nb=nb, tb=tb, n_blocks=n_blocks),
        grid=(m // rows,),
        in_specs=[pl.BlockSpec((rows, W_NSA), lambda i: (i, 0)),
                  pl.BlockSpec((nb, LANES, 256), (lambda i: (i, 0, 0)) if nb > 1 else (lambda i: (i // n_pos, 0, 0))),
                  pl.BlockSpec((None, G_KV, HPG * tb, LANES), lambda i: (i % n_pos, 0, 0, 0)),
                  pl.BlockSpec((None, rows, LANES), lambda i: (i % n_pos, 0, 0)),
                  pl.BlockSpec((None, rows, LANES), lambda i: (i % n_pos, 0, 0)),
                  full(indk), full(kg0), full(cvec)],
        out_specs=[pl.BlockSpec((G_KV, HPG, rows, LANES), lambda i: (0, 0, i, 0)),
                   pl.BlockSpec((rows, W_NSA), lambda i: (i, 0))],
        out_shape=[jax.ShapeDtypeStruct((G_KV, HPG, m, LANES), qa_dtype), jax.ShapeDtypeStruct((m, W_NSA), F32)],
        compiler_params=pltpu.CompilerParams(dimension_semantics=("arbitrary",), vmem_limit_bytes=VMEM_LIMIT),
        name="cmpsel",
    )(qn, kvc, bias, sbase, useimp, indk, kg0, cvec)


def _mix_gates(g_ref, bg_ref, expand_ref, gate_ref, ocmp_ref, o_slc, o_win, y_ref):
    gl = _sigmoid(g_ref[...] + bg_ref[...])
    gx = _dot_split(gl, expand_ref[...])
    for j in range(HPG):
        sl = slice(128 * j, 128 * j + 128)
        o = (gx[:, 128 * j:128 * j + 128] * ocmp_ref[:, sl]
             + gx[:, 128 * (HPG + j):128 * (HPG + j) + 128] * o_slc[j]
             + gx[:, 128 * (2 * HPG + j):128 * (2 * HPG + j) + 128] * o_win[j])
        y_ref[:, sl] = (o * _silu(gate_ref[:, sl])).astype(y_ref.dtype)


def _softmax_step(q, k, v, bias, m_scr, l_scr, acc_scr):
    s = _dot_nt(q, k)
    if bias is not None:
        s = s + bias
    m_prev = m_scr[...]
    m_new = jnp.maximum(m_prev, jnp.max(s, axis=-1, keepdims=True))
    alpha = jnp.exp(m_prev - m_new)
    p = jnp.exp(s - m_new[:, 0:1])
    l_scr[...] = alpha * l_scr[...] + jnp.sum(p, axis=-1, keepdims=True)
    acc_scr[...] = alpha * acc_scr[...] + _dot(p.astype(BF16), v)
    m_scr[...] = m_new


def _flash_body(qa_ref, ksa_ref, vs_ref, kwa_ref, vw_ref, dt_ref, edge_ref, ocmp_ref, g_ref, gate_ref,
                bg_ref, expand_ref, y_ref, m_scr, l_scr, acc_scr, res_scr):
    i = pl.program_id(1)
    rows = HPG * TQ
    q0 = pl.multiple_of(i * TQ, TQ)
    qn = pl.multiple_of(jnp.maximum(i - 1, 0) * TQ, TQ)
    qe = pl.multiple_of(jnp.maximum(i - 2, 0) * TQ, TQ)

    def reset():
        m_scr[...] = jnp.full((rows, LANES), NEG, F32)
        l_scr[...] = jnp.zeros((rows, LANES), F32)
        acc_scr[...] = jnp.zeros((rows, LANES), F32)

    for g in range(G_KV):
        q = qa_ref[g].reshape(rows, LANES)
        for br, (ka_ref, v_ref) in enumerate(((ksa_ref, vs_ref), (kwa_ref, vw_ref))):
            reset()
            if br == 0:
                def far(kt, _):
                    k0 = pl.multiple_of(kt * TQ, TQ)
                    _softmax_step(q, ka_ref[g, pl.ds(k0, TQ), :], v_ref[pl.ds(k0, TQ), :], None, m_scr, l_scr, acc_scr)
                    return 0

                lax.fori_loop(0, jnp.maximum(i - 1, 0), far, 0)
            else:
                @pl.when(i >= 2)
                def _():
                    _softmax_step(q, ka_ref[g, pl.ds(qe, TQ), :], v_ref[pl.ds(qe, TQ), :], edge_ref[...],
                                  m_scr, l_scr, acc_scr)

            @pl.when(i >= 1)
            def _():
                _softmax_step(q, ka_ref[g, pl.ds(qn, TQ), :], v_ref[pl.ds(qn, TQ), :], dt_ref[1, g], m_scr, l_scr, acc_scr)

            _softmax_step(q, ka_ref[g, pl.ds(q0, TQ), :], v_ref[pl.ds(q0, TQ), :], dt_ref[0, g], m_scr, l_scr, acc_scr)
            res_scr[br, g] = acc_scr[...] / l_scr[...]

    lane = _lane((TQ, LANES))
    outs = []
    for br in range(2):
        outs.append([jnp.where(lane < 64, res_scr[br, 0, j * TQ:(j + 1) * TQ, :], res_scr[br, 1, j * TQ:(j + 1) * TQ, :])
                     for j in range(HPG)])
    _mix_gates(g_ref, bg_ref, expand_ref, gate_ref, ocmp_ref, outs[0], outs[1], y_ref)


def _flash(qa, ksa, vs, kwa, vw, dt, edge, ocmp, ng, ngate, bg, expand, *, n_b, t_len):
    m = n_b * t_len
    nq = t_len // TQ
    rows = HPG * TQ
    full = lambda a: pl.BlockSpec(a.shape, lambda b, i: (0,) * a.ndim)
    kspec = pl.BlockSpec((G_KV, None, t_len, LANES), lambda b, i: (0, b, 0, 0))
    vspec = pl.BlockSpec((None, t_len, LANES), lambda b, i: (b, 0, 0))
    row = lambda n: pl.BlockSpec((TQ, n), lambda b, i: (b * nq + i, 0))
    return pl.pallas_call(
        _flash_body,
        grid=(n_b, nq),
        in_specs=[pl.BlockSpec((G_KV, HPG, TQ, LANES), lambda b, i: (0, 0, b * nq + i, 0)),
                  kspec, vspec, kspec, vspec, full(dt), full(edge), row(W_NSA), row(LANES), row(W_NSA),
                  full(bg), full(expand)],
        out_specs=row(W_NSA), out_shape=jax.ShapeDtypeStruct((m, W_NSA), BF16),
        scratch_shapes=[pltpu.VMEM((rows, LANES), F32), pltpu.VMEM((rows, LANES), F32),
                        pltpu.VMEM((rows, LANES), F32), pltpu.VMEM((2, G_KV, rows, LANES), F32)],
        compiler_params=pltpu.CompilerParams(dimension_semantics=("arbitrary", "arbitrary"),
                                             vmem_limit_bytes=VMEM_LIMIT),
        name="flash",
    )(qa, ksa.reshape(G_KV, n_b, t_len, LANES), vs.reshape(n_b, t_len, LANES),
      kwa.reshape(G_KV, n_b, t_len, LANES), vw.reshape(n_b, t_len, LANES), dt, edge, ocmp, ng, ngate, bg, expand)


def _dec_attn_body(pt_ref, *refs, n_pages, t_new, n_keys, n_wkeys):
    pages = refs[:n_pages]
    (qa_ref, slc_ref, win_ref, wst_ref, dsl_ref, dwn_ref, ocmp_ref, g_ref, gate_ref, bg_ref, expand_ref,
     y_ref, wout_ref, ka_scr, va_scr, kw_scr, vw_scr) = refs[n_pages:]
    past = n_pages * PAGE_SIZE
    w_past = wst_ref.shape[0]
    rows = HPG * t_new

    def aug(k, g, blk):
        lane = _lane(k.shape)
        extra = ((lane == CONST_LANE) | (lane == CONST_LANE + 1)).astype(F32)
        if blk is not None:
            extra = extra + (lane == SEL_LANE0 + blk).astype(F32)
        return (_half_pad(k, g == 1) + extra).astype(BF16)

    for p in range(n_pages):
        pg = pages[p][...]
        for g in range(G_KV):
            ka_scr[g, p * PAGE_SIZE:(p + 1) * PAGE_SIZE, :] = aug(pg[:, 0:128], g, (p * PAGE_SIZE) // SEL_BLOCK)
            ka_scr[g, p * PAGE_SIZE + SEL_BLOCK:(p + 1) * PAGE_SIZE, :] = aug(
                pg[SEL_BLOCK:, 0:128], g, (p * PAGE_SIZE) // SEL_BLOCK + 1)
        va_scr[p * PAGE_SIZE:(p + 1) * PAGE_SIZE, :] = pg[:, 128:256].astype(BF16)
    tail = n_keys - past
    new = slc_ref[...]
    va_scr[past:n_keys, :] = jnp.zeros((tail, LANES), BF16)
    va_scr[past:past + t_new, :] = new[:, 128:256].astype(BF16)
    for g in range(G_KV):
        ka_scr[g, past:n_keys, :] = jnp.zeros((tail, LANES), BF16)
        ka_scr[g, past:past + t_new, :] = aug(new[:, 0:128], g, past // SEL_BLOCK)
    wst = wst_ref[...]
    wnew = win_ref[...]
    wtail = n_wkeys - w_past
    vw_scr[0:w_past, :] = wst[:, 128:256].astype(BF16)
    vw_scr[w_past:n_wkeys, :] = jnp.zeros((wtail, LANES), BF16)
    vw_scr[w_past:w_past + t_new, :] = wnew[:, 128:256].astype(BF16)
    for g in range(G_KV):
        kw_scr[g, 0:w_past, :] = aug(wst[:, 0:128], g, None)
        kw_scr[g, w_past:n_wkeys, :] = jnp.zeros((wtail, LANES), BF16)
        kw_scr[g, w_past:w_past + t_new, :] = aug(wnew[:, 0:128], g, None)
    wout_ref[0:w_past - t_new, :] = wst[t_new:, :]
    wout_ref[w_past - t_new:w_past, :] = wnew

    res = [[None, None], [None, None]]
    for g in range(G_KV):
        q = qa_ref[g].reshape(rows, LANES).astype(BF16)
        for br, (k_scr, v_scr, b_ref) in enumerate(((ka_scr, va_scr, dsl_ref), (kw_scr, vw_scr, dwn_ref))):
            s = _dot_nt(q, k_scr[g]) + b_ref[g]
            mx = jnp.max(s, axis=-1, keepdims=True)
            p = jnp.exp(s - mx)
            den = jnp.sum(p, axis=-1, keepdims=True)
            res[br][g] = _dot(p.astype(BF16), v_scr[...]) / den
    lane = _lane((t_new, LANES))
    outs = [[jnp.where(lane < 64, res[br][0][j * t_new:(j + 1) * t_new], res[br][1][j * t_new:(j + 1) * t_new])
             for j in range(HPG)] for br in range(2)]
    _mix_gates(g_ref, bg_ref, expand_ref, gate_ref, ocmp_ref, outs[0], outs[1], y_ref)


def _dec_attn(page_table, cache, layer, qa, slc_new, win_new, win_state, dsl, dwn, ocmp, ng, ngate, bg, expand, *, t_new):
    n_b, n_pages = page_table.shape
    m = n_b * t_new
    w_past = win_state.shape[1]
    n_keys = dsl.shape[-1]
    n_wkeys = dwn.shape[-1]
    full = lambda a: pl.BlockSpec(a.shape, lambda b, pt: (0,) * a.ndim)
    row = lambda n: pl.BlockSpec((t_new, n), lambda b, pt: (b, 0))
    grid_spec = pltpu.PrefetchScalarGridSpec(
        num_scalar_prefetch=1, grid=(n_b,),
        in_specs=_paged_specs(layer, n_pages, (PAGE_SIZE, 256)) + [
            pl.BlockSpec((G_KV, HPG, t_new, LANES), lambda b, pt: (0, 0, b, 0)),
            row(256), row(256), pl.BlockSpec((None, w_past, 256), lambda b, pt: (b, 0, 0)),
            full(dsl), full(dwn), row(W_NSA), row(LANES), row(W_NSA), full(bg), full(expand)],
        out_specs=[row(W_NSA), pl.BlockSpec((None, w_past, 256), lambda b, pt: (b, 0, 0))],
        scratch_shapes=[pltpu.VMEM((G_KV, n_keys, LANES), BF16), pltpu.VMEM((n_keys, LANES), BF16),
                        pltpu.VMEM((G_KV, n_wkeys, LANES), BF16), pltpu.VMEM((n_wkeys, LANES), BF16)])
    return pl.pallas_call(
        functools.partial(_dec_attn_body, n_pages=n_pages, t_new=t_new, n_keys=n_keys, n_wkeys=n_wkeys),
        grid_spec=grid_spec,
        out_shape=[jax.ShapeDtypeStruct((m, W_NSA), F32), jax.ShapeDtypeStruct((n_b, w_past, 256), F32)],
        compiler_params=pltpu.CompilerParams(dimension_semantics=("arbitrary",), vmem_limit_bytes=VMEM_LIMIT),
        name="dec_attn",
    )(page_table, *([cache] * n_pages), qa, slc_new, win_new, win_state, dsl, dwn, ocmp, ng, ngate, bg, expand)


def _bias_rows(table, qpos, kpos, valid, sub_far):
    dist = qpos[:, None] - kpos[None, :]
    ok = valid & (dist >= 0)
    bk = jnp.asarray(_rel_bucket_np(dist))
    b = table[bk]
    if sub_far:
        b = b - table[N_BUCKETS - 1][None, None, :]
    b = jnp.where(jnp.asarray(ok)[:, :, None], b, NEG)
    t, k = dist.shape
    return b.reshape(t, k, G_KV, HPG).transpose(2, 3, 0, 1).reshape(G_KV, HPG * t, k)


def _cmp_tables(table, qpos_blocks, n_blocks):
    slot = np.arange(LANES)
    blk = np.where(slot < ODD_SLOT0, 2 * slot, 2 * (slot - ODD_SLOT0) + 1)
    real = (blk < 2 * n_blocks) & ((slot % ODD_SLOT0) < n_blocks)
    blk_end = blk * CMP_BLOCK + CMP_BLOCK - 1
    biases, sbases, useimps = [], [], []
    for qpos in qpos_blocks:
        biases.append(_bias_rows(table, qpos, blk_end, real[None, :], False))
        j = np.arange(LANES)[None, :]
        cur = (qpos // SEL_BLOCK)[:, None]
        forced = (j == 0) | (j == cur) | (j == cur - 1)
        valid = j * SEL_BLOCK <= qpos[:, None]
        inrange = j < n_blocks
        sbases.append(np.where(~inrange, -2.0, np.where(forced, FORCE, -1.0)).astype(np.float32))
        useimps.append((inrange & ~forced & valid).astype(np.float32))
    return jnp.stack(biases), jnp.asarray(np.stack(sbases)), jnp.asarray(np.stack(useimps))


def _prep_weights(rel_bias, w_in, conv_w_pw, gla_w_gk, nsa_q_norm_g, nsa_k_norm_g, nsa_pe_cmp, nsa_w_cmp,
                  nsa_b_gate, w_out):
    depth = w_in.shape[0]
    cols = _in_proj_columns()
    w_in_p = jnp.where(jnp.asarray(cols >= 0)[None, None, :],
                       jnp.take(w_in, jnp.asarray(np.maximum(cols, 0)), axis=2), 0.0).astype(BF16)
    perm = np.concatenate([np.arange(512), 512 + _nsa_perm()])
    w_out_p = jnp.take(w_out, jnp.asarray(perm), axis=1).astype(BF16)
    qg = jnp.tile(nsa_q_norm_g, (1, H_NSA))[:, None, :] * SCALE
    kg = jnp.tile(nsa_k_norm_g, (1, 1, G_KV))
    wgk = jnp.zeros((depth, LANES, LANES), F32).at[:, :GATE_RANK, :].set(gla_w_gk)
    wgk_hi = wgk.astype(BF16)
    wgk2 = jnp.stack([wgk_hi, (wgk - wgk_hi.astype(F32)).astype(BF16)], axis=1)
    wbd = jnp.zeros((depth, CMP_BLOCK, 256, 256), F32)
    for s in range(2):
        for g in range(G_KV):
            o = 128 * s + 64 * g
            wbd = wbd.at[:, :, o:o + 64, o:o + 64].set(nsa_w_cmp[:, s])
    wbd = wbd.reshape(depth, KW, 256)
    pe = jnp.tile(nsa_pe_cmp.transpose(0, 2, 1, 3)[:, :, :, None, :], (1, 1, 1, G_KV, 1)).reshape(depth, 1, KW)
    bg = jnp.zeros((depth, 1, LANES), F32).at[:, 0, :3 * H_NSA].set(nsa_b_gate)
    expand = np.zeros((LANES, 3 * HPG * LANES), np.float32)
    for br in range(3):
        for g in range(G_KV):
            for j in range(HPG):
                c = LANES * (HPG * br + j) + 64 * g
                expand[H_NSA * br + HPG * g + j, c:c + 64] = 1.0
    far = rel_bias[N_BUCKETS - 1]
    far_hi = far.astype(BF16).astype(F32)
    cvec = jnp.zeros((H_NSA, 1, LANES), F32).at[:, 0, CONST_LANE].set(far_hi).at[:, 0, CONST_LANE + 1].set(far - far_hi)
    return dict(w_in=w_in_p, w_out=w_out_p, qg=qg, kg=kg, wgk=wgk2, wbd=wbd.astype(BF16), pe=pe, bg=bg,
                expand=jnp.asarray(expand, BF16), cvec=cvec, wpw=conv_w_pw.astype(BF16),
                indq=jnp.asarray(_blockdiag_ones(W_NSA, HD), BF16), indk=jnp.asarray(_blockdiag_ones(LANES, HD), BF16))


def kernel(x_prompt, x_sample, cache_cmp_kv, cache_slc_kv, page_table, state_win_kv, state_gla, state_conv, rel_bias, norm_g, w_in, conv_w_dw, conv_b_dw, conv_ln_g, conv_ln_b, conv_w_pw, conv_b_pw, gla_w_gk, gla_b_gk, gla_norm_g, nsa_q_norm_g, nsa_k_norm_g, nsa_pe_cmp, nsa_w_cmp, nsa_b_gate, w_out, norm_f):
    depth = w_in.shape[0]
    n_bp, t_p, _ = x_prompt.shape
    n_bs, t_s, _ = x_sample.shape
    n_pages = page_table.shape[1]
    past = n_pages * PAGE_SIZE
    w_past = state_win_kv.shape[2]
    assert t_p % TQ == 0 and WINDOW == 2 * TQ and t_s == 8 and w_past == WINDOW and past % SEL_BLOCK == 0
    wp = _prep_weights(rel_bias, w_in, conv_w_pw, gla_w_gk, nsa_q_norm_g, nsa_k_norm_g, nsa_pe_cmp, nsa_w_cmp,
                       nsa_b_gate, w_out)
    table = rel_bias
    nblk_p = t_p // SEL_BLOCK
    nblk_s = (past + SEL_BLOCK) // SEL_BLOCK
    pos_p = [np.arange(i * LANES, (i + 1) * LANES) for i in range(t_p // LANES)]
    bias_p, sbase_p, useimp_p = _cmp_tables(table, pos_p, nblk_p)
    nb_s = LANES // t_s
    qpos_s = past + np.arange(t_s)
    bias_s, _, _ = _cmp_tables(table, [qpos_s], nblk_s)
    _, sbase_1, useimp_1 = _cmp_tables(table, [np.tile(qpos_s, nb_s)], nblk_s)
    tq = np.arange(TQ)
    dt = jnp.stack([_bias_rows(table, tq, tq, np.ones((1, 1), bool), True),
                    _bias_rows(table, TQ + tq, tq, np.ones((1, 1), bool), True)])
    edge = jnp.asarray(np.tile(np.where(tq[None, :] > tq[:, None], 0.0, NEG).astype(np.float32), (HPG, 1)))
    n_keys = past + LANES
    kpos = np.arange(n_keys)
    dsl = _bias_rows(table, qpos_s, kpos, (kpos < past + t_s)[None, :], True)
    n_wkeys = w_past + LANES
    wk = np.arange(n_wkeys)
    wpos = past - w_past + wk
    wvalid = (wk < w_past + t_s)[None, :] & ((qpos_s[:, None] - wpos[None, :]) < WINDOW) & (wpos >= 0)[None, :]
    dwn = _bias_rows(table, qpos_s, wpos, wvalid, True)

    cache_cmp = cache_cmp_kv.reshape(cache_cmp_kv.shape[:3] + (256,))
    cache_slc = cache_slc_kv.reshape(cache_slc_kv.shape[:3] + (256,))
    win_state = state_win_kv.reshape(depth, n_bs, w_past, 256)
    s0 = state_gla.transpose(0, 1, 2, 4, 3).reshape(depth, n_bs, W_GLA, DK_GLA)
    i256 = np.arange(W_GLA)
    i128 = np.arange(LANES)
    bdmask = jnp.asarray((i256[:, None] // DV_GLA == i128[None, :] // DK_GLA).astype(np.float32))
    s0 = jnp.tile(s0, (1, 1, 1, H_GLA)) * bdmask
    zero_s0 = jnp.zeros((n_bp, W_GLA, LANES), F32)
    zero_hist = jnp.zeros((n_bp, HIST, W_CONV), F32)

    def unpack_state(st):
        b = st.shape[0]
        blocks = [st[:, DV_GLA * h:DV_GLA * (h + 1), DK_GLA * h:DK_GLA * (h + 1)] for h in range(H_GLA)]
        return jnp.stack(blocks, axis=1).transpose(0, 1, 3, 2)

    xp = x_prompt.reshape(n_bp * t_p, D_MODEL)
    xs = x_sample.reshape(n_bs * t_s, D_MODEL)
    outs_p, outs_s = [], []
    for l in range(depth):
        final = l == depth - 1
        row = lambda a: a[l][None, :]
        common = dict(wdw=conv_w_dw[l], bdw=row(conv_b_dw), lng=row(conv_ln_g), lnb=row(conv_ln_b), wpw=wp["wpw"][l],
                      bpw=row(conv_b_pw))
        bgk = jnp.zeros((1, LANES), F32).at[0, :].set(gla_b_gk[l])
        gng = jnp.tile(gla_norm_g[l], (H_GLA,))[None, :]
        kg12 = wp["kg"][l, 1:3]
        kg0 = wp["kg"][l, 0:1]
        zc, zg, qn, cmp_n, slc_n, win_n, ngate, ng, ksa, vs, kwa, vw = _in_proj(
            xp, row(norm_g), wp["w_in"][l], wp["indq"], wp["indk"], wp["qg"][l], kg12, seq_len=t_p, prompt=True)
        yc, conv_st = _conv(zc, zero_hist, nb=1, t_len=t_p, **common)
        yg, gla_st = _gla(zg, zero_s0, wp["wgk"][l], bgk, gng, nb=1, t_len=t_p, chunk=16)
        kvc = _compress_prompt(cmp_n, wp["wbd"][l], wp["pe"][l], n_b=n_bp, t_len=t_p)
        qa, ocmp = _cmpsel(qn, kvc, bias_p, sbase_p, useimp_p, wp["indk"], kg0, wp["cvec"], nb=1, tb=LANES,
                           n_blocks=nblk_p, qa_dtype=BF16)
        yn = _flash(qa, ksa, vs, kwa, vw, dt, edge, ocmp, ng, ngate, wp["bg"][l], wp["expand"], n_b=n_bp, t_len=t_p)
        xp = _out_proj(xp, yc, yg, yn, wp["w_out"][l], norm_f[None, :], final=final)
        w_keep = min(WINDOW, t_p)
        outs_p.append((conv_st, unpack_state(gla_st),
                       win_n.reshape(n_bp, t_p, 256)[:, t_p - w_keep:], cmp_n, slc_n))
        zc, zg, qn, cmp_n, slc_n, win_n, ngate, ng = _in_proj(
            xs, row(norm_g), wp["w_in"][l], wp["indq"], wp["indk"], wp["qg"][l], kg12, seq_len=t_s, prompt=False)
        yc, conv_st = _conv(zc, state_conv[l], nb=nb_s, t_len=t_s, **common)
        yg, gla_st = _gla(zg, s0[l], wp["wgk"][l], bgk, gng, nb=nb_s, t_len=t_s, chunk=t_s)
        kvc = _compress_paged(page_table, cache_cmp, l, cmp_n, wp["wbd"][l], wp["pe"][l], t_new=t_s)
        qa, ocmp = _cmpsel(qn, kvc, bias_s, sbase_1, useimp_1, wp["indk"], kg0, wp["cvec"], nb=nb_s, tb=t_s,
                           n_blocks=nblk_s, qa_dtype=F32)
        yn, win_st = _dec_attn(page_table, cache_slc, l, qa, slc_n, win_n, win_state[l], dsl, dwn, ocmp, ng, ngate,
                               wp["bg"][l], wp["expand"], t_new=t_s)
        xs = _out_proj(xs, yc, yg, yn, wp["w_out"][l], norm_f[None, :], final=final)
        outs_s.append((conv_st, unpack_state(gla_st), win_st, cmp_n, slc_n))

    def stack(outs, k, shape):
        return jnp.stack([o[k] for o in outs]).reshape(shape)

    kv = (2, G_KV, HD)
    return (xp.reshape(n_bp, t_p, D_MODEL), xs.reshape(n_bs, t_s, D_MODEL),
            stack(outs_p, 0, (depth, n_bp, HIST, W_CONV)), stack(outs_s, 0, (depth, n_bs, HIST, W_CONV)),
            stack(outs_p, 1, (depth, n_bp, H_GLA, DK_GLA, DV_GLA)), stack(outs_s, 1, (depth, n_bs, H_GLA, DK_GLA, DV_GLA)),
            stack(outs_p, 2, (depth, n_bp, min(WINDOW, t_p)) + kv), stack(outs_s, 2, (depth, n_bs, w_past) + kv),
            stack(outs_p, 3, (depth, n_bp, t_p) + kv), stack(outs_s, 3, (depth, n_bs, t_s) + kv),
            stack(outs_p, 4, (depth, n_bp, t_p) + kv), stack(outs_s, 4, (depth, n_bs, t_s) + kv))
```

```python
import functools
import math

import numpy as np
import jax
import jax.numpy as jnp
from jax import lax
from jax.experimental import pallas as pl
from jax.experimental.pallas import tpu as pltpu

F32 = jnp.float32
BF16 = jnp.bfloat16

D_MODEL = 1024
W_CONV = 256
CONV_WIDTH = 31
HIST = CONV_WIDTH - 1
H_GLA = 4
DK_GLA = 32
DV_GLA = 64
W_GLA = H_GLA * DV_GLA
GATE_RANK = 16
GATE_NORMALIZER = 16.0
H_NSA = 8
HD = 64
G_KV = 2
HPG = H_NSA // G_KV
W_NSA = H_NSA * HD
CMP_BLOCK = 32
SEL_BLOCK = 64
TOP_N = 8
WINDOW = 512
N_BUCKETS = 32
MAX_EXACT = N_BUCKETS // 2
MAX_DISTANCE = 128
PAGE_SIZE = 128
SCALE = HD ** -0.5
EPS = 1e-6
NEG = -1e30
FORCE = 1e4

LANES = 128
SEL_LANE0 = 64
SEL_SLOTS = 40
CONST_LANE = 112
ODD_SLOT0 = 64
TQ = 256
VMEM_LIMIT = 56 * 1024 * 1024

C_CONV = 0
C_GLA = 3 * W_CONV
N_GLA = 2 * H_GLA * DK_GLA + 2 * W_GLA + LANES
C_NSA = C_GLA + N_GLA
N_NSA = W_NSA + 3 * 256 + W_NSA + LANES
N_IN = C_NSA + N_NSA


def _rel_bucket_np(dist):
    n = np.maximum(dist, 0)
    nf = np.maximum(n, 1).astype(np.float32)
    large = MAX_EXACT + (np.log(nf / np.float32(MAX_EXACT)) / np.float32(math.log(MAX_DISTANCE / MAX_EXACT))
                         * np.float32(N_BUCKETS - MAX_EXACT)).astype(np.int32)
    large = np.minimum(large, N_BUCKETS - 1)
    return np.where(n < MAX_EXACT, n, large).astype(np.int32)


def _in_proj_columns():
    o = {}
    off = 0
    for name, w in (("c_a", 256), ("c_b", 256), ("c_gate", 256), ("l_q", 128), ("l_k", 128), ("l_v", 256),
                    ("l_gk", 16), ("l_gate", 256), ("n_q", 512), ("n_cmp", 256), ("n_slc", 256),
                    ("n_win", 256), ("n_g", 24), ("n_gate", 512)):
        o[name] = off
        off += w
    cols = -np.ones((N_IN,), np.int64)

    def put(dst, name, width):
        cols[dst:dst + width] = o[name] + np.arange(width)

    put(0, "c_a", 256); put(256, "c_b", 256); put(512, "c_gate", 256)
    g = C_GLA
    put(g, "l_q", 128); put(g + 128, "l_k", 128); put(g + 256, "l_v", 256); put(g + 512, "l_gate", 256)
    put(g + 768, "l_gk", 16)
    n = C_NSA
    put(n, "n_q", 512); put(n + 512, "n_cmp", 256); put(n + 768, "n_slc", 256); put(n + 1024, "n_win", 256)
    cols[n + 1280:n + 1792] = o["n_gate"] + _nsa_perm()
    put(n + 1792, "n_g", 24)
    return cols


def _nsa_perm():
    p = np.zeros((W_NSA,), np.int64)
    for j in range(HPG):
        for g in range(G_KV):
            p[128 * j + 64 * g:128 * j + 64 * g + 64] = 64 * (HPG * g + j) + np.arange(64)
    return p


def _blockdiag_ones(n, blk):
    i = np.arange(n)
    return (i[:, None] // blk == i[None, :] // blk).astype(np.float32)


def _split2(x):
    hi = x.astype(BF16)
    lo = (x - hi.astype(F32)).astype(BF16)
    return hi, lo


def _dot(a, b):
    return jnp.dot(a, b, preferred_element_type=F32)


def _dot_nt(a, b):
    return lax.dot_general(a, b, (((1,), (1,)), ((), ())), preferred_element_type=F32)


def _dot_tn(a, b):
    return lax.dot_general(a, b, (((0,), (0,)), ((), ())), preferred_element_type=F32)


def _dot_split(x, m_bf16):
    hi, lo = _split2(x)
    return _dot(hi, m_bf16) + _dot(lo, m_bf16)


def _seg_rms(x, ind_ref, seg):
    return lax.rsqrt(_dot_split(x * x, ind_ref[...]) * (1.0 / seg) + EPS)


def _sigmoid(x):
    return 1.0 / (1.0 + jnp.exp(-x))


def _silu(x):
    return x * _sigmoid(x)


def _lane(shape):
    return lax.broadcasted_iota(jnp.int32, shape, len(shape) - 1)


def _half_pad(x, odd):
    if odd:
        x = pltpu.roll(x, 64, axis=1)
    return jnp.where(_lane(x.shape) < 64, x, 0.0)


def _inproj_body(x_ref, ng_ref, w_ref, indq_ref, indk_ref, qg_ref, kg_ref, *outs, tm, seq_len, prompt):
    zc_ref, zg_ref, qn_ref, cmp_ref, slc_ref, win_ref, gate_ref, g_ref = outs[:8]
    x = x_ref[...]
    ms = jnp.mean(x * x, axis=-1, keepdims=True)
    h = (x * lax.rsqrt(ms + EPS) * ng_ref[...]).astype(BF16)

    def mm(lo, width):
        return _dot(h, w_ref[:, lo:lo + width])

    zc_ref[...] = mm(C_CONV, 3 * W_CONV)
    zg_ref[...] = mm(C_GLA, N_GLA)
    q = mm(C_NSA, W_NSA)
    qn_ref[...] = (q * _seg_rms(q, indq_ref, HD) * qg_ref[...]).astype(BF16)
    cmp_ref[...] = mm(C_NSA + 512, 256)
    slc = mm(C_NSA + 768, 256)
    win = mm(C_NSA + 1024, 256)
    ks = slc[:, 0:128]
    ks = ks * _seg_rms(ks, indk_ref, HD) * kg_ref[0:1, :]
    kw = win[:, 0:128]
    kw = kw * _seg_rms(kw, indk_ref, HD) * kg_ref[1:2, :]
    slc_ref[:, 0:128] = ks
    slc_ref[:, 128:256] = slc[:, 128:256]
    win_ref[:, 0:128] = kw
    win_ref[:, 128:256] = win[:, 128:256]
    gate_ref[...] = mm(C_NSA + 1280, W_NSA)
    g_ref[...] = mm(C_NSA + 1792, LANES)
    if prompt:
        ksa_ref, vs_ref, kwa_ref, vw_ref = outs[8:]
        lane = _lane((tm, LANES))
        row = lax.broadcasted_iota(jnp.int32, (tm, LANES), 0)
        t = lax.rem(pl.program_id(0) * tm + row, seq_len)
        ones = ((lane == CONST_LANE) | (lane == CONST_LANE + 1)).astype(F32)
        onehot = ((lane - SEL_LANE0) == lax.shift_right_logical(t, 6)).astype(F32)
        for g in range(G_KV):
            ksa_ref[g] = (_half_pad(ks, g == 1) + onehot + ones).astype(BF16)
            kwa_ref[g] = (_half_pad(kw, g == 1) + ones).astype(BF16)
        vs_ref[...] = slc[:, 128:256].T.astype(BF16)
        vw_ref[...] = win[:, 128:256].T.astype(BF16)


def _in_proj(x2d, norm_g, w, indq, indk, qg, kg, *, seq_len, prompt):
    m = x2d.shape[0]
    tm = min(256, m)
    assert m % tm == 0 and (not prompt or seq_len % tm == 0)
    row = lambda n: pl.BlockSpec((tm, n), lambda i: (i, 0))
    full = lambda a: pl.BlockSpec(a.shape, lambda i: (0,) * a.ndim)
    out_shape = [jax.ShapeDtypeStruct((m, 3 * W_CONV), F32), jax.ShapeDtypeStruct((m, N_GLA), F32),
                 jax.ShapeDtypeStruct((m, W_NSA), BF16), jax.ShapeDtypeStruct((m, 256), F32),
                 jax.ShapeDtypeStruct((m, 256), F32), jax.ShapeDtypeStruct((m, 256), F32),
                 jax.ShapeDtypeStruct((m, W_NSA), F32), jax.ShapeDtypeStruct((m, LANES), F32)]
    out_specs = [row(3 * W_CONV), row(N_GLA), row(W_NSA), row(256), row(256), row(256), row(W_NSA), row(LANES)]
    if prompt:
        aug = pl.BlockSpec((G_KV, tm, LANES), lambda i: (0, i, 0))
        assert tm == TQ
        vt = pl.BlockSpec((None, LANES, tm), lambda i: (i, 0, 0))
        out_shape += [jax.ShapeDtypeStruct((G_KV, m, LANES), BF16), jax.ShapeDtypeStruct((m // tm, LANES, tm), BF16),
                      jax.ShapeDtypeStruct((G_KV, m, LANES), BF16), jax.ShapeDtypeStruct((m // tm, LANES, tm), BF16)]
        out_specs += [aug, vt, aug, vt]
    return pl.pallas_call(
        functools.partial(_inproj_body, tm=tm, seq_len=seq_len, prompt=prompt),
        grid=(m // tm,),
        in_specs=[row(D_MODEL), full(norm_g), full(w), full(indq), full(indk), full(qg), full(kg)],
        out_specs=out_specs, out_shape=out_shape,
        compiler_params=pltpu.CompilerParams(dimension_semantics=("arbitrary",), vmem_limit_bytes=VMEM_LIMIT),
        name="in_proj",
    )(x2d, norm_g, w, indq, indk, qg, kg)


def _outproj_body(x_ref, yc_ref, yg_ref, yn_ref, w_ref, nf_ref, o_ref, *, final):
    x = x_ref[...]
    x = (x + _dot(yc_ref[...], w_ref[0:256, :]) + _dot(yg_ref[...], w_ref[256:512, :])
         + _dot(yn_ref[...].astype(BF16), w_ref[512:1024, :]))
    if final:
        ms = jnp.mean(x * x, axis=-1, keepdims=True)
        x = x * lax.rsqrt(ms + EPS) * nf_ref[...]
    o_ref[...] = x


def _out_proj(x2d, yc, yg, yn, w, norm_f, *, final):
    m = x2d.shape[0]
    tm = min(512, m)
    assert m % tm == 0
    row = lambda n: pl.BlockSpec((tm, n), lambda i: (i, 0))
    full = lambda a: pl.BlockSpec(a.shape, lambda i: (0,) * a.ndim)
    return pl.pallas_call(
        functools.partial(_outproj_body, final=final),
        grid=(m // tm,),
        in_specs=[row(D_MODEL), row(256), row(256), row(512), full(w), full(norm_f)],
        out_specs=row(D_MODEL), out_shape=jax.ShapeDtypeStruct((m, D_MODEL), F32),
        compiler_params=pltpu.CompilerParams(dimension_semantics=("arbitrary",), vmem_limit_bytes=VMEM_LIMIT),
        name="out_proj",
    )(x2d, yc, yg, yn, w, norm_f)


def _conv_body(zc_ref, hist_ref, wdw_ref, bdw_ref, lng_ref, lnb_ref, wpw_ref, bpw_ref, y_ref, st_ref,
               ext_scr, act_scr, *, nb, t_len, tc, tmm):
    n_chunks = t_len // tc

    def per_batch(bi, _):
        base = bi * t_len
        ext_scr[0:32, :] = jnp.zeros((32, W_CONV), F32)
        ext_scr[2:32, :] = hist_ref[bi]

        def chunk(c, _):
            r0 = pl.multiple_of(c * tc, tc)
            g0 = pl.multiple_of(base + r0, tc)
            a = zc_ref[pl.ds(g0, tc), 0:256]
            b = zc_ref[pl.ds(g0, tc), 256:512]
            ext_scr[pl.ds(32 + r0, tc), :] = a * _sigmoid(b)
            win = ext_scr[pl.ds(r0, tc + 32), :]
            acc = jnp.zeros((tc, W_CONV), F32) + bdw_ref[...]
            for r in range(8):
                taps = [j for j in range(r, CONV_WIDTH, 8)]
                wr = win[2 + r:2 + r + tc + 8 * (len(taps) - 1)]
                for mi, j in enumerate(taps):
                    acc = acc + wr[8 * mi:8 * mi + tc] * wdw_ref[j:j + 1, :]
            mu = jnp.mean(acc, axis=-1, keepdims=True)
            d = acc - mu
            var = jnp.mean(d * d, axis=-1, keepdims=True)
            yn = d * lax.rsqrt(var + EPS) * lng_ref[...] + lnb_ref[...]
            act_scr[pl.ds(g0, tc), :] = _silu(yn)
            return 0

        lax.fori_loop(0, n_chunks, chunk, 0)
        st_ref[bi] = ext_scr[2 + t_len:32 + t_len, :]
        return 0

    lax.fori_loop(0, nb, per_batch, 0)

    def mm(c, _):
        r0 = pl.multiple_of(c * tmm, tmm)
        y = _dot(act_scr[pl.ds(r0, tmm), :].astype(BF16), wpw_ref[...]) + bpw_ref[...]
        y_ref[pl.ds(r0, tmm), :] = (y * _silu(zc_ref[pl.ds(r0, tmm), 512:768])).astype(BF16)
        return 0

    lax.fori_loop(0, nb * t_len // tmm, mm, 0)


def _conv(zc, hist, wdw, bdw, lng, lnb, wpw, bpw, *, nb, t_len):
    m = zc.shape[0]
    n_b = m // t_len
    tc = min(64, t_len)
    tmm = min(256, nb * t_len)
    full = lambda a: pl.BlockSpec(a.shape, lambda i: (0,) * a.ndim)
    return pl.pallas_call(
        functools.partial(_conv_body, nb=nb, t_len=t_len, tc=tc, tmm=tmm),
        grid=(n_b // nb,),
        in_specs=[pl.BlockSpec((nb * t_len, 3 * W_CONV), lambda i: (i, 0)),
                  pl.BlockSpec((nb, HIST, W_CONV), lambda i: (i, 0, 0)),
                  full(wdw), full(bdw), full(lng), full(lnb), full(wpw), full(bpw)],
        out_specs=[pl.BlockSpec((nb * t_len, W_CONV), lambda i: (i, 0)),
                   pl.BlockSpec((nb, HIST, W_CONV), lambda i: (i, 0, 0))],
        out_shape=[jax.ShapeDtypeStruct((m, W_CONV), BF16), jax.ShapeDtypeStruct((n_b, HIST, W_CONV), F32)],
        scratch_shapes=[pltpu.VMEM((32 + t_len, W_CONV), F32), pltpu.VMEM((nb * t_len, W_CONV), F32)],
        compiler_params=pltpu.CompilerParams(dimension_semantics=("arbitrary",), vmem_limit_bytes=VMEM_LIMIT),
        name="conv",
    )(zc, hist, wdw, bdw, lng, lnb, wpw, bpw)


def _gla_body(zg_ref, s0_ref, wgk_ref, bgk_ref, ltri_ref, lsum_ref, ind_ref, indv_ref, ng_ref, bd_ref,
              y_ref, st_ref, bc_scr, qe_scr, ke_scr, dec_scr, o_scr, s_scr, *, nb, t_len, chunk, rb):
    rows = nb * t_len
    n_rb = rows // rb
    cpb = t_len // chunk

    def phase1(i, _):
        r0 = pl.multiple_of(i * rb, rb)
        gk = zg_ref[pl.ds(r0, rb), 768:896]
        pre = _dot_split(gk, wgk_ref[0]) + _dot(gk.astype(BF16), wgk_ref[1]) + bgk_ref[...]
        la = (jnp.minimum(pre, 0.0) - jnp.log(1.0 + jnp.exp(-jnp.abs(pre)))) * (1.0 / GATE_NORMALIZER)
        h1, l1 = _split2(la)
        l2 = (la - h1.astype(F32) - l1.astype(F32)).astype(BF16)
        bc = _dot(ltri_ref[...], h1) + _dot(ltri_ref[...], l1) + _dot(ltri_ref[...], l2)
        bt = _dot(lsum_ref[...], h1) + _dot(lsum_ref[...], l1) + _dot(lsum_ref[...], l2)
        bc_scr[pl.ds(r0, rb), :] = bc
        qe_scr[pl.ds(r0, rb), :] = zg_ref[pl.ds(r0, rb), 0:128] * (DK_GLA ** -0.5) * jnp.exp(bc)
        ke_scr[pl.ds(r0, rb), :] = zg_ref[pl.ds(r0, rb), 128:256] * jnp.exp(bt - bc)
        dec_scr[pl.ds(r0, rb), :] = jnp.exp(bt)
        return 0

    lax.fori_loop(0, n_rb, phase1, 0)

    ti = lax.broadcasted_iota(jnp.int32, (chunk, chunk, LANES), 0)
    si = lax.broadcasted_iota(jnp.int32, (chunk, chunk, LANES), 1)
    causal = si <= ti

    def phase2(c, _):
        r0 = pl.multiple_of(c * chunk, chunk)
        bc = bc_scr[pl.ds(r0, chunk), :]
        q = zg_ref[pl.ds(r0, chunk), 0:128] * (DK_GLA ** -0.5)
        k = zg_ref[pl.ds(r0, chunk), 128:256]
        v = zg_ref[pl.ds(r0, chunk), 256:512]
        e = jnp.exp(jnp.where(causal, bc[:, None, :] - bc[None, :, :], NEG))
        p = (q[:, None, :] * k[None, :, :] * e).reshape(chunk * chunk, LANES)
        att = _dot_split(p, ind_ref[...]).reshape(chunk, chunk, W_GLA)
        o_scr[pl.ds(r0, chunk), :] = (att * v[None, :, :]).sum(axis=1)
        return 0

    lax.fori_loop(0, rows // chunk, phase2, 0)

    def per_batch(bi, _):
        s_scr[...] = s0_ref[bi]

        def step(c, _):
            r0 = pl.multiple_of(bi * t_len + c * chunk, chunk)
            s = s_scr[...]
            o_scr[pl.ds(r0, chunk), :] += _dot_nt(qe_scr[pl.ds(r0, chunk), :].astype(BF16), s.astype(BF16))
            upd = _dot_tn(zg_ref[pl.ds(r0, chunk), 256:512].astype(BF16), ke_scr[pl.ds(r0, chunk), :].astype(BF16))
            s_scr[...] = s * dec_scr[pl.ds(r0, 1), :] + upd * bd_ref[...]
            return 0

        lax.fori_loop(0, cpb, step, 0)
        st_ref[bi] = s_scr[...]
        return 0

    lax.fori_loop(0, nb, per_batch, 0)

    def phase4(i, _):
        r0 = pl.multiple_of(i * rb, rb)
        o = o_scr[pl.ds(r0, rb), :]
        o = o * _seg_rms(o, indv_ref, DV_GLA) * ng_ref[...]
        y_ref[pl.ds(r0, rb), :] = (o * _silu(zg_ref[pl.ds(r0, rb), 512:768])).astype(BF16)
        return 0

    lax.fori_loop(0, n_rb, phase4, 0)


def _gla(zg, s0t, wgk, bgk, ng, *, nb, t_len, chunk):
    m = zg.shape[0]
    n_b = m // t_len
    rows = nb * t_len
    rb = min(128, rows)
    assert rows % rb == 0 and rb % chunk == 0 and t_len % chunk == 0
    ltri = jnp.asarray(_blockdiag_ones(rb, chunk) * np.tril(np.ones((rb, rb), np.float32)), BF16)
    lsum = jnp.asarray(_blockdiag_ones(rb, chunk), BF16)
    i128 = np.arange(LANES)
    i256 = np.arange(W_GLA)
    ind = jnp.asarray((i128[:, None] // DK_GLA == i256[None, :] // DV_GLA).astype(np.float32), BF16)
    indv = jnp.asarray(_blockdiag_ones(W_GLA, DV_GLA), BF16)
    bd = jnp.asarray((i256[:, None] // DV_GLA == i128[None, :] // DK_GLA).astype(np.float32))
    full = lambda a: pl.BlockSpec(a.shape, lambda i: (0,) * a.ndim)
    return pl.pallas_call(
        functools.partial(_gla_body, nb=nb, t_len=t_len, chunk=chunk, rb=rb),
        grid=(n_b // nb,),
        in_specs=[pl.BlockSpec((rows, N_GLA), lambda i: (i, 0)),
                  pl.BlockSpec((nb, W_GLA, LANES), lambda i: (i, 0, 0)),
                  full(wgk), full(bgk), full(ltri), full(lsum), full(ind), full(indv), full(ng), full(bd)],
        out_specs=[pl.BlockSpec((rows, W_GLA), lambda i: (i, 0)),
                   pl.BlockSpec((nb, W_GLA, LANES), lambda i: (i, 0, 0))],
        out_shape=[jax.ShapeDtypeStruct((m, W_GLA), BF16), jax.ShapeDtypeStruct((n_b, W_GLA, LANES), F32)],
        scratch_shapes=[pltpu.VMEM((rows, LANES), F32), pltpu.VMEM((rows, LANES), F32),
                        pltpu.VMEM((rows, LANES), F32), pltpu.VMEM((rows, LANES), F32),
                        pltpu.VMEM((rows, W_GLA), F32), pltpu.VMEM((W_GLA, LANES), F32)],
        compiler_params=pltpu.CompilerParams(dimension_semantics=("arbitrary",), vmem_limit_bytes=VMEM_LIMIT),
        name="gla",
    )(zg, s0t, wgk, bgk, ltri, lsum, ind, indv, ng, bd)


KW = CMP_BLOCK * 256


def _compress_rows(x_ref, w_ref, pe_ref, kvc_ref, n_rows):
    kvc_ref[...] = jnp.zeros(kvc_ref.shape, F32)
    for par in range(2):
        x = (x_ref[:, par * KW:(par + 1) * KW] + pe_ref[...]).astype(BF16)
        kvc_ref[ODD_SLOT0 * par:ODD_SLOT0 * par + n_rows, :] = _dot(x, w_ref[...])


def _compress_prompt_body(x_ref, w_ref, pe_ref, kvc_ref, *, n_rows):
    _compress_rows(x_ref, w_ref, pe_ref, kvc_ref, n_rows)


def _compress_prompt(cmp_rows, w, pe, *, n_b, t_len):
    n_rows = t_len // SEL_BLOCK
    full = lambda a: pl.BlockSpec(a.shape, lambda i: (0,) * a.ndim)
    return pl.pallas_call(
        functools.partial(_compress_prompt_body, n_rows=n_rows),
        grid=(n_b,),
        in_specs=[pl.BlockSpec((n_rows, 2 * KW), lambda i: (i, 0)), full(w), full(pe)],
        out_specs=pl.BlockSpec((None, LANES, 256), lambda i: (i, 0, 0)),
        out_shape=jax.ShapeDtypeStruct((n_b, LANES, 256), F32),
        compiler_params=pltpu.CompilerParams(dimension_semantics=("arbitrary",), vmem_limit_bytes=VMEM_LIMIT),
        name="compress_prompt",
    )(cmp_rows.reshape(n_b * n_rows, 2 * KW), w, pe)


def _compress_paged_body(pt_ref, *refs, n_pages, t_new, n_rows):
    pages = refs[:n_pages]
    new_ref, w_ref, pe_ref, kvc_ref, x_scr = refs[n_pages:]
    per_page = PAGE_SIZE // SEL_BLOCK
    for p in range(n_pages):
        x_scr[per_page * p:per_page * (p + 1), :] = pages[p][...]
    n_past = per_page * n_pages
    x_scr[n_past:, :] = jnp.zeros((n_rows - n_past, 2 * KW), F32)
    x_scr[n_past:n_past + 1, 0:t_new * 256] = new_ref[...]
    _compress_rows(x_scr, w_ref, pe_ref, kvc_ref, n_rows)


def _paged_specs(layer, n_pages, block):
    return [pl.BlockSpec((None, None) + block, functools.partial(
        lambda b, pt, p: (layer, pt[b, p], 0, 0), p=p)) for p in range(n_pages)]


def _compress_paged(page_table, cache, layer, cmp_new, w, pe, *, t_new):
    n_b, n_pages = page_table.shape
    per_page = PAGE_SIZE // SEL_BLOCK
    n_rows = -(-(per_page * n_pages + 1) // 8) * 8
    cache = cache.reshape(cache.shape[:2] + (per_page, 2 * KW))
    full = lambda a: pl.BlockSpec(a.shape, lambda b, pt: (0,) * a.ndim)
    grid_spec = pltpu.PrefetchScalarGridSpec(
        num_scalar_prefetch=1, grid=(n_b,),
        in_specs=_paged_specs(layer, n_pages, (per_page, 2 * KW)) + [
            pl.BlockSpec((None, 1, t_new * 256), lambda b, pt: (b, 0, 0)), full(w), full(pe)],
        out_specs=pl.BlockSpec((None, LANES, 256), lambda b, pt: (b, 0, 0)),
        scratch_shapes=[pltpu.VMEM((n_rows, 2 * KW), F32)])
    return pl.pallas_call(
        functools.partial(_compress_paged_body, n_pages=n_pages, t_new=t_new, n_rows=n_rows),
        grid_spec=grid_spec, out_shape=jax.ShapeDtypeStruct((n_b, LANES, 256), F32),
        compiler_params=pltpu.CompilerParams(dimension_semantics=("arbitrary",), vmem_limit_bytes=VMEM_LIMIT),
        name="compress_paged",
    )(page_table, *([cache] * n_pages), cmp_new.reshape(n_b, 1, t_new * 256), w, pe)


def _cmpsel_body(qn_ref, kvc_ref, bias_ref, sbase_ref, useimp_ref, indk_ref, kg_ref, cvec_ref,
                 *outs, nb, tb, n_blocks, q_t):
    q_refs, ocmp_ref = outs[:-1], outs[-1]
    rows = nb * tb
    lane = _lane((rows, LANES))
    qn = qn_ref[...].astype(F32)
    qpad = [_half_pad(qn[:, 128 * (h // 2):128 * (h // 2) + 128], h % 2 == 1) for h in range(H_NSA)]
    imp_parts = []
    o_parts = [[None] * nb for _ in range(H_NSA)]
    for bi in range(nb):
        kvc = kvc_ref[bi]
        kc = kvc[:, 0:128]
        kc = kc * _seg_rms(kc, indk_ref, HD) * kg_ref[...]
        vboth = kvc[:, 128:256].astype(BF16)
        imp_b = []
        for g in range(G_KV):
            kpad = _half_pad(kc, g == 1).astype(BF16)
            qs = jnp.concatenate([qpad[HPG * g + j][bi * tb:(bi + 1) * tb] for j in range(HPG)], axis=0).astype(BF16)
            bias = bias_ref[g]
            l = _dot_nt(qs, kpad) + bias
            vis = bias > 0.5 * NEG
            mx = jnp.max(l, axis=-1, keepdims=True)
            p = jnp.where(vis, jnp.exp(l - mx), 0.0)
            p = p / jnp.maximum(jnp.sum(p, axis=-1, keepdims=True), 1e-30)
            o = _dot(p.astype(BF16), vboth)
            ig = p[0:tb]
            for j in range(1, HPG):
                ig = ig + p[j * tb:(j + 1) * tb]
            imp_b.append(ig)
            for j in range(HPG):
                o_parts[HPG * g + j][bi] = o[j * tb:(j + 1) * tb]
        imp_parts.append(imp_b)
    for j in range(HPG):
        o0 = jnp.concatenate(o_parts[j], axis=0) if nb > 1 else o_parts[j][0]
        o1 = jnp.concatenate(o_parts[HPG + j], axis=0) if nb > 1 else o_parts[HPG + j][0]
        ocmp_ref[:, 128 * j:128 * j + 128] = jnp.where(lane < 64, o0, o1)
    for g in range(G_KV):
        ig = jnp.concatenate([imp_parts[bi][g] for bi in range(nb)], axis=0) if nb > 1 else imp_parts[0][g]
        imp = ig + pltpu.roll(ig, LANES - ODD_SLOT0, axis=1)
        score = jnp.where(useimp_ref[...] > 0.5, imp, sbase_ref[...])
        st = score.T[0:SEL_SLOTS]
        jj = lax.broadcasted_iota(jnp.int32, (SEL_SLOTS, rows), 0)
        rank = jnp.zeros((SEL_SLOTS, rows), F32)
        for jp in range(n_blocks):
            r = st[jp:jp + 1, :]
            before = (r > st) | ((r == st) & (jp < jj))
            rank = rank + before.astype(F32)
        selb = jnp.where(rank < TOP_N - 0.5, 0.0, NEG)
        full_t = jnp.concatenate([jnp.zeros((SEL_LANE0, rows), F32), selb,
                                  jnp.zeros((LANES - SEL_LANE0 - SEL_SLOTS, rows), F32)], axis=0)
        if q_t:
            for j in range(HPG):
                h = HPG * g + j
                q_refs[j][g] = ((qpad[h] + cvec_ref[h]).T + full_t).astype(BF16)
        else:
            extra = full_t.T
            for j in range(HPG):
                h = HPG * g + j
                q_refs[0][g, j] = (qpad[h] + extra + cvec_ref[h]).astype(q_refs[0].dtype)


def _cmpsel(qn, kvc, bias, sbase, useimp, indk, kg0, cvec, *, nb, tb, n_blocks, q_t):
    m = qn.shape[0]
    rows = nb * tb
    assert rows == LANES and m % rows == 0
    n_pos = bias.shape[0]
    full = lambda a: pl.BlockSpec(a.shape, lambda i: (0,) * a.ndim)
    if q_t:
        q_specs = [pl.BlockSpec((G_KV, LANES, rows), lambda i: (0, 0, i))] * HPG
        q_shapes = [jax.ShapeDtypeStruct((G_KV, LANES, m), BF16)] * HPG
    else:
        q_specs = [pl.BlockSpec((G_KV, HPG, rows, LANES), lambda i: (0, 0, i, 0))]
        q_shapes = [jax.ShapeDtypeStruct((G_KV, HPG, m, LANES), F32)]
    return pl.pallas_call(
        functools.partial(_cmpsel_body, nb=nb, tb=tb, n_blocks=n_blocks, q_t=q_t),
        grid=(m // rows,),
        in_specs=[pl.BlockSpec((rows, W_NSA), lambda i: (i, 0)),
                  pl.BlockSpec((nb, LANES, 256), (lambda i: (i, 0, 0)) if nb > 1 else (lambda i: (i // n_pos, 0, 0))),
                  pl.BlockSpec((None, G_KV, HPG * tb, LANES), lambda i: (i % n_pos, 0, 0, 0)),
                  pl.BlockSpec((None, rows, LANES), lambda i: (i % n_pos, 0, 0)),
                  pl.BlockSpec((None, rows, LANES), lambda i: (i % n_pos, 0, 0)),
                  full(indk), full(kg0), full(cvec)],
        out_specs=q_specs + [pl.BlockSpec((rows, W_NSA), lambda i: (i, 0))],
        out_shape=q_shapes + [jax.ShapeDtypeStruct((m, W_NSA), F32)],
        compiler_params=pltpu.CompilerParams(dimension_semantics=("arbitrary",), vmem_limit_bytes=VMEM_LIMIT),
        name="cmpsel",
    )(qn, kvc, bias, sbase, useimp, indk, kg0, cvec)


def _mix_gates(g_ref, bg_ref, expand_ref, gate_ref, ocmp_ref, o_slc, o_win, y_ref):
    gl = _sigmoid(g_ref[...] + bg_ref[...])
    gx = _dot_split(gl, expand_ref[...])
    for j in range(HPG):
        sl = slice(128 * j, 128 * j + 128)
        o = (gx[:, 128 * j:128 * j + 128] * ocmp_ref[:, sl]
             + gx[:, 128 * (HPG + j):128 * (HPG + j) + 128] * o_slc[j]
             + gx[:, 128 * (2 * HPG + j):128 * (2 * HPG + j) + 128] * o_win[j])
        y_ref[:, sl] = (o * _silu(gate_ref[:, sl])).astype(y_ref.dtype)


def _softmax_step(q_t, k, v_t, bias_t, m_scr, l_scr, acc_scr):
    s = _dot(k, q_t)
    if bias_t is not None:
        s = s + bias_t
    m_prev = m_scr[...]
    m_new = jnp.maximum(m_prev, jnp.max(s, axis=0, keepdims=True))
    alpha = jnp.exp(m_prev - m_new)
    p = jnp.exp(s - m_new)
    l_scr[...] = alpha * l_scr[...] + jnp.sum(p, axis=0, keepdims=True)
    acc_scr[...] = alpha * acc_scr[...] + _dot(v_t, p.astype(BF16))
    m_scr[...] = m_new


def _flash_body(q0_ref, q1_ref, q2_ref, q3_ref, ksa_ref, vs_ref, kwa_ref, vw_ref, dt_ref, edge_ref, ocmp_ref,
                g_ref, gate_ref, bg_ref, expand_ref, y_ref, m_scr, l_scr, acc_scr, res_scr):
    i = pl.program_id(1)
    cols = HPG * TQ
    near = jnp.maximum(i - 1, 0)
    edge = jnp.maximum(i - 2, 0)

    def reset():
        m_scr[...] = jnp.full((1, cols), NEG, F32)
        l_scr[...] = jnp.zeros((1, cols), F32)
        acc_scr[...] = jnp.zeros((LANES, cols), F32)

    def step(q_t, ka_ref, v_ref, g, kt, bias_t):
        k0 = pl.multiple_of(kt * TQ, TQ)
        _softmax_step(q_t, ka_ref[g, pl.ds(k0, TQ), :], v_ref[kt], bias_t, m_scr, l_scr, acc_scr)

    for g in range(G_KV):
        q_t = jnp.concatenate([r[g] for r in (q0_ref, q1_ref, q2_ref, q3_ref)], axis=1)
        for br, (ka_ref, v_ref) in enumerate(((ksa_ref, vs_ref), (kwa_ref, vw_ref))):
            reset()
            if br == 0:
                def far(kt, _):
                    step(q_t, ka_ref, v_ref, g, kt, None)
                    return 0

                lax.fori_loop(0, near, far, 0)
            else:
                @pl.when(i >= 2)
                def _():
                    step(q_t, ka_ref, v_ref, g, edge, edge_ref[...])

            @pl.when(i >= 1)
            def _():
                step(q_t, ka_ref, v_ref, g, near, dt_ref[1, g])

            step(q_t, ka_ref, v_ref, g, i, dt_ref[0, g])
            res_scr[br, g] = acc_scr[...] / l_scr[...]

    outs = []
    for br in range(2):
        o_t = jnp.concatenate([res_scr[br, 0, 0:64, :], res_scr[br, 1, 64:128, :]], axis=0)
        outs.append([o_t[:, j * TQ:(j + 1) * TQ].T for j in range(HPG)])
    _mix_gates(g_ref, bg_ref, expand_ref, gate_ref, ocmp_ref, outs[0], outs[1], y_ref)


def _flash(q_ts, ksa, vs, kwa, vw, dt, edge, ocmp, ng, ngate, bg, expand, *, n_b, t_len):
    m = n_b * t_len
    nq = t_len // TQ
    cols = HPG * TQ
    full = lambda a: pl.BlockSpec(a.shape, lambda b, i: (0,) * a.ndim)
    qspec = pl.BlockSpec((G_KV, LANES, TQ), lambda b, i: (0, 0, b * nq + i))
    kspec = pl.BlockSpec((G_KV, None, t_len, LANES), lambda b, i: (0, b, 0, 0))
    vspec = pl.BlockSpec((nq, LANES, TQ), lambda b, i: (b, 0, 0))
    row = lambda n: pl.BlockSpec((TQ, n), lambda b, i: (b * nq + i, 0))
    return pl.pallas_call(
        _flash_body,
        grid=(n_b, nq),
        in_specs=[qspec] * HPG + [kspec, vspec, kspec, vspec, full(dt), full(edge), row(W_NSA), row(LANES),
                                  row(W_NSA), full(bg), full(expand)],
        out_specs=row(W_NSA), out_shape=jax.ShapeDtypeStruct((m, W_NSA), BF16),
        scratch_shapes=[pltpu.VMEM((1, cols), F32), pltpu.VMEM((1, cols), F32),
                        pltpu.VMEM((LANES, cols), F32), pltpu.VMEM((2, G_KV, LANES, cols), F32)],
        compiler_params=pltpu.CompilerParams(dimension_semantics=("arbitrary", "arbitrary"),
                                             vmem_limit_bytes=VMEM_LIMIT),
        name="flash",
    )(*q_ts, ksa.reshape(G_KV, n_b, t_len, LANES), vs, kwa.reshape(G_KV, n_b, t_len, LANES), vw,
      dt, edge, ocmp, ng, ngate, bg, expand)


def _dec_attn_body(pt_ref, *refs, n_pages, t_new, n_keys, n_wkeys):
    pages = refs[:n_pages]
    (qa_ref, slc_ref, win_ref, wst_ref, dsl_ref, dwn_ref, ocmp_ref, g_ref, gate_ref, bg_ref, expand_ref,
     y_ref, wout_ref, ka_scr, va_scr, kw_scr, vw_scr) = refs[n_pages:]
    past = n_pages * PAGE_SIZE
    w_past = wst_ref.shape[0]
    rows = HPG * t_new

    def aug(k, g, blk):
        lane = _lane(k.shape)
        extra = ((lane == CONST_LANE) | (lane == CONST_LANE + 1)).astype(F32)
        if blk is not None:
            extra = extra + (lane == SEL_LANE0 + blk).astype(F32)
        return (_half_pad(k, g == 1) + extra).astype(BF16)

    for p in range(n_pages):
        pg = pages[p][...]
        for g in range(G_KV):
            ka_scr[g, p * PAGE_SIZE:(p + 1) * PAGE_SIZE, :] = aug(pg[:, 0:128], g, (p * PAGE_SIZE) // SEL_BLOCK)
            ka_scr[g, p * PAGE_SIZE + SEL_BLOCK:(p + 1) * PAGE_SIZE, :] = aug(
                pg[SEL_BLOCK:, 0:128], g, (p * PAGE_SIZE) // SEL_BLOCK + 1)
        va_scr[p * PAGE_SIZE:(p + 1) * PAGE_SIZE, :] = pg[:, 128:256].astype(BF16)
    tail = n_keys - past
    new = slc_ref[...]
    va_scr[past:n_keys, :] = jnp.zeros((tail, LANES), BF16)
    va_scr[past:past + t_new, :] = new[:, 128:256].astype(BF16)
    for g in range(G_KV):
        ka_scr[g, past:n_keys, :] = jnp.zeros((tail, LANES), BF16)
        ka_scr[g, past:past + t_new, :] = aug(new[:, 0:128], g, past // SEL_BLOCK)
    wst = wst_ref[...]
    wnew = win_ref[...]
    wtail = n_wkeys - w_past
    vw_scr[0:w_past, :] = wst[:, 128:256].astype(BF16)
    vw_scr[w_past:n_wkeys, :] = jnp.zeros((wtail, LANES), BF16)
    vw_scr[w_past:w_past + t_new, :] = wnew[:, 128:256].astype(BF16)
    for g in range(G_KV):
        kw_scr[g, 0:w_past, :] = aug(wst[:, 0:128], g, None)
        kw_scr[g, w_past:n_wkeys, :] = jnp.zeros((wtail, LANES), BF16)
        kw_scr[g, w_past:w_past + t_new, :] = aug(wnew[:, 0:128], g, None)
    wout_ref[0:w_past - t_new, :] = wst[t_new:, :]
    wout_ref[w_past - t_new:w_past, :] = wnew

    res = [[None, None], [None, None]]
    for g in range(G_KV):
        q = qa_ref[g].reshape(rows, LANES).astype(BF16)
        for br, (k_scr, v_scr, b_ref) in enumerate(((ka_scr, va_scr, dsl_ref), (kw_scr, vw_scr, dwn_ref))):
            s = _dot_nt(q, k_scr[g]) + b_ref[g]
            mx = jnp.max(s, axis=-1, keepdims=True)
            p = jnp.exp(s - mx)
            den = jnp.sum(p, axis=-1, keepdims=True)
            res[br][g] = _dot(p.astype(BF16), v_scr[...]) / den
    lane = _lane((t_new, LANES))
    outs = [[jnp.where(lane < 64, res[br][0][j * t_new:(j + 1) * t_new], res[br][1][j * t_new:(j + 1) * t_new])
             for j in range(HPG)] for br in range(2)]
    _mix_gates(g_ref, bg_ref, expand_ref, gate_ref, ocmp_ref, outs[0], outs[1], y_ref)


def _dec_attn(page_table, cache, layer, qa, slc_new, win_new, win_state, dsl, dwn, ocmp, ng, ngate, bg, expand, *, t_new):
    n_b, n_pages = page_table.shape
    m = n_b * t_new
    w_past = win_state.shape[1]
    n_keys = dsl.shape[-1]
    n_wkeys = dwn.shape[-1]
    full = lambda a: pl.BlockSpec(a.shape, lambda b, pt: (0,) * a.ndim)
    row = lambda n: pl.BlockSpec((t_new, n), lambda b, pt: (b, 0))
    grid_spec = pltpu.PrefetchScalarGridSpec(
        num_scalar_prefetch=1, grid=(n_b,),
        in_specs=_paged_specs(layer, n_pages, (PAGE_SIZE, 256)) + [
            pl.BlockSpec((G_KV, HPG, t_new, LANES), lambda b, pt: (0, 0, b, 0)),
            row(256), row(256), pl.BlockSpec((None, w_past, 256), lambda b, pt: (b, 0, 0)),
            full(dsl), full(dwn), row(W_NSA), row(LANES), row(W_NSA), full(bg), full(expand)],
        out_specs=[row(W_NSA), pl.BlockSpec((None, w_past, 256), lambda b, pt: (b, 0, 0))],
        scratch_shapes=[pltpu.VMEM((G_KV, n_keys, LANES), BF16), pltpu.VMEM((n_keys, LANES), BF16),
                        pltpu.VMEM((G_KV, n_wkeys, LANES), BF16), pltpu.VMEM((n_wkeys, LANES), BF16)])
    return pl.pallas_call(
        functools.partial(_dec_attn_body, n_pages=n_pages, t_new=t_new, n_keys=n_keys, n_wkeys=n_wkeys),
        grid_spec=grid_spec,
        out_shape=[jax.ShapeDtypeStruct((m, W_NSA), F32), jax.ShapeDtypeStruct((n_b, w_past, 256), F32)],
        compiler_params=pltpu.CompilerParams(dimension_semantics=("arbitrary",), vmem_limit_bytes=VMEM_LIMIT),
        name="dec_attn",
    )(page_table, *([cache] * n_pages), qa, slc_new, win_new, win_state, dsl, dwn, ocmp, ng, ngate, bg, expand)


def _bucket_matrix(qpos, kpos, valid):
    dist = qpos[:, None] - kpos[None, :]
    return np.where(valid & (dist >= 0), _rel_bucket_np(dist), -1).astype(np.int32)


def _bias_body(tab_ref, bk_ref, o_ref, *, sub_far):
    bk = bk_ref[...]
    for h in range(H_NSA):
        far = tab_ref[N_BUCKETS - 1, h] if sub_far else 0.0
        acc = jnp.full(bk.shape, NEG, F32)
        for b in range(N_BUCKETS):
            acc = jnp.where(bk == b, tab_ref[b, h] - far, acc)
        o_ref[h] = acc


def _bias_table(table, bk, *, sub_far):
    r, c = bk.shape
    tr = 8 if r <= 8 else 64
    assert r % tr == 0
    return pl.pallas_call(
        functools.partial(_bias_body, sub_far=sub_far),
        grid=(r // tr,),
        in_specs=[pl.BlockSpec(memory_space=pltpu.SMEM), pl.BlockSpec((tr, c), lambda i: (i, 0))],
        out_specs=pl.BlockSpec((H_NSA, tr, c), lambda i: (0, i, 0)),
        out_shape=jax.ShapeDtypeStruct((H_NSA, r, c), F32),
        compiler_params=pltpu.CompilerParams(dimension_semantics=("arbitrary",)),
        name="bias_table",
    )(table, jnp.asarray(bk))


def _cmp_tables(qpos_blocks, n_blocks):
    slot = np.arange(LANES)
    blk = np.where(slot < ODD_SLOT0, 2 * slot, 2 * (slot - ODD_SLOT0) + 1)
    real = (blk < 2 * n_blocks) & ((slot % ODD_SLOT0) < n_blocks)
    blk_end = blk * CMP_BLOCK + CMP_BLOCK - 1
    buckets, sbases, useimps = [], [], []
    for qpos in qpos_blocks:
        buckets.append(_bucket_matrix(qpos, blk_end, real[None, :]))
        j = np.arange(LANES)[None, :]
        cur = (qpos // SEL_BLOCK)[:, None]
        forced = (j == 0) | (j == cur) | (j == cur - 1)
        valid = j * SEL_BLOCK <= qpos[:, None]
        inrange = j < n_blocks
        sbases.append(np.where(~inrange, -2.0, np.where(forced, FORCE, -1.0)).astype(np.float32))
        useimps.append((inrange & ~forced & valid).astype(np.float32))
    return np.concatenate(buckets), jnp.asarray(np.stack(sbases)), jnp.asarray(np.stack(useimps))


def _head_rows(b, t):
    k = b.shape[-1]
    n = b.shape[1] // t
    return b.reshape(G_KV, HPG, n, t, k).transpose(2, 0, 1, 3, 4).reshape(n, G_KV, HPG * t, k)


def _prep_weights(rel_bias, w_in, conv_w_pw, gla_w_gk, nsa_q_norm_g, nsa_k_norm_g, nsa_pe_cmp, nsa_w_cmp,
                  nsa_b_gate, w_out):
    depth = w_in.shape[0]
    cols = _in_proj_columns()
    w_in_p = jnp.where(jnp.asarray(cols >= 0)[None, None, :],
                       jnp.take(w_in, jnp.asarray(np.maximum(cols, 0)), axis=2), 0.0).astype(BF16)
    perm = np.concatenate([np.arange(512), 512 + _nsa_perm()])
    w_out_p = jnp.take(w_out, jnp.asarray(perm), axis=1).astype(BF16)
    qg = jnp.tile(nsa_q_norm_g, (1, H_NSA))[:, None, :] * SCALE
    kg = jnp.tile(nsa_k_norm_g, (1, 1, G_KV))
    wgk = jnp.zeros((depth, LANES, LANES), F32).at[:, :GATE_RANK, :].set(gla_w_gk)
    wgk_hi = wgk.astype(BF16)
    wgk2 = jnp.stack([wgk_hi, (wgk - wgk_hi.astype(F32)).astype(BF16)], axis=1)
    wbd = jnp.zeros((depth, CMP_BLOCK, 256, 256), F32)
    for s in range(2):
        for g in range(G_KV):
            o = 128 * s + 64 * g
            wbd = wbd.at[:, :, o:o + 64, o:o + 64].set(nsa_w_cmp[:, s])
    wbd = wbd.reshape(depth, KW, 256)
    pe = jnp.tile(nsa_pe_cmp.transpose(0, 2, 1, 3)[:, :, :, None, :], (1, 1, 1, G_KV, 1)).reshape(depth, 1, KW)
    bg = jnp.zeros((depth, 1, LANES), F32).at[:, 0, :3 * H_NSA].set(nsa_b_gate)
    expand = np.zeros((LANES, 3 * HPG * LANES), np.float32)
    for br in range(3):
        for g in range(G_KV):
            for j in range(HPG):
                c = LANES * (HPG * br + j) + 64 * g
                expand[H_NSA * br + HPG * g + j, c:c + 64] = 1.0
    far = rel_bias[N_BUCKETS - 1]
    far_hi = far.astype(BF16).astype(F32)
    cvec = jnp.zeros((H_NSA, 1, LANES), F32).at[:, 0, CONST_LANE].set(far_hi).at[:, 0, CONST_LANE + 1].set(far - far_hi)
    return dict(w_in=w_in_p, w_out=w_out_p, qg=qg, kg=kg, wgk=wgk2, wbd=wbd.astype(BF16), pe=pe, bg=bg,
                expand=jnp.asarray(expand, BF16), cvec=cvec, wpw=conv_w_pw.astype(BF16),
                indq=jnp.asarray(_blockdiag_ones(W_NSA, HD), BF16), indk=jnp.asarray(_blockdiag_ones(LANES, HD), BF16))


def kernel(x_prompt, x_sample, cache_cmp_kv, cache_slc_kv, page_table, state_win_kv, state_gla, state_conv, rel_bias, norm_g, w_in, conv_w_dw, conv_b_dw, conv_ln_g, conv_ln_b, conv_w_pw, conv_b_pw, gla_w_gk, gla_b_gk, gla_norm_g, nsa_q_norm_g, nsa_k_norm_g, nsa_pe_cmp, nsa_w_cmp, nsa_b_gate, w_out, norm_f):
    depth = w_in.shape[0]
    n_bp, t_p, _ = x_prompt.shape
    n_bs, t_s, _ = x_sample.shape
    n_pages = page_table.shape[1]
    past = n_pages * PAGE_SIZE
    w_past = state_win_kv.shape[2]
    assert t_p % TQ == 0 and WINDOW == 2 * TQ and t_s == 8 and w_past == WINDOW and past % SEL_BLOCK == 0
    wp = _prep_weights(rel_bias, w_in, conv_w_pw, gla_w_gk, nsa_q_norm_g, nsa_k_norm_g, nsa_pe_cmp, nsa_w_cmp,
                       nsa_b_gate, w_out)
    table = rel_bias
    nblk_p = t_p // SEL_BLOCK
    nblk_s = (past + SEL_BLOCK) // SEL_BLOCK
    pos_p = [np.arange(i * LANES, (i + 1) * LANES) for i in range(t_p // LANES)]
    bk_p, sbase_p, useimp_p = _cmp_tables(pos_p, nblk_p)
    bias_p = _head_rows(_bias_table(table, bk_p, sub_far=False), LANES)
    nb_s = LANES // t_s
    qpos_s = past + np.arange(t_s)
    bk_s, _, _ = _cmp_tables([qpos_s], nblk_s)
    bias_s = _head_rows(_bias_table(table, bk_s, sub_far=False), t_s)
    _, sbase_1, useimp_1 = _cmp_tables([np.tile(qpos_s, nb_s)], nblk_s)
    tq = np.arange(TQ)
    every = np.ones((1, 1), bool)
    bk_dt = np.concatenate([_bucket_matrix(tq, tq, every).T, _bucket_matrix(TQ + tq, tq, every).T])
    dt = _bias_table(table, bk_dt, sub_far=True)
    dt = dt.reshape(G_KV, HPG, 2, TQ, TQ).transpose(2, 0, 3, 1, 4).reshape(2, G_KV, TQ, HPG * TQ)
    edge = jnp.asarray(np.tile(np.where(tq[:, None] > tq[None, :], 0.0, NEG).astype(np.float32), (1, HPG)))
    n_keys = past + LANES
    kpos = np.arange(n_keys)
    dsl = _head_rows(_bias_table(table, _bucket_matrix(qpos_s, kpos, (kpos < past + t_s)[None, :]),
                                 sub_far=True), t_s)[0]
    n_wkeys = w_past + LANES
    wk = np.arange(n_wkeys)
    wpos = past - w_past + wk
    wvalid = (wk < w_past + t_s)[None, :] & ((qpos_s[:, None] - wpos[None, :]) < WINDOW) & (wpos >= 0)[None, :]
    dwn = _head_rows(_bias_table(table, _bucket_matrix(qpos_s, wpos, wvalid), sub_far=True), t_s)[0]

    cache_cmp = cache_cmp_kv.reshape(cache_cmp_kv.shape[:3] + (256,))
    cache_slc = cache_slc_kv.reshape(cache_slc_kv.shape[:3] + (256,))
    win_state = state_win_kv.reshape(depth, n_bs, w_past, 256)
    s0 = state_gla.transpose(0, 1, 2, 4, 3).reshape(depth, n_bs, W_GLA, DK_GLA)
    i256 = np.arange(W_GLA)
    i128 = np.arange(LANES)
    bdmask = jnp.asarray((i256[:, None] // DV_GLA == i128[None, :] // DK_GLA).astype(np.float32))
    s0 = jnp.tile(s0, (1, 1, 1, H_GLA)) * bdmask
    zero_s0 = jnp.zeros((n_bp, W_GLA, LANES), F32)
    zero_hist = jnp.zeros((n_bp, HIST, W_CONV), F32)

    def unpack_state(st):
        b = st.shape[0]
        blocks = [st[:, DV_GLA * h:DV_GLA * (h + 1), DK_GLA * h:DK_GLA * (h + 1)] for h in range(H_GLA)]
        return jnp.stack(blocks, axis=1).transpose(0, 1, 3, 2)

    xp = x_prompt.reshape(n_bp * t_p, D_MODEL)
    xs = x_sample.reshape(n_bs * t_s, D_MODEL)
    outs_p, outs_s = [], []
    for l in range(depth):
        final = l == depth - 1
        row = lambda a: a[l][None, :]
        common = dict(wdw=conv_w_dw[l], bdw=row(conv_b_dw), lng=row(conv_ln_g), lnb=row(conv_ln_b), wpw=wp["wpw"][l],
                      bpw=row(conv_b_pw))
        bgk = jnp.zeros((1, LANES), F32).at[0, :].set(gla_b_gk[l])
        gng = jnp.tile(gla_norm_g[l], (H_GLA,))[None, :]
        kg12 = wp["kg"][l, 1:3]
        kg0 = wp["kg"][l, 0:1]
        zc, zg, qn, cmp_n, slc_n, win_n, ngate, ng, ksa, vs, kwa, vw = _in_proj(
            xp, row(norm_g), wp["w_in"][l], wp["indq"], wp["indk"], wp["qg"][l], kg12, seq_len=t_p, prompt=True)
        yc, conv_st = _conv(zc, zero_hist, nb=1, t_len=t_p, **common)
        yg, gla_st = _gla(zg, zero_s0, wp["wgk"][l], bgk, gng, nb=1, t_len=t_p, chunk=16)
        kvc = _compress_prompt(cmp_n, wp["wbd"][l], wp["pe"][l], n_b=n_bp, t_len=t_p)
        *q_ts, ocmp = _cmpsel(qn, kvc, bias_p, sbase_p, useimp_p, wp["indk"], kg0, wp["cvec"], nb=1, tb=LANES,
                              n_blocks=nblk_p, q_t=True)
        yn = _flash(q_ts, ksa, vs, kwa, vw, dt, edge, ocmp, ng, ngate, wp["bg"][l], wp["expand"], n_b=n_bp, t_len=t_p)
        xp = _out_proj(xp, yc, yg, yn, wp["w_out"][l], norm_f[None, :], final=final)
        w_keep = min(WINDOW, t_p)
        outs_p.append((conv_st, unpack_state(gla_st),
                       win_n.reshape(n_bp, t_p, 256)[:, t_p - w_keep:], cmp_n, slc_n))
        zc, zg, qn, cmp_n, slc_n, win_n, ngate, ng = _in_proj(
            xs, row(norm_g), wp["w_in"][l], wp["indq"], wp["indk"], wp["qg"][l], kg12, seq_len=t_s, prompt=False)
        yc, conv_st = _conv(zc, state_conv[l], nb=nb_s, t_len=t_s, **common)
        yg, gla_st = _gla(zg, s0[l], wp["wgk"][l], bgk, gng, nb=nb_s, t_len=t_s, chunk=t_s)
        kvc = _compress_paged(page_table, cache_cmp, l, cmp_n, wp["wbd"][l], wp["pe"][l], t_new=t_s)
        qa, ocmp = _cmpsel(qn, kvc, bias_s, sbase_1, useimp_1, wp["indk"], kg0, wp["cvec"], nb=nb_s, tb=t_s,
                           n_blocks=nblk_s, q_t=False)
        yn, win_st = _dec_attn(page_table, cache_slc, l, qa, slc_n, win_n, win_state[l], dsl, dwn, ocmp, ng, ngate,
                               wp["bg"][l], wp["expand"], t_new=t_s)
        xs = _out_proj(xs, yc, yg, yn, wp["w_out"][l], norm_f[None, :], final=final)
        outs_s.append((conv_st, unpack_state(gla_st), win_st, cmp_n, slc_n))

    def stack(outs, k, shape):
        return jnp.stack([o[k] for o in outs]).reshape(shape)

    kv = (2, G_KV, HD)
    return (xp.reshape(n_bp, t_p, D_MODEL), xs.reshape(n_bs, t_s, D_MODEL),
            stack(outs_p, 0, (depth, n_bp, HIST, W_CONV)), stack(outs_s, 0, (depth, n_bs, HIST, W_CONV)),
            stack(outs_p, 1, (depth, n_bp, H_GLA, DK_GLA, DV_GLA)), stack(outs_s, 1, (depth, n_bs, H_GLA, DK_GLA, DV_GLA)),
            stack(outs_p, 2, (depth, n_bp, min(WINDOW, t_p)) + kv), stack(outs_s, 2, (depth, n_bs, w_past) + kv),
            stack(outs_p, 3, (depth, n_bp, t_p) + kv), stack(outs_s, 3, (depth, n_bs, t_s) + kv),
            stack(outs_p, 4, (depth, n_bp, t_p) + kv), stack(outs_s, 4, (depth, n_bs, t_s) + kv))
```

```python
import functools
import math

import numpy as np
import jax
import jax.numpy as jnp
from jax import lax
from jax.experimental import pallas as pl
from jax.experimental.pallas import tpu as pltpu

F32 = jnp.float32
BF16 = jnp.bfloat16

D_MODEL = 1024
W_CONV = 256
CONV_WIDTH = 31
HIST = CONV_WIDTH - 1
H_GLA = 4
DK_GLA = 32
DV_GLA = 64
W_GLA = H_GLA * DV_GLA
GATE_RANK = 16
GATE_NORMALIZER = 16.0
H_NSA = 8
HD = 64
G_KV = 2
HPG = H_NSA // G_KV
W_NSA = H_NSA * HD
CMP_BLOCK = 32
SEL_BLOCK = 64
TOP_N = 8
WINDOW = 512
N_BUCKETS = 32
MAX_EXACT = N_BUCKETS // 2
MAX_DISTANCE = 128
PAGE_SIZE = 128
SCALE = HD ** -0.5
EPS = 1e-6
NEG = -1e30
FORCE = 1e4

LANES = 128
SEL_LANE0 = 64
SEL_SLOTS = 40
CONST_LANE = 112
ODD_SLOT0 = 64
TQ = 256
VMEM_LIMIT = 56 * 1024 * 1024

C_CONV = 0
C_GLA = 3 * W_CONV
N_GLA = 2 * H_GLA * DK_GLA + 2 * W_GLA + LANES
C_NSA = C_GLA + N_GLA
N_NSA = W_NSA + 3 * 256 + W_NSA + LANES
N_IN = C_NSA + N_NSA


def _rel_bucket_np(dist):
    n = np.maximum(dist, 0)
    nf = np.maximum(n, 1).astype(np.float32)
    large = MAX_EXACT + (np.log(nf / np.float32(MAX_EXACT)) / np.float32(math.log(MAX_DISTANCE / MAX_EXACT))
                         * np.float32(N_BUCKETS - MAX_EXACT)).astype(np.int32)
    large = np.minimum(large, N_BUCKETS - 1)
    return np.where(n < MAX_EXACT, n, large).astype(np.int32)


def _in_proj_columns():
    o = {}
    off = 0
    for name, w in (("c_a", 256), ("c_b", 256), ("c_gate", 256), ("l_q", 128), ("l_k", 128), ("l_v", 256),
                    ("l_gk", 16), ("l_gate", 256), ("n_q", 512), ("n_cmp", 256), ("n_slc", 256),
                    ("n_win", 256), ("n_g", 24), ("n_gate", 512)):
        o[name] = off
        off += w
    cols = -np.ones((N_IN,), np.int64)

    def put(dst, name, width):
        cols[dst:dst + width] = o[name] + np.arange(width)

    put(0, "c_a", 256); put(256, "c_b", 256); put(512, "c_gate", 256)
    g = C_GLA
    put(g, "l_q", 128); put(g + 128, "l_k", 128); put(g + 256, "l_v", 256); put(g + 512, "l_gate", 256)
    put(g + 768, "l_gk", 16)
    n = C_NSA
    put(n, "n_q", 512); put(n + 512, "n_cmp", 256); put(n + 768, "n_slc", 256); put(n + 1024, "n_win", 256)
    cols[n + 1280:n + 1792] = o["n_gate"] + _nsa_perm()
    put(n + 1792, "n_g", 24)
    return cols


def _nsa_perm():
    p = np.zeros((W_NSA,), np.int64)
    for j in range(HPG):
        for g in range(G_KV):
            p[128 * j + 64 * g:128 * j + 64 * g + 64] = 64 * (HPG * g + j) + np.arange(64)
    return p


def _blockdiag_ones(n, blk):
    i = np.arange(n)
    return (i[:, None] // blk == i[None, :] // blk).astype(np.float32)


def _split2(x):
    hi = x.astype(BF16)
    lo = (x - hi.astype(F32)).astype(BF16)
    return hi, lo


def _dot(a, b):
    return jnp.dot(a, b, preferred_element_type=F32)


def _dot_nt(a, b):
    return lax.dot_general(a, b, (((1,), (1,)), ((), ())), preferred_element_type=F32)


def _dot_tn(a, b):
    return lax.dot_general(a, b, (((0,), (0,)), ((), ())), preferred_element_type=F32)


def _dot_split(x, m_bf16):
    hi, lo = _split2(x)
    return _dot(hi, m_bf16) + _dot(lo, m_bf16)


def _seg_rms(x, ind_ref, seg):
    return lax.rsqrt(_dot_split(x * x, ind_ref[...]) * (1.0 / seg) + EPS)


def _sigmoid(x):
    return 1.0 / (1.0 + jnp.exp(-x))


def _silu(x):
    return x * _sigmoid(x)


def _lane(shape):
    return lax.broadcasted_iota(jnp.int32, shape, len(shape) - 1)


def _half_pad(x, odd):
    if odd:
        x = pltpu.roll(x, 64, axis=1)
    return jnp.where(_lane(x.shape) < 64, x, 0.0)


def _inproj_body(x_ref, ng_ref, w_ref, indq_ref, indk_ref, qg_ref, kg_ref, *outs, tm, seq_len, prompt):
    zc_ref, zg_ref, qn_ref, cmp_ref, slc_ref, win_ref, gate_ref, g_ref = outs[:8]
    x = x_ref[...]
    ms = jnp.mean(x * x, axis=-1, keepdims=True)
    h = (x * lax.rsqrt(ms + EPS) * ng_ref[...]).astype(BF16)

    def mm(lo, width):
        return _dot(h, w_ref[:, lo:lo + width])

    zc_ref[...] = mm(C_CONV, 3 * W_CONV)
    zg_ref[...] = mm(C_GLA, N_GLA)
    q = mm(C_NSA, W_NSA)
    qn_ref[...] = (q * _seg_rms(q, indq_ref, HD) * qg_ref[...]).astype(BF16)
    cmp_ref[...] = mm(C_NSA + 512, 256)
    slc = mm(C_NSA + 768, 256)
    win = mm(C_NSA + 1024, 256)
    ks = slc[:, 0:128]
    ks = ks * _seg_rms(ks, indk_ref, HD) * kg_ref[0:1, :]
    kw = win[:, 0:128]
    kw = kw * _seg_rms(kw, indk_ref, HD) * kg_ref[1:2, :]
    slc_ref[:, 0:128] = ks
    slc_ref[:, 128:256] = slc[:, 128:256]
    win_ref[:, 0:128] = kw
    win_ref[:, 128:256] = win[:, 128:256]
    gate_ref[...] = mm(C_NSA + 1280, W_NSA)
    g_ref[...] = mm(C_NSA + 1792, LANES)
    if prompt:
        ksa_ref, vs_ref, kwa_ref, vw_ref = outs[8:]
        lane = _lane((tm, LANES))
        row = lax.broadcasted_iota(jnp.int32, (tm, LANES), 0)
        t = lax.rem(pl.program_id(0) * tm + row, seq_len)
        ones = ((lane == CONST_LANE) | (lane == CONST_LANE + 1)).astype(F32)
        onehot = ((lane - SEL_LANE0) == lax.shift_right_logical(t, 6)).astype(F32)
        for g in range(G_KV):
            ksa_ref[g] = (_half_pad(ks, g == 1) + onehot + ones).astype(BF16)
            kwa_ref[g] = (_half_pad(kw, g == 1) + ones).astype(BF16)
        vs_ref[...] = slc[:, 128:256].T.astype(BF16)
        vw_ref[...] = win[:, 128:256].T.astype(BF16)


def _in_proj(x2d, norm_g, w, indq, indk, qg, kg, *, seq_len, prompt):
    m = x2d.shape[0]
    tm = min(256, m)
    assert m % tm == 0 and (not prompt or seq_len % tm == 0)
    row = lambda n: pl.BlockSpec((tm, n), lambda i: (i, 0))
    full = lambda a: pl.BlockSpec(a.shape, lambda i: (0,) * a.ndim)
    out_shape = [jax.ShapeDtypeStruct((m, 3 * W_CONV), F32), jax.ShapeDtypeStruct((m, N_GLA), F32),
                 jax.ShapeDtypeStruct((m, W_NSA), BF16), jax.ShapeDtypeStruct((m, 256), F32),
                 jax.ShapeDtypeStruct((m, 256), F32), jax.ShapeDtypeStruct((m, 256), F32),
                 jax.ShapeDtypeStruct((m, W_NSA), F32), jax.ShapeDtypeStruct((m, LANES), F32)]
    out_specs = [row(3 * W_CONV), row(N_GLA), row(W_NSA), row(256), row(256), row(256), row(W_NSA), row(LANES)]
    if prompt:
        aug = pl.BlockSpec((G_KV, tm, LANES), lambda i: (0, i, 0))
        assert tm == TQ
        vt = pl.BlockSpec((None, LANES, tm), lambda i: (i, 0, 0))
        out_shape += [jax.ShapeDtypeStruct((G_KV, m, LANES), BF16), jax.ShapeDtypeStruct((m // tm, LANES, tm), BF16),
                      jax.ShapeDtypeStruct((G_KV, m, LANES), BF16), jax.ShapeDtypeStruct((m // tm, LANES, tm), BF16)]
        out_specs += [aug, vt, aug, vt]
    return pl.pallas_call(
        functools.partial(_inproj_body, tm=tm, seq_len=seq_len, prompt=prompt),
        grid=(m // tm,),
        in_specs=[row(D_MODEL), full(norm_g), full(w), full(indq), full(indk), full(qg), full(kg)],
        out_specs=out_specs, out_shape=out_shape,
        compiler_params=pltpu.CompilerParams(dimension_semantics=("arbitrary",), vmem_limit_bytes=VMEM_LIMIT),
        name="in_proj",
    )(x2d, norm_g, w, indq, indk, qg, kg)


def _outproj_body(x_ref, yc_ref, yg_ref, yn_ref, w_ref, nf_ref, o_ref, *, final):
    x = x_ref[...]
    x = (x + _dot(yc_ref[...], w_ref[0:256, :]) + _dot(yg_ref[...], w_ref[256:512, :])
         + _dot(yn_ref[...].astype(BF16), w_ref[512:1024, :]))
    if final:
        ms = jnp.mean(x * x, axis=-1, keepdims=True)
        x = x * lax.rsqrt(ms + EPS) * nf_ref[...]
    o_ref[...] = x


def _out_proj(x2d, yc, yg, yn, w, norm_f, *, final):
    m = x2d.shape[0]
    tm = min(512, m)
    assert m % tm == 0
    row = lambda n: pl.BlockSpec((tm, n), lambda i: (i, 0))
    full = lambda a: pl.BlockSpec(a.shape, lambda i: (0,) * a.ndim)
    return pl.pallas_call(
        functools.partial(_outproj_body, final=final),
        grid=(m // tm,),
        in_specs=[row(D_MODEL), row(256), row(256), row(512), full(w), full(norm_f)],
        out_specs=row(D_MODEL), out_shape=jax.ShapeDtypeStruct((m, D_MODEL), F32),
        compiler_params=pltpu.CompilerParams(dimension_semantics=("arbitrary",), vmem_limit_bytes=VMEM_LIMIT),
        name="out_proj",
    )(x2d, yc, yg, yn, w, norm_f)


def _conv_body(zc_ref, hist_ref, wdw_ref, bdw_ref, lng_ref, lnb_ref, wpw_ref, bpw_ref, y_ref, st_ref,
               ext_scr, act_scr, *, nb, t_len, tc, tmm):
    n_chunks = t_len // tc

    def per_batch(bi, _):
        base = bi * t_len
        ext_scr[0:32, :] = jnp.zeros((32, W_CONV), F32)
        ext_scr[2:32, :] = hist_ref[bi]

        def chunk(c, _):
            r0 = pl.multiple_of(c * tc, tc)
            g0 = pl.multiple_of(base + r0, tc)
            a = zc_ref[pl.ds(g0, tc), 0:256]
            b = zc_ref[pl.ds(g0, tc), 256:512]
            ext_scr[pl.ds(32 + r0, tc), :] = a * _sigmoid(b)
            win = ext_scr[pl.ds(r0, tc + 32), :]
            acc = jnp.zeros((tc, W_CONV), F32) + bdw_ref[...]
            for r in range(8):
                taps = [j for j in range(r, CONV_WIDTH, 8)]
                wr = win[2 + r:2 + r + tc + 8 * (len(taps) - 1)]
                for mi, j in enumerate(taps):
                    acc = acc + wr[8 * mi:8 * mi + tc] * wdw_ref[j:j + 1, :]
            mu = jnp.mean(acc, axis=-1, keepdims=True)
            d = acc - mu
            var = jnp.mean(d * d, axis=-1, keepdims=True)
            yn = d * lax.rsqrt(var + EPS) * lng_ref[...] + lnb_ref[...]
            act_scr[pl.ds(g0, tc), :] = _silu(yn)
            return 0

        lax.fori_loop(0, n_chunks, chunk, 0)
        st_ref[bi] = ext_scr[2 + t_len:32 + t_len, :]
        return 0

    lax.fori_loop(0, nb, per_batch, 0)

    def mm(c, _):
        r0 = pl.multiple_of(c * tmm, tmm)
        y = _dot(act_scr[pl.ds(r0, tmm), :].astype(BF16), wpw_ref[...]) + bpw_ref[...]
        y_ref[pl.ds(r0, tmm), :] = (y * _silu(zc_ref[pl.ds(r0, tmm), 512:768])).astype(BF16)
        return 0

    lax.fori_loop(0, nb * t_len // tmm, mm, 0)


def _conv(zc, hist, wdw, bdw, lng, lnb, wpw, bpw, *, nb, t_len):
    m = zc.shape[0]
    n_b = m // t_len
    tc = min(64, t_len)
    tmm = min(256, nb * t_len)
    full = lambda a: pl.BlockSpec(a.shape, lambda i: (0,) * a.ndim)
    return pl.pallas_call(
        functools.partial(_conv_body, nb=nb, t_len=t_len, tc=tc, tmm=tmm),
        grid=(n_b // nb,),
        in_specs=[pl.BlockSpec((nb * t_len, 3 * W_CONV), lambda i: (i, 0)),
                  pl.BlockSpec((nb, HIST, W_CONV), lambda i: (i, 0, 0)),
                  full(wdw), full(bdw), full(lng), full(lnb), full(wpw), full(bpw)],
        out_specs=[pl.BlockSpec((nb * t_len, W_CONV), lambda i: (i, 0)),
                   pl.BlockSpec((nb, HIST, W_CONV), lambda i: (i, 0, 0))],
        out_shape=[jax.ShapeDtypeStruct((m, W_CONV), BF16), jax.ShapeDtypeStruct((n_b, HIST, W_CONV), F32)],
        scratch_shapes=[pltpu.VMEM((32 + t_len, W_CONV), F32), pltpu.VMEM((nb * t_len, W_CONV), F32)],
        compiler_params=pltpu.CompilerParams(dimension_semantics=("arbitrary",), vmem_limit_bytes=VMEM_LIMIT),
        name="conv",
    )(zc, hist, wdw, bdw, lng, lnb, wpw, bpw)


def _gla_body(zg_ref, s0_ref, wgk_ref, bgk_ref, ltri_ref, lsum_ref, ind_ref, indv_ref, ng_ref, bd_ref,
              y_ref, st_ref, bc_scr, qe_scr, ke_scr, dec_scr, o_scr, s_scr, *, nb, t_len, chunk, rb):
    rows = nb * t_len
    n_rb = rows // rb
    cpb = t_len // chunk

    def phase1(i, _):
        r0 = pl.multiple_of(i * rb, rb)
        gk = zg_ref[pl.ds(r0, rb), 768:896]
        pre = _dot_split(gk, wgk_ref[0]) + _dot(gk.astype(BF16), wgk_ref[1]) + bgk_ref[...]
        la = (jnp.minimum(pre, 0.0) - jnp.log(1.0 + jnp.exp(-jnp.abs(pre)))) * (1.0 / GATE_NORMALIZER)
        h1, l1 = _split2(la)
        l2 = (la - h1.astype(F32) - l1.astype(F32)).astype(BF16)
        bc = _dot(ltri_ref[...], h1) + _dot(ltri_ref[...], l1) + _dot(ltri_ref[...], l2)
        bt = _dot(lsum_ref[...], h1) + _dot(lsum_ref[...], l1) + _dot(lsum_ref[...], l2)
        bc_scr[pl.ds(r0, rb), :] = bc
        qe_scr[pl.ds(r0, rb), :] = zg_ref[pl.ds(r0, rb), 0:128] * (DK_GLA ** -0.5) * jnp.exp(bc)
        ke_scr[pl.ds(r0, rb), :] = zg_ref[pl.ds(r0, rb), 128:256] * jnp.exp(bt - bc)
        dec_scr[pl.ds(r0, rb), :] = jnp.exp(bt)
        return 0

    lax.fori_loop(0, n_rb, phase1, 0)

    ti = lax.broadcasted_iota(jnp.int32, (chunk, chunk, LANES), 0)
    si = lax.broadcasted_iota(jnp.int32, (chunk, chunk, LANES), 1)
    causal = si <= ti

    def phase2(c, _):
        r0 = pl.multiple_of(c * chunk, chunk)
        bc = bc_scr[pl.ds(r0, chunk), :]
        q = zg_ref[pl.ds(r0, chunk), 0:128] * (DK_GLA ** -0.5)
        k = zg_ref[pl.ds(r0, chunk), 128:256]
        v = zg_ref[pl.ds(r0, chunk), 256:512]
        e = jnp.exp(jnp.where(causal, bc[:, None, :] - bc[None, :, :], NEG))
        p = (q[:, None, :] * k[None, :, :] * e).reshape(chunk * chunk, LANES)
        att = _dot_split(p, ind_ref[...]).reshape(chunk, chunk, W_GLA)
        o_scr[pl.ds(r0, chunk), :] = (att * v[None, :, :]).sum(axis=1)
        return 0

    lax.fori_loop(0, rows // chunk, phase2, 0)

    def per_batch(bi, _):
        s_scr[...] = s0_ref[bi]

        def step(c, _):
            r0 = pl.multiple_of(bi * t_len + c * chunk, chunk)
            s = s_scr[...]
            o_scr[pl.ds(r0, chunk), :] += _dot_nt(qe_scr[pl.ds(r0, chunk), :].astype(BF16), s.astype(BF16))
            upd = _dot_tn(zg_ref[pl.ds(r0, chunk), 256:512].astype(BF16), ke_scr[pl.ds(r0, chunk), :].astype(BF16))
            s_scr[...] = s * dec_scr[pl.ds(r0, 1), :] + upd * bd_ref[...]
            return 0

        lax.fori_loop(0, cpb, step, 0)
        st_ref[bi] = s_scr[...]
        return 0

    lax.fori_loop(0, nb, per_batch, 0)

    def phase4(i, _):
        r0 = pl.multiple_of(i * rb, rb)
        o = o_scr[pl.ds(r0, rb), :]
        o = o * _seg_rms(o, indv_ref, DV_GLA) * ng_ref[...]
        y_ref[pl.ds(r0, rb), :] = (o * _silu(zg_ref[pl.ds(r0, rb), 512:768])).astype(BF16)
        return 0

    lax.fori_loop(0, n_rb, phase4, 0)


def _gla(zg, s0t, wgk, bgk, ng, *, nb, t_len, chunk):
    m = zg.shape[0]
    n_b = m // t_len
    rows = nb * t_len
    rb = min(128, rows)
    assert rows % rb == 0 and rb % chunk == 0 and t_len % chunk == 0
    ltri = jnp.asarray(_blockdiag_ones(rb, chunk) * np.tril(np.ones((rb, rb), np.float32)), BF16)
    lsum = jnp.asarray(_blockdiag_ones(rb, chunk), BF16)
    i128 = np.arange(LANES)
    i256 = np.arange(W_GLA)
    ind = jnp.asarray((i128[:, None] // DK_GLA == i256[None, :] // DV_GLA).astype(np.float32), BF16)
    indv = jnp.asarray(_blockdiag_ones(W_GLA, DV_GLA), BF16)
    bd = jnp.asarray((i256[:, None] // DV_GLA == i128[None, :] // DK_GLA).astype(np.float32))
    full = lambda a: pl.BlockSpec(a.shape, lambda i: (0,) * a.ndim)
    return pl.pallas_call(
        functools.partial(_gla_body, nb=nb, t_len=t_len, chunk=chunk, rb=rb),
        grid=(n_b // nb,),
        in_specs=[pl.BlockSpec((rows, N_GLA), lambda i: (i, 0)),
                  pl.BlockSpec((nb, W_GLA, LANES), lambda i: (i, 0, 0)),
                  full(wgk), full(bgk), full(ltri), full(lsum), full(ind), full(indv), full(ng), full(bd)],
        out_specs=[pl.BlockSpec((rows, W_GLA), lambda i: (i, 0)),
                   pl.BlockSpec((nb, W_GLA, LANES), lambda i: (i, 0, 0))],
        out_shape=[jax.ShapeDtypeStruct((m, W_GLA), BF16), jax.ShapeDtypeStruct((n_b, W_GLA, LANES), F32)],
        scratch_shapes=[pltpu.VMEM((rows, LANES), F32), pltpu.VMEM((rows, LANES), F32),
                        pltpu.VMEM((rows, LANES), F32), pltpu.VMEM((rows, LANES), F32),
                        pltpu.VMEM((rows, W_GLA), F32), pltpu.VMEM((W_GLA, LANES), F32)],
        compiler_params=pltpu.CompilerParams(dimension_semantics=("arbitrary",), vmem_limit_bytes=VMEM_LIMIT),
        name="gla",
    )(zg, s0t, wgk, bgk, ltri, lsum, ind, indv, ng, bd)


KW = CMP_BLOCK * 256


def _compress_rows(x_ref, w_ref, pe_ref, kvc_ref, n_rows):
    kvc_ref[...] = jnp.zeros(kvc_ref.shape, F32)
    for par in range(2):
        x = (x_ref[:, par * KW:(par + 1) * KW] + pe_ref[...]).astype(BF16)
        kvc_ref[ODD_SLOT0 * par:ODD_SLOT0 * par + n_rows, :] = _dot(x, w_ref[...])


def _compress_prompt_body(x_ref, w_ref, pe_ref, kvc_ref, *, n_rows):
    _compress_rows(x_ref, w_ref, pe_ref, kvc_ref, n_rows)


def _compress_prompt(cmp_rows, w, pe, *, n_b, t_len):
    n_rows = t_len // SEL_BLOCK
    full = lambda a: pl.BlockSpec(a.shape, lambda i: (0,) * a.ndim)
    return pl.pallas_call(
        functools.partial(_compress_prompt_body, n_rows=n_rows),
        grid=(n_b,),
        in_specs=[pl.BlockSpec((n_rows, 2 * KW), lambda i: (i, 0)), full(w), full(pe)],
        out_specs=pl.BlockSpec((None, LANES, 256), lambda i: (i, 0, 0)),
        out_shape=jax.ShapeDtypeStruct((n_b, LANES, 256), F32),
        compiler_params=pltpu.CompilerParams(dimension_semantics=("arbitrary",), vmem_limit_bytes=VMEM_LIMIT),
        name="compress_prompt",
    )(cmp_rows.reshape(n_b * n_rows, 2 * KW), w, pe)


def _compress_paged_body(pt_ref, *refs, n_pages, t_new, n_rows):
    pages = refs[:n_pages]
    new_ref, w_ref, pe_ref, kvc_ref, x_scr = refs[n_pages:]
    past = n_pages * PAGE_SIZE
    new = new_ref[...]
    for s in range(2):
        for p in range(n_pages):
            x_scr[s, p * PAGE_SIZE:(p + 1) * PAGE_SIZE, :] = pages[p][s].reshape(LANES, PAGE_SIZE).T
        x_scr[s, past:, :] = jnp.zeros((n_rows * SEL_BLOCK - past, LANES), F32)
        x_scr[s, past:past + t_new, :] = new[:, LANES * s:LANES * (s + 1)]
    kvc_ref[...] = jnp.zeros(kvc_ref.shape, F32)
    for s in range(2):
        for par in range(2):
            acc = jnp.zeros((n_rows, LANES), F32)
            for j in range(CMP_BLOCK):
                x = x_scr[s, pl.ds(CMP_BLOCK * par + j, n_rows, stride=SEL_BLOCK), :] + pe_ref[s, j:j + 1, :]
                acc = acc + _dot(x.astype(BF16), w_ref[s, j])
            kvc_ref[ODD_SLOT0 * par:ODD_SLOT0 * par + n_rows, LANES * s:LANES * (s + 1)] = acc


def _paged_specs(layer, n_pages, block):
    return [pl.BlockSpec((None, None) + block, functools.partial(
        lambda b, pt, p: (layer, pt[b, p]) + (0,) * len(block), p=p)) for p in range(n_pages)]


PAGE_BLOCK = (2, G_KV, HD, PAGE_SIZE)


def _compress_paged(page_table, cache_t, layer, cmp_new, w, pe, *, t_new):
    n_b, n_pages = page_table.shape
    n_rows = -(-((n_pages * PAGE_SIZE + SEL_BLOCK) // SEL_BLOCK) // 8) * 8
    full = lambda a: pl.BlockSpec(a.shape, lambda b, pt: (0,) * a.ndim)
    grid_spec = pltpu.PrefetchScalarGridSpec(
        num_scalar_prefetch=1, grid=(n_b,),
        in_specs=_paged_specs(layer, n_pages, PAGE_BLOCK) + [
            pl.BlockSpec((t_new, 256), lambda b, pt: (b, 0)), full(w), full(pe)],
        out_specs=pl.BlockSpec((None, LANES, 256), lambda b, pt: (b, 0, 0)),
        scratch_shapes=[pltpu.VMEM((2, n_rows * SEL_BLOCK, LANES), F32)])
    return pl.pallas_call(
        functools.partial(_compress_paged_body, n_pages=n_pages, t_new=t_new, n_rows=n_rows),
        grid_spec=grid_spec, out_shape=jax.ShapeDtypeStruct((n_b, LANES, 256), F32),
        compiler_params=pltpu.CompilerParams(dimension_semantics=("arbitrary",), vmem_limit_bytes=VMEM_LIMIT),
        name="compress_paged",
    )(page_table, *([cache_t] * n_pages), cmp_new, w, pe)


def _cmpsel_body(qn_ref, kvc_ref, bias_ref, sbase_ref, useimp_ref, indk_ref, kg_ref, cvec_ref,
                 *outs, nb, tb, n_blocks, q_t):
    q_refs, ocmp_ref = outs[:-1], outs[-1]
    rows = nb * tb
    lane = _lane((rows, LANES))
    qn = qn_ref[...].astype(F32)
    qpad = [_half_pad(qn[:, 128 * (h // 2):128 * (h // 2) + 128], h % 2 == 1) for h in range(H_NSA)]
    imp_parts = []
    o_parts = [[None] * nb for _ in range(H_NSA)]
    for bi in range(nb):
        kvc = kvc_ref[bi]
        kc = kvc[:, 0:128]
        kc = kc * _seg_rms(kc, indk_ref, HD) * kg_ref[...]
        vboth = kvc[:, 128:256].astype(BF16)
        imp_b = []
        for g in range(G_KV):
            kpad = _half_pad(kc, g == 1).astype(BF16)
            qs = jnp.concatenate([qpad[HPG * g + j][bi * tb:(bi + 1) * tb] for j in range(HPG)], axis=0).astype(BF16)
            bias = bias_ref[g]
            l = _dot_nt(qs, kpad) + bias
            vis = bias > 0.5 * NEG
            mx = jnp.max(l, axis=-1, keepdims=True)
            p = jnp.where(vis, jnp.exp(l - mx), 0.0)
            p = p / jnp.maximum(jnp.sum(p, axis=-1, keepdims=True), 1e-30)
            o = _dot(p.astype(BF16), vboth)
            ig = p[0:tb]
            for j in range(1, HPG):
                ig = ig + p[j * tb:(j + 1) * tb]
            imp_b.append(ig)
            for j in range(HPG):
                o_parts[HPG * g + j][bi] = o[j * tb:(j + 1) * tb]
        imp_parts.append(imp_b)
    for j in range(HPG):
        o0 = jnp.concatenate(o_parts[j], axis=0) if nb > 1 else o_parts[j][0]
        o1 = jnp.concatenate(o_parts[HPG + j], axis=0) if nb > 1 else o_parts[HPG + j][0]
        ocmp_ref[:, 128 * j:128 * j + 128] = jnp.where(lane < 64, o0, o1)
    for g in range(G_KV):
        ig = jnp.concatenate([imp_parts[bi][g] for bi in range(nb)], axis=0) if nb > 1 else imp_parts[0][g]
        imp = ig + pltpu.roll(ig, LANES - ODD_SLOT0, axis=1)
        score = jnp.where(useimp_ref[...] > 0.5, imp, sbase_ref[...])
        st = score.T[0:SEL_SLOTS]
        jj = lax.broadcasted_iota(jnp.int32, (SEL_SLOTS, rows), 0)
        rank = jnp.zeros((SEL_SLOTS, rows), F32)
        for jp in range(n_blocks):
            r = st[jp:jp + 1, :]
            before = (r > st) | ((r == st) & (jp < jj))
            rank = rank + before.astype(F32)
        selb = jnp.where(rank < TOP_N - 0.5, 0.0, NEG)
        full_t = jnp.concatenate([jnp.zeros((SEL_LANE0, rows), F32), selb,
                                  jnp.zeros((LANES - SEL_LANE0 - SEL_SLOTS, rows), F32)], axis=0)
        if q_t:
            for j in range(HPG):
                h = HPG * g + j
                q_refs[j][g] = ((qpad[h] + cvec_ref[h]).T + full_t).astype(BF16)
        else:
            extra = full_t.T
            for j in range(HPG):
                h = HPG * g + j
                q_refs[0][g, j] = (qpad[h] + extra + cvec_ref[h]).astype(q_refs[0].dtype)


def _cmpsel(qn, kvc, bias, sbase, useimp, indk, kg0, cvec, *, nb, tb, n_blocks, q_t):
    m = qn.shape[0]
    rows = nb * tb
    assert rows == LANES and m % rows == 0
    n_pos = bias.shape[0]
    full = lambda a: pl.BlockSpec(a.shape, lambda i: (0,) * a.ndim)
    if q_t:
        q_specs = [pl.BlockSpec((G_KV, LANES, rows), lambda i: (0, 0, i))] * HPG
        q_shapes = [jax.ShapeDtypeStruct((G_KV, LANES, m), BF16)] * HPG
    else:
        q_specs = [pl.BlockSpec((G_KV, HPG, rows, LANES), lambda i: (0, 0, i, 0))]
        q_shapes = [jax.ShapeDtypeStruct((G_KV, HPG, m, LANES), F32)]
    return pl.pallas_call(
        functools.partial(_cmpsel_body, nb=nb, tb=tb, n_blocks=n_blocks, q_t=q_t),
        grid=(m // rows,),
        in_specs=[pl.BlockSpec((rows, W_NSA), lambda i: (i, 0)),
                  pl.BlockSpec((nb, LANES, 256), (lambda i: (i, 0, 0)) if nb > 1 else (lambda i: (i // n_pos, 0, 0))),
                  pl.BlockSpec((None, G_KV, HPG * tb, LANES), lambda i: (i % n_pos, 0, 0, 0)),
                  pl.BlockSpec((None, rows, LANES), lambda i: (i % n_pos, 0, 0)),
                  pl.BlockSpec((None, rows, LANES), lambda i: (i % n_pos, 0, 0)),
                  full(indk), full(kg0), full(cvec)],
        out_specs=q_specs + [pl.BlockSpec((rows, W_NSA), lambda i: (i, 0))],
        out_shape=q_shapes + [jax.ShapeDtypeStruct((m, W_NSA), F32)],
        compiler_params=pltpu.CompilerParams(dimension_semantics=("arbitrary",), vmem_limit_bytes=VMEM_LIMIT),
        name="cmpsel",
    )(qn, kvc, bias, sbase, useimp, indk, kg0, cvec)


def _mix_gates(g_ref, bg_ref, expand_ref, gate_ref, ocmp_ref, o_slc, o_win, y_ref):
    gl = _sigmoid(g_ref[...] + bg_ref[...])
    gx = _dot_split(gl, expand_ref[...])
    for j in range(HPG):
        sl = slice(128 * j, 128 * j + 128)
        o = (gx[:, 128 * j:128 * j + 128] * ocmp_ref[:, sl]
             + gx[:, 128 * (HPG + j):128 * (HPG + j) + 128] * o_slc[j]
             + gx[:, 128 * (2 * HPG + j):128 * (2 * HPG + j) + 128] * o_win[j])
        y_ref[:, sl] = (o * _silu(gate_ref[:, sl])).astype(y_ref.dtype)


def _softmax_step(q_t, k, v_t, bias_t, m_scr, l_scr, acc_scr):
    s = _dot(k, q_t)
    if bias_t is not None:
        s = s + bias_t
    m_prev = m_scr[...]
    m_new = jnp.maximum(m_prev, jnp.max(s, axis=0, keepdims=True))
    alpha = jnp.exp(m_prev - m_new)
    p = jnp.exp(s - m_new)
    l_scr[...] = alpha * l_scr[...] + jnp.sum(p, axis=0, keepdims=True)
    acc_scr[...] = alpha * acc_scr[...] + _dot(v_t, p.astype(BF16))
    m_scr[...] = m_new


def _flash_body(q0_ref, q1_ref, q2_ref, q3_ref, ksa_ref, vs_ref, kwa_ref, vw_ref, dt_ref, edge_ref, ocmp_ref,
                g_ref, gate_ref, bg_ref, expand_ref, y_ref, m_scr, l_scr, acc_scr, res_scr):
    i = pl.program_id(1)
    cols = HPG * TQ
    near = jnp.maximum(i - 1, 0)
    edge = jnp.maximum(i - 2, 0)

    def reset():
        m_scr[...] = jnp.full((1, cols), NEG, F32)
        l_scr[...] = jnp.zeros((1, cols), F32)
        acc_scr[...] = jnp.zeros((LANES, cols), F32)

    def step(q_t, ka_ref, v_ref, g, kt, bias_t):
        k0 = pl.multiple_of(kt * TQ, TQ)
        _softmax_step(q_t, ka_ref[g, pl.ds(k0, TQ), :], v_ref[kt], bias_t, m_scr, l_scr, acc_scr)

    for g in range(G_KV):
        q_t = jnp.concatenate([r[g] for r in (q0_ref, q1_ref, q2_ref, q3_ref)], axis=1)
        for br, (ka_ref, v_ref) in enumerate(((ksa_ref, vs_ref), (kwa_ref, vw_ref))):
            reset()
            if br == 0:
                def far(kt, _):
                    step(q_t, ka_ref, v_ref, g, kt, None)
                    return 0

                lax.fori_loop(0, near, far, 0)
            else:
                @pl.when(i >= 2)
                def _():
                    step(q_t, ka_ref, v_ref, g, edge, edge_ref[...])

            @pl.when(i >= 1)
            def _():
                step(q_t, ka_ref, v_ref, g, near, dt_ref[1, g])

            step(q_t, ka_ref, v_ref, g, i, dt_ref[0, g])
            res_scr[br, g] = acc_scr[...] / l_scr[...]

    outs = []
    for br in range(2):
        o_t = jnp.concatenate([res_scr[br, 0, 0:64, :], res_scr[br, 1, 64:128, :]], axis=0)
        outs.append([o_t[:, j * TQ:(j + 1) * TQ].T for j in range(HPG)])
    _mix_gates(g_ref, bg_ref, expand_ref, gate_ref, ocmp_ref, outs[0], outs[1], y_ref)


def _flash(q_ts, ksa, vs, kwa, vw, dt, edge, ocmp, ng, ngate, bg, expand, *, n_b, t_len):
    m = n_b * t_len
    nq = t_len // TQ
    cols = HPG * TQ
    full = lambda a: pl.BlockSpec(a.shape, lambda b, i: (0,) * a.ndim)
    qspec = pl.BlockSpec((G_KV, LANES, TQ), lambda b, i: (0, 0, b * nq + i))
    kspec = pl.BlockSpec((G_KV, None, t_len, LANES), lambda b, i: (0, b, 0, 0))
    vspec = pl.BlockSpec((nq, LANES, TQ), lambda b, i: (b, 0, 0))
    row = lambda n: pl.BlockSpec((TQ, n), lambda b, i: (b * nq + i, 0))
    return pl.pallas_call(
        _flash_body,
        grid=(n_b, nq),
        in_specs=[qspec] * HPG + [kspec, vspec, kspec, vspec, full(dt), full(edge), row(W_NSA), row(LANES),
                                  row(W_NSA), full(bg), full(expand)],
        out_specs=row(W_NSA), out_shape=jax.ShapeDtypeStruct((m, W_NSA), BF16),
        scratch_shapes=[pltpu.VMEM((1, cols), F32), pltpu.VMEM((1, cols), F32),
                        pltpu.VMEM((LANES, cols), F32), pltpu.VMEM((2, G_KV, LANES, cols), F32)],
        compiler_params=pltpu.CompilerParams(dimension_semantics=("arbitrary", "arbitrary"),
                                             vmem_limit_bytes=VMEM_LIMIT),
        name="flash",
    )(*q_ts, ksa.reshape(G_KV, n_b, t_len, LANES), vs, kwa.reshape(G_KV, n_b, t_len, LANES), vw,
      dt, edge, ocmp, ng, ngate, bg, expand)


def _dec_attn_body(pt_ref, *refs, n_pages, t_new, n_keys, n_wkeys):
    pages = refs[:n_pages]
    (qa_ref, slc_ref, win_ref, wst_ref, dsl_ref, dwn_ref, ocmp_ref, g_ref, gate_ref, bg_ref, expand_ref,
     y_ref, wout_ref, ka_scr, va_scr, kw_scr, vw_scr) = refs[n_pages:]
    past = n_pages * PAGE_SIZE
    w_past = wst_ref.shape[-1]
    rows = HPG * t_new

    @pl.when(pl.program_id(0) == 0)
    def _():
        for k_scr, n, with_blocks in ((ka_scr, n_keys, True), (kw_scr, n_wkeys, False)):
            feat = lax.broadcasted_iota(jnp.int32, (LANES, n), 0)
            key = lax.broadcasted_iota(jnp.int32, (LANES, n), 1)
            aug = (feat == CONST_LANE) | (feat == CONST_LANE + 1)
            if with_blocks:
                aug = aug | ((feat - SEL_LANE0) == lax.shift_right_logical(key, 6))
            for g in range(G_KV):
                k_scr[g] = aug.astype(F32).astype(BF16)

    def new_keys_t(new):
        pad = jnp.concatenate([new, jnp.zeros((LANES - t_new, 256), F32)], axis=0)
        return pad[:, 0:128].T, pad[:, 128:256].T

    for p in range(n_pages):
        sl = slice(p * PAGE_SIZE, (p + 1) * PAGE_SIZE)
        for g in range(G_KV):
            ka_scr[g, 0:HD, sl] = pages[p][0, g].astype(BF16)
            va_scr[HD * g:HD * (g + 1), sl] = pages[p][1, g].astype(BF16)
    kn_t, vn_t = new_keys_t(slc_ref[...])
    va_scr[:, past:n_keys] = vn_t.astype(BF16)
    kwn_t, vwn_t = new_keys_t(win_ref[...])
    vw_scr[:, w_past:n_wkeys] = vwn_t.astype(BF16)
    lane = _lane((HD, LANES))
    for g in range(G_KV):
        ka_scr[g, 0:HD, past:n_keys] = kn_t[HD * g:HD * (g + 1)].astype(BF16)
        kw_scr[g, 0:HD, 0:w_past] = wst_ref[0, g].astype(BF16)
        kw_scr[g, 0:HD, w_past:n_wkeys] = kwn_t[HD * g:HD * (g + 1)].astype(BF16)
        vw_scr[HD * g:HD * (g + 1), 0:w_past] = wst_ref[1, g].astype(BF16)
        for s, new_t in ((0, kwn_t), (1, vwn_t)):
            sh = pltpu.roll(wst_ref[s, g], w_past - t_new, axis=1)
            tail = pltpu.roll(new_t[HD * g:HD * (g + 1)], LANES - t_new, axis=1)
            wout_ref[s, g, :, 0:w_past - LANES] = sh[:, 0:w_past - LANES]
            wout_ref[s, g, :, w_past - LANES:w_past] = jnp.where(lane >= LANES - t_new, tail, sh[:, w_past - LANES:])

    res = [[None, None], [None, None]]
    for g in range(G_KV):
        q = qa_ref[g].reshape(rows, LANES).astype(BF16)
        for br, (k_scr, v_scr, b_ref) in enumerate(((ka_scr, va_scr, dsl_ref), (kw_scr, vw_scr, dwn_ref))):
            s = _dot(q, k_scr[g]) + b_ref[g]
            mx = jnp.max(s, axis=-1, keepdims=True)
            p = jnp.exp(s - mx)
            den = jnp.sum(p, axis=-1, keepdims=True)
            res[br][g] = _dot_nt(p.astype(BF16), v_scr[...]) / den
    lane = _lane((t_new, LANES))
    outs = [[jnp.where(lane < 64, res[br][0][j * t_new:(j + 1) * t_new], res[br][1][j * t_new:(j + 1) * t_new])
             for j in range(HPG)] for br in range(2)]
    _mix_gates(g_ref, bg_ref, expand_ref, gate_ref, ocmp_ref, outs[0], outs[1], y_ref)


def _dec_attn(page_table, cache, layer, qa, slc_new, win_new, win_state, dsl, dwn, ocmp, ng, ngate, bg, expand, *, t_new):
    n_b, n_pages = page_table.shape
    m = n_b * t_new
    w_past = win_state.shape[-1]
    n_keys = dsl.shape[-1]
    n_wkeys = dwn.shape[-1]
    wblock = (2, G_KV, HD, w_past)
    full = lambda a: pl.BlockSpec(a.shape, lambda b, pt: (0,) * a.ndim)
    row = lambda n: pl.BlockSpec((t_new, n), lambda b, pt: (b, 0))
    grid_spec = pltpu.PrefetchScalarGridSpec(
        num_scalar_prefetch=1, grid=(n_b,),
        in_specs=_paged_specs(layer, n_pages, PAGE_BLOCK) + [
            pl.BlockSpec((G_KV, HPG, t_new, LANES), lambda b, pt: (0, 0, b, 0)),
            row(256), row(256), pl.BlockSpec((None, None) + wblock, lambda b, pt: (layer, b, 0, 0, 0, 0)),
            full(dsl), full(dwn), row(W_NSA), row(LANES), row(W_NSA), full(bg), full(expand)],
        out_specs=[row(W_NSA), pl.BlockSpec((None,) + wblock, lambda b, pt: (b, 0, 0, 0, 0))],
        scratch_shapes=[pltpu.VMEM((G_KV, LANES, n_keys), BF16), pltpu.VMEM((LANES, n_keys), BF16),
                        pltpu.VMEM((G_KV, LANES, n_wkeys), BF16), pltpu.VMEM((LANES, n_wkeys), BF16)])
    return pl.pallas_call(
        functools.partial(_dec_attn_body, n_pages=n_pages, t_new=t_new, n_keys=n_keys, n_wkeys=n_wkeys),
        grid_spec=grid_spec,
        out_shape=[jax.ShapeDtypeStruct((m, W_NSA), F32), jax.ShapeDtypeStruct((n_b,) + wblock, F32)],
        compiler_params=pltpu.CompilerParams(dimension_semantics=("arbitrary",), vmem_limit_bytes=VMEM_LIMIT),
        name="dec_attn",
    )(page_table, *([cache] * n_pages), qa, slc_new, win_new, win_state, dsl, dwn, ocmp, ng, ngate, bg, expand)


def _bucket_matrix(qpos, kpos, valid):
    dist = qpos[:, None] - kpos[None, :]
    return np.where(valid & (dist >= 0), _rel_bucket_np(dist), -1).astype(np.int32)


def _bias_body(tab_ref, bk_ref, o_ref, *, sub_far):
    bk = bk_ref[...]
    for h in range(H_NSA):
        far = tab_ref[N_BUCKETS - 1, h] if sub_far else 0.0
        acc = jnp.full(bk.shape, NEG, F32)
        for b in range(N_BUCKETS):
            acc = jnp.where(bk == b, tab_ref[b, h] - far, acc)
        o_ref[h] = acc


def _bias_table(table, bk, *, sub_far):
    r, c = bk.shape
    tr = 8 if r <= 8 else 64
    assert r % tr == 0
    return pl.pallas_call(
        functools.partial(_bias_body, sub_far=sub_far),
        grid=(r // tr,),
        in_specs=[pl.BlockSpec(memory_space=pltpu.SMEM), pl.BlockSpec((tr, c), lambda i: (i, 0))],
        out_specs=pl.BlockSpec((H_NSA, tr, c), lambda i: (0, i, 0)),
        out_shape=jax.ShapeDtypeStruct((H_NSA, r, c), F32),
        compiler_params=pltpu.CompilerParams(dimension_semantics=("arbitrary",)),
        name="bias_table",
    )(table, jnp.asarray(bk))


def _cmp_tables(qpos_blocks, n_blocks):
    slot = np.arange(LANES)
    blk = np.where(slot < ODD_SLOT0, 2 * slot, 2 * (slot - ODD_SLOT0) + 1)
    real = (blk < 2 * n_blocks) & ((slot % ODD_SLOT0) < n_blocks)
    blk_end = blk * CMP_BLOCK + CMP_BLOCK - 1
    buckets, sbases, useimps = [], [], []
    for qpos in qpos_blocks:
        buckets.append(_bucket_matrix(qpos, blk_end, real[None, :]))
        j = np.arange(LANES)[None, :]
        cur = (qpos // SEL_BLOCK)[:, None]
        forced = (j == 0) | (j == cur) | (j == cur - 1)
        valid = j * SEL_BLOCK <= qpos[:, None]
        inrange = j < n_blocks
        sbases.append(np.where(~inrange, -2.0, np.where(forced, FORCE, -1.0)).astype(np.float32))
        useimps.append((inrange & ~forced & valid).astype(np.float32))
    return np.concatenate(buckets), jnp.asarray(np.stack(sbases)), jnp.asarray(np.stack(useimps))


def _head_rows(b, t):
    k = b.shape[-1]
    n = b.shape[1] // t
    return b.reshape(G_KV, HPG, n, t, k).transpose(2, 0, 1, 3, 4).reshape(n, G_KV, HPG * t, k)


def _prep_weights(rel_bias, w_in, conv_w_pw, gla_w_gk, nsa_q_norm_g, nsa_k_norm_g, nsa_pe_cmp, nsa_w_cmp,
                  nsa_b_gate, w_out):
    depth = w_in.shape[0]
    cols = _in_proj_columns()
    w_in_p = jnp.where(jnp.asarray(cols >= 0)[None, None, :],
                       jnp.take(w_in, jnp.asarray(np.maximum(cols, 0)), axis=2), 0.0).astype(BF16)
    perm = np.concatenate([np.arange(512), 512 + _nsa_perm()])
    w_out_p = jnp.take(w_out, jnp.asarray(perm), axis=1).astype(BF16)
    qg = jnp.tile(nsa_q_norm_g, (1, H_NSA))[:, None, :] * SCALE
    kg = jnp.tile(nsa_k_norm_g, (1, 1, G_KV))
    wgk = jnp.zeros((depth, LANES, LANES), F32).at[:, :GATE_RANK, :].set(gla_w_gk)
    wgk_hi = wgk.astype(BF16)
    wgk2 = jnp.stack([wgk_hi, (wgk - wgk_hi.astype(F32)).astype(BF16)], axis=1)
    wbd = jnp.zeros((depth, CMP_BLOCK, 256, 256), F32)
    for s in range(2):
        for g in range(G_KV):
            o = 128 * s + 64 * g
            wbd = wbd.at[:, :, o:o + 64, o:o + 64].set(nsa_w_cmp[:, s])
    wtap = jnp.stack([wbd[:, :, 0:128, 0:128], wbd[:, :, 128:256, 128:256]], axis=1).astype(BF16)
    wbd = wbd.reshape(depth, KW, 256)
    pe = jnp.tile(nsa_pe_cmp.transpose(0, 2, 1, 3)[:, :, :, None, :], (1, 1, 1, G_KV, 1))
    petap = pe.transpose(0, 2, 1, 3, 4).reshape(depth, 2, CMP_BLOCK, LANES)
    pe = pe.reshape(depth, 1, KW)
    bg = jnp.zeros((depth, 1, LANES), F32).at[:, 0, :3 * H_NSA].set(nsa_b_gate)
    expand = np.zeros((LANES, 3 * HPG * LANES), np.float32)
    for br in range(3):
        for g in range(G_KV):
            for j in range(HPG):
                c = LANES * (HPG * br + j) + 64 * g
                expand[H_NSA * br + HPG * g + j, c:c + 64] = 1.0
    far = rel_bias[N_BUCKETS - 1]
    far_hi = far.astype(BF16).astype(F32)
    cvec = jnp.zeros((H_NSA, 1, LANES), F32).at[:, 0, CONST_LANE].set(far_hi).at[:, 0, CONST_LANE + 1].set(far - far_hi)
    return dict(w_in=w_in_p, w_out=w_out_p, qg=qg, kg=kg, wgk=wgk2, wbd=wbd.astype(BF16), pe=pe, bg=bg,
                wtap=wtap, petap=petap,
                expand=jnp.asarray(expand, BF16), cvec=cvec, wpw=conv_w_pw.astype(BF16),
                indq=jnp.asarray(_blockdiag_ones(W_NSA, HD), BF16), indk=jnp.asarray(_blockdiag_ones(LANES, HD), BF16))


def kernel(x_prompt, x_sample, cache_cmp_kv, cache_slc_kv, page_table, state_win_kv, state_gla, state_conv, rel_bias, norm_g, w_in, conv_w_dw, conv_b_dw, conv_ln_g, conv_ln_b, conv_w_pw, conv_b_pw, gla_w_gk, gla_b_gk, gla_norm_g, nsa_q_norm_g, nsa_k_norm_g, nsa_pe_cmp, nsa_w_cmp, nsa_b_gate, w_out, norm_f):
    depth = w_in.shape[0]
    n_bp, t_p, _ = x_prompt.shape
    n_bs, t_s, _ = x_sample.shape
    n_pages = page_table.shape[1]
    past = n_pages * PAGE_SIZE
    w_past = state_win_kv.shape[2]
    assert t_p % TQ == 0 and WINDOW == 2 * TQ and t_s == 8 and w_past == WINDOW and past % SEL_BLOCK == 0
    wp = _prep_weights(rel_bias, w_in, conv_w_pw, gla_w_gk, nsa_q_norm_g, nsa_k_norm_g, nsa_pe_cmp, nsa_w_cmp,
                       nsa_b_gate, w_out)
    table = rel_bias
    nblk_p = t_p // SEL_BLOCK
    nblk_s = (past + SEL_BLOCK) // SEL_BLOCK
    pos_p = [np.arange(i * LANES, (i + 1) * LANES) for i in range(t_p // LANES)]
    bk_p, sbase_p, useimp_p = _cmp_tables(pos_p, nblk_p)
    bias_p = _head_rows(_bias_table(table, bk_p, sub_far=False), LANES)
    nb_s = LANES // t_s
    qpos_s = past + np.arange(t_s)
    bk_s, _, _ = _cmp_tables([qpos_s], nblk_s)
    bias_s = _head_rows(_bias_table(table, bk_s, sub_far=False), t_s)
    _, sbase_1, useimp_1 = _cmp_tables([np.tile(qpos_s, nb_s)], nblk_s)
    tq = np.arange(TQ)
    every = np.ones((1, 1), bool)
    bk_dt = np.concatenate([_bucket_matrix(tq, tq, every).T, _bucket_matrix(TQ + tq, tq, every).T])
    dt = _bias_table(table, bk_dt, sub_far=True)
    dt = dt.reshape(G_KV, HPG, 2, TQ, TQ).transpose(2, 0, 3, 1, 4).reshape(2, G_KV, TQ, HPG * TQ)
    edge = jnp.asarray(np.tile(np.where(tq[:, None] > tq[None, :], 0.0, NEG).astype(np.float32), (1, HPG)))
    n_keys = past + LANES
    kpos = np.arange(n_keys)
    dsl = _head_rows(_bias_table(table, _bucket_matrix(qpos_s, kpos, (kpos < past + t_s)[None, :]),
                                 sub_far=True), t_s)[0]
    n_wkeys = w_past + LANES
    wk = np.arange(n_wkeys)
    wpos = past - w_past + wk
    wvalid = (wk < w_past + t_s)[None, :] & ((qpos_s[:, None] - wpos[None, :]) < WINDOW) & (wpos >= 0)[None, :]
    dwn = _head_rows(_bias_table(table, _bucket_matrix(qpos_s, wpos, wvalid), sub_far=True), t_s)[0]

    cache_cmp = cache_cmp_kv.transpose(0, 1, 3, 4, 5, 2)
    cache_slc = cache_slc_kv.transpose(0, 1, 3, 4, 5, 2)
    win_state = state_win_kv.transpose(0, 1, 3, 4, 5, 2)
    s0 = state_gla.transpose(0, 1, 2, 4, 3).reshape(depth, n_bs, W_GLA, DK_GLA)
    i256 = np.arange(W_GLA)
    i128 = np.arange(LANES)
    bdmask = jnp.asarray((i256[:, None] // DV_GLA == i128[None, :] // DK_GLA).astype(np.float32))
    s0 = jnp.tile(s0, (1, 1, 1, H_GLA)) * bdmask
    zero_s0 = jnp.zeros((n_bp, W_GLA, LANES), F32)
    zero_hist = jnp.zeros((n_bp, HIST, W_CONV), F32)

    def unpack_state(st):
        b = st.shape[0]
        blocks = [st[:, DV_GLA * h:DV_GLA * (h + 1), DK_GLA * h:DK_GLA * (h + 1)] for h in range(H_GLA)]
        return jnp.stack(blocks, axis=1).transpose(0, 1, 3, 2)

    xp = x_prompt.reshape(n_bp * t_p, D_MODEL)
    xs = x_sample.reshape(n_bs * t_s, D_MODEL)
    outs_p, outs_s = [], []
    for l in range(depth):
        final = l == depth - 1
        row = lambda a: a[l][None, :]
        common = dict(wdw=conv_w_dw[l], bdw=row(conv_b_dw), lng=row(conv_ln_g), lnb=row(conv_ln_b), wpw=wp["wpw"][l],
                      bpw=row(conv_b_pw))
        bgk = jnp.zeros((1, LANES), F32).at[0, :].set(gla_b_gk[l])
        gng = jnp.tile(gla_norm_g[l], (H_GLA,))[None, :]
        kg12 = wp["kg"][l, 1:3]
        kg0 = wp["kg"][l, 0:1]
        zc, zg, qn, cmp_n, slc_n, win_n, ngate, ng, ksa, vs, kwa, vw = _in_proj(
            xp, row(norm_g), wp["w_in"][l], wp["indq"], wp["indk"], wp["qg"][l], kg12, seq_len=t_p, prompt=True)
        yc, conv_st = _conv(zc, zero_hist, nb=1, t_len=t_p, **common)
        yg, gla_st = _gla(zg, zero_s0, wp["wgk"][l], bgk, gng, nb=1, t_len=t_p, chunk=16)
        kvc = _compress_prompt(cmp_n, wp["wbd"][l], wp["pe"][l], n_b=n_bp, t_len=t_p)
        *q_ts, ocmp = _cmpsel(qn, kvc, bias_p, sbase_p, useimp_p, wp["indk"], kg0, wp["cvec"], nb=1, tb=LANES,
                              n_blocks=nblk_p, q_t=True)
        yn = _flash(q_ts, ksa, vs, kwa, vw, dt, edge, ocmp, ng, ngate, wp["bg"][l], wp["expand"], n_b=n_bp, t_len=t_p)
        xp = _out_proj(xp, yc, yg, yn, wp["w_out"][l], norm_f[None, :], final=final)
        w_keep = min(WINDOW, t_p)
        outs_p.append((conv_st, unpack_state(gla_st),
                       win_n.reshape(n_bp, t_p, 256)[:, t_p - w_keep:], cmp_n, slc_n))
        zc, zg, qn, cmp_n, slc_n, win_n, ngate, ng = _in_proj(
            xs, row(norm_g), wp["w_in"][l], wp["indq"], wp["indk"], wp["qg"][l], kg12, seq_len=t_s, prompt=False)
        yc, conv_st = _conv(zc, state_conv[l], nb=nb_s, t_len=t_s, **common)
        yg, gla_st = _gla(zg, s0[l], wp["wgk"][l], bgk, gng, nb=nb_s, t_len=t_s, chunk=t_s)
        kvc = _compress_paged(page_table, cache_cmp, l, cmp_n, wp["wtap"][l], wp["petap"][l], t_new=t_s)
        qa, ocmp = _cmpsel(qn, kvc, bias_s, sbase_1, useimp_1, wp["indk"], kg0, wp["cvec"], nb=nb_s, tb=t_s,
                           n_blocks=nblk_s, q_t=False)
        yn, win_st = _dec_attn(page_table, cache_slc, l, qa, slc_n, win_n, win_state, dsl, dwn, ocmp, ng, ngate,
                               wp["bg"][l], wp["expand"], t_new=t_s)
        xs = _out_proj(xs, yc, yg, yn, wp["w_out"][l], norm_f[None, :], final=final)
        outs_s.append((conv_st, unpack_state(gla_st), win_st, cmp_n, slc_n))

    def stack(outs, k, shape):
        return jnp.stack([o[k] for o in outs]).reshape(shape)

    kv = (2, G_KV, HD)
    return (xp.reshape(n_bp, t_p, D_MODEL), xs.reshape(n_bs, t_s, D_MODEL),
            stack(outs_p, 0, (depth, n_bp, HIST, W_CONV)), stack(outs_s, 0, (depth, n_bs, HIST, W_CONV)),
            stack(outs_p, 1, (depth, n_bp, H_GLA, DK_GLA, DV_GLA)), stack(outs_s, 1, (depth, n_bs, H_GLA, DK_GLA, DV_GLA)),
            stack(outs_p, 2, (depth, n_bp, min(WINDOW, t_p)) + kv), jnp.stack([o[2] for o in outs_s]).transpose(0, 1, 5, 2, 3, 4),
            stack(outs_p, 3, (depth, n_bp, t_p) + kv), stack(outs_s, 3, (depth, n_bs, t_s) + kv),
            stack(outs_p, 4, (depth, n_bp, t_p) + kv), stack(outs_s, 4, (depth, n_bs, t_s) + kv))
```

```python
import functools
import math

import numpy as np
import jax
import jax.numpy as jnp
from jax import lax
from jax.experimental import pallas as pl
from jax.experimental.pallas import tpu as pltpu

F32 = jnp.float32
BF16 = jnp.bfloat16

D_MODEL = 1024
W_CONV = 256
CONV_WIDTH = 31
HIST = CONV_WIDTH - 1
H_GLA = 4
DK_GLA = 32
DV_GLA = 64
W_GLA = H_GLA * DV_GLA
GATE_RANK = 16
GATE_NORMALIZER = 16.0
H_NSA = 8
HD = 64
G_KV = 2
HPG = H_NSA // G_KV
W_NSA = H_NSA * HD
CMP_BLOCK = 32
SEL_BLOCK = 64
TOP_N = 8
WINDOW = 512
N_BUCKETS = 32
MAX_EXACT = N_BUCKETS // 2
MAX_DISTANCE = 128
PAGE_SIZE = 128
SCALE = HD ** -0.5
EPS = 1e-6
NEG = -1e30
FORCE = 1e4

LANES = 128
SEL_LANE0 = 64
SEL_SLOTS = 40
CONST_LANE = 112
ODD_SLOT0 = 64
TQ = 256
V_ROWS = 128
KEY_CHUNK = 256
COL_CHUNK = 128
VMEM_LIMIT = 56 * 1024 * 1024

C_CONV = 0
C_GLA = 3 * W_CONV
N_GLA = 2 * H_GLA * DK_GLA + 2 * W_GLA + LANES
C_NSA = C_GLA + N_GLA
N_NSA = W_NSA + 3 * 256 + W_NSA + LANES
N_IN = C_NSA + N_NSA


def _rel_bucket_np(dist):
    n = np.maximum(dist, 0)
    nf = np.maximum(n, 1).astype(np.float32)
    large = MAX_EXACT + (np.log(nf / np.float32(MAX_EXACT)) / np.float32(math.log(MAX_DISTANCE / MAX_EXACT))
                         * np.float32(N_BUCKETS - MAX_EXACT)).astype(np.int32)
    large = np.minimum(large, N_BUCKETS - 1)
    return np.where(n < MAX_EXACT, n, large).astype(np.int32)


def _in_proj_columns():
    o = {}
    off = 0
    for name, w in (("c_a", 256), ("c_b", 256), ("c_gate", 256), ("l_q", 128), ("l_k", 128), ("l_v", 256),
                    ("l_gk", 16), ("l_gate", 256), ("n_q", 512), ("n_cmp", 256), ("n_slc", 256),
                    ("n_win", 256), ("n_g", 24), ("n_gate", 512)):
        o[name] = off
        off += w
    cols = -np.ones((N_IN,), np.int64)

    def put(dst, name, width):
        cols[dst:dst + width] = o[name] + np.arange(width)

    put(0, "c_a", 256); put(256, "c_b", 256); put(512, "c_gate", 256)
    g = C_GLA
    put(g, "l_q", 128); put(g + 128, "l_k", 128); put(g + 256, "l_v", 256); put(g + 512, "l_gate", 256)
    put(g + 768, "l_gk", 16)
    n = C_NSA
    put(n, "n_q", 512); put(n + 512, "n_cmp", 256); put(n + 768, "n_slc", 256); put(n + 1024, "n_win", 256)
    cols[n + 1280:n + 1792] = o["n_gate"] + _nsa_perm()
    put(n + 1792, "n_g", 24)
    return cols


def _nsa_perm():
    p = np.zeros((W_NSA,), np.int64)
    for j in range(HPG):
        for g in range(G_KV):
            p[128 * j + 64 * g:128 * j + 64 * g + 64] = 64 * (HPG * g + j) + np.arange(64)
    return p


def _blockdiag_ones(n, blk):
    i = np.arange(n)
    return (i[:, None] // blk == i[None, :] // blk).astype(np.float32)


def _split2(x):
    hi = x.astype(BF16)
    lo = (x - hi.astype(F32)).astype(BF16)
    return hi, lo


def _dot(a, b):
    return jnp.dot(a, b, preferred_element_type=F32)


def _dot_nt(a, b):
    return lax.dot_general(a, b, (((1,), (1,)), ((), ())), preferred_element_type=F32)


def _dot_tn(a, b):
    return lax.dot_general(a, b, (((0,), (0,)), ((), ())), preferred_element_type=F32)


def _dot_split(x, m_bf16):
    hi, lo = _split2(x)
    return _dot(hi, m_bf16) + _dot(lo, m_bf16)


def _seg_rms(x, ind_ref, seg):
    return lax.rsqrt(_dot_split(x * x, ind_ref[...]) * (1.0 / seg) + EPS)


def _sigmoid(x):
    return 1.0 / (1.0 + jnp.exp(-x))


def _silu(x):
    return x * _sigmoid(x)


def _lane(shape):
    return lax.broadcasted_iota(jnp.int32, shape, len(shape) - 1)


def _half_pad(x, odd):
    if odd:
        x = pltpu.roll(x, 64, axis=1)
    return jnp.where(_lane(x.shape) < 64, x, 0.0)


def _inproj_body(x_ref, ng_ref, w_ref, indq_ref, indk_ref, qg_ref, kg_ref, *outs, tm, seq_len, prompt):
    zc_ref, zg_ref, qn_ref, cmp_ref, slc_ref, win_ref, gate_ref, g_ref = outs[:8]
    x = x_ref[...]
    ms = jnp.mean(x * x, axis=-1, keepdims=True)
    h = (x * lax.rsqrt(ms + EPS) * ng_ref[...]).astype(BF16)

    def mm(lo, width):
        return _dot(h, w_ref[:, lo:lo + width])

    zc_ref[...] = mm(C_CONV, 3 * W_CONV)
    zg_ref[...] = mm(C_GLA, N_GLA)
    q = mm(C_NSA, W_NSA)
    qn_ref[...] = (q * _seg_rms(q, indq_ref, HD) * qg_ref[...]).astype(BF16)
    cmp_ref[...] = mm(C_NSA + 512, 256)
    slc = mm(C_NSA + 768, 256)
    win = mm(C_NSA + 1024, 256)
    ks = slc[:, 0:128]
    ks = ks * _seg_rms(ks, indk_ref, HD) * kg_ref[0:1, :]
    kw = win[:, 0:128]
    kw = kw * _seg_rms(kw, indk_ref, HD) * kg_ref[1:2, :]
    slc_ref[:, 0:128] = ks
    slc_ref[:, 128:256] = slc[:, 128:256]
    win_ref[:, 0:128] = kw
    win_ref[:, 128:256] = win[:, 128:256]
    gate_ref[...] = mm(C_NSA + 1280, W_NSA)
    g_ref[...] = mm(C_NSA + 1792, LANES)
    if prompt:
        ksa_ref, vs_ref, kwa_ref, vw_ref = outs[8:]
        lane = _lane((tm, LANES))
        row = lax.broadcasted_iota(jnp.int32, (tm, LANES), 0)
        t = lax.rem(pl.program_id(0) * tm + row, seq_len)
        ones = ((lane == CONST_LANE) | (lane == CONST_LANE + 1)).astype(F32)
        onehot = ((lane - SEL_LANE0) == lax.shift_right_logical(t, 6)).astype(F32)
        for g in range(G_KV):
            ksa_ref[g] = (_half_pad(ks, g == 1) + onehot + ones).astype(BF16)
            kwa_ref[g] = (_half_pad(kw, g == 1) + ones).astype(BF16)
        feat = lax.broadcasted_iota(jnp.int32, (LANES, tm), 0)
        for v_ref, src in ((vs_ref, slc), (vw_ref, win)):
            vt = src[:, 128:256].T
            for g in range(G_KV):
                own = vt if g == 0 else pltpu.roll(vt, HD, axis=0)
                v_ref[g] = jnp.where(feat < HD, own, (feat == HD).astype(F32))[0:V_ROWS].astype(BF16)


def _in_proj(x2d, norm_g, w, indq, indk, qg, kg, *, seq_len, prompt):
    m = x2d.shape[0]
    tm = min(256, m)
    assert m % tm == 0 and (not prompt or seq_len % tm == 0)
    row = lambda n: pl.BlockSpec((tm, n), lambda i: (i, 0))
    full = lambda a: pl.BlockSpec(a.shape, lambda i: (0,) * a.ndim)
    out_shape = [jax.ShapeDtypeStruct((m, 3 * W_CONV), F32), jax.ShapeDtypeStruct((m, N_GLA), F32),
                 jax.ShapeDtypeStruct((m, W_NSA), BF16), jax.ShapeDtypeStruct((m, 256), F32),
                 jax.ShapeDtypeStruct((m, 256), F32), jax.ShapeDtypeStruct((m, 256), F32),
                 jax.ShapeDtypeStruct((m, W_NSA), F32), jax.ShapeDtypeStruct((m, LANES), F32)]
    out_specs = [row(3 * W_CONV), row(N_GLA), row(W_NSA), row(256), row(256), row(256), row(W_NSA), row(LANES)]
    if prompt:
        aug = pl.BlockSpec((G_KV, tm, LANES), lambda i: (0, i, 0))
        assert tm == TQ
        vt = pl.BlockSpec((None, G_KV, V_ROWS, tm), lambda i: (i, 0, 0, 0))
        vt_shape = jax.ShapeDtypeStruct((m // tm, G_KV, V_ROWS, tm), BF16)
        out_shape += [jax.ShapeDtypeStruct((G_KV, m, LANES), BF16), vt_shape,
                      jax.ShapeDtypeStruct((G_KV, m, LANES), BF16), vt_shape]
        out_specs += [aug, vt, aug, vt]
    return pl.pallas_call(
        functools.partial(_inproj_body, tm=tm, seq_len=seq_len, prompt=prompt),
        grid=(m // tm,),
        in_specs=[row(D_MODEL), full(norm_g), full(w), full(indq), full(indk), full(qg), full(kg)],
        out_specs=out_specs, out_shape=out_shape,
        compiler_params=pltpu.CompilerParams(dimension_semantics=("arbitrary",), vmem_limit_bytes=VMEM_LIMIT),
        name="in_proj",
    )(x2d, norm_g, w, indq, indk, qg, kg)


def _outproj_body(x_ref, yc_ref, yg_ref, yn_ref, w_ref, nf_ref, o_ref, *, final):
    x = x_ref[...]
    x = (x + _dot(yc_ref[...], w_ref[0:256, :]) + _dot(yg_ref[...], w_ref[256:512, :])
         + _dot(yn_ref[...].astype(BF16), w_ref[512:1024, :]))
    if final:
        ms = jnp.mean(x * x, axis=-1, keepdims=True)
        x = x * lax.rsqrt(ms + EPS) * nf_ref[...]
    o_ref[...] = x


def _out_proj(x2d, yc, yg, yn, w, norm_f, *, final):
    m = x2d.shape[0]
    tm = min(512, m)
    assert m % tm == 0
    row = lambda n: pl.BlockSpec((tm, n), lambda i: (i, 0))
    full = lambda a: pl.BlockSpec(a.shape, lambda i: (0,) * a.ndim)
    return pl.pallas_call(
        functools.partial(_outproj_body, final=final),
        grid=(m // tm,),
        in_specs=[row(D_MODEL), row(256), row(256), row(512), full(w), full(norm_f)],
        out_specs=row(D_MODEL), out_shape=jax.ShapeDtypeStruct((m, D_MODEL), F32),
        compiler_params=pltpu.CompilerParams(dimension_semantics=("arbitrary",), vmem_limit_bytes=VMEM_LIMIT),
        name="out_proj",
    )(x2d, yc, yg, yn, w, norm_f)


def _conv_body(zc_ref, hist_ref, wdw_ref, bdw_ref, lng_ref, lnb_ref, wpw_ref, bpw_ref, y_ref, st_ref,
               ext_scr, act_scr, *, nb, t_len, tc, tmm):
    n_chunks = t_len // tc

    def per_batch(bi, _):
        base = bi * t_len
        ext_scr[0:32, :] = jnp.zeros((32, W_CONV), F32)
        ext_scr[2:32, :] = hist_ref[bi]

        def chunk(c, _):
            r0 = pl.multiple_of(c * tc, tc)
            g0 = pl.multiple_of(base + r0, tc)
            a = zc_ref[pl.ds(g0, tc), 0:256]
            b = zc_ref[pl.ds(g0, tc), 256:512]
            ext_scr[pl.ds(32 + r0, tc), :] = a * _sigmoid(b)
            win = ext_scr[pl.ds(r0, tc + 32), :]
            acc = jnp.zeros((tc, W_CONV), F32) + bdw_ref[...]
            for r in range(8):
                taps = [j for j in range(r, CONV_WIDTH, 8)]
                wr = win[2 + r:2 + r + tc + 8 * (len(taps) - 1)]
                for mi, j in enumerate(taps):
                    acc = acc + wr[8 * mi:8 * mi + tc] * wdw_ref[j:j + 1, :]
            mu = jnp.mean(acc, axis=-1, keepdims=True)
            d = acc - mu
            var = jnp.mean(d * d, axis=-1, keepdims=True)
            yn = d * lax.rsqrt(var + EPS) * lng_ref[...] + lnb_ref[...]
            act_scr[pl.ds(g0, tc), :] = _silu(yn)
            return 0

        lax.fori_loop(0, n_chunks, chunk, 0)
        st_ref[bi] = ext_scr[2 + t_len:32 + t_len, :]
        return 0

    lax.fori_loop(0, nb, per_batch, 0)

    def mm(c, _):
        r0 = pl.multiple_of(c * tmm, tmm)
        y = _dot(act_scr[pl.ds(r0, tmm), :].astype(BF16), wpw_ref[...]) + bpw_ref[...]
        y_ref[pl.ds(r0, tmm), :] = (y * _silu(zc_ref[pl.ds(r0, tmm), 512:768])).astype(BF16)
        return 0

    lax.fori_loop(0, nb * t_len // tmm, mm, 0)


def _conv(zc, hist, wdw, bdw, lng, lnb, wpw, bpw, *, nb, t_len):
    m = zc.shape[0]
    n_b = m // t_len
    tc = min(64, t_len)
    tmm = min(256, nb * t_len)
    full = lambda a: pl.BlockSpec(a.shape, lambda i: (0,) * a.ndim)
    return pl.pallas_call(
        functools.partial(_conv_body, nb=nb, t_len=t_len, tc=tc, tmm=tmm),
        grid=(n_b // nb,),
        in_specs=[pl.BlockSpec((nb * t_len, 3 * W_CONV), lambda i: (i, 0)),
                  pl.BlockSpec((nb, HIST, W_CONV), lambda i: (i, 0, 0)),
                  full(wdw), full(bdw), full(lng), full(lnb), full(wpw), full(bpw)],
        out_specs=[pl.BlockSpec((nb * t_len, W_CONV), lambda i: (i, 0)),
                   pl.BlockSpec((nb, HIST, W_CONV), lambda i: (i, 0, 0))],
        out_shape=[jax.ShapeDtypeStruct((m, W_CONV), BF16), jax.ShapeDtypeStruct((n_b, HIST, W_CONV), F32)],
        scratch_shapes=[pltpu.VMEM((32 + t_len, W_CONV), F32), pltpu.VMEM((nb * t_len, W_CONV), F32)],
        compiler_params=pltpu.CompilerParams(dimension_semantics=("arbitrary",), vmem_limit_bytes=VMEM_LIMIT),
        name="conv",
    )(zc, hist, wdw, bdw, lng, lnb, wpw, bpw)


def _gla_body(zg_ref, s0_ref, wgk_ref, bgk_ref, ltri_ref, lsum_ref, ind_ref, indv_ref, ng_ref, bd_ref,
              y_ref, st_ref, bc_scr, qe_scr, ke_scr, dec_scr, o_scr, *, nb, t_len, chunk, rb):
    rows = nb * t_len
    n_rb = rows // rb
    cpb = t_len // chunk

    def phase1(i, _):
        r0 = pl.multiple_of(i * rb, rb)
        gk = zg_ref[pl.ds(r0, rb), 768:896]
        pre = _dot_split(gk, wgk_ref[0]) + _dot(gk.astype(BF16), wgk_ref[1]) + bgk_ref[...]
        la = (jnp.minimum(pre, 0.0) - jnp.log(1.0 + jnp.exp(-jnp.abs(pre)))) * (1.0 / GATE_NORMALIZER)
        h1, l1 = _split2(la)
        l2 = (la - h1.astype(F32) - l1.astype(F32)).astype(BF16)
        bc = _dot(ltri_ref[...], h1) + _dot(ltri_ref[...], l1) + _dot(ltri_ref[...], l2)
        bt = _dot(lsum_ref[...], h1) + _dot(lsum_ref[...], l1) + _dot(lsum_ref[...], l2)
        bc_scr[pl.ds(r0, rb), :] = bc
        qe_scr[pl.ds(r0, rb), :] = zg_ref[pl.ds(r0, rb), 0:128] * (DK_GLA ** -0.5) * jnp.exp(bc)
        ke_scr[pl.ds(r0, rb), :] = zg_ref[pl.ds(r0, rb), 128:256] * jnp.exp(bt - bc)
        dec_scr[pl.ds(r0, rb), :] = jnp.exp(bt)
        return 0

    lax.fori_loop(0, n_rb, phase1, 0)

    ti = lax.broadcasted_iota(jnp.int32, (chunk, chunk, LANES), 0)
    si = lax.broadcasted_iota(jnp.int32, (chunk, chunk, LANES), 1)
    causal = si <= ti

    u2 = 2 if (rows // chunk) % 2 == 0 else 1
    u3 = 4 if cpb % 4 == 0 else 1

    def phase2(cg, _):
        for u in range(u2):
            r0 = pl.multiple_of((cg * u2 + u) * chunk, chunk)
            bc = bc_scr[pl.ds(r0, chunk), :]
            q = zg_ref[pl.ds(r0, chunk), 0:128] * (DK_GLA ** -0.5)
            k = zg_ref[pl.ds(r0, chunk), 128:256]
            v = zg_ref[pl.ds(r0, chunk), 256:512]
            e = jnp.exp(jnp.where(causal, bc[:, None, :] - bc[None, :, :], NEG))
            p = (q[:, None, :] * k[None, :, :] * e).reshape(chunk * chunk, LANES)
            att = _dot_split(p, ind_ref[...]).reshape(chunk, chunk, W_GLA)
            o_scr[pl.ds(r0, chunk), :] = (att * v[None, :, :]).sum(axis=1)
        return 0

    lax.fori_loop(0, rows // chunk // u2, phase2, 0)

    def per_batch(bi, _):
        def group(cg, s):
            for u in range(u3):
                r0 = pl.multiple_of(bi * t_len + (cg * u3 + u) * chunk, chunk)
                o_scr[pl.ds(r0, chunk), :] += _dot_nt(qe_scr[pl.ds(r0, chunk), :].astype(BF16), s.astype(BF16))
                upd = _dot_tn(zg_ref[pl.ds(r0, chunk), 256:512].astype(BF16),
                              ke_scr[pl.ds(r0, chunk), :].astype(BF16))
                s = s * dec_scr[pl.ds(r0, 1), :] + upd * bd_ref[...]
            return s

        st_ref[bi] = lax.fori_loop(0, cpb // u3, group, s0_ref[bi])
        return 0

    lax.fori_loop(0, nb, per_batch, 0)

    def phase4(i, _):
        r0 = pl.multiple_of(i * rb, rb)
        o = o_scr[pl.ds(r0, rb), :]
        o = o * _seg_rms(o, indv_ref, DV_GLA) * ng_ref[...]
        y_ref[pl.ds(r0, rb), :] = (o * _silu(zg_ref[pl.ds(r0, rb), 512:768])).astype(BF16)
        return 0

    lax.fori_loop(0, n_rb, phase4, 0)


def _gla(zg, s0t, wgk, bgk, ng, *, nb, t_len, chunk):
    m = zg.shape[0]
    n_b = m // t_len
    rows = nb * t_len
    rb = min(128, rows)
    assert rows % rb == 0 and rb % chunk == 0 and t_len % chunk == 0
    ltri = jnp.asarray(_blockdiag_ones(rb, chunk) * np.tril(np.ones((rb, rb), np.float32)), BF16)
    lsum = jnp.asarray(_blockdiag_ones(rb, chunk), BF16)
    i128 = np.arange(LANES)
    i256 = np.arange(W_GLA)
    ind = jnp.asarray((i128[:, None] // DK_GLA == i256[None, :] // DV_GLA).astype(np.float32), BF16)
    indv = jnp.asarray(_blockdiag_ones(W_GLA, DV_GLA), BF16)
    bd = jnp.asarray((i256[:, None] // DV_GLA == i128[None, :] // DK_GLA).astype(np.float32))
    full = lambda a: pl.BlockSpec(a.shape, lambda i: (0,) * a.ndim)
    return pl.pallas_call(
        functools.partial(_gla_body, nb=nb, t_len=t_len, chunk=chunk, rb=rb),
        grid=(n_b // nb,),
        in_specs=[pl.BlockSpec((rows, N_GLA), lambda i: (i, 0)),
                  pl.BlockSpec((nb, W_GLA, LANES), lambda i: (i, 0, 0)),
                  full(wgk), full(bgk), full(ltri), full(lsum), full(ind), full(indv), full(ng), full(bd)],
        out_specs=[pl.BlockSpec((rows, W_GLA), lambda i: (i, 0)),
                   pl.BlockSpec((nb, W_GLA, LANES), lambda i: (i, 0, 0))],
        out_shape=[jax.ShapeDtypeStruct((m, W_GLA), BF16), jax.ShapeDtypeStruct((n_b, W_GLA, LANES), F32)],
        scratch_shapes=[pltpu.VMEM((rows, LANES), F32), pltpu.VMEM((rows, LANES), F32),
                        pltpu.VMEM((rows, LANES), F32), pltpu.VMEM((rows, LANES), F32),
                        pltpu.VMEM((rows, W_GLA), F32)],
        compiler_params=pltpu.CompilerParams(dimension_semantics=("arbitrary",), vmem_limit_bytes=VMEM_LIMIT),
        name="gla",
    )(zg, s0t, wgk, bgk, ltri, lsum, ind, indv, ng, bd)


KW = CMP_BLOCK * 256


def _compress_rows(x_ref, w_ref, pe_ref, kvc_ref, n_rows):
    kvc_ref[...] = jnp.zeros(kvc_ref.shape, F32)
    for par in range(2):
        x = (x_ref[:, par * KW:(par + 1) * KW] + pe_ref[...]).astype(BF16)
        kvc_ref[ODD_SLOT0 * par:ODD_SLOT0 * par + n_rows, :] = _dot(x, w_ref[...])


def _compress_prompt_body(x_ref, w_ref, pe_ref, kvc_ref, *, n_rows):
    _compress_rows(x_ref, w_ref, pe_ref, kvc_ref, n_rows)


def _compress_prompt(cmp_rows, w, pe, *, n_b, t_len):
    n_rows = t_len // SEL_BLOCK
    full = lambda a: pl.BlockSpec(a.shape, lambda i: (0,) * a.ndim)
    return pl.pallas_call(
        functools.partial(_compress_prompt_body, n_rows=n_rows),
        grid=(n_b,),
        in_specs=[pl.BlockSpec((n_rows, 2 * KW), lambda i: (i, 0)), full(w), full(pe)],
        out_specs=pl.BlockSpec((None, LANES, 256), lambda i: (i, 0, 0)),
        out_shape=jax.ShapeDtypeStruct((n_b, LANES, 256), F32),
        compiler_params=pltpu.CompilerParams(dimension_semantics=("arbitrary",), vmem_limit_bytes=VMEM_LIMIT),
        name="compress_prompt",
    )(cmp_rows.reshape(n_b * n_rows, 2 * KW), w, pe)


def _compress_paged_body(pt_ref, *refs, n_pages, t_new, n_rows):
    pages = refs[:n_pages]
    new_ref, w_ref, pe_ref, kvc_ref, x_scr, acc_scr = refs[n_pages:]
    past = n_pages * PAGE_SIZE
    new = new_ref[...]
    for s in range(2):
        for p in range(n_pages):
            x_scr[s, p * PAGE_SIZE:(p + 1) * PAGE_SIZE, :] = pages[p][s].reshape(LANES, PAGE_SIZE).T
        x_scr[s, past:, :] = jnp.zeros((n_rows * SEL_BLOCK - past, LANES), F32)
        x_scr[s, past:past + t_new, :] = new[:, LANES * s:LANES * (s + 1)]
    kvc_ref[...] = jnp.zeros(kvc_ref.shape, F32)
    for s in range(2):
        acc = jnp.zeros((2 * n_rows, LANES), F32)
        for j in range(CMP_BLOCK):
            x = x_scr[s, pl.ds(j, 2 * n_rows, stride=CMP_BLOCK), :] + pe_ref[s, j:j + 1, :]
            acc = acc + _dot(x.astype(BF16), w_ref[s, j])
        acc_scr[...] = acc
        for par in range(2):
            kvc_ref[ODD_SLOT0 * par:ODD_SLOT0 * par + n_rows, LANES * s:LANES * (s + 1)] = (
                acc_scr[pl.ds(par, n_rows, stride=2), :])


def _paged_specs(layer, n_pages, block):
    return [pl.BlockSpec((None, None) + block, functools.partial(
        lambda b, pt, p: (layer, pt[b, p]) + (0,) * len(block), p=p)) for p in range(n_pages)]


PAGE_BLOCK = (2, G_KV, HD, PAGE_SIZE)


def _compress_paged(page_table, cache_t, layer, cmp_new, w, pe, *, t_new):
    n_b, n_pages = page_table.shape
    n_rows = -(-((n_pages * PAGE_SIZE + SEL_BLOCK) // SEL_BLOCK) // 8) * 8
    full = lambda a: pl.BlockSpec(a.shape, lambda b, pt: (0,) * a.ndim)
    grid_spec = pltpu.PrefetchScalarGridSpec(
        num_scalar_prefetch=1, grid=(n_b,),
        in_specs=_paged_specs(layer, n_pages, PAGE_BLOCK) + [
            pl.BlockSpec((t_new, 256), lambda b, pt: (b, 0)), full(w), full(pe)],
        out_specs=pl.BlockSpec((None, LANES, 256), lambda b, pt: (b, 0, 0)),
        scratch_shapes=[pltpu.VMEM((2, n_rows * SEL_BLOCK, LANES), F32), pltpu.VMEM((2 * n_rows, LANES), F32)])
    return pl.pallas_call(
        functools.partial(_compress_paged_body, n_pages=n_pages, t_new=t_new, n_rows=n_rows),
        grid_spec=grid_spec, out_shape=jax.ShapeDtypeStruct((n_b, LANES, 256), F32),
        compiler_params=pltpu.CompilerParams(dimension_semantics=("arbitrary",), vmem_limit_bytes=VMEM_LIMIT),
        name="compress_paged",
    )(page_table, *([cache_t] * n_pages), cmp_new, w, pe)


def _cmpsel_body(qn_ref, kvc_ref, bias_ref, sbase_ref, useimp_ref, indk_ref, kg_ref, cvec_ref,
                 *outs, nb, tb, n_blocks, q_t):
    q_refs, ocmp_ref = outs[:-1], outs[-1]
    rows = nb * tb
    lane = _lane((rows, LANES))
    qn = qn_ref[...].astype(F32)
    qpad = [_half_pad(qn[:, 128 * (h // 2):128 * (h // 2) + 128], h % 2 == 1) for h in range(H_NSA)]
    imp_parts = []
    o_parts = [[None] * nb for _ in range(H_NSA)]
    for bi in range(nb):
        kvc = kvc_ref[bi]
        kc = kvc[:, 0:128]
        kc = kc * _seg_rms(kc, indk_ref, HD) * kg_ref[...]
        vboth = kvc[:, 128:256].astype(BF16)
        imp_b = []
        for g in range(G_KV):
            kpad = _half_pad(kc, g == 1).astype(BF16)
            qs = jnp.concatenate([qpad[HPG * g + j][bi * tb:(bi + 1) * tb] for j in range(HPG)], axis=0).astype(BF16)
            bias = bias_ref[g]
            l = _dot_nt(qs, kpad) + bias
            vis = bias > 0.5 * NEG
            mx = jnp.max(l, axis=-1, keepdims=True)
            p = jnp.where(vis, jnp.exp(l - mx), 0.0)
            p = p / jnp.maximum(jnp.sum(p, axis=-1, keepdims=True), 1e-30)
            o = _dot(p.astype(BF16), vboth)
            ig = p[0:tb]
            for j in range(1, HPG):
                ig = ig + p[j * tb:(j + 1) * tb]
            imp_b.append(ig)
            for j in range(HPG):
                o_parts[HPG * g + j][bi] = o[j * tb:(j + 1) * tb]
        imp_parts.append(imp_b)
    for j in range(HPG):
        o0 = jnp.concatenate(o_parts[j], axis=0) if nb > 1 else o_parts[j][0]
        o1 = jnp.concatenate(o_parts[HPG + j], axis=0) if nb > 1 else o_parts[HPG + j][0]
        ocmp_ref[:, 128 * j:128 * j + 128] = jnp.where(lane < 64, o0, o1)
    for g in range(G_KV):
        ig = jnp.concatenate([imp_parts[bi][g] for bi in range(nb)], axis=0) if nb > 1 else imp_parts[0][g]
        imp = ig + pltpu.roll(ig, LANES - ODD_SLOT0, axis=1)
        score = jnp.where(useimp_ref[...] > 0.5, imp, sbase_ref[...])
        st = score.T[0:SEL_SLOTS]
        jj = lax.broadcasted_iota(jnp.int32, (SEL_SLOTS, rows), 0)
        rank = jnp.zeros((SEL_SLOTS, rows), F32)
        for jp in range(n_blocks):
            r = st[jp:jp + 1, :]
            before = (r > st) | ((r == st) & (jp < jj))
            rank = rank + before.astype(F32)
        selb = jnp.where(rank < TOP_N - 0.5, 0.0, NEG)
        full_t = jnp.concatenate([jnp.zeros((SEL_LANE0, rows), F32), selb,
                                  jnp.zeros((LANES - SEL_LANE0 - SEL_SLOTS, rows), F32)], axis=0)
        if q_t:
            for j in range(HPG):
                h = HPG * g + j
                q_refs[j][g] = ((qpad[h] + cvec_ref[h]).T + full_t).astype(BF16)
        else:
            extra = full_t.T
            for j in range(HPG):
                h = HPG * g + j
                q_refs[0][g, j] = (qpad[h] + extra + cvec_ref[h]).astype(q_refs[0].dtype)


def _cmpsel(qn, kvc, bias, sbase, useimp, indk, kg0, cvec, *, nb, tb, n_blocks, q_t):
    m = qn.shape[0]
    rows = nb * tb
    assert rows == LANES and m % rows == 0
    n_pos = bias.shape[0]
    full = lambda a: pl.BlockSpec(a.shape, lambda i: (0,) * a.ndim)
    if q_t:
        q_specs = [pl.BlockSpec((G_KV, LANES, rows), lambda i: (0, 0, i))] * HPG
        q_shapes = [jax.ShapeDtypeStruct((G_KV, LANES, m), BF16)] * HPG
    else:
        q_specs = [pl.BlockSpec((G_KV, HPG, rows, LANES), lambda i: (0, 0, i, 0))]
        q_shapes = [jax.ShapeDtypeStruct((G_KV, HPG, m, LANES), F32)]
    return pl.pallas_call(
        functools.partial(_cmpsel_body, nb=nb, tb=tb, n_blocks=n_blocks, q_t=q_t),
        grid=(m // rows,),
        in_specs=[pl.BlockSpec((rows, W_NSA), lambda i: (i, 0)),
                  pl.BlockSpec((nb, LANES, 256), (lambda i: (i, 0, 0)) if nb > 1 else (lambda i: (i // n_pos, 0, 0))),
                  pl.BlockSpec((None, G_KV, HPG * tb, LANES), lambda i: (i % n_pos, 0, 0, 0)),
                  pl.BlockSpec((None, rows, LANES), lambda i: (i % n_pos, 0, 0)),
                  pl.BlockSpec((None, rows, LANES), lambda i: (i % n_pos, 0, 0)),
                  full(indk), full(kg0), full(cvec)],
        out_specs=q_specs + [pl.BlockSpec((rows, W_NSA), lambda i: (i, 0))],
        out_shape=q_shapes + [jax.ShapeDtypeStruct((m, W_NSA), F32)],
        compiler_params=pltpu.CompilerParams(dimension_semantics=("arbitrary",), vmem_limit_bytes=VMEM_LIMIT),
        name="cmpsel",
    )(qn, kvc, bias, sbase, useimp, indk, kg0, cvec)


def _mix_gates(g_ref, bg_ref, expand_ref, gate_ref, ocmp_ref, o_slc, o_win, y_ref):
    gl = _sigmoid(g_ref[...] + bg_ref[...])
    gx = _dot_split(gl, expand_ref[...])
    for j in range(HPG):
        sl = slice(128 * j, 128 * j + 128)
        o = (gx[:, 128 * j:128 * j + 128] * ocmp_ref[:, sl]
             + gx[:, 128 * (HPG + j):128 * (HPG + j) + 128] * o_slc[j]
             + gx[:, 128 * (2 * HPG + j):128 * (2 * HPG + j) + 128] * o_win[j])
        y_ref[:, sl] = (o * _silu(gate_ref[:, sl])).astype(y_ref.dtype)


def _softmax_step(q_t, k, v_t, bias_t, m_scr, acc_scr):
    n_key, n_col = k.shape[0], q_t.shape[1]
    for k0 in range(0, n_key, KEY_CHUNK):
        ks = slice(k0, k0 + KEY_CHUNK)
        for c0 in range(0, n_col, COL_CHUNK):
            cs = slice(c0, c0 + COL_CHUNK)
            s = _dot(k[ks], q_t[:, cs])
            if bias_t is not None:
                s = s + bias_t(ks, cs)
            m_prev = m_scr[:, cs]
            m_new = jnp.maximum(m_prev, jnp.max(s, axis=0, keepdims=True))
            p = jnp.exp(s - m_new)
            acc_scr[:, cs] = jnp.exp(m_prev - m_new) * acc_scr[:, cs] + _dot(v_t[:, ks], p.astype(BF16))
            m_scr[:, cs] = m_new


def _flash_body(q0_ref, q1_ref, q2_ref, q3_ref, ksa_ref, vs_ref, kwa_ref, vw_ref, dt_ref, edge_ref, ocmp_ref,
                g_ref, gate_ref, bg_ref, expand_ref, y_ref, m_scr, acc_scr):
    i = pl.program_id(1)
    cols = HPG * TQ
    m_scr[...] = jnp.full(m_scr.shape, NEG, F32)
    acc_scr[...] = jnp.zeros(acc_scr.shape, F32)

    def run(n_near):
        for g in range(G_KV):
            q_t = jnp.concatenate([r[g] for r in (q0_ref, q1_ref, q2_ref, q3_ref)], axis=1)
            for br, (ka_ref, v_ref) in enumerate(((ksa_ref, vs_ref), (kwa_ref, vw_ref))):
                def step(kt, bias_t):
                    k0 = pl.multiple_of(kt * TQ, TQ)
                    _softmax_step(q_t, ka_ref[g, pl.ds(k0, TQ), :], v_ref[kt, g], bias_t, m_scr.at[br, g],
                                  acc_scr.at[br, g])

                if n_near == 2:
                    if br == 0:
                        def far_pair(kp, _):
                            step(2 * kp, None)
                            step(2 * kp + 1, None)
                            return 0

                        lax.fori_loop(0, lax.shift_right_logical(i - 1, 1), far_pair, 0)

                        @pl.when(lax.rem(i - 1, 2) == 1)
                        def _():
                            step(i - 2, None)
                    else:
                        step(i - 2, lambda ks, cs: edge_ref[ks, cs])
                if n_near >= 1:
                    step(i - 1, lambda ks, cs: dt_ref[1, g, ks, cs])
                step(i, lambda ks, cs: dt_ref[0, g, ks, cs])

    for n_near, cond in ((0, i == 0), (1, i == 1), (2, i >= 2)):
        pl.when(cond)(functools.partial(run, n_near))

    outs = []
    for br in range(2):
        o_t = jnp.concatenate([acc_scr[br, g, 0:HD, :] / acc_scr[br, g, HD:HD + 1, :] for g in range(G_KV)], axis=0)
        outs.append([o_t[:, j * TQ:(j + 1) * TQ].T for j in range(HPG)])
    _mix_gates(g_ref, bg_ref, expand_ref, gate_ref, ocmp_ref, outs[0], outs[1], y_ref)


def _flash(q_ts, ksa, vs, kwa, vw, dt, edge, ocmp, ng, ngate, bg, expand, *, n_b, t_len):
    m = n_b * t_len
    nq = t_len // TQ
    cols = HPG * TQ
    full = lambda a: pl.BlockSpec(a.shape, lambda b, i: (0,) * a.ndim)
    qspec = pl.BlockSpec((G_KV, LANES, TQ), lambda b, i: (0, 0, b * nq + i))
    kspec = pl.BlockSpec((G_KV, None, t_len, LANES), lambda b, i: (0, b, 0, 0))
    vspec = pl.BlockSpec((nq, G_KV, V_ROWS, TQ), lambda b, i: (b, 0, 0, 0))
    row = lambda n: pl.BlockSpec((TQ, n), lambda b, i: (b * nq + i, 0))
    return pl.pallas_call(
        _flash_body,
        grid=(n_b, nq),
        in_specs=[qspec] * HPG + [kspec, vspec, kspec, vspec, full(dt), full(edge), row(W_NSA), row(LANES),
                                  row(W_NSA), full(bg), full(expand)],
        out_specs=row(W_NSA), out_shape=jax.ShapeDtypeStruct((m, W_NSA), BF16),
        scratch_shapes=[pltpu.VMEM((2, G_KV, 1, cols), F32), pltpu.VMEM((2, G_KV, V_ROWS, cols), F32)],
        compiler_params=pltpu.CompilerParams(dimension_semantics=("arbitrary", "arbitrary"),
                                             vmem_limit_bytes=VMEM_LIMIT),
        name="flash",
    )(*q_ts, ksa.reshape(G_KV, n_b, t_len, LANES), vs, kwa.reshape(G_KV, n_b, t_len, LANES), vw,
      dt, edge, ocmp, ng, ngate, bg, expand)


def _dec_attn_body(pt_ref, *refs, n_pages, t_new, n_keys, n_wkeys):
    pages = refs[:n_pages]
    (qa_ref, slc_ref, win_ref, wst_ref, dsl_ref, dwn_ref, ocmp_ref, g_ref, gate_ref, bg_ref, expand_ref,
     y_ref, wout_ref, ka_scr, va_scr, kw_scr, vw_scr) = refs[n_pages:]
    past = n_pages * PAGE_SIZE
    w_past = wst_ref.shape[-1]
    rows = HPG * t_new

    @pl.when(pl.program_id(0) == 0)
    def _():
        for k_scr, n, with_blocks in ((ka_scr, n_keys, True), (kw_scr, n_wkeys, False)):
            feat = lax.broadcasted_iota(jnp.int32, (LANES, n), 0)
            key = lax.broadcasted_iota(jnp.int32, (LANES, n), 1)
            aug = (feat == CONST_LANE) | (feat == CONST_LANE + 1)
            if with_blocks:
                aug = aug | ((feat - SEL_LANE0) == lax.shift_right_logical(key, 6))
            for g in range(G_KV):
                k_scr[g] = aug.astype(F32).astype(BF16)

    def new_keys_t(new):
        pad = jnp.concatenate([new, jnp.zeros((LANES - t_new, 256), F32)], axis=0)
        return pad[:, 0:128].T, pad[:, 128:256].T

    for p in range(n_pages):
        sl = slice(p * PAGE_SIZE, (p + 1) * PAGE_SIZE)
        for g in range(G_KV):
            ka_scr[g, 0:HD, sl] = pages[p][0, g].astype(BF16)
            va_scr[HD * g:HD * (g + 1), sl] = pages[p][1, g].astype(BF16)
    kn_t, vn_t = new_keys_t(slc_ref[...])
    va_scr[:, past:n_keys] = vn_t.astype(BF16)
    kwn_t, vwn_t = new_keys_t(win_ref[...])
    vw_scr[:, w_past:n_wkeys] = vwn_t.astype(BF16)
    lane = _lane((HD, LANES))
    for g in range(G_KV):
        ka_scr[g, 0:HD, past:n_keys] = kn_t[HD * g:HD * (g + 1)].astype(BF16)
        kw_scr[g, 0:HD, 0:w_past] = wst_ref[0, g].astype(BF16)
        kw_scr[g, 0:HD, w_past:n_wkeys] = kwn_t[HD * g:HD * (g + 1)].astype(BF16)
        vw_scr[HD * g:HD * (g + 1), 0:w_past] = wst_ref[1, g].astype(BF16)
        for s, new_t in ((0, kwn_t), (1, vwn_t)):
            sh = pltpu.roll(wst_ref[s, g], w_past - t_new, axis=1)
            tail = pltpu.roll(new_t[HD * g:HD * (g + 1)], LANES - t_new, axis=1)
            wout_ref[s, g, :, 0:w_past - LANES] = sh[:, 0:w_past - LANES]
            wout_ref[s, g, :, w_past - LANES:w_past] = jnp.where(lane >= LANES - t_new, tail, sh[:, w_past - LANES:])

    res = [[None, None], [None, None]]
    for g in range(G_KV):
        q = qa_ref[g].reshape(rows, LANES).astype(BF16)
        for br, (k_scr, v_scr, b_ref) in enumerate(((ka_scr, va_scr, dsl_ref), (kw_scr, vw_scr, dwn_ref))):
            s = _dot(q, k_scr[g]) + b_ref[g]
            mx = jnp.max(s, axis=-1, keepdims=True)
            p = jnp.exp(s - mx)
            den = jnp.sum(p, axis=-1, keepdims=True)
            res[br][g] = _dot_nt(p.astype(BF16), v_scr[...]) / den
    lane = _lane((t_new, LANES))
    outs = [[jnp.where(lane < 64, res[br][0][j * t_new:(j + 1) * t_new], res[br][1][j * t_new:(j + 1) * t_new])
             for j in range(HPG)] for br in range(2)]
    _mix_gates(g_ref, bg_ref, expand_ref, gate_ref, ocmp_ref, outs[0], outs[1], y_ref)


def _dec_attn(page_table, cache, layer, qa, slc_new, win_new, win_state, dsl, dwn, ocmp, ng, ngate, bg, expand, *, t_new):
    n_b, n_pages = page_table.shape
    m = n_b * t_new
    w_past = win_state.shape[-1]
    n_keys = dsl.shape[-1]
    n_wkeys = dwn.shape[-1]
    wblock = (2, G_KV, HD, w_past)
    full = lambda a: pl.BlockSpec(a.shape, lambda b, pt: (0,) * a.ndim)
    row = lambda n: pl.BlockSpec((t_new, n), lambda b, pt: (b, 0))
    grid_spec = pltpu.PrefetchScalarGridSpec(
        num_scalar_prefetch=1, grid=(n_b,),
        in_specs=_paged_specs(layer, n_pages, PAGE_BLOCK) + [
            pl.BlockSpec((G_KV, HPG, t_new, LANES), lambda b, pt: (0, 0, b, 0)),
            row(256), row(256), pl.BlockSpec((None, None) + wblock, lambda b, pt: (layer, b, 0, 0, 0, 0)),
            full(dsl), full(dwn), row(W_NSA), row(LANES), row(W_NSA), full(bg), full(expand)],
        out_specs=[row(W_NSA), pl.BlockSpec((None,) + wblock, lambda b, pt: (b, 0, 0, 0, 0))],
        scratch_shapes=[pltpu.VMEM((G_KV, LANES, n_keys), BF16), pltpu.VMEM((LANES, n_keys), BF16),
                        pltpu.VMEM((G_KV, LANES, n_wkeys), BF16), pltpu.VMEM((LANES, n_wkeys), BF16)])
    return pl.pallas_call(
        functools.partial(_dec_attn_body, n_pages=n_pages, t_new=t_new, n_keys=n_keys, n_wkeys=n_wkeys),
        grid_spec=grid_spec,
        out_shape=[jax.ShapeDtypeStruct((m, W_NSA), F32), jax.ShapeDtypeStruct((n_b,) + wblock, F32)],
        compiler_params=pltpu.CompilerParams(dimension_semantics=("arbitrary",), vmem_limit_bytes=VMEM_LIMIT),
        name="dec_attn",
    )(page_table, *([cache] * n_pages), qa, slc_new, win_new, win_state, dsl, dwn, ocmp, ng, ngate, bg, expand)


def _bucket_matrix(qpos, kpos, valid):
    dist = qpos[:, None] - kpos[None, :]
    return np.where(valid & (dist >= 0), _rel_bucket_np(dist), -1).astype(np.int32)


def _bias_body(tab_ref, bk_ref, o_ref, *, sub_far):
    bk = bk_ref[...]
    for h in range(H_NSA):
        far = tab_ref[N_BUCKETS - 1, h] if sub_far else 0.0
        acc = jnp.full(bk.shape, NEG, F32)
        for b in range(N_BUCKETS):
            acc = jnp.where(bk == b, tab_ref[b, h] - far, acc)
        o_ref[h] = acc


def _bias_table(table, bk, *, sub_far):
    r, c = bk.shape
    tr = 8 if r <= 8 else 64
    assert r % tr == 0
    return pl.pallas_call(
        functools.partial(_bias_body, sub_far=sub_far),
        grid=(r // tr,),
        in_specs=[pl.BlockSpec(memory_space=pltpu.SMEM), pl.BlockSpec((tr, c), lambda i: (i, 0))],
        out_specs=pl.BlockSpec((H_NSA, tr, c), lambda i: (0, i, 0)),
        out_shape=jax.ShapeDtypeStruct((H_NSA, r, c), F32),
        compiler_params=pltpu.CompilerParams(dimension_semantics=("arbitrary",)),
        name="bias_table",
    )(table, jnp.asarray(bk))


def _cmp_tables(qpos_blocks, n_blocks):
    slot = np.arange(LANES)
    blk = np.where(slot < ODD_SLOT0, 2 * slot, 2 * (slot - ODD_SLOT0) + 1)
    real = (blk < 2 * n_blocks) & ((slot % ODD_SLOT0) < n_blocks)
    blk_end = blk * CMP_BLOCK + CMP_BLOCK - 1
    buckets, sbases, useimps = [], [], []
    for qpos in qpos_blocks:
        buckets.append(_bucket_matrix(qpos, blk_end, real[None, :]))
        j = np.arange(LANES)[None, :]
        cur = (qpos // SEL_BLOCK)[:, None]
        forced = (j == 0) | (j == cur) | (j == cur - 1)
        valid = j * SEL_BLOCK <= qpos[:, None]
        inrange = j < n_blocks
        sbases.append(np.where(~inrange, -2.0, np.where(forced, FORCE, -1.0)).astype(np.float32))
        useimps.append((inrange & ~forced & valid).astype(np.float32))
    return np.concatenate(buckets), jnp.asarray(np.stack(sbases)), jnp.asarray(np.stack(useimps))


def _head_rows(b, t):
    k = b.shape[-1]
    n = b.shape[1] // t
    return b.reshape(G_KV, HPG, n, t, k).transpose(2, 0, 1, 3, 4).reshape(n, G_KV, HPG * t, k)


def _prep_weights(rel_bias, w_in, conv_w_pw, gla_w_gk, nsa_q_norm_g, nsa_k_norm_g, nsa_pe_cmp, nsa_w_cmp,
                  nsa_b_gate, w_out):
    depth = w_in.shape[0]
    cols = _in_proj_columns()
    pieces, start = [], 0
    for c in range(1, N_IN + 1):
        pad = cols[start] < 0
        if c == N_IN or (cols[c] >= 0 if pad else cols[c] != cols[c - 1] + 1):
            src = int(cols[start])
            pieces.append(jnp.zeros((depth, D_MODEL, c - start), w_in.dtype) if pad else w_in[:, :, src:src + c - start])
            start = c
    w_in_p = jnp.concatenate(pieces, axis=2).astype(BF16)
    perm = 512 + _nsa_perm()
    w_out_p = jnp.concatenate([w_out[:, 0:512]] + [w_out[:, int(perm[c]):int(perm[c]) + HD]
                                                   for c in range(0, W_NSA, HD)], axis=1).astype(BF16)
    qg = jnp.tile(nsa_q_norm_g, (1, H_NSA))[:, None, :] * SCALE
    kg = jnp.tile(nsa_k_norm_g, (1, 1, G_KV))
    wgk = jnp.zeros((depth, LANES, LANES), F32).at[:, :GATE_RANK, :].set(gla_w_gk)
    wgk_hi = wgk.astype(BF16)
    wgk2 = jnp.stack([wgk_hi, (wgk - wgk_hi.astype(F32)).astype(BF16)], axis=1)
    def blockdiag2(a, b):
        za = jnp.zeros(a.shape[:-1] + (b.shape[-1],), a.dtype)
        zb = jnp.zeros(b.shape[:-1] + (a.shape[-1],), b.dtype)
        return jnp.concatenate([jnp.concatenate([a, za], axis=-1), jnp.concatenate([zb, b], axis=-1)], axis=-2)

    wc = nsa_w_cmp.astype(BF16)
    wtap = blockdiag2(wc, wc)
    wbd = blockdiag2(wtap[:, 0], wtap[:, 1]).reshape(depth, KW, 256)
    pe = jnp.tile(nsa_pe_cmp.transpose(0, 2, 1, 3)[:, :, :, None, :], (1, 1, 1, G_KV, 1))
    petap = pe.transpose(0, 2, 1, 3, 4).reshape(depth, 2, CMP_BLOCK, LANES)
    pe = pe.reshape(depth, 1, KW)
    bg = jnp.zeros((depth, 1, LANES), F32).at[:, 0, :3 * H_NSA].set(nsa_b_gate)
    expand = np.zeros((LANES, 3 * HPG * LANES), np.float32)
    for br in range(3):
        for g in range(G_KV):
            for j in range(HPG):
                c = LANES * (HPG * br + j) + 64 * g
                expand[H_NSA * br + HPG * g + j, c:c + 64] = 1.0
    far = rel_bias[N_BUCKETS - 1]
    far_hi = far.astype(BF16).astype(F32)
    cvec = jnp.zeros((H_NSA, 1, LANES), F32).at[:, 0, CONST_LANE].set(far_hi).at[:, 0, CONST_LANE + 1].set(far - far_hi)
    return dict(w_in=w_in_p, w_out=w_out_p, qg=qg, kg=kg, wgk=wgk2, wbd=wbd, pe=pe, bg=bg,
                wtap=wtap, petap=petap,
                expand=jnp.asarray(expand, BF16), cvec=cvec, wpw=conv_w_pw.astype(BF16),
                indq=jnp.asarray(_blockdiag_ones(W_NSA, HD), BF16), indk=jnp.asarray(_blockdiag_ones(LANES, HD), BF16))


def kernel(x_prompt, x_sample, cache_cmp_kv, cache_slc_kv, page_table, state_win_kv, state_gla, state_conv, rel_bias, norm_g, w_in, conv_w_dw, conv_b_dw, conv_ln_g, conv_ln_b, conv_w_pw, conv_b_pw, gla_w_gk, gla_b_gk, gla_norm_g, nsa_q_norm_g, nsa_k_norm_g, nsa_pe_cmp, nsa_w_cmp, nsa_b_gate, w_out, norm_f):
    depth = w_in.shape[0]
    n_bp, t_p, _ = x_prompt.shape
    n_bs, t_s, _ = x_sample.shape
    n_pages = page_table.shape[1]
    past = n_pages * PAGE_SIZE
    w_past = state_win_kv.shape[2]
    assert t_p % TQ == 0 and WINDOW == 2 * TQ and t_s == 8 and w_past == WINDOW and past % SEL_BLOCK == 0
    wp = _prep_weights(rel_bias, w_in, conv_w_pw, gla_w_gk, nsa_q_norm_g, nsa_k_norm_g, nsa_pe_cmp, nsa_w_cmp,
                       nsa_b_gate, w_out)
    table = rel_bias
    nblk_p = t_p // SEL_BLOCK
    nblk_s = (past + SEL_BLOCK) // SEL_BLOCK
    pos_p = [np.arange(i * LANES, (i + 1) * LANES) for i in range(t_p // LANES)]
    bk_p, sbase_p, useimp_p = _cmp_tables(pos_p, nblk_p)
    bias_p = _head_rows(_bias_table(table, bk_p, sub_far=False), LANES)
    nb_s = LANES // t_s
    qpos_s = past + np.arange(t_s)
    bk_s, _, _ = _cmp_tables([qpos_s], nblk_s)
    bias_s = _head_rows(_bias_table(table, bk_s, sub_far=False), t_s)
    _, sbase_1, useimp_1 = _cmp_tables([np.tile(qpos_s, nb_s)], nblk_s)
    tq = np.arange(TQ)
    every = np.ones((1, 1), bool)
    bk_dt = np.concatenate([_bucket_matrix(tq, tq, every).T, _bucket_matrix(TQ + tq, tq, every).T])
    dt = _bias_table(table, bk_dt, sub_far=True)
    dt = dt.reshape(G_KV, HPG, 2, TQ, TQ).transpose(2, 0, 3, 1, 4).reshape(2, G_KV, TQ, HPG * TQ)
    edge = jnp.asarray(np.tile(np.where(tq[:, None] > tq[None, :], 0.0, NEG).astype(np.float32), (1, HPG)))
    n_keys = past + LANES
    kpos = np.arange(n_keys)
    dsl = _head_rows(_bias_table(table, _bucket_matrix(qpos_s, kpos, (kpos < past + t_s)[None, :]),
                                 sub_far=True), t_s)[0]
    n_wkeys = w_past + LANES
    wk = np.arange(n_wkeys)
    wpos = past - w_past + wk
    wvalid = (wk < w_past + t_s)[None, :] & ((qpos_s[:, None] - wpos[None, :]) < WINDOW) & (wpos >= 0)[None, :]
    dwn = _head_rows(_bias_table(table, _bucket_matrix(qpos_s, wpos, wvalid), sub_far=True), t_s)[0]

    cache_cmp = cache_cmp_kv.transpose(0, 1, 3, 4, 5, 2)
    cache_slc = cache_slc_kv.transpose(0, 1, 3, 4, 5, 2)
    win_state = state_win_kv.transpose(0, 1, 3, 4, 5, 2)
    s0 = state_gla.transpose(0, 1, 2, 4, 3).reshape(depth, n_bs, W_GLA, DK_GLA)
    i256 = np.arange(W_GLA)
    i128 = np.arange(LANES)
    bdmask = jnp.asarray((i256[:, None] // DV_GLA == i128[None, :] // DK_GLA).astype(np.float32))
    s0 = jnp.tile(s0, (1, 1, 1, H_GLA)) * bdmask
    zero_s0 = jnp.zeros((n_bp, W_GLA, LANES), F32)
    zero_hist = jnp.zeros((n_bp, HIST, W_CONV), F32)

    def unpack_state(st):
        b = st.shape[0]
        blocks = [st[:, DV_GLA * h:DV_GLA * (h + 1), DK_GLA * h:DK_GLA * (h + 1)] for h in range(H_GLA)]
        return jnp.stack(blocks, axis=1).transpose(0, 1, 3, 2)

    xp = x_prompt.reshape(n_bp * t_p, D_MODEL)
    xs = x_sample.reshape(n_bs * t_s, D_MODEL)
    outs_p, outs_s = [], []
    for l in range(depth):
        final = l == depth - 1
        row = lambda a: a[l][None, :]
        common = dict(wdw=conv_w_dw[l], bdw=row(conv_b_dw), lng=row(conv_ln_g), lnb=row(conv_ln_b), wpw=wp["wpw"][l],
                      bpw=row(conv_b_pw))
        bgk = jnp.zeros((1, LANES), F32).at[0, :].set(gla_b_gk[l])
        gng = jnp.tile(gla_norm_g[l], (H_GLA,))[None, :]
        kg12 = wp["kg"][l, 1:3]
        kg0 = wp["kg"][l, 0:1]
        zc, zg, qn, cmp_n, slc_n, win_n, ngate, ng, ksa, vs, kwa, vw = _in_proj(
            xp, row(norm_g), wp["w_in"][l], wp["indq"], wp["indk"], wp["qg"][l], kg12, seq_len=t_p, prompt=True)
        yc, conv_st = _conv(zc, zero_hist, nb=1, t_len=t_p, **common)
        yg, gla_st = _gla(zg, zero_s0, wp["wgk"][l], bgk, gng, nb=1, t_len=t_p, chunk=16)
        kvc = _compress_prompt(cmp_n, wp["wbd"][l], wp["pe"][l], n_b=n_bp, t_len=t_p)
        *q_ts, ocmp = _cmpsel(qn, kvc, bias_p, sbase_p, useimp_p, wp["indk"], kg0, wp["cvec"], nb=1, tb=LANES,
                              n_blocks=nblk_p, q_t=True)
        yn = _flash(q_ts, ksa, vs, kwa, vw, dt, edge, ocmp, ng, ngate, wp["bg"][l], wp["expand"], n_b=n_bp, t_len=t_p)
        xp = _out_proj(xp, yc, yg, yn, wp["w_out"][l], norm_f[None, :], final=final)
        w_keep = min(WINDOW, t_p)
        outs_p.append((conv_st, unpack_state(gla_st),
                       win_n.reshape(n_bp, t_p, 256)[:, t_p - w_keep:], cmp_n, slc_n))
        zc, zg, qn, cmp_n, slc_n, win_n, ngate, ng = _in_proj(
            xs, row(norm_g), wp["w_in"][l], wp["indq"], wp["indk"], wp["qg"][l], kg12, seq_len=t_s, prompt=False)
        yc, conv_st = _conv(zc, state_conv[l], nb=nb_s, t_len=t_s, **common)
        yg, gla_st = _gla(zg, s0[l], wp["wgk"][l], bgk, gng, nb=nb_s, t_len=t_s, chunk=t_s)
        kvc = _compress_paged(page_table, cache_cmp, l, cmp_n, wp["wtap"][l], wp["petap"][l], t_new=t_s)
        qa, ocmp = _cmpsel(qn, kvc, bias_s, sbase_1, useimp_1, wp["indk"], kg0, wp["cvec"], nb=nb_s, tb=t_s,
                           n_blocks=nblk_s, q_t=False)
        yn, win_st = _dec_attn(page_table, cache_slc, l, qa, slc_n, win_n, win_state, dsl, dwn, ocmp, ng, ngate,
                               wp["bg"][l], wp["expand"], t_new=t_s)
        xs = _out_proj(xs, yc, yg, yn, wp["w_out"][l], norm_f[None, :], final=final)
        outs_s.append((conv_st, unpack_state(gla_st), win_st, cmp_n, slc_n))

    def stack(outs, k, shape):
        return jnp.stack([o[k] for o in outs]).reshape(shape)

    kv = (2, G_KV, HD)
    return (xp.reshape(n_bp, t_p, D_MODEL), xs.reshape(n_bs, t_s, D_MODEL),
            stack(outs_p, 0, (depth, n_bp, HIST, W_CONV)), stack(outs_s, 0, (depth, n_bs, HIST, W_CONV)),
            stack(outs_p, 1, (depth, n_bp, H_GLA, DK_GLA, DV_GLA)), stack(outs_s, 1, (depth, n_bs, H_GLA, DK_GLA, DV_GLA)),
            stack(outs_p, 2, (depth, n_bp, min(WINDOW, t_p)) + kv), jnp.stack([o[2] for o in outs_s]).transpose(0, 1, 5, 2, 3, 4),
            stack(outs_p, 3, (depth, n_bp, t_p) + kv), stack(outs_s, 3, (depth, n_bs, t_s) + kv),
            stack(outs_p, 4, (depth, n_bp, t_p) + kv), stack(outs_s, 4, (depth, n_bs, t_s) + kv))
```

```python
import functools
import math

import numpy as np
import jax
import jax.numpy as jnp
from jax import lax
from jax.experimental import pallas as pl
from jax.experimental.pallas import tpu as pltpu

F32 = jnp.float32
BF16 = jnp.bfloat16

D_MODEL = 1024
W_CONV = 256
CONV_WIDTH = 31
HIST = CONV_WIDTH - 1
H_GLA = 4
DK_GLA = 32
DV_GLA = 64
W_GLA = H_GLA * DV_GLA
GATE_RANK = 16
GATE_NORMALIZER = 16.0
H_NSA = 8
HD = 64
G_KV = 2
HPG = H_NSA // G_KV
W_NSA = H_NSA * HD
CMP_BLOCK = 32
SEL_BLOCK = 64
TOP_N = 8
WINDOW = 512
N_BUCKETS = 32
MAX_EXACT = N_BUCKETS // 2
MAX_DISTANCE = 128
PAGE_SIZE = 128
SCALE = HD ** -0.5
EPS = 1e-6
NEG = -1e30
FORCE = 1e4

LANES = 128
SEL_LANE0 = 64
SEL_SLOTS = 40
CONST_LANE = 112
ODD_SLOT0 = 64
TQ = 256
V_ROWS = 128
KEY_CHUNK = 256
COL_CHUNK = 128
VMEM_LIMIT = 56 * 1024 * 1024

C_CONV = 0
C_GLA = 3 * W_CONV
N_GLA = 2 * H_GLA * DK_GLA + 2 * W_GLA + LANES
C_NSA = C_GLA + N_GLA
N_NSA = W_NSA + 3 * 256 + W_NSA + LANES
N_IN = C_NSA + N_NSA


def _rel_bucket_np(dist):
    n = np.maximum(dist, 0)
    nf = np.maximum(n, 1).astype(np.float32)
    large = MAX_EXACT + (np.log(nf / np.float32(MAX_EXACT)) / np.float32(math.log(MAX_DISTANCE / MAX_EXACT))
                         * np.float32(N_BUCKETS - MAX_EXACT)).astype(np.int32)
    large = np.minimum(large, N_BUCKETS - 1)
    return np.where(n < MAX_EXACT, n, large).astype(np.int32)


def _in_proj_columns():
    o = {}
    off = 0
    for name, w in (("c_a", 256), ("c_b", 256), ("c_gate", 256), ("l_q", 128), ("l_k", 128), ("l_v", 256),
                    ("l_gk", 16), ("l_gate", 256), ("n_q", 512), ("n_cmp", 256), ("n_slc", 256),
                    ("n_win", 256), ("n_g", 24), ("n_gate", 512)):
        o[name] = off
        off += w
    cols = -np.ones((N_IN,), np.int64)

    def put(dst, name, width):
        cols[dst:dst + width] = o[name] + np.arange(width)

    put(0, "c_a", 256); put(256, "c_b", 256); put(512, "c_gate", 256)
    g = C_GLA
    put(g, "l_q", 128); put(g + 128, "l_k", 128); put(g + 256, "l_v", 256); put(g + 512, "l_gate", 256)
    put(g + 768, "l_gk", 16)
    n = C_NSA
    put(n, "n_q", 512); put(n + 512, "n_cmp", 256); put(n + 768, "n_slc", 256); put(n + 1024, "n_win", 256)
    cols[n + 1280:n + 1792] = o["n_gate"] + _nsa_perm()
    put(n + 1792, "n_g", 24)
    return cols


def _nsa_perm():
    p = np.zeros((W_NSA,), np.int64)
    for j in range(HPG):
        for g in range(G_KV):
            p[128 * j + 64 * g:128 * j + 64 * g + 64] = 64 * (HPG * g + j) + np.arange(64)
    return p


def _blockdiag_ones(n, blk):
    i = np.arange(n)
    return (i[:, None] // blk == i[None, :] // blk).astype(np.float32)


def _split2(x):
    hi = x.astype(BF16)
    lo = (x - hi.astype(F32)).astype(BF16)
    return hi, lo


def _dot(a, b):
    return jnp.dot(a, b, preferred_element_type=F32)


def _dot_nt(a, b):
    return lax.dot_general(a, b, (((1,), (1,)), ((), ())), preferred_element_type=F32)


def _dot_tn(a, b):
    return lax.dot_general(a, b, (((0,), (0,)), ((), ())), preferred_element_type=F32)


def _dot_split(x, m_bf16):
    hi, lo = _split2(x)
    return _dot(hi, m_bf16) + _dot(lo, m_bf16)


def _seg_rms(x, ind_ref, seg):
    return lax.rsqrt(_dot_split(x * x, ind_ref[...]) * (1.0 / seg) + EPS)


def _sigmoid(x):
    return 1.0 / (1.0 + jnp.exp(-x))


def _silu(x):
    return x * _sigmoid(x)


def _lane(shape):
    return lax.broadcasted_iota(jnp.int32, shape, len(shape) - 1)


def _half_pad(x, odd):
    if odd:
        x = pltpu.roll(x, 64, axis=1)
    return jnp.where(_lane(x.shape) < 64, x, 0.0)


def _inproj_body(x_ref, ng_ref, w_ref, indq_ref, indk_ref, qg_ref, kg_ref, *outs, tm, seq_len, prompt, n_chained):
    outs = outs[n_chained:]
    zc_ref, zg_ref, qn_ref, cmp_ref, slc_ref, win_ref, gate_ref, g_ref = outs[:8]
    x = x_ref[...]
    ms = jnp.mean(x * x, axis=-1, keepdims=True)
    h = (x * lax.rsqrt(ms + EPS) * ng_ref[...]).astype(BF16)

    def mm(lo, width):
        return _dot(h, w_ref[:, lo:lo + width])

    zc_ref[...] = mm(C_CONV, 3 * W_CONV)
    zg_ref[...] = mm(C_GLA, N_GLA)
    q = mm(C_NSA, W_NSA)
    qn_ref[...] = (q * _seg_rms(q, indq_ref, HD) * qg_ref[...]).astype(BF16)
    cmp_ref[...] = mm(C_NSA + 512, 256)
    slc = mm(C_NSA + 768, 256)
    win = mm(C_NSA + 1024, 256)
    ks = slc[:, 0:128]
    ks = ks * _seg_rms(ks, indk_ref, HD) * kg_ref[0:1, :]
    kw = win[:, 0:128]
    kw = kw * _seg_rms(kw, indk_ref, HD) * kg_ref[1:2, :]
    slc_ref[:, 0:128] = ks
    slc_ref[:, 128:256] = slc[:, 128:256]
    win_ref[:, 0:128] = kw
    win_ref[:, 128:256] = win[:, 128:256]
    gate_ref[...] = mm(C_NSA + 1280, W_NSA)
    g_ref[...] = mm(C_NSA + 1792, LANES)
    if prompt:
        ksa_ref, vs_ref, kwa_ref, vw_ref = outs[8:]
        lane = _lane((tm, LANES))
        row = lax.broadcasted_iota(jnp.int32, (tm, LANES), 0)
        t = lax.rem(pl.program_id(0) * tm + row, seq_len)
        ones = ((lane == CONST_LANE) | (lane == CONST_LANE + 1)).astype(F32)
        onehot = ((lane - SEL_LANE0) == lax.shift_right_logical(t, 6)).astype(F32)
        for g in range(G_KV):
            ksa_ref[g] = (_half_pad(ks, g == 1) + onehot + ones).astype(BF16)
            kwa_ref[g] = (_half_pad(kw, g == 1) + ones).astype(BF16)
        feat = lax.broadcasted_iota(jnp.int32, (LANES, tm), 0)
        for v_ref, src in ((vs_ref, slc), (vw_ref, win)):
            vt = src[:, 128:256].T
            for g in range(G_KV):
                own = vt if g == 0 else pltpu.roll(vt, HD, axis=0)
                v_ref[g] = jnp.where(feat < HD, own, (feat == HD).astype(F32))[0:V_ROWS].astype(BF16)


def _in_proj(x2d, norm_g, w, indq, indk, qg, kg, *, seq_len, prompt, layer=0, depth=1, chain=()):
    m = x2d.shape[0]
    tm = min(256, m)
    assert m % tm == 0 and (not prompt or seq_len % tm == 0)
    row = lambda n: pl.BlockSpec((tm, n), lambda i: (i, 0))
    slab = pl.BlockSpec((None, tm, 256), lambda i: (layer, i, 0))
    full = lambda a: pl.BlockSpec(a.shape, lambda i: (0,) * a.ndim)
    out_shape = [jax.ShapeDtypeStruct((m, 3 * W_CONV), F32), jax.ShapeDtypeStruct((m, N_GLA), F32),
                 jax.ShapeDtypeStruct((m, W_NSA), BF16), jax.ShapeDtypeStruct((depth, m, 256), F32),
                 jax.ShapeDtypeStruct((depth, m, 256), F32), jax.ShapeDtypeStruct((m, 256), F32),
                 jax.ShapeDtypeStruct((m, W_NSA), F32), jax.ShapeDtypeStruct((m, LANES), F32)]
    out_specs = [row(3 * W_CONV), row(N_GLA), row(W_NSA), slab, slab, row(256), row(W_NSA), row(LANES)]
    if prompt:
        aug = pl.BlockSpec((G_KV, tm, LANES), lambda i: (0, i, 0))
        assert tm == TQ
        vt = pl.BlockSpec((None, G_KV, V_ROWS, tm), lambda i: (i, 0, 0, 0))
        vt_shape = jax.ShapeDtypeStruct((m // tm, G_KV, V_ROWS, tm), BF16)
        out_shape += [jax.ShapeDtypeStruct((G_KV, m, LANES), BF16), vt_shape,
                      jax.ShapeDtypeStruct((G_KV, m, LANES), BF16), vt_shape]
        out_specs += [aug, vt, aug, vt]
    args = [x2d, norm_g, w, indq, indk, qg, kg]
    return pl.pallas_call(
        functools.partial(_inproj_body, tm=tm, seq_len=seq_len, prompt=prompt, n_chained=len(chain)),
        grid=(m // tm,),
        in_specs=[row(D_MODEL), full(norm_g), full(w), full(indq), full(indk), full(qg), full(kg)]
        + [pl.BlockSpec(memory_space=pl.ANY)] * len(chain),
        out_specs=out_specs, out_shape=out_shape,
        input_output_aliases={len(args) + k: 3 + k for k in range(len(chain))},
        compiler_params=pltpu.CompilerParams(dimension_semantics=("arbitrary",), vmem_limit_bytes=VMEM_LIMIT),
        name="in_proj",
    )(*args, *chain)


def _outproj_body(x_ref, yc_ref, yg_ref, yn_ref, w_ref, nf_ref, o_ref, *, final):
    x = x_ref[...]
    x = (x + _dot(yc_ref[...], w_ref[0:256, :]) + _dot(yg_ref[...], w_ref[256:512, :])
         + _dot(yn_ref[...].astype(BF16), w_ref[512:1024, :]))
    if final:
        ms = jnp.mean(x * x, axis=-1, keepdims=True)
        x = x * lax.rsqrt(ms + EPS) * nf_ref[...]
    o_ref[...] = x


def _out_proj(x2d, yc, yg, yn, w, norm_f, *, final):
    m = x2d.shape[0]
    tm = min(512, m)
    assert m % tm == 0
    row = lambda n: pl.BlockSpec((tm, n), lambda i: (i, 0))
    full = lambda a: pl.BlockSpec(a.shape, lambda i: (0,) * a.ndim)
    return pl.pallas_call(
        functools.partial(_outproj_body, final=final),
        grid=(m // tm,),
        in_specs=[row(D_MODEL), row(256), row(256), row(512), full(w), full(norm_f)],
        out_specs=row(D_MODEL), out_shape=jax.ShapeDtypeStruct((m, D_MODEL), F32),
        compiler_params=pltpu.CompilerParams(dimension_semantics=("arbitrary",), vmem_limit_bytes=VMEM_LIMIT),
        name="out_proj",
    )(x2d, yc, yg, yn, w, norm_f)


def _conv_body(zc_ref, hist_ref, wdw_ref, bdw_ref, lng_ref, lnb_ref, wpw_ref, bpw_ref, y_ref, st_ref,
               ext_scr, act_scr, *, nb, t_len, tc, tmm):
    n_chunks = t_len // tc

    def per_batch(bi, _):
        base = bi * t_len
        ext_scr[0:32, :] = jnp.zeros((32, W_CONV), F32)
        ext_scr[2:32, :] = hist_ref[bi]

        def chunk(c, _):
            r0 = pl.multiple_of(c * tc, tc)
            g0 = pl.multiple_of(base + r0, tc)
            a = zc_ref[pl.ds(g0, tc), 0:256]
            b = zc_ref[pl.ds(g0, tc), 256:512]
            ext_scr[pl.ds(32 + r0, tc), :] = a * _sigmoid(b)
            win = ext_scr[pl.ds(r0, tc + 32), :]
            acc = jnp.zeros((tc, W_CONV), F32) + bdw_ref[...]
            for r in range(8):
                taps = [j for j in range(r, CONV_WIDTH, 8)]
                wr = win[2 + r:2 + r + tc + 8 * (len(taps) - 1)]
                for mi, j in enumerate(taps):
                    acc = acc + wr[8 * mi:8 * mi + tc] * wdw_ref[j:j + 1, :]
            mu = jnp.mean(acc, axis=-1, keepdims=True)
            d = acc - mu
            var = jnp.mean(d * d, axis=-1, keepdims=True)
            yn = d * lax.rsqrt(var + EPS) * lng_ref[...] + lnb_ref[...]
            act_scr[pl.ds(g0, tc), :] = _silu(yn)
            return 0

        lax.fori_loop(0, n_chunks, chunk, 0)
        st_ref[bi] = ext_scr[2 + t_len:32 + t_len, :]
        return 0

    lax.fori_loop(0, nb, per_batch, 0)

    def mm(c, _):
        r0 = pl.multiple_of(c * tmm, tmm)
        y = _dot(act_scr[pl.ds(r0, tmm), :].astype(BF16), wpw_ref[...]) + bpw_ref[...]
        y_ref[pl.ds(r0, tmm), :] = (y * _silu(zc_ref[pl.ds(r0, tmm), 512:768])).astype(BF16)
        return 0

    lax.fori_loop(0, nb * t_len // tmm, mm, 0)


def _conv(zc, hist, wdw, bdw, lng, lnb, wpw, bpw, *, nb, t_len):
    m = zc.shape[0]
    n_b = m // t_len
    tc = min(64, t_len)
    tmm = min(256, nb * t_len)
    full = lambda a: pl.BlockSpec(a.shape, lambda i: (0,) * a.ndim)
    return pl.pallas_call(
        functools.partial(_conv_body, nb=nb, t_len=t_len, tc=tc, tmm=tmm),
        grid=(n_b // nb,),
        in_specs=[pl.BlockSpec((nb * t_len, 3 * W_CONV), lambda i: (i, 0)),
                  pl.BlockSpec((nb, HIST, W_CONV), lambda i: (i, 0, 0)),
                  full(wdw), full(bdw), full(lng), full(lnb), full(wpw), full(bpw)],
        out_specs=[pl.BlockSpec((nb * t_len, W_CONV), lambda i: (i, 0)),
                   pl.BlockSpec((nb, HIST, W_CONV), lambda i: (i, 0, 0))],
        out_shape=[jax.ShapeDtypeStruct((m, W_CONV), BF16), jax.ShapeDtypeStruct((n_b, HIST, W_CONV), F32)],
        scratch_shapes=[pltpu.VMEM((32 + t_len, W_CONV), F32), pltpu.VMEM((nb * t_len, W_CONV), F32)],
        compiler_params=pltpu.CompilerParams(dimension_semantics=("arbitrary",), vmem_limit_bytes=VMEM_LIMIT),
        name="conv",
    )(zc, hist, wdw, bdw, lng, lnb, wpw, bpw)


def _gla_body(zg_ref, s0_ref, wgk_ref, bgk_ref, ltri_ref, lsum_ref, ind_ref, indv_ref, ng_ref, bd_ref,
              y_ref, st_ref, bc_scr, qe_scr, ke_scr, dec_scr, o_scr, *, nb, t_len, chunk, rb):
    rows = nb * t_len
    n_rb = rows // rb
    cpb = t_len // chunk

    def phase1(i, _):
        r0 = pl.multiple_of(i * rb, rb)
        gk = zg_ref[pl.ds(r0, rb), 768:896]
        pre = _dot_split(gk, wgk_ref[0]) + _dot(gk.astype(BF16), wgk_ref[1]) + bgk_ref[...]
        la = (jnp.minimum(pre, 0.0) - jnp.log(1.0 + jnp.exp(-jnp.abs(pre)))) * (1.0 / GATE_NORMALIZER)
        h1, l1 = _split2(la)
        l2 = (la - h1.astype(F32) - l1.astype(F32)).astype(BF16)
        bc = _dot(ltri_ref[...], h1) + _dot(ltri_ref[...], l1) + _dot(ltri_ref[...], l2)
        bt = _dot(lsum_ref[...], h1) + _dot(lsum_ref[...], l1) + _dot(lsum_ref[...], l2)
        bc_scr[pl.ds(r0, rb), :] = bc
        qe_scr[pl.ds(r0, rb), :] = zg_ref[pl.ds(r0, rb), 0:128] * (DK_GLA ** -0.5) * jnp.exp(bc)
        ke_scr[pl.ds(r0, rb), :] = zg_ref[pl.ds(r0, rb), 128:256] * jnp.exp(bt - bc)
        dec_scr[pl.ds(r0, rb), :] = jnp.exp(bt)
        return 0

    lax.fori_loop(0, n_rb, phase1, 0)

    ti = lax.broadcasted_iota(jnp.int32, (chunk, chunk, LANES), 0)
    si = lax.broadcasted_iota(jnp.int32, (chunk, chunk, LANES), 1)
    causal = si <= ti

    u2 = 2 if (rows // chunk) % 2 == 0 else 1
    u3 = 4 if cpb % 4 == 0 else 1

    def phase2(cg, _):
        for u in range(u2):
            r0 = pl.multiple_of((cg * u2 + u) * chunk, chunk)
            bc = bc_scr[pl.ds(r0, chunk), :]
            q = zg_ref[pl.ds(r0, chunk), 0:128] * (DK_GLA ** -0.5)
            k = zg_ref[pl.ds(r0, chunk), 128:256]
            v = zg_ref[pl.ds(r0, chunk), 256:512]
            e = jnp.exp(jnp.where(causal, bc[:, None, :] - bc[None, :, :], NEG))
            p = (q[:, None, :] * k[None, :, :] * e).reshape(chunk * chunk, LANES)
            att = _dot_split(p, ind_ref[...]).reshape(chunk, chunk, W_GLA)
            o_scr[pl.ds(r0, chunk), :] = (att * v[None, :, :]).sum(axis=1)
        return 0

    lax.fori_loop(0, rows // chunk // u2, phase2, 0)

    def per_batch(bi, _):
        def group(cg, s):
            for u in range(u3):
                r0 = pl.multiple_of(bi * t_len + (cg * u3 + u) * chunk, chunk)
                o_scr[pl.ds(r0, chunk), :] += _dot_nt(qe_scr[pl.ds(r0, chunk), :].astype(BF16), s.astype(BF16))
                upd = _dot_tn(zg_ref[pl.ds(r0, chunk), 256:512].astype(BF16),
                              ke_scr[pl.ds(r0, chunk), :].astype(BF16))
                s = s * dec_scr[pl.ds(r0, 1), :] + upd * bd_ref[...]
            return s

        st_ref[bi] = lax.fori_loop(0, cpb // u3, group, s0_ref[bi])
        return 0

    lax.fori_loop(0, nb, per_batch, 0)

    def phase4(i, _):
        r0 = pl.multiple_of(i * rb, rb)
        o = o_scr[pl.ds(r0, rb), :]
        o = o * _seg_rms(o, indv_ref, DV_GLA) * ng_ref[...]
        y_ref[pl.ds(r0, rb), :] = (o * _silu(zg_ref[pl.ds(r0, rb), 512:768])).astype(BF16)
        return 0

    lax.fori_loop(0, n_rb, phase4, 0)


def _gla(zg, s0t, wgk, bgk, ng, *, nb, t_len, chunk):
    m = zg.shape[0]
    n_b = m // t_len
    rows = nb * t_len
    rb = min(128, rows)
    assert rows % rb == 0 and rb % chunk == 0 and t_len % chunk == 0
    ltri = jnp.asarray(_blockdiag_ones(rb, chunk) * np.tril(np.ones((rb, rb), np.float32)), BF16)
    lsum = jnp.asarray(_blockdiag_ones(rb, chunk), BF16)
    i128 = np.arange(LANES)
    i256 = np.arange(W_GLA)
    ind = jnp.asarray((i128[:, None] // DK_GLA == i256[None, :] // DV_GLA).astype(np.float32), BF16)
    indv = jnp.asarray(_blockdiag_ones(W_GLA, DV_GLA), BF16)
    bd = jnp.asarray((i256[:, None] // DV_GLA == i128[None, :] // DK_GLA).astype(np.float32))
    full = lambda a: pl.BlockSpec(a.shape, lambda i: (0,) * a.ndim)
    return pl.pallas_call(
        functools.partial(_gla_body, nb=nb, t_len=t_len, chunk=chunk, rb=rb),
        grid=(n_b // nb,),
        in_specs=[pl.BlockSpec((rows, N_GLA), lambda i: (i, 0)),
                  pl.BlockSpec((nb, W_GLA, LANES), lambda i: (i, 0, 0)),
                  full(wgk), full(bgk), full(ltri), full(lsum), full(ind), full(indv), full(ng), full(bd)],
        out_specs=[pl.BlockSpec((rows, W_GLA), lambda i: (i, 0)),
                   pl.BlockSpec((nb, W_GLA, LANES), lambda i: (i, 0, 0))],
        out_shape=[jax.ShapeDtypeStruct((m, W_GLA), BF16), jax.ShapeDtypeStruct((n_b, W_GLA, LANES), F32)],
        scratch_shapes=[pltpu.VMEM((rows, LANES), F32), pltpu.VMEM((rows, LANES), F32),
                        pltpu.VMEM((rows, LANES), F32), pltpu.VMEM((rows, LANES), F32),
                        pltpu.VMEM((rows, W_GLA), F32)],
        compiler_params=pltpu.CompilerParams(dimension_semantics=("arbitrary",), vmem_limit_bytes=VMEM_LIMIT),
        name="gla",
    )(zg, s0t, wgk, bgk, ltri, lsum, ind, indv, ng, bd)


KW = CMP_BLOCK * 256


def _compress_rows(x_ref, w_ref, pe_ref, kvc_ref, n_rows):
    kvc_ref[...] = jnp.zeros(kvc_ref.shape, F32)
    for par in range(2):
        x = (x_ref[:, par * KW:(par + 1) * KW] + pe_ref[...]).astype(BF16)
        kvc_ref[ODD_SLOT0 * par:ODD_SLOT0 * par + n_rows, :] = _dot(x, w_ref[...])


def _compress_prompt_body(x_ref, w_ref, pe_ref, kvc_ref, *, n_rows):
    _compress_rows(x_ref, w_ref, pe_ref, kvc_ref, n_rows)


def _compress_prompt(cmp_all, layer, w, pe, *, n_b, t_len):
    n_rows = t_len // SEL_BLOCK
    full = lambda a: pl.BlockSpec(a.shape, lambda i: (0,) * a.ndim)
    return pl.pallas_call(
        functools.partial(_compress_prompt_body, n_rows=n_rows),
        grid=(n_b,),
        in_specs=[pl.BlockSpec((None, n_rows, 2 * KW), lambda i: (layer, i, 0)), full(w), full(pe)],
        out_specs=pl.BlockSpec((None, LANES, 256), lambda i: (i, 0, 0)),
        out_shape=jax.ShapeDtypeStruct((n_b, LANES, 256), F32),
        compiler_params=pltpu.CompilerParams(dimension_semantics=("arbitrary",), vmem_limit_bytes=VMEM_LIMIT),
        name="compress_prompt",
    )(cmp_all.reshape(cmp_all.shape[0], n_b * n_rows, 2 * KW), w, pe)


def _compress_paged_body(pt_ref, *refs, n_pages, t_new, n_rows):
    pages = refs[:n_pages]
    new_ref, w_ref, pe_ref, kvc_ref, x_scr, acc_scr = refs[n_pages:]
    per_page = PAGE_SIZE // CMP_BLOCK
    n_past = n_pages * per_page
    new = new_ref[...]
    for s in range(2):
        for p in range(n_pages):
            rows = pages[p][s].reshape(LANES, PAGE_SIZE).T
            for c in range(per_page):
                r0 = X_PITCH * (per_page * p + c)
                x_scr[s, r0:r0 + CMP_BLOCK, :] = rows[CMP_BLOCK * c:CMP_BLOCK * (c + 1)]
        x_scr[s, X_PITCH * n_past:, :] = jnp.zeros((X_PITCH * (2 * n_rows - n_past), LANES), F32)
        x_scr[s, X_PITCH * n_past:X_PITCH * n_past + t_new, :] = new[:, LANES * s:LANES * (s + 1)]
    kvc_ref[...] = jnp.zeros(kvc_ref.shape, F32)
    for s in range(2):
        acc = jnp.zeros((2 * n_rows, LANES), F32)
        for j in range(CMP_BLOCK):
            x = x_scr[s, pl.ds(j, 2 * n_rows, stride=X_PITCH), :] + pe_ref[s, j:j + 1, :]
            acc = acc + _dot(x.astype(BF16), w_ref[s, j])
        acc_scr[...] = acc
        for par in range(2):
            kvc_ref[ODD_SLOT0 * par:ODD_SLOT0 * par + n_rows, LANES * s:LANES * (s + 1)] = (
                acc_scr[pl.ds(par, n_rows, stride=2), :])


def _paged_specs(layer, n_pages, block):
    return [pl.BlockSpec((None, None) + block, functools.partial(
        lambda b, pt, p: (layer, pt[b, p]) + (0,) * len(block), p=p)) for p in range(n_pages)]


PAGE_BLOCK = (2, G_KV, HD, PAGE_SIZE)
X_PITCH = CMP_BLOCK + 4


def _compress_paged(page_table, cache_t, layer, cmp_new, w, pe, *, t_new):
    n_b, n_pages = page_table.shape
    n_rows = -(-((n_pages * PAGE_SIZE + SEL_BLOCK) // SEL_BLOCK) // 8) * 8
    full = lambda a: pl.BlockSpec(a.shape, lambda b, pt: (0,) * a.ndim)
    grid_spec = pltpu.PrefetchScalarGridSpec(
        num_scalar_prefetch=1, grid=(n_b,),
        in_specs=_paged_specs(layer, n_pages, PAGE_BLOCK) + [
            pl.BlockSpec((t_new, 256), lambda b, pt: (b, 0)), full(w), full(pe)],
        out_specs=pl.BlockSpec((None, LANES, 256), lambda b, pt: (b, 0, 0)),
        scratch_shapes=[pltpu.VMEM((2, 2 * n_rows * X_PITCH, LANES), F32), pltpu.VMEM((2 * n_rows, LANES), F32)])
    return pl.pallas_call(
        functools.partial(_compress_paged_body, n_pages=n_pages, t_new=t_new, n_rows=n_rows),
        grid_spec=grid_spec, out_shape=jax.ShapeDtypeStruct((n_b, LANES, 256), F32),
        compiler_params=pltpu.CompilerParams(dimension_semantics=("arbitrary",), vmem_limit_bytes=VMEM_LIMIT),
        name="compress_paged",
    )(page_table, *([cache_t] * n_pages), cmp_new, w, pe)


def _cmpsel_body(qn_ref, kvc_ref, bias_ref, sbase_ref, useimp_ref, indk_ref, kg_ref, cvec_ref,
                 *outs, nb, tb, n_blocks, q_t):
    q_refs, ocmp_ref = outs[:-1], outs[-1]
    rows = nb * tb
    lane = _lane((rows, LANES))
    qn = qn_ref[...].astype(F32)
    qpad = [_half_pad(qn[:, 128 * (h // 2):128 * (h // 2) + 128], h % 2 == 1) for h in range(H_NSA)]
    imp_parts = []
    o_parts = [[None] * nb for _ in range(H_NSA)]
    kc_all = kvc_ref[:, :, 0:128].reshape(nb * LANES, LANES)
    kc_all = kc_all * _seg_rms(kc_all, indk_ref, HD) * kg_ref[...]
    for bi in range(nb):
        kc = kc_all[bi * LANES:(bi + 1) * LANES]
        vboth = kvc_ref[bi, :, 128:256].astype(BF16)
        imp_b = []
        for g in range(G_KV):
            kpad = _half_pad(kc, g == 1).astype(BF16)
            qs = jnp.concatenate([qpad[HPG * g + j][bi * tb:(bi + 1) * tb] for j in range(HPG)], axis=0).astype(BF16)
            bias = bias_ref[g]
            l = _dot_nt(qs, kpad) + bias
            vis = bias > 0.5 * NEG
            mx = jnp.max(l, axis=-1, keepdims=True)
            p = jnp.where(vis, jnp.exp(l - mx), 0.0)
            p = p / jnp.maximum(jnp.sum(p, axis=-1, keepdims=True), 1e-30)
            o = _dot(p.astype(BF16), vboth)
            ig = p[0:tb]
            for j in range(1, HPG):
                ig = ig + p[j * tb:(j + 1) * tb]
            imp_b.append(ig)
            for j in range(HPG):
                o_parts[HPG * g + j][bi] = o[j * tb:(j + 1) * tb]
        imp_parts.append(imp_b)
    for j in range(HPG):
        o0 = jnp.concatenate(o_parts[j], axis=0) if nb > 1 else o_parts[j][0]
        o1 = jnp.concatenate(o_parts[HPG + j], axis=0) if nb > 1 else o_parts[HPG + j][0]
        ocmp_ref[:, 128 * j:128 * j + 128] = jnp.where(lane < 64, o0, o1)
    for g in range(G_KV):
        ig = jnp.concatenate([imp_parts[bi][g] for bi in range(nb)], axis=0) if nb > 1 else imp_parts[0][g]
        imp = ig + pltpu.roll(ig, LANES - ODD_SLOT0, axis=1)
        score = jnp.where(useimp_ref[...] > 0.5, imp, sbase_ref[...])
        st = score.T[0:SEL_SLOTS]
        jj = lax.broadcasted_iota(jnp.int32, (SEL_SLOTS, rows), 0)
        rank = jnp.zeros((SEL_SLOTS, rows), F32)
        for jp in range(n_blocks):
            r = st[jp:jp + 1, :]
            before = (r > st) | ((r == st) & (jp < jj))
            rank = rank + before.astype(F32)
        selb = jnp.where(rank < TOP_N - 0.5, 0.0, NEG)
        full_t = jnp.concatenate([jnp.zeros((SEL_LANE0, rows), F32), selb,
                                  jnp.zeros((LANES - SEL_LANE0 - SEL_SLOTS, rows), F32)], axis=0)
        if q_t:
            for j in range(HPG):
                h = HPG * g + j
                q_refs[j][g] = ((qpad[h] + cvec_ref[h]).T + full_t).astype(BF16)
        else:
            extra = full_t.T
            for j in range(HPG):
                h = HPG * g + j
                q_refs[0][g, j] = (qpad[h] + extra + cvec_ref[h]).astype(q_refs[0].dtype)


def _cmpsel(qn, kvc, bias, sbase, useimp, indk, kg0, cvec, *, nb, tb, n_blocks, q_t):
    m = qn.shape[0]
    rows = nb * tb
    assert rows == LANES and m % rows == 0
    n_pos = bias.shape[0]
    full = lambda a: pl.BlockSpec(a.shape, lambda i: (0,) * a.ndim)
    if q_t:
        q_specs = [pl.BlockSpec((G_KV, LANES, rows), lambda i: (0, 0, i))] * HPG
        q_shapes = [jax.ShapeDtypeStruct((G_KV, LANES, m), BF16)] * HPG
    else:
        q_specs = [pl.BlockSpec((G_KV, HPG, rows, LANES), lambda i: (0, 0, i, 0))]
        q_shapes = [jax.ShapeDtypeStruct((G_KV, HPG, m, LANES), F32)]
    return pl.pallas_call(
        functools.partial(_cmpsel_body, nb=nb, tb=tb, n_blocks=n_blocks, q_t=q_t),
        grid=(m // rows,),
        in_specs=[pl.BlockSpec((rows, W_NSA), lambda i: (i, 0)),
                  pl.BlockSpec((nb, LANES, 256), (lambda i: (i, 0, 0)) if nb > 1 else (lambda i: (i // n_pos, 0, 0))),
                  pl.BlockSpec((None, G_KV, HPG * tb, LANES), lambda i: (i % n_pos, 0, 0, 0)),
                  pl.BlockSpec((None, rows, LANES), lambda i: (i % n_pos, 0, 0)),
                  pl.BlockSpec((None, rows, LANES), lambda i: (i % n_pos, 0, 0)),
                  full(indk), full(kg0), full(cvec)],
        out_specs=q_specs + [pl.BlockSpec((rows, W_NSA), lambda i: (i, 0))],
        out_shape=q_shapes + [jax.ShapeDtypeStruct((m, W_NSA), F32)],
        compiler_params=pltpu.CompilerParams(dimension_semantics=("arbitrary",), vmem_limit_bytes=VMEM_LIMIT),
        name="cmpsel",
    )(qn, kvc, bias, sbase, useimp, indk, kg0, cvec)


def _mix_gates(g_ref, bg_ref, expand_ref, gate_ref, ocmp_ref, o_slc, o_win, y_ref):
    gl = _sigmoid(g_ref[...] + bg_ref[...])
    gx = _dot_split(gl, expand_ref[...])
    for j in range(HPG):
        sl = slice(128 * j, 128 * j + 128)
        o = (gx[:, 128 * j:128 * j + 128] * ocmp_ref[:, sl]
             + gx[:, 128 * (HPG + j):128 * (HPG + j) + 128] * o_slc[j]
             + gx[:, 128 * (2 * HPG + j):128 * (2 * HPG + j) + 128] * o_win[j])
        y_ref[:, sl] = (o * _silu(gate_ref[:, sl])).astype(y_ref.dtype)


def _softmax_step(q_t, k, v_t, bias_t, m_scr, acc_scr):
    n_key, n_col = k.shape[0], q_t.shape[1]
    for k0 in range(0, n_key, KEY_CHUNK):
        ks = slice(k0, k0 + KEY_CHUNK)
        for c0 in range(0, n_col, COL_CHUNK):
            cs = slice(c0, c0 + COL_CHUNK)
            s = _dot(k[ks], q_t[:, cs])
            if bias_t is not None:
                s = s + bias_t(ks, cs)
            m_prev = m_scr[:, cs]
            m_new = jnp.maximum(m_prev, jnp.max(s, axis=0, keepdims=True))
            p = jnp.exp(s - m_new)
            acc_scr[:, cs] = jnp.exp(m_prev - m_new) * acc_scr[:, cs] + _dot(v_t[:, ks], p.astype(BF16))
            m_scr[:, cs] = m_new


def _flash_body(q0_ref, q1_ref, q2_ref, q3_ref, ksa_ref, vs_ref, kwa_ref, vw_ref, dt_ref, edge_ref, ocmp_ref,
                g_ref, gate_ref, bg_ref, expand_ref, y_ref, m_scr, acc_scr):
    i = pl.program_id(1)
    cols = HPG * TQ
    m_scr[...] = jnp.full(m_scr.shape, NEG, F32)
    acc_scr[...] = jnp.zeros(acc_scr.shape, F32)

    def run(n_near):
        for g in range(G_KV):
            q_t = jnp.concatenate([r[g] for r in (q0_ref, q1_ref, q2_ref, q3_ref)], axis=1)
            for br, (ka_ref, v_ref) in enumerate(((ksa_ref, vs_ref), (kwa_ref, vw_ref))):
                def step(kt, bias_t):
                    k0 = pl.multiple_of(kt * TQ, TQ)
                    _softmax_step(q_t, ka_ref[g, pl.ds(k0, TQ), :], v_ref[kt, g], bias_t, m_scr.at[br, g],
                                  acc_scr.at[br, g])

                if n_near == 2:
                    if br == 0:
                        def far_pair(kp, _):
                            step(2 * kp, None)
                            step(2 * kp + 1, None)
                            return 0

                        lax.fori_loop(0, lax.shift_right_logical(i - 1, 1), far_pair, 0)

                        @pl.when(lax.rem(i - 1, 2) == 1)
                        def _():
                            step(i - 2, None)
                    else:
                        step(i - 2, lambda ks, cs: edge_ref[ks, cs])
                if n_near >= 1:
                    step(i - 1, lambda ks, cs: dt_ref[1, g, ks, cs])
                step(i, lambda ks, cs: dt_ref[0, g, ks, cs])

    for n_near, cond in ((0, i == 0), (1, i == 1), (2, i >= 2)):
        pl.when(cond)(functools.partial(run, n_near))

    outs = []
    for br in range(2):
        o_t = jnp.concatenate([acc_scr[br, g, 0:HD, :] / acc_scr[br, g, HD:HD + 1, :] for g in range(G_KV)], axis=0)
        outs.append([o_t[:, j * TQ:(j + 1) * TQ].T for j in range(HPG)])
    _mix_gates(g_ref, bg_ref, expand_ref, gate_ref, ocmp_ref, outs[0], outs[1], y_ref)


def _flash(q_ts, ksa, vs, kwa, vw, dt, edge, ocmp, ng, ngate, bg, expand, *, n_b, t_len):
    m = n_b * t_len
    nq = t_len // TQ
    cols = HPG * TQ
    full = lambda a: pl.BlockSpec(a.shape, lambda b, i: (0,) * a.ndim)
    qspec = pl.BlockSpec((G_KV, LANES, TQ), lambda b, i: (0, 0, b * nq + i))
    kspec = pl.BlockSpec((G_KV, None, t_len, LANES), lambda b, i: (0, b, 0, 0))
    vspec = pl.BlockSpec((nq, G_KV, V_ROWS, TQ), lambda b, i: (b, 0, 0, 0))
    row = lambda n: pl.BlockSpec((TQ, n), lambda b, i: (b * nq + i, 0))
    return pl.pallas_call(
        _flash_body,
        grid=(n_b, nq),
        in_specs=[qspec] * HPG + [kspec, vspec, kspec, vspec, full(dt), full(edge), row(W_NSA), row(LANES),
                                  row(W_NSA), full(bg), full(expand)],
        out_specs=row(W_NSA), out_shape=jax.ShapeDtypeStruct((m, W_NSA), BF16),
        scratch_shapes=[pltpu.VMEM((2, G_KV, 1, cols), F32), pltpu.VMEM((2, G_KV, V_ROWS, cols), F32)],
        compiler_params=pltpu.CompilerParams(dimension_semantics=("arbitrary", "arbitrary"),
                                             vmem_limit_bytes=VMEM_LIMIT),
        name="flash",
    )(*q_ts, ksa.reshape(G_KV, n_b, t_len, LANES), vs, kwa.reshape(G_KV, n_b, t_len, LANES), vw,
      dt, edge, ocmp, ng, ngate, bg, expand)


def _dec_attn_body(pt_ref, *refs, n_pages, t_new, n_keys, n_wkeys, chained):
    pages = refs[:n_pages]
    refs = list(refs[n_pages:])
    if chained:
        del refs[11]
    (qa_ref, slc_ref, win_ref, wst_ref, dsl_ref, dwn_ref, ocmp_ref, g_ref, gate_ref, bg_ref, expand_ref,
     y_ref, wout_ref, ka_scr, va_scr, kw_scr, vw_scr) = refs
    past = n_pages * PAGE_SIZE
    w_past = wst_ref.shape[-1]
    rows = HPG * t_new

    @pl.when(pl.program_id(0) == 0)
    def _():
        for k_scr, n, with_blocks in ((ka_scr, n_keys, True), (kw_scr, n_wkeys, False)):
            feat = lax.broadcasted_iota(jnp.int32, (LANES, n), 0)
            key = lax.broadcasted_iota(jnp.int32, (LANES, n), 1)
            aug = (feat == CONST_LANE) | (feat == CONST_LANE + 1)
            if with_blocks:
                aug = aug | ((feat - SEL_LANE0) == lax.shift_right_logical(key, 6))
            for g in range(G_KV):
                k_scr[g] = aug.astype(F32).astype(BF16)

    def new_keys_t(new):
        pad = jnp.concatenate([new, jnp.zeros((LANES - t_new, 256), F32)], axis=0)
        return pad[:, 0:128].T, pad[:, 128:256].T

    for p in range(n_pages):
        sl = slice(p * PAGE_SIZE, (p + 1) * PAGE_SIZE)
        for g in range(G_KV):
            ka_scr[g, 0:HD, sl] = pages[p][0, g].astype(BF16)
            va_scr[HD * g:HD * (g + 1), sl] = pages[p][1, g].astype(BF16)
    kn_t, vn_t = new_keys_t(slc_ref[...])
    va_scr[:, past:n_keys] = vn_t.astype(BF16)
    kwn_t, vwn_t = new_keys_t(win_ref[...])
    vw_scr[:, w_past:n_wkeys] = vwn_t.astype(BF16)
    lane = _lane((HD, LANES))
    for g in range(G_KV):
        ka_scr[g, 0:HD, past:n_keys] = kn_t[HD * g:HD * (g + 1)].astype(BF16)
        kw_scr[g, 0:HD, 0:w_past] = wst_ref[0, g].astype(BF16)
        kw_scr[g, 0:HD, w_past:n_wkeys] = kwn_t[HD * g:HD * (g + 1)].astype(BF16)
        vw_scr[HD * g:HD * (g + 1), 0:w_past] = wst_ref[1, g].astype(BF16)
        for s, new_t in ((0, kwn_t), (1, vwn_t)):
            sh = pltpu.roll(wst_ref[s, g], w_past - t_new, axis=1)
            tail = pltpu.roll(new_t[HD * g:HD * (g + 1)], LANES - t_new, axis=1)
            wout_ref[s, g, :, 0:w_past - LANES] = sh[:, 0:w_past - LANES]
            wout_ref[s, g, :, w_past - LANES:w_past] = jnp.where(lane >= LANES - t_new, tail, sh[:, w_past - LANES:])

    res = [[None, None], [None, None]]
    for g in range(G_KV):
        q = qa_ref[g].reshape(rows, LANES).astype(BF16)
        for br, (k_scr, v_scr, b_ref) in enumerate(((ka_scr, va_scr, dsl_ref), (kw_scr, vw_scr, dwn_ref))):
            s = _dot(q, k_scr[g]) + b_ref[g]
            mx = jnp.max(s, axis=-1, keepdims=True)
            p = jnp.exp(s - mx)
            den = jnp.sum(p, axis=-1, keepdims=True)
            res[br][g] = _dot_nt(p.astype(BF16), v_scr[...]) / den
    lane = _lane((t_new, LANES))
    outs = [[jnp.where(lane < 64, res[br][0][j * t_new:(j + 1) * t_new], res[br][1][j * t_new:(j + 1) * t_new])
             for j in range(HPG)] for br in range(2)]
    _mix_gates(g_ref, bg_ref, expand_ref, gate_ref, ocmp_ref, outs[0], outs[1], y_ref)


def _dec_attn(page_table, cache, layer, qa, slc_new, win_new, win_state, dsl, dwn, ocmp, ng, ngate, bg, expand,
              win_all, *, t_new):
    n_b, n_pages = page_table.shape
    m = n_b * t_new
    w_past = win_state.shape[-1]
    n_keys = dsl.shape[-1]
    n_wkeys = dwn.shape[-1]
    wblock = (2, G_KV, HD, w_past)
    full = lambda a: pl.BlockSpec(a.shape, lambda b, pt: (0,) * a.ndim)
    row = lambda n: pl.BlockSpec((t_new, n), lambda b, pt: (b, 0))
    grid_spec = pltpu.PrefetchScalarGridSpec(
        num_scalar_prefetch=1, grid=(n_b,),
        in_specs=_paged_specs(layer, n_pages, PAGE_BLOCK) + [
            pl.BlockSpec((G_KV, HPG, t_new, LANES), lambda b, pt: (0, 0, b, 0)),
            row(256), row(256), pl.BlockSpec((None, None) + wblock, lambda b, pt: (layer, b, 0, 0, 0, 0)),
            full(dsl), full(dwn), row(W_NSA), row(LANES), row(W_NSA), full(bg), full(expand)]
        + ([] if win_all is None else [pl.BlockSpec(memory_space=pl.ANY)]),
        out_specs=[row(W_NSA), pl.BlockSpec((None, None) + wblock, lambda b, pt: (layer, b, 0, 0, 0, 0))],
        scratch_shapes=[pltpu.VMEM((G_KV, LANES, n_keys), BF16), pltpu.VMEM((LANES, n_keys), BF16),
                        pltpu.VMEM((G_KV, LANES, n_wkeys), BF16), pltpu.VMEM((LANES, n_wkeys), BF16)])
    args = [page_table, *([cache] * n_pages), qa, slc_new, win_new, win_state, dsl, dwn, ocmp, ng, ngate, bg, expand]
    aliases = {}
    if win_all is not None:
        aliases = {len(args): 1}
        args.append(win_all)
    return pl.pallas_call(
        functools.partial(_dec_attn_body, n_pages=n_pages, t_new=t_new, n_keys=n_keys, n_wkeys=n_wkeys,
                          chained=win_all is not None),
        grid_spec=grid_spec,
        out_shape=[jax.ShapeDtypeStruct((m, W_NSA), F32), jax.ShapeDtypeStruct(win_state.shape[:2] + wblock, F32)],
        input_output_aliases=aliases,
        compiler_params=pltpu.CompilerParams(dimension_semantics=("arbitrary",), vmem_limit_bytes=VMEM_LIMIT),
        name="dec_attn",
    )(*args)


def _bucket_matrix(qpos, kpos, valid):
    dist = qpos[:, None] - kpos[None, :]
    return np.where(valid & (dist >= 0), _rel_bucket_np(dist), -1).astype(np.int32)


def _bias_body(tab_ref, bk_ref, o_ref, *, sub_far):
    bk = bk_ref[...]
    for h in range(H_NSA):
        far = tab_ref[N_BUCKETS - 1, h] if sub_far else 0.0
        acc = jnp.full(bk.shape, NEG, F32)
        for b in range(N_BUCKETS):
            acc = jnp.where(bk == b, tab_ref[b, h] - far, acc)
        o_ref[h] = acc


def _bias_table(table, bk, *, sub_far):
    r, c = bk.shape
    tr = 8 if r <= 8 else 64
    assert r % tr == 0
    return pl.pallas_call(
        functools.partial(_bias_body, sub_far=sub_far),
        grid=(r // tr,),
        in_specs=[pl.BlockSpec(memory_space=pltpu.SMEM), pl.BlockSpec((tr, c), lambda i: (i, 0))],
        out_specs=pl.BlockSpec((H_NSA, tr, c), lambda i: (0, i, 0)),
        out_shape=jax.ShapeDtypeStruct((H_NSA, r, c), F32),
        compiler_params=pltpu.CompilerParams(dimension_semantics=("arbitrary",)),
        name="bias_table",
    )(table, jnp.asarray(bk))


def _cmp_tables(qpos_blocks, n_blocks):
    slot = np.arange(LANES)
    blk = np.where(slot < ODD_SLOT0, 2 * slot, 2 * (slot - ODD_SLOT0) + 1)
    real = (blk < 2 * n_blocks) & ((slot % ODD_SLOT0) < n_blocks)
    blk_end = blk * CMP_BLOCK + CMP_BLOCK - 1
    buckets, sbases, useimps = [], [], []
    for qpos in qpos_blocks:
        buckets.append(_bucket_matrix(qpos, blk_end, real[None, :]))
        j = np.arange(LANES)[None, :]
        cur = (qpos // SEL_BLOCK)[:, None]
        forced = (j == 0) | (j == cur) | (j == cur - 1)
        valid = j * SEL_BLOCK <= qpos[:, None]
        inrange = j < n_blocks
        sbases.append(np.where(~inrange, -2.0, np.where(forced, FORCE, -1.0)).astype(np.float32))
        useimps.append((inrange & ~forced & valid).astype(np.float32))
    return np.concatenate(buckets), jnp.asarray(np.stack(sbases)), jnp.asarray(np.stack(useimps))


def _head_rows(b, t):
    k = b.shape[-1]
    n = b.shape[1] // t
    return b.reshape(G_KV, HPG, n, t, k).transpose(2, 0, 1, 3, 4).reshape(n, G_KV, HPG * t, k)


def _prep_weights(rel_bias, w_in, conv_w_pw, gla_w_gk, nsa_q_norm_g, nsa_k_norm_g, nsa_pe_cmp, nsa_w_cmp,
                  nsa_b_gate, w_out):
    depth = w_in.shape[0]
    cols = _in_proj_columns()
    pieces, start = [], 0
    for c in range(1, N_IN + 1):
        pad = cols[start] < 0
        if c == N_IN or (cols[c] >= 0 if pad else cols[c] != cols[c - 1] + 1):
            src = int(cols[start])
            pieces.append(jnp.zeros((depth, D_MODEL, c - start), w_in.dtype) if pad else w_in[:, :, src:src + c - start])
            start = c
    w_in_p = jnp.concatenate(pieces, axis=2).astype(BF16)
    perm = 512 + _nsa_perm()
    w_out_p = jnp.concatenate([w_out[:, 0:512]] + [w_out[:, int(perm[c]):int(perm[c]) + HD]
                                                   for c in range(0, W_NSA, HD)], axis=1).astype(BF16)
    qg = jnp.tile(nsa_q_norm_g, (1, H_NSA))[:, None, :] * SCALE
    kg = jnp.tile(nsa_k_norm_g, (1, 1, G_KV))
    wgk = jnp.zeros((depth, LANES, LANES), F32).at[:, :GATE_RANK, :].set(gla_w_gk)
    wgk_hi = wgk.astype(BF16)
    wgk2 = jnp.stack([wgk_hi, (wgk - wgk_hi.astype(F32)).astype(BF16)], axis=1)
    def blockdiag2(a, b):
        za = jnp.zeros(a.shape[:-1] + (b.shape[-1],), a.dtype)
        zb = jnp.zeros(b.shape[:-1] + (a.shape[-1],), b.dtype)
        return jnp.concatenate([jnp.concatenate([a, za], axis=-1), jnp.concatenate([zb, b], axis=-1)], axis=-2)

    wc = nsa_w_cmp.astype(BF16)
    wtap = blockdiag2(wc, wc)
    wbd = blockdiag2(wtap[:, 0], wtap[:, 1]).reshape(depth, KW, 256)
    pe = jnp.tile(nsa_pe_cmp.transpose(0, 2, 1, 3)[:, :, :, None, :], (1, 1, 1, G_KV, 1))
    petap = pe.transpose(0, 2, 1, 3, 4).reshape(depth, 2, CMP_BLOCK, LANES)
    pe = pe.reshape(depth, 1, KW)
    bg = jnp.zeros((depth, 1, LANES), F32).at[:, 0, :3 * H_NSA].set(nsa_b_gate)
    expand = np.zeros((LANES, 3 * HPG * LANES), np.float32)
    for br in range(3):
        for g in range(G_KV):
            for j in range(HPG):
                c = LANES * (HPG * br + j) + 64 * g
                expand[H_NSA * br + HPG * g + j, c:c + 64] = 1.0
    far = rel_bias[N_BUCKETS - 1]
    far_hi = far.astype(BF16).astype(F32)
    cvec = jnp.zeros((H_NSA, 1, LANES), F32).at[:, 0, CONST_LANE].set(far_hi).at[:, 0, CONST_LANE + 1].set(far - far_hi)
    return dict(w_in=w_in_p, w_out=w_out_p, qg=qg, kg=kg, wgk=wgk2, wbd=wbd, pe=pe, bg=bg,
                wtap=wtap, petap=petap,
                expand=jnp.asarray(expand, BF16), cvec=cvec, wpw=conv_w_pw.astype(BF16),
                indq=jnp.asarray(_blockdiag_ones(W_NSA, HD), BF16), indk=jnp.asarray(_blockdiag_ones(LANES, HD), BF16))


def kernel(x_prompt, x_sample, cache_cmp_kv, cache_slc_kv, page_table, state_win_kv, state_gla, state_conv, rel_bias, norm_g, w_in, conv_w_dw, conv_b_dw, conv_ln_g, conv_ln_b, conv_w_pw, conv_b_pw, gla_w_gk, gla_b_gk, gla_norm_g, nsa_q_norm_g, nsa_k_norm_g, nsa_pe_cmp, nsa_w_cmp, nsa_b_gate, w_out, norm_f):
    depth = w_in.shape[0]
    n_bp, t_p, _ = x_prompt.shape
    n_bs, t_s, _ = x_sample.shape
    n_pages = page_table.shape[1]
    past = n_pages * PAGE_SIZE
    w_past = state_win_kv.shape[2]
    assert t_p % TQ == 0 and WINDOW == 2 * TQ and t_s == 8 and w_past == WINDOW and past % SEL_BLOCK == 0
    wp = _prep_weights(rel_bias, w_in, conv_w_pw, gla_w_gk, nsa_q_norm_g, nsa_k_norm_g, nsa_pe_cmp, nsa_w_cmp,
                       nsa_b_gate, w_out)
    table = rel_bias
    nblk_p = t_p // SEL_BLOCK
    nblk_s = (past + SEL_BLOCK) // SEL_BLOCK
    pos_p = [np.arange(i * LANES, (i + 1) * LANES) for i in range(t_p // LANES)]
    bk_p, sbase_p, useimp_p = _cmp_tables(pos_p, nblk_p)
    bias_p = _head_rows(_bias_table(table, bk_p, sub_far=False), LANES)
    nb_s = LANES // t_s
    qpos_s = past + np.arange(t_s)
    bk_s, _, _ = _cmp_tables([qpos_s], nblk_s)
    bias_s = _head_rows(_bias_table(table, bk_s, sub_far=False), t_s)
    _, sbase_1, useimp_1 = _cmp_tables([np.tile(qpos_s, nb_s)], nblk_s)
    tq = np.arange(TQ)
    every = np.ones((1, 1), bool)
    bk_dt = np.concatenate([_bucket_matrix(tq, tq, every).T, _bucket_matrix(TQ + tq, tq, every).T])
    dt = _bias_table(table, bk_dt, sub_far=True)
    dt = dt.reshape(G_KV, HPG, 2, TQ, TQ).transpose(2, 0, 3, 1, 4).reshape(2, G_KV, TQ, HPG * TQ)
    edge = jnp.asarray(np.tile(np.where(tq[:, None] > tq[None, :], 0.0, NEG).astype(np.float32), (1, HPG)))
    n_keys = past + LANES
    kpos = np.arange(n_keys)
    dsl = _head_rows(_bias_table(table, _bucket_matrix(qpos_s, kpos, (kpos < past + t_s)[None, :]),
                                 sub_far=True), t_s)[0]
    n_wkeys = w_past + LANES
    wk = np.arange(n_wkeys)
    wpos = past - w_past + wk
    wvalid = (wk < w_past + t_s)[None, :] & ((qpos_s[:, None] - wpos[None, :]) < WINDOW) & (wpos >= 0)[None, :]
    dwn = _head_rows(_bias_table(table, _bucket_matrix(qpos_s, wpos, wvalid), sub_far=True), t_s)[0]

    cache_cmp = cache_cmp_kv.transpose(0, 1, 3, 4, 5, 2)
    cache_slc = cache_slc_kv.transpose(0, 1, 3, 4, 5, 2)
    win_state = state_win_kv.transpose(0, 1, 3, 4, 5, 2)
    s0 = state_gla.transpose(0, 1, 2, 4, 3).reshape(depth, n_bs, W_GLA, DK_GLA)
    i256 = np.arange(W_GLA)
    i128 = np.arange(LANES)
    bdmask = jnp.asarray((i256[:, None] // DV_GLA == i128[None, :] // DK_GLA).astype(np.float32))
    s0 = jnp.tile(s0, (1, 1, 1, H_GLA)) * bdmask
    zero_s0 = jnp.zeros((n_bp, W_GLA, LANES), F32)
    zero_hist = jnp.zeros((n_bp, HIST, W_CONV), F32)

    def unpack_state(st):
        b = st.shape[0]
        blocks = [st[:, DV_GLA * h:DV_GLA * (h + 1), DK_GLA * h:DK_GLA * (h + 1)] for h in range(H_GLA)]
        return jnp.stack(blocks, axis=1).transpose(0, 1, 3, 2)

    xp = x_prompt.reshape(n_bp * t_p, D_MODEL)
    xs = x_sample.reshape(n_bs * t_s, D_MODEL)
    outs_p, outs_s = [], []
    chain_p, win_all = (), None
    for l in range(depth):
        final = l == depth - 1
        row = lambda a: a[l][None, :]
        common = dict(wdw=conv_w_dw[l], bdw=row(conv_b_dw), lng=row(conv_ln_g), lnb=row(conv_ln_b), wpw=wp["wpw"][l],
                      bpw=row(conv_b_pw))
        bgk = jnp.zeros((1, LANES), F32).at[0, :].set(gla_b_gk[l])
        gng = jnp.tile(gla_norm_g[l], (H_GLA,))[None, :]
        kg12 = wp["kg"][l, 1:3]
        kg0 = wp["kg"][l, 0:1]
        zc, zg, qn, cmp_all, slc_all, win_n, ngate, ng, ksa, vs, kwa, vw = _in_proj(
            xp, row(norm_g), wp["w_in"][l], wp["indq"], wp["indk"], wp["qg"][l], kg12, seq_len=t_p, prompt=True,
            layer=l, depth=depth, chain=chain_p)
        chain_p = (cmp_all, slc_all)
        yc, conv_st = _conv(zc, zero_hist, nb=1, t_len=t_p, **common)
        yg, gla_st = _gla(zg, zero_s0, wp["wgk"][l], bgk, gng, nb=1, t_len=t_p, chunk=16)
        kvc = _compress_prompt(cmp_all, l, wp["wbd"][l], wp["pe"][l], n_b=n_bp, t_len=t_p)
        *q_ts, ocmp = _cmpsel(qn, kvc, bias_p, sbase_p, useimp_p, wp["indk"], kg0, wp["cvec"], nb=1, tb=LANES,
                              n_blocks=nblk_p, q_t=True)
        yn = _flash(q_ts, ksa, vs, kwa, vw, dt, edge, ocmp, ng, ngate, wp["bg"][l], wp["expand"], n_b=n_bp, t_len=t_p)
        xp = _out_proj(xp, yc, yg, yn, wp["w_out"][l], norm_f[None, :], final=final)
        w_keep = min(WINDOW, t_p)
        outs_p.append((conv_st, unpack_state(gla_st), win_n.reshape(n_bp, t_p, 256)[:, t_p - w_keep:]))
        zc, zg, qn, cmp_n, slc_n, win_n, ngate, ng = _in_proj(
            xs, row(norm_g), wp["w_in"][l], wp["indq"], wp["indk"], wp["qg"][l], kg12, seq_len=t_s, prompt=False)
        cmp_n, slc_n = cmp_n[0], slc_n[0]
        yc, conv_st = _conv(zc, state_conv[l], nb=nb_s, t_len=t_s, **common)
        yg, gla_st = _gla(zg, s0[l], wp["wgk"][l], bgk, gng, nb=nb_s, t_len=t_s, chunk=t_s)
        kvc = _compress_paged(page_table, cache_cmp, l, cmp_n, wp["wtap"][l], wp["petap"][l], t_new=t_s)
        qa, ocmp = _cmpsel(qn, kvc, bias_s, sbase_1, useimp_1, wp["indk"], kg0, wp["cvec"], nb=nb_s, tb=t_s,
                           n_blocks=nblk_s, q_t=False)
        yn, win_all = _dec_attn(page_table, cache_slc, l, qa, slc_n, win_n, win_state, dsl, dwn, ocmp, ng, ngate,
                                wp["bg"][l], wp["expand"], win_all, t_new=t_s)
        xs = _out_proj(xs, yc, yg, yn, wp["w_out"][l], norm_f[None, :], final=final)
        outs_s.append((conv_st, unpack_state(gla_st), None, cmp_n, slc_n))

    def stack(outs, k, shape):
        return jnp.stack([o[k] for o in outs]).reshape(shape)

    kv = (2, G_KV, HD)
    return (xp.reshape(n_bp, t_p, D_MODEL), xs.reshape(n_bs, t_s, D_MODEL),
            stack(outs_p, 0, (depth, n_bp, HIST, W_CONV)), stack(outs_s, 0, (depth, n_bs, HIST, W_CONV)),
            stack(outs_p, 1, (depth, n_bp, H_GLA, DK_GLA, DV_GLA)), stack(outs_s, 1, (depth, n_bs, H_GLA, DK_GLA, DV_GLA)),
            stack(outs_p, 2, (depth, n_bp, min(WINDOW, t_p)) + kv), win_all.transpose(0, 1, 5, 2, 3, 4),
            chain_p[0].reshape((depth, n_bp, t_p) + kv), stack(outs_s, 3, (depth, n_bs, t_s) + kv),
            chain_p[1].reshape((depth, n_bp, t_p) + kv), stack(outs_s, 4, (depth, n_bs, t_s) + kv))
```

```python
import functools
import math

import numpy as np
import jax
import jax.numpy as jnp
from jax import lax
from jax.experimental import pallas as pl
from jax.experimental.pallas import tpu as pltpu

F32 = jnp.float32
BF16 = jnp.bfloat16

D_MODEL = 1024
W_CONV = 256
CONV_WIDTH = 31
HIST = CONV_WIDTH - 1
H_GLA = 4
DK_GLA = 32
DV_GLA = 64
W_GLA = H_GLA * DV_GLA
GATE_RANK = 16
GATE_NORMALIZER = 16.0
H_NSA = 8
HD = 64
G_KV = 2
HPG = H_NSA // G_KV
W_NSA = H_NSA * HD
CMP_BLOCK = 32
SEL_BLOCK = 64
TOP_N = 8
WINDOW = 512
N_BUCKETS = 32
MAX_EXACT = N_BUCKETS // 2
MAX_DISTANCE = 128
PAGE_SIZE = 128
SCALE = HD ** -0.5
EPS = 1e-6
NEG = -1e30
FORCE = 1e4

LANES = 128
SEL_LANE0 = 64
SEL_SLOTS = 40
CONST_LANE = 112
ODD_SLOT0 = 64
TQ = 256
V_ROWS = 128
KEY_CHUNK = 256
COL_CHUNK = 128
VMEM_LIMIT = 56 * 1024 * 1024

C_CONV = 0
C_GLA = 3 * W_CONV
N_GLA = 2 * H_GLA * DK_GLA + 2 * W_GLA + LANES
C_NSA = C_GLA + N_GLA
N_NSA = W_NSA + 3 * 256 + W_NSA + LANES
N_IN = C_NSA + N_NSA


def _rel_bucket_np(dist):
    n = np.maximum(dist, 0)
    nf = np.maximum(n, 1).astype(np.float32)
    large = MAX_EXACT + (np.log(nf / np.float32(MAX_EXACT)) / np.float32(math.log(MAX_DISTANCE / MAX_EXACT))
                         * np.float32(N_BUCKETS - MAX_EXACT)).astype(np.int32)
    large = np.minimum(large, N_BUCKETS - 1)
    return np.where(n < MAX_EXACT, n, large).astype(np.int32)


def _in_proj_columns():
    o = {}
    off = 0
    for name, w in (("c_a", 256), ("c_b", 256), ("c_gate", 256), ("l_q", 128), ("l_k", 128), ("l_v", 256),
                    ("l_gk", 16), ("l_gate", 256), ("n_q", 512), ("n_cmp", 256), ("n_slc", 256),
                    ("n_win", 256), ("n_g", 24), ("n_gate", 512)):
        o[name] = off
        off += w
    cols = -np.ones((N_IN,), np.int64)

    def put(dst, name, width):
        cols[dst:dst + width] = o[name] + np.arange(width)

    put(0, "c_a", 256); put(256, "c_b", 256); put(512, "c_gate", 256)
    g = C_GLA
    put(g, "l_q", 128); put(g + 128, "l_k", 128); put(g + 256, "l_v", 256); put(g + 512, "l_gate", 256)
    put(g + 768, "l_gk", 16)
    n = C_NSA
    put(n, "n_q", 512); put(n + 512, "n_cmp", 256); put(n + 768, "n_slc", 256); put(n + 1024, "n_win", 256)
    cols[n + 1280:n + 1792] = o["n_gate"] + _nsa_perm()
    put(n + 1792, "n_g", 24)
    return cols


def _nsa_perm():
    p = np.zeros((W_NSA,), np.int64)
    for j in range(HPG):
        for g in range(G_KV):
            p[128 * j + 64 * g:128 * j + 64 * g + 64] = 64 * (HPG * g + j) + np.arange(64)
    return p


def _blockdiag_ones(n, blk):
    i = np.arange(n)
    return (i[:, None] // blk == i[None, :] // blk).astype(np.float32)


def _split2(x):
    hi = x.astype(BF16)
    lo = (x - hi.astype(F32)).astype(BF16)
    return hi, lo


def _dot(a, b):
    return jnp.dot(a, b, preferred_element_type=F32)


def _dot_nt(a, b):
    return lax.dot_general(a, b, (((1,), (1,)), ((), ())), preferred_element_type=F32)


def _dot_tn(a, b):
    return lax.dot_general(a, b, (((0,), (0,)), ((), ())), preferred_element_type=F32)


def _dot_split(x, m_bf16):
    hi, lo = _split2(x)
    return _dot(hi, m_bf16) + _dot(lo, m_bf16)


def _seg_rms(x, ind_ref, seg):
    return lax.rsqrt(_dot_split(x * x, ind_ref[...]) * (1.0 / seg) + EPS)


def _sigmoid(x):
    return 1.0 / (1.0 + jnp.exp(-x))


def _silu(x):
    return x * _sigmoid(x)


def _lane(shape):
    return lax.broadcasted_iota(jnp.int32, shape, len(shape) - 1)


def _half_pad(x, odd):
    if odd:
        x = pltpu.roll(x, 64, axis=1)
    return jnp.where(_lane(x.shape) < 64, x, 0.0)


def _inproj_body(x_ref, ng_ref, w_ref, indq_ref, indk_ref, qg_ref, kg_ref, *outs, tm, seq_len, prompt, n_chained):
    outs = outs[n_chained:]
    zc_ref, zg_ref, qn_ref, cmp_ref, slc_ref, win_ref, gate_ref, g_ref = outs[:8]
    x = x_ref[...]
    ms = jnp.mean(x * x, axis=-1, keepdims=True)
    h = (x * lax.rsqrt(ms + EPS) * ng_ref[...]).astype(BF16)

    def mm(lo, width):
        return _dot(h, w_ref[:, lo:lo + width])

    zc_ref[...] = mm(C_CONV, 3 * W_CONV)
    zg_ref[...] = mm(C_GLA, N_GLA)
    q = mm(C_NSA, W_NSA)
    qn_ref[...] = (q * _seg_rms(q, indq_ref, HD) * qg_ref[...]).astype(BF16)
    cmp = mm(C_NSA + 512, 256)
    cmp_ref[...] = cmp
    slc = mm(C_NSA + 768, 256)
    win = mm(C_NSA + 1024, 256)
    ks = slc[:, 0:128]
    ks = ks * _seg_rms(ks, indk_ref, HD) * kg_ref[0:1, :]
    kw = win[:, 0:128]
    kw = kw * _seg_rms(kw, indk_ref, HD) * kg_ref[1:2, :]
    slc_ref[:, 0:128] = ks
    slc_ref[:, 128:256] = slc[:, 128:256]
    win_ref[:, 0:128] = kw
    win_ref[:, 128:256] = win[:, 128:256]
    gate_ref[...] = mm(C_NSA + 1280, W_NSA)
    g_ref[...] = mm(C_NSA + 1792, LANES)
    if prompt:
        ksa_ref, vs_ref, kwa_ref, vw_ref, cmpt_ref, slct_ref = outs[8:]
        cmpt_ref[...] = cmp.T.reshape(2, G_KV, HD, tm)
        slct_ref[...] = jnp.concatenate([ks, slc[:, 128:256]], axis=1).T.reshape(2, G_KV, HD, tm)
        lane = _lane((tm, LANES))
        row = lax.broadcasted_iota(jnp.int32, (tm, LANES), 0)
        t = lax.rem(pl.program_id(0) * tm + row, seq_len)
        ones = ((lane == CONST_LANE) | (lane == CONST_LANE + 1)).astype(F32)
        onehot = ((lane - SEL_LANE0) == lax.shift_right_logical(t, 6)).astype(F32)
        for g in range(G_KV):
            ksa_ref[g] = (_half_pad(ks, g == 1) + onehot + ones).astype(BF16)
            kwa_ref[g] = (_half_pad(kw, g == 1) + ones).astype(BF16)
        feat = lax.broadcasted_iota(jnp.int32, (LANES, tm), 0)
        for v_ref, src in ((vs_ref, slc), (vw_ref, win)):
            vt = src[:, 128:256].T
            for g in range(G_KV):
                own = vt if g == 0 else pltpu.roll(vt, HD, axis=0)
                v_ref[g] = jnp.where(feat < HD, own, (feat == HD).astype(F32))[0:V_ROWS].astype(BF16)


def _in_proj(x2d, norm_g, w, indq, indk, qg, kg, *, seq_len, prompt, layer=0, depth=1, chain=()):
    m = x2d.shape[0]
    tm = min(256, m)
    assert m % tm == 0 and (not prompt or seq_len % tm == 0)
    row = lambda n: pl.BlockSpec((tm, n), lambda i: (i, 0))
    full = lambda a: pl.BlockSpec(a.shape, lambda i: (0,) * a.ndim)
    out_shape = [jax.ShapeDtypeStruct((m, 3 * W_CONV), F32), jax.ShapeDtypeStruct((m, N_GLA), F32),
                 jax.ShapeDtypeStruct((m, W_NSA), BF16), jax.ShapeDtypeStruct((m, 256), F32),
                 jax.ShapeDtypeStruct((m, 256), F32), jax.ShapeDtypeStruct((m, 256), F32),
                 jax.ShapeDtypeStruct((m, W_NSA), F32), jax.ShapeDtypeStruct((m, LANES), F32)]
    out_specs = [row(3 * W_CONV), row(N_GLA), row(W_NSA), row(256), row(256), row(256), row(W_NSA), row(LANES)]
    if prompt:
        aug = pl.BlockSpec((G_KV, tm, LANES), lambda i: (0, i, 0))
        assert tm == TQ
        vt = pl.BlockSpec((None, G_KV, V_ROWS, tm), lambda i: (i, 0, 0, 0))
        vt_shape = jax.ShapeDtypeStruct((m // tm, G_KV, V_ROWS, tm), BF16)
        per_seq = seq_len // tm
        leaf = pl.BlockSpec((None, None, 2, G_KV, HD, tm), lambda i: (layer, i // per_seq, 0, 0, 0, i % per_seq))
        leaf_shape = jax.ShapeDtypeStruct((depth, m // seq_len, 2, G_KV, HD, seq_len), F32)
        out_shape += [jax.ShapeDtypeStruct((G_KV, m, LANES), BF16), vt_shape,
                      jax.ShapeDtypeStruct((G_KV, m, LANES), BF16), vt_shape, leaf_shape, leaf_shape]
        out_specs += [aug, vt, aug, vt, leaf, leaf]
    args = [x2d, norm_g, w, indq, indk, qg, kg]
    return pl.pallas_call(
        functools.partial(_inproj_body, tm=tm, seq_len=seq_len, prompt=prompt, n_chained=len(chain)),
        grid=(m // tm,),
        in_specs=[row(D_MODEL), full(norm_g), full(w), full(indq), full(indk), full(qg), full(kg)]
        + [pl.BlockSpec(memory_space=pl.ANY)] * len(chain),
        out_specs=out_specs, out_shape=out_shape,
        input_output_aliases={len(args) + k: 12 + k for k in range(len(chain))},
        compiler_params=pltpu.CompilerParams(dimension_semantics=("arbitrary",), vmem_limit_bytes=VMEM_LIMIT),
        name="in_proj",
    )(*args, *chain)


def _outproj_body(x_ref, yc_ref, yg_ref, yn_ref, w_ref, nf_ref, o_ref, *, final):
    x = x_ref[...]
    x = (x + _dot(yc_ref[...], w_ref[0:256, :]) + _dot(yg_ref[...], w_ref[256:512, :])
         + _dot(yn_ref[...].astype(BF16), w_ref[512:1024, :]))
    if final:
        ms = jnp.mean(x * x, axis=-1, keepdims=True)
        x = x * lax.rsqrt(ms + EPS) * nf_ref[...]
    o_ref[...] = x


def _out_proj(x2d, yc, yg, yn, w, norm_f, *, final):
    m = x2d.shape[0]
    tm = min(512, m)
    assert m % tm == 0
    row = lambda n: pl.BlockSpec((tm, n), lambda i: (i, 0))
    full = lambda a: pl.BlockSpec(a.shape, lambda i: (0,) * a.ndim)
    return pl.pallas_call(
        functools.partial(_outproj_body, final=final),
        grid=(m // tm,),
        in_specs=[row(D_MODEL), row(256), row(256), row(512), full(w), full(norm_f)],
        out_specs=row(D_MODEL), out_shape=jax.ShapeDtypeStruct((m, D_MODEL), F32),
        compiler_params=pltpu.CompilerParams(dimension_semantics=("arbitrary",), vmem_limit_bytes=VMEM_LIMIT),
        name="out_proj",
    )(x2d, yc, yg, yn, w, norm_f)


def _conv_body(zc_ref, hist_ref, wdw_ref, bdw_ref, lng_ref, lnb_ref, wpw_ref, bpw_ref, y_ref, st_ref,
               ext_scr, act_scr, *, nb, t_len, tc, tmm):
    n_chunks = t_len // tc

    def per_batch(bi, _):
        base = bi * t_len
        ext_scr[0:32, :] = jnp.zeros((32, W_CONV), F32)
        ext_scr[2:32, :] = hist_ref[bi]

        def chunk(c, _):
            r0 = pl.multiple_of(c * tc, tc)
            g0 = pl.multiple_of(base + r0, tc)
            a = zc_ref[pl.ds(g0, tc), 0:256]
            b = zc_ref[pl.ds(g0, tc), 256:512]
            ext_scr[pl.ds(32 + r0, tc), :] = a * _sigmoid(b)
            win = ext_scr[pl.ds(r0, tc + 32), :]
            acc = jnp.zeros((tc, W_CONV), F32) + bdw_ref[...]
            for r in range(8):
                wr = pltpu.roll(win, tc + 32 - (2 + r), axis=0)
                for mi, j in enumerate(range(r, CONV_WIDTH, 8)):
                    acc = acc + wr[8 * mi:8 * mi + tc] * wdw_ref[j:j + 1, :]
            mu = jnp.mean(acc, axis=-1, keepdims=True)
            d = acc - mu
            var = jnp.mean(d * d, axis=-1, keepdims=True)
            yn = d * lax.rsqrt(var + EPS) * lng_ref[...] + lnb_ref[...]
            act_scr[pl.ds(g0, tc), :] = _silu(yn)
            return 0

        lax.fori_loop(0, n_chunks, chunk, 0)
        st_ref[bi] = ext_scr[2 + t_len:32 + t_len, :]
        return 0

    lax.fori_loop(0, nb, per_batch, 0)

    def mm(c, _):
        r0 = pl.multiple_of(c * tmm, tmm)
        y = _dot(act_scr[pl.ds(r0, tmm), :].astype(BF16), wpw_ref[...]) + bpw_ref[...]
        y_ref[pl.ds(r0, tmm), :] = (y * _silu(zc_ref[pl.ds(r0, tmm), 512:768])).astype(BF16)
        return 0

    lax.fori_loop(0, nb * t_len // tmm, mm, 0)


def _conv(zc, hist, wdw, bdw, lng, lnb, wpw, bpw, *, nb, t_len):
    m = zc.shape[0]
    n_b = m // t_len
    tc = min(64, t_len)
    tmm = min(256, nb * t_len)
    full = lambda a: pl.BlockSpec(a.shape, lambda i: (0,) * a.ndim)
    return pl.pallas_call(
        functools.partial(_conv_body, nb=nb, t_len=t_len, tc=tc, tmm=tmm),
        grid=(n_b // nb,),
        in_specs=[pl.BlockSpec((nb * t_len, 3 * W_CONV), lambda i: (i, 0)),
                  pl.BlockSpec((nb, HIST, W_CONV), lambda i: (i, 0, 0)),
                  full(wdw), full(bdw), full(lng), full(lnb), full(wpw), full(bpw)],
        out_specs=[pl.BlockSpec((nb * t_len, W_CONV), lambda i: (i, 0)),
                   pl.BlockSpec((nb, HIST, W_CONV), lambda i: (i, 0, 0))],
        out_shape=[jax.ShapeDtypeStruct((m, W_CONV), BF16), jax.ShapeDtypeStruct((n_b, HIST, W_CONV), F32)],
        scratch_shapes=[pltpu.VMEM((32 + t_len, W_CONV), F32), pltpu.VMEM((nb * t_len, W_CONV), F32)],
        compiler_params=pltpu.CompilerParams(dimension_semantics=("arbitrary",), vmem_limit_bytes=VMEM_LIMIT),
        name="conv",
    )(zc, hist, wdw, bdw, lng, lnb, wpw, bpw)


def _gla_body(zg_ref, s0_ref, wgk_ref, bgk_ref, ltri_ref, lsum_ref, ind_ref, indv_ref, ng_ref, bd_ref,
              y_ref, st_ref, bc_scr, qe_scr, ke_scr, dec_scr, o_scr, *, nb, t_len, chunk, rb):
    rows = nb * t_len
    n_rb = rows // rb
    cpb = t_len // chunk

    def phase1(i, _):
        r0 = pl.multiple_of(i * rb, rb)
        gk = zg_ref[pl.ds(r0, rb), 768:896]
        pre = _dot_split(gk, wgk_ref[0]) + _dot(gk.astype(BF16), wgk_ref[1]) + bgk_ref[...]
        la = (jnp.minimum(pre, 0.0) - jnp.log(1.0 + jnp.exp(-jnp.abs(pre)))) * (1.0 / GATE_NORMALIZER)
        h1, l1 = _split2(la)
        l2 = (la - h1.astype(F32) - l1.astype(F32)).astype(BF16)
        bc = _dot(ltri_ref[...], h1) + _dot(ltri_ref[...], l1) + _dot(ltri_ref[...], l2)
        bt = _dot(lsum_ref[...], h1) + _dot(lsum_ref[...], l1) + _dot(lsum_ref[...], l2)
        bc_scr[pl.ds(r0, rb), :] = bc
        qe_scr[pl.ds(r0, rb), :] = zg_ref[pl.ds(r0, rb), 0:128] * (DK_GLA ** -0.5) * jnp.exp(bc)
        ke_scr[pl.ds(r0, rb), :] = zg_ref[pl.ds(r0, rb), 128:256] * jnp.exp(bt - bc)
        dec_scr[pl.ds(r0, rb), :] = jnp.exp(bt)
        return 0

    lax.fori_loop(0, n_rb, phase1, 0)

    ti = lax.broadcasted_iota(jnp.int32, (chunk, chunk, LANES), 0)
    si = lax.broadcasted_iota(jnp.int32, (chunk, chunk, LANES), 1)
    causal = si <= ti

    u2 = 2 if (rows // chunk) % 2 == 0 else 1
    u3 = 4 if cpb % 4 == 0 else 1

    def phase2(cg, _):
        for u in range(u2):
            r0 = pl.multiple_of((cg * u2 + u) * chunk, chunk)
            bc = bc_scr[pl.ds(r0, chunk), :]
            q = zg_ref[pl.ds(r0, chunk), 0:128] * (DK_GLA ** -0.5)
            k = zg_ref[pl.ds(r0, chunk), 128:256]
            v = zg_ref[pl.ds(r0, chunk), 256:512]
            e = jnp.exp(jnp.where(causal, bc[:, None, :] - bc[None, :, :], NEG))
            p = (q[:, None, :] * k[None, :, :] * e).reshape(chunk * chunk, LANES)
            att = _dot_split(p, ind_ref[...]).reshape(chunk, chunk, W_GLA)
            o_scr[pl.ds(r0, chunk), :] = (att * v[None, :, :]).sum(axis=1)
        return 0

    lax.fori_loop(0, rows // chunk // u2, phase2, 0)

    def per_batch(bi, _):
        def group(cg, s):
            for u in range(u3):
                r0 = pl.multiple_of(bi * t_len + (cg * u3 + u) * chunk, chunk)
                o_scr[pl.ds(r0, chunk), :] += _dot_nt(qe_scr[pl.ds(r0, chunk), :].astype(BF16), s.astype(BF16))
                upd = _dot_tn(zg_ref[pl.ds(r0, chunk), 256:512].astype(BF16),
                              ke_scr[pl.ds(r0, chunk), :].astype(BF16))
                s = s * dec_scr[pl.ds(r0, 1), :] + upd * bd_ref[...]
            return s

        st_ref[bi] = lax.fori_loop(0, cpb // u3, group, s0_ref[bi])
        return 0

    lax.fori_loop(0, nb, per_batch, 0)

    def phase4(i, _):
        r0 = pl.multiple_of(i * rb, rb)
        o = o_scr[pl.ds(r0, rb), :]
        o = o * _seg_rms(o, indv_ref, DV_GLA) * ng_ref[...]
        y_ref[pl.ds(r0, rb), :] = (o * _silu(zg_ref[pl.ds(r0, rb), 512:768])).astype(BF16)
        return 0

    lax.fori_loop(0, n_rb, phase4, 0)


def _gla(zg, s0t, wgk, bgk, ng, *, nb, t_len, chunk):
    m = zg.shape[0]
    n_b = m // t_len
    rows = nb * t_len
    rb = min(128, rows)
    assert rows % rb == 0 and rb % chunk == 0 and t_len % chunk == 0
    ltri = jnp.asarray(_blockdiag_ones(rb, chunk) * np.tril(np.ones((rb, rb), np.float32)), BF16)
    lsum = jnp.asarray(_blockdiag_ones(rb, chunk), BF16)
    i128 = np.arange(LANES)
    i256 = np.arange(W_GLA)
    ind = jnp.asarray((i128[:, None] // DK_GLA == i256[None, :] // DV_GLA).astype(np.float32), BF16)
    indv = jnp.asarray(_blockdiag_ones(W_GLA, DV_GLA), BF16)
    bd = jnp.asarray((i256[:, None] // DV_GLA == i128[None, :] // DK_GLA).astype(np.float32))
    full = lambda a: pl.BlockSpec(a.shape, lambda i: (0,) * a.ndim)
    return pl.pallas_call(
        functools.partial(_gla_body, nb=nb, t_len=t_len, chunk=chunk, rb=rb),
        grid=(n_b // nb,),
        in_specs=[pl.BlockSpec((rows, N_GLA), lambda i: (i, 0)),
                  pl.BlockSpec((nb, W_GLA, LANES), lambda i: (i, 0, 0)),
                  full(wgk), full(bgk), full(ltri), full(lsum), full(ind), full(indv), full(ng), full(bd)],
        out_specs=[pl.BlockSpec((rows, W_GLA), lambda i: (i, 0)),
                   pl.BlockSpec((nb, W_GLA, LANES), lambda i: (i, 0, 0))],
        out_shape=[jax.ShapeDtypeStruct((m, W_GLA), BF16), jax.ShapeDtypeStruct((n_b, W_GLA, LANES), F32)],
        scratch_shapes=[pltpu.VMEM((rows, LANES), F32), pltpu.VMEM((rows, LANES), F32),
                        pltpu.VMEM((rows, LANES), F32), pltpu.VMEM((rows, LANES), F32),
                        pltpu.VMEM((rows, W_GLA), F32)],
        compiler_params=pltpu.CompilerParams(dimension_semantics=("arbitrary",), vmem_limit_bytes=VMEM_LIMIT),
        name="gla",
    )(zg, s0t, wgk, bgk, ltri, lsum, ind, indv, ng, bd)


KW = CMP_BLOCK * 256


def _compress_rows(x_ref, w_ref, pe_ref, kvc_ref, n_rows):
    kvc_ref[...] = jnp.zeros(kvc_ref.shape, F32)
    for par in range(2):
        x = (x_ref[:, par * KW:(par + 1) * KW] + pe_ref[...]).astype(BF16)
        kvc_ref[ODD_SLOT0 * par:ODD_SLOT0 * par + n_rows, :] = _dot(x, w_ref[...])


def _compress_prompt_body(x_ref, w_ref, pe_ref, kvc_ref, *, n_rows):
    _compress_rows(x_ref, w_ref, pe_ref, kvc_ref, n_rows)


def _compress_prompt(cmp_rows, w, pe, *, n_b, t_len):
    n_rows = t_len // SEL_BLOCK
    full = lambda a: pl.BlockSpec(a.shape, lambda i: (0,) * a.ndim)
    return pl.pallas_call(
        functools.partial(_compress_prompt_body, n_rows=n_rows),
        grid=(n_b,),
        in_specs=[pl.BlockSpec((n_rows, 2 * KW), lambda i: (i, 0)), full(w), full(pe)],
        out_specs=pl.BlockSpec((None, LANES, 256), lambda i: (i, 0, 0)),
        out_shape=jax.ShapeDtypeStruct((n_b, LANES, 256), F32),
        compiler_params=pltpu.CompilerParams(dimension_semantics=("arbitrary",), vmem_limit_bytes=VMEM_LIMIT),
        name="compress_prompt",
    )(cmp_rows.reshape(n_b * n_rows, 2 * KW), w, pe)


def _compress_paged_body(pt_ref, *refs, n_pages, t_new, n_rows):
    pages = refs[:n_pages]
    new_ref, w_ref, pe_ref, kvc_ref, x_scr, acc_scr = refs[n_pages:]
    per_page = PAGE_SIZE // CMP_BLOCK
    n_past = n_pages * per_page
    new = new_ref[...]
    for s in range(2):
        for p in range(n_pages):
            rows = pages[p][s].reshape(LANES, PAGE_SIZE).T
            for c in range(per_page):
                r0 = X_PITCH * (per_page * p + c)
                x_scr[s, r0:r0 + CMP_BLOCK, :] = rows[CMP_BLOCK * c:CMP_BLOCK * (c + 1)]
        x_scr[s, X_PITCH * n_past:, :] = jnp.zeros((X_PITCH * (2 * n_rows - n_past), LANES), F32)
        x_scr[s, X_PITCH * n_past:X_PITCH * n_past + t_new, :] = new[:, LANES * s:LANES * (s + 1)]
    kvc_ref[...] = jnp.zeros(kvc_ref.shape, F32)
    for s in range(2):
        acc = jnp.zeros((2 * n_rows, LANES), F32)
        for j in range(CMP_BLOCK):
            x = x_scr[s, pl.ds(j, 2 * n_rows, stride=X_PITCH), :] + pe_ref[s, j:j + 1, :]
            acc = acc + _dot(x.astype(BF16), w_ref[s, j])
        acc_scr[...] = acc
        for par in range(2):
            kvc_ref[ODD_SLOT0 * par:ODD_SLOT0 * par + n_rows, LANES * s:LANES * (s + 1)] = (
                acc_scr[pl.ds(par, n_rows, stride=2), :])


def _paged_specs(layer, n_pages, block):
    return [pl.BlockSpec((None, None) + block, functools.partial(
        lambda b, pt, p: (layer, pt[b, p]) + (0,) * len(block), p=p)) for p in range(n_pages)]


PAGE_BLOCK = (2, G_KV, HD, PAGE_SIZE)
X_PITCH = CMP_BLOCK + 4


def _compress_paged(page_table, cache_t, layer, cmp_new, w, pe, *, t_new):
    n_b, n_pages = page_table.shape
    n_rows = -(-((n_pages * PAGE_SIZE + SEL_BLOCK) // SEL_BLOCK) // 8) * 8
    full = lambda a: pl.BlockSpec(a.shape, lambda b, pt: (0,) * a.ndim)
    grid_spec = pltpu.PrefetchScalarGridSpec(
        num_scalar_prefetch=1, grid=(n_b,),
        in_specs=_paged_specs(layer, n_pages, PAGE_BLOCK) + [
            pl.BlockSpec((t_new, 256), lambda b, pt: (b, 0)), full(w), full(pe)],
        out_specs=pl.BlockSpec((None, LANES, 256), lambda b, pt: (b, 0, 0)),
        scratch_shapes=[pltpu.VMEM((2, 2 * n_rows * X_PITCH, LANES), F32), pltpu.VMEM((2 * n_rows, LANES), F32)])
    return pl.pallas_call(
        functools.partial(_compress_paged_body, n_pages=n_pages, t_new=t_new, n_rows=n_rows),
        grid_spec=grid_spec, out_shape=jax.ShapeDtypeStruct((n_b, LANES, 256), F32),
        compiler_params=pltpu.CompilerParams(dimension_semantics=("arbitrary",), vmem_limit_bytes=VMEM_LIMIT),
        name="compress_paged",
    )(page_table, *([cache_t] * n_pages), cmp_new, w, pe)


def _cmpsel_body(qn_ref, kvc_ref, bias_ref, sbase_ref, useimp_ref, indk_ref, kg_ref, cvec_ref,
                 *outs, nb, tb, n_blocks, q_t):
    q_refs, ocmp_ref = outs[:-1], outs[-1]
    rows = nb * tb
    lane = _lane((rows, LANES))
    qn = qn_ref[...].astype(F32)
    qpad = [_half_pad(qn[:, 128 * (h // 2):128 * (h // 2) + 128], h % 2 == 1) for h in range(H_NSA)]
    imp_parts = []
    o_parts = [[None] * nb for _ in range(H_NSA)]
    kc_all = kvc_ref[:, :, 0:128].reshape(nb * LANES, LANES)
    kc_all = kc_all * _seg_rms(kc_all, indk_ref, HD) * kg_ref[...]
    for bi in range(nb):
        kc = kc_all[bi * LANES:(bi + 1) * LANES]
        vboth = kvc_ref[bi, :, 128:256].astype(BF16)
        imp_b = []
        for g in range(G_KV):
            kpad = _half_pad(kc, g == 1).astype(BF16)
            qs = jnp.concatenate([qpad[HPG * g + j][bi * tb:(bi + 1) * tb] for j in range(HPG)], axis=0).astype(BF16)
            bias = bias_ref[g]
            l = _dot_nt(qs, kpad) + bias
            vis = bias > 0.5 * NEG
            mx = jnp.max(l, axis=-1, keepdims=True)
            p = jnp.where(vis, jnp.exp(l - mx), 0.0)
            p = p / jnp.maximum(jnp.sum(p, axis=-1, keepdims=True), 1e-30)
            o = _dot(p.astype(BF16), vboth)
            ig = p[0:tb]
            for j in range(1, HPG):
                ig = ig + p[j * tb:(j + 1) * tb]
            imp_b.append(ig)
            for j in range(HPG):
                o_parts[HPG * g + j][bi] = o[j * tb:(j + 1) * tb]
        imp_parts.append(imp_b)
    for j in range(HPG):
        o0 = jnp.concatenate(o_parts[j], axis=0) if nb > 1 else o_parts[j][0]
        o1 = jnp.concatenate(o_parts[HPG + j], axis=0) if nb > 1 else o_parts[HPG + j][0]
        ocmp_ref[:, 128 * j:128 * j + 128] = jnp.where(lane < 64, o0, o1)
    for g in range(G_KV):
        ig = jnp.concatenate([imp_parts[bi][g] for bi in range(nb)], axis=0) if nb > 1 else imp_parts[0][g]
        imp = ig + pltpu.roll(ig, LANES - ODD_SLOT0, axis=1)
        score = jnp.where(useimp_ref[...] > 0.5, imp, sbase_ref[...])
        st = score.T[0:SEL_SLOTS]
        jj = lax.broadcasted_iota(jnp.int32, (SEL_SLOTS, rows), 0)
        rank = jnp.zeros((SEL_SLOTS, rows), F32)
        for jp in range(n_blocks):
            r = st[jp:jp + 1, :]
            before = (r > st) | ((r == st) & (jp < jj))
            rank = rank + before.astype(F32)
        selb = jnp.where(rank < TOP_N - 0.5, 0.0, NEG)
        full_t = jnp.concatenate([jnp.zeros((SEL_LANE0, rows), F32), selb,
                                  jnp.zeros((LANES - SEL_LANE0 - SEL_SLOTS, rows), F32)], axis=0)
        if q_t:
            for j in range(HPG):
                h = HPG * g + j
                q_refs[j][g] = ((qpad[h] + cvec_ref[h]).T + full_t).astype(BF16)
        else:
            extra = full_t.T
            for j in range(HPG):
                h = HPG * g + j
                q_refs[0][g, j] = (qpad[h] + extra + cvec_ref[h]).astype(q_refs[0].dtype)


def _cmpsel(qn, kvc, bias, sbase, useimp, indk, kg0, cvec, *, nb, tb, n_blocks, q_t):
    m = qn.shape[0]
    rows = nb * tb
    assert rows == LANES and m % rows == 0
    n_pos = bias.shape[0]
    full = lambda a: pl.BlockSpec(a.shape, lambda i: (0,) * a.ndim)
    if q_t:
        q_specs = [pl.BlockSpec((G_KV, LANES, rows), lambda i: (0, 0, i))] * HPG
        q_shapes = [jax.ShapeDtypeStruct((G_KV, LANES, m), BF16)] * HPG
    else:
        q_specs = [pl.BlockSpec((G_KV, HPG, rows, LANES), lambda i: (0, 0, i, 0))]
        q_shapes = [jax.ShapeDtypeStruct((G_KV, HPG, m, LANES), F32)]
    return pl.pallas_call(
        functools.partial(_cmpsel_body, nb=nb, tb=tb, n_blocks=n_blocks, q_t=q_t),
        grid=(m // rows,),
        in_specs=[pl.BlockSpec((rows, W_NSA), lambda i: (i, 0)),
                  pl.BlockSpec((nb, LANES, 256), (lambda i: (i, 0, 0)) if nb > 1 else (lambda i: (i // n_pos, 0, 0))),
                  pl.BlockSpec((None, G_KV, HPG * tb, LANES), lambda i: (i % n_pos, 0, 0, 0)),
                  pl.BlockSpec((None, rows, LANES), lambda i: (i % n_pos, 0, 0)),
                  pl.BlockSpec((None, rows, LANES), lambda i: (i % n_pos, 0, 0)),
                  full(indk), full(kg0), full(cvec)],
        out_specs=q_specs + [pl.BlockSpec((rows, W_NSA), lambda i: (i, 0))],
        out_shape=q_shapes + [jax.ShapeDtypeStruct((m, W_NSA), F32)],
        compiler_params=pltpu.CompilerParams(dimension_semantics=("arbitrary",), vmem_limit_bytes=VMEM_LIMIT),
        name="cmpsel",
    )(qn, kvc, bias, sbase, useimp, indk, kg0, cvec)


def _mix_gates(g_ref, bg_ref, expand_ref, gate_ref, ocmp_ref, o_slc, o_win, y_ref):
    gl = _sigmoid(g_ref[...] + bg_ref[...])
    gx = _dot_split(gl, expand_ref[...])
    for j in range(HPG):
        sl = slice(128 * j, 128 * j + 128)
        o = (gx[:, 128 * j:128 * j + 128] * ocmp_ref[:, sl]
             + gx[:, 128 * (HPG + j):128 * (HPG + j) + 128] * o_slc[j]
             + gx[:, 128 * (2 * HPG + j):128 * (2 * HPG + j) + 128] * o_win[j])
        y_ref[:, sl] = (o * _silu(gate_ref[:, sl])).astype(y_ref.dtype)


def _softmax_step(q_t, k, v_t, bias_t, m_scr, acc_scr):
    n_key, n_col = k.shape[0], q_t.shape[1]
    for k0 in range(0, n_key, KEY_CHUNK):
        ks = slice(k0, k0 + KEY_CHUNK)
        for c0 in range(0, n_col, COL_CHUNK):
            cs = slice(c0, c0 + COL_CHUNK)
            s = _dot(k[ks], q_t[:, cs])
            if bias_t is not None:
                s = s + bias_t(ks, cs)
            m_prev = m_scr[:, cs]
            m_new = jnp.maximum(m_prev, jnp.max(s, axis=0, keepdims=True))
            p = jnp.exp(s - m_new)
            acc_scr[:, cs] = jnp.exp(m_prev - m_new) * acc_scr[:, cs] + _dot(v_t[:, ks], p.astype(BF16))
            m_scr[:, cs] = m_new


def _flash_body(q0_ref, q1_ref, q2_ref, q3_ref, ksa_ref, vs_ref, kwa_ref, vw_ref, dt_ref, edge_ref, ocmp_ref,
                g_ref, gate_ref, bg_ref, expand_ref, y_ref, m_scr, acc_scr):
    i = pl.program_id(1)
    cols = HPG * TQ
    m_scr[...] = jnp.full(m_scr.shape, NEG, F32)
    acc_scr[...] = jnp.zeros(acc_scr.shape, F32)

    def run(n_near):
        for g in range(G_KV):
            q_t = jnp.concatenate([r[g] for r in (q0_ref, q1_ref, q2_ref, q3_ref)], axis=1)
            for br, (ka_ref, v_ref) in enumerate(((ksa_ref, vs_ref), (kwa_ref, vw_ref))):
                def step(kt, bias_t):
                    k0 = pl.multiple_of(kt * TQ, TQ)
                    _softmax_step(q_t, ka_ref[g, pl.ds(k0, TQ), :], v_ref[kt, g], bias_t, m_scr.at[br, g],
                                  acc_scr.at[br, g])

                if n_near == 2:
                    if br == 0:
                        def far_pair(kp, _):
                            step(2 * kp, None)
                            step(2 * kp + 1, None)
                            return 0

                        lax.fori_loop(0, lax.shift_right_logical(i - 1, 1), far_pair, 0)

                        @pl.when(lax.rem(i - 1, 2) == 1)
                        def _():
                            step(i - 2, None)
                    else:
                        step(i - 2, lambda ks, cs: edge_ref[ks, cs])
                if n_near >= 1:
                    step(i - 1, lambda ks, cs: dt_ref[1, g, ks, cs])
                step(i, lambda ks, cs: dt_ref[0, g, ks, cs])

    for n_near, cond in ((0, i == 0), (1, i == 1), (2, i >= 2)):
        pl.when(cond)(functools.partial(run, n_near))

    outs = []
    for br in range(2):
        o_t = jnp.concatenate([acc_scr[br, g, 0:HD, :] / acc_scr[br, g, HD:HD + 1, :] for g in range(G_KV)], axis=0)
        outs.append([o_t[:, j * TQ:(j + 1) * TQ].T for j in range(HPG)])
    _mix_gates(g_ref, bg_ref, expand_ref, gate_ref, ocmp_ref, outs[0], outs[1], y_ref)


def _flash(q_ts, ksa, vs, kwa, vw, dt, edge, ocmp, ng, ngate, bg, expand, *, n_b, t_len):
    m = n_b * t_len
    nq = t_len // TQ
    cols = HPG * TQ
    full = lambda a: pl.BlockSpec(a.shape, lambda b, i: (0,) * a.ndim)
    qspec = pl.BlockSpec((G_KV, LANES, TQ), lambda b, i: (0, 0, b * nq + i))
    kspec = pl.BlockSpec((G_KV, None, t_len, LANES), lambda b, i: (0, b, 0, 0))
    vspec = pl.BlockSpec((nq, G_KV, V_ROWS, TQ), lambda b, i: (b, 0, 0, 0))
    row = lambda n: pl.BlockSpec((TQ, n), lambda b, i: (b * nq + i, 0))
    return pl.pallas_call(
        _flash_body,
        grid=(n_b, nq),
        in_specs=[qspec] * HPG + [kspec, vspec, kspec, vspec, full(dt), full(edge), row(W_NSA), row(LANES),
                                  row(W_NSA), full(bg), full(expand)],
        out_specs=row(W_NSA), out_shape=jax.ShapeDtypeStruct((m, W_NSA), BF16),
        scratch_shapes=[pltpu.VMEM((2, G_KV, 1, cols), F32), pltpu.VMEM((2, G_KV, V_ROWS, cols), F32)],
        compiler_params=pltpu.CompilerParams(dimension_semantics=("arbitrary", "arbitrary"),
                                             vmem_limit_bytes=VMEM_LIMIT),
        name="flash",
    )(*q_ts, ksa.reshape(G_KV, n_b, t_len, LANES), vs, kwa.reshape(G_KV, n_b, t_len, LANES), vw,
      dt, edge, ocmp, ng, ngate, bg, expand)


def _dec_attn_body(pt_ref, *refs, n_pages, t_new, n_keys, n_wkeys, chained):
    pages = refs[:n_pages]
    refs = list(refs[n_pages:])
    if chained:
        del refs[11]
    (qa_ref, slc_ref, win_ref, wst_ref, dsl_ref, dwn_ref, ocmp_ref, g_ref, gate_ref, bg_ref, expand_ref,
     y_ref, wout_ref, ka_scr, va_scr, kw_scr, vw_scr) = refs
    past = n_pages * PAGE_SIZE
    w_past = wst_ref.shape[-1]
    rows = HPG * t_new

    @pl.when(pl.program_id(0) == 0)
    def _():
        for k_scr, n, with_blocks in ((ka_scr, n_keys, True), (kw_scr, n_wkeys, False)):
            feat = lax.broadcasted_iota(jnp.int32, (LANES, n), 0)
            key = lax.broadcasted_iota(jnp.int32, (LANES, n), 1)
            aug = (feat == CONST_LANE) | (feat == CONST_LANE + 1)
            if with_blocks:
                aug = aug | ((feat - SEL_LANE0) == lax.shift_right_logical(key, 6))
            for g in range(G_KV):
                k_scr[g] = aug.astype(F32).astype(BF16)

    def new_keys_t(new):
        pad = jnp.concatenate([new, jnp.zeros((LANES - t_new, 256), F32)], axis=0)
        return pad[:, 0:128].T, pad[:, 128:256].T

    for p in range(n_pages):
        sl = slice(p * PAGE_SIZE, (p + 1) * PAGE_SIZE)
        for g in range(G_KV):
            ka_scr[g, 0:HD, sl] = pages[p][0, g].astype(BF16)
            va_scr[HD * g:HD * (g + 1), sl] = pages[p][1, g].astype(BF16)
    kn_t, vn_t = new_keys_t(slc_ref[...])
    va_scr[:, past:n_keys] = vn_t.astype(BF16)
    kwn_t, vwn_t = new_keys_t(win_ref[...])
    vw_scr[:, w_past:n_wkeys] = vwn_t.astype(BF16)
    lane = _lane((HD, LANES))
    for g in range(G_KV):
        ka_scr[g, 0:HD, past:n_keys] = kn_t[HD * g:HD * (g + 1)].astype(BF16)
        kw_scr[g, 0:HD, 0:w_past] = wst_ref[0, g].astype(BF16)
        kw_scr[g, 0:HD, w_past:n_wkeys] = kwn_t[HD * g:HD * (g + 1)].astype(BF16)
        vw_scr[HD * g:HD * (g + 1), 0:w_past] = wst_ref[1, g].astype(BF16)
        for s, new_t in ((0, kwn_t), (1, vwn_t)):
            sh = pltpu.roll(wst_ref[s, g], w_past - t_new, axis=1)
            tail = pltpu.roll(new_t[HD * g:HD * (g + 1)], LANES - t_new, axis=1)
            wout_ref[s, g, :, 0:w_past - LANES] = sh[:, 0:w_past - LANES]
            wout_ref[s, g, :, w_past - LANES:w_past] = jnp.where(lane >= LANES - t_new, tail, sh[:, w_past - LANES:])

    res = [[None, None], [None, None]]
    for g in range(G_KV):
        q = qa_ref[g].reshape(rows, LANES).astype(BF16)
        for br, (k_scr, v_scr, b_ref) in enumerate(((ka_scr, va_scr, dsl_ref), (kw_scr, vw_scr, dwn_ref))):
            s = _dot(q, k_scr[g]) + b_ref[g]
            mx = jnp.max(s, axis=-1, keepdims=True)
            p = jnp.exp(s - mx)
            den = jnp.sum(p, axis=-1, keepdims=True)
            res[br][g] = _dot_nt(p.astype(BF16), v_scr[...]) / den
    lane = _lane((t_new, LANES))
    outs = [[jnp.where(lane < 64, res[br][0][j * t_new:(j + 1) * t_new], res[br][1][j * t_new:(j + 1) * t_new])
             for j in range(HPG)] for br in range(2)]
    _mix_gates(g_ref, bg_ref, expand_ref, gate_ref, ocmp_ref, outs[0], outs[1], y_ref)


def _dec_attn(page_table, cache, layer, qa, slc_new, win_new, win_state, dsl, dwn, ocmp, ng, ngate, bg, expand,
              win_all, *, t_new):
    n_b, n_pages = page_table.shape
    m = n_b * t_new
    w_past = win_state.shape[-1]
    n_keys = dsl.shape[-1]
    n_wkeys = dwn.shape[-1]
    wblock = (2, G_KV, HD, w_past)
    full = lambda a: pl.BlockSpec(a.shape, lambda b, pt: (0,) * a.ndim)
    row = lambda n: pl.BlockSpec((t_new, n), lambda b, pt: (b, 0))
    grid_spec = pltpu.PrefetchScalarGridSpec(
        num_scalar_prefetch=1, grid=(n_b,),
        in_specs=_paged_specs(layer, n_pages, PAGE_BLOCK) + [
            pl.BlockSpec((G_KV, HPG, t_new, LANES), lambda b, pt: (0, 0, b, 0)),
            row(256), row(256), pl.BlockSpec((None, None) + wblock, lambda b, pt: (layer, b, 0, 0, 0, 0)),
            full(dsl), full(dwn), row(W_NSA), row(LANES), row(W_NSA), full(bg), full(expand)]
        + ([] if win_all is None else [pl.BlockSpec(memory_space=pl.ANY)]),
        out_specs=[row(W_NSA), pl.BlockSpec((None, None) + wblock, lambda b, pt: (layer, b, 0, 0, 0, 0))],
        scratch_shapes=[pltpu.VMEM((G_KV, LANES, n_keys), BF16), pltpu.VMEM((LANES, n_keys), BF16),
                        pltpu.VMEM((G_KV, LANES, n_wkeys), BF16), pltpu.VMEM((LANES, n_wkeys), BF16)])
    args = [page_table, *([cache] * n_pages), qa, slc_new, win_new, win_state, dsl, dwn, ocmp, ng, ngate, bg, expand]
    aliases = {}
    if win_all is not None:
        aliases = {len(args): 1}
        args.append(win_all)
    return pl.pallas_call(
        functools.partial(_dec_attn_body, n_pages=n_pages, t_new=t_new, n_keys=n_keys, n_wkeys=n_wkeys,
                          chained=win_all is not None),
        grid_spec=grid_spec,
        out_shape=[jax.ShapeDtypeStruct((m, W_NSA), F32), jax.ShapeDtypeStruct(win_state.shape[:2] + wblock, F32)],
        input_output_aliases=aliases,
        compiler_params=pltpu.CompilerParams(dimension_semantics=("arbitrary",), vmem_limit_bytes=VMEM_LIMIT),
        name="dec_attn",
    )(*args)


def _bucket_matrix(qpos, kpos, valid):
    dist = qpos[:, None] - kpos[None, :]
    return np.where(valid & (dist >= 0), _rel_bucket_np(dist), -1).astype(np.int32)


def _bias_body(tab_ref, bk_ref, o_ref, *, sub_far):
    bk = bk_ref[...]
    for h in range(H_NSA):
        far = tab_ref[N_BUCKETS - 1, h] if sub_far else 0.0
        acc = jnp.full(bk.shape, NEG, F32)
        for b in range(N_BUCKETS):
            acc = jnp.where(bk == b, tab_ref[b, h] - far, acc)
        o_ref[h] = acc


def _bias_table(table, bk, *, sub_far):
    r, c = bk.shape
    tr = 8 if r <= 8 else 64
    assert r % tr == 0
    return pl.pallas_call(
        functools.partial(_bias_body, sub_far=sub_far),
        grid=(r // tr,),
        in_specs=[pl.BlockSpec(memory_space=pltpu.SMEM), pl.BlockSpec((tr, c), lambda i: (i, 0))],
        out_specs=pl.BlockSpec((H_NSA, tr, c), lambda i: (0, i, 0)),
        out_shape=jax.ShapeDtypeStruct((H_NSA, r, c), F32),
        compiler_params=pltpu.CompilerParams(dimension_semantics=("arbitrary",)),
        name="bias_table",
    )(table, jnp.asarray(bk))


def _cmp_tables(qpos_blocks, n_blocks):
    slot = np.arange(LANES)
    blk = np.where(slot < ODD_SLOT0, 2 * slot, 2 * (slot - ODD_SLOT0) + 1)
    real = (blk < 2 * n_blocks) & ((slot % ODD_SLOT0) < n_blocks)
    blk_end = blk * CMP_BLOCK + CMP_BLOCK - 1
    buckets, sbases, useimps = [], [], []
    for qpos in qpos_blocks:
        buckets.append(_bucket_matrix(qpos, blk_end, real[None, :]))
        j = np.arange(LANES)[None, :]
        cur = (qpos // SEL_BLOCK)[:, None]
        forced = (j == 0) | (j == cur) | (j == cur - 1)
        valid = j * SEL_BLOCK <= qpos[:, None]
        inrange = j < n_blocks
        sbases.append(np.where(~inrange, -2.0, np.where(forced, FORCE, -1.0)).astype(np.float32))
        useimps.append((inrange & ~forced & valid).astype(np.float32))
    return np.concatenate(buckets), jnp.asarray(np.stack(sbases)), jnp.asarray(np.stack(useimps))


def _head_rows(b, t):
    k = b.shape[-1]
    n = b.shape[1] // t
    return b.reshape(G_KV, HPG, n, t, k).transpose(2, 0, 1, 3, 4).reshape(n, G_KV, HPG * t, k)


def _prep_weights(rel_bias, w_in, conv_w_pw, gla_w_gk, nsa_q_norm_g, nsa_k_norm_g, nsa_pe_cmp, nsa_w_cmp,
                  nsa_b_gate, w_out):
    depth = w_in.shape[0]
    cols = _in_proj_columns()
    pieces, start = [], 0
    for c in range(1, N_IN + 1):
        pad = cols[start] < 0
        if c == N_IN or (cols[c] >= 0 if pad else cols[c] != cols[c - 1] + 1):
            src = int(cols[start])
            pieces.append(jnp.zeros((depth, D_MODEL, c - start), w_in.dtype) if pad else w_in[:, :, src:src + c - start])
            start = c
    w_in_p = jnp.concatenate(pieces, axis=2).astype(BF16)
    perm = 512 + _nsa_perm()
    w_out_p = jnp.concatenate([w_out[:, 0:512]] + [w_out[:, int(perm[c]):int(perm[c]) + HD]
                                                   for c in range(0, W_NSA, HD)], axis=1).astype(BF16)
    qg = jnp.tile(nsa_q_norm_g, (1, H_NSA))[:, None, :] * SCALE
    kg = jnp.tile(nsa_k_norm_g, (1, 1, G_KV))
    wgk = jnp.zeros((depth, LANES, LANES), F32).at[:, :GATE_RANK, :].set(gla_w_gk)
    wgk_hi = wgk.astype(BF16)
    wgk2 = jnp.stack([wgk_hi, (wgk - wgk_hi.astype(F32)).astype(BF16)], axis=1)
    def blockdiag2(a, b):
        za = jnp.zeros(a.shape[:-1] + (b.shape[-1],), a.dtype)
        zb = jnp.zeros(b.shape[:-1] + (a.shape[-1],), b.dtype)
        return jnp.concatenate([jnp.concatenate([a, za], axis=-1), jnp.concatenate([zb, b], axis=-1)], axis=-2)

    wc = nsa_w_cmp.astype(BF16)
    wtap = blockdiag2(wc, wc)
    wbd = blockdiag2(wtap[:, 0], wtap[:, 1]).reshape(depth, KW, 256)
    pe = jnp.tile(nsa_pe_cmp.transpose(0, 2, 1, 3)[:, :, :, None, :], (1, 1, 1, G_KV, 1))
    petap = pe.transpose(0, 2, 1, 3, 4).reshape(depth, 2, CMP_BLOCK, LANES)
    pe = pe.reshape(depth, 1, KW)
    bg = jnp.zeros((depth, 1, LANES), F32).at[:, 0, :3 * H_NSA].set(nsa_b_gate)
    expand = np.zeros((LANES, 3 * HPG * LANES), np.float32)
    for br in range(3):
        for g in range(G_KV):
            for j in range(HPG):
                c = LANES * (HPG * br + j) + 64 * g
                expand[H_NSA * br + HPG * g + j, c:c + 64] = 1.0
    far = rel_bias[N_BUCKETS - 1]
    far_hi = far.astype(BF16).astype(F32)
    cvec = jnp.zeros((H_NSA, 1, LANES), F32).at[:, 0, CONST_LANE].set(far_hi).at[:, 0, CONST_LANE + 1].set(far - far_hi)
    return dict(w_in=w_in_p, w_out=w_out_p, qg=qg, kg=kg, wgk=wgk2, wbd=wbd, pe=pe, bg=bg,
                wtap=wtap, petap=petap,
                expand=jnp.asarray(expand, BF16), cvec=cvec, wpw=conv_w_pw.astype(BF16),
                indq=jnp.asarray(_blockdiag_ones(W_NSA, HD), BF16), indk=jnp.asarray(_blockdiag_ones(LANES, HD), BF16))


def kernel(x_prompt, x_sample, cache_cmp_kv, cache_slc_kv, page_table, state_win_kv, state_gla, state_conv, rel_bias, norm_g, w_in, conv_w_dw, conv_b_dw, conv_ln_g, conv_ln_b, conv_w_pw, conv_b_pw, gla_w_gk, gla_b_gk, gla_norm_g, nsa_q_norm_g, nsa_k_norm_g, nsa_pe_cmp, nsa_w_cmp, nsa_b_gate, w_out, norm_f):
    depth = w_in.shape[0]
    n_bp, t_p, _ = x_prompt.shape
    n_bs, t_s, _ = x_sample.shape
    n_pages = page_table.shape[1]
    past = n_pages * PAGE_SIZE
    w_past = state_win_kv.shape[2]
    assert t_p % TQ == 0 and WINDOW == 2 * TQ and t_s == 8 and w_past == WINDOW and past % SEL_BLOCK == 0
    wp = _prep_weights(rel_bias, w_in, conv_w_pw, gla_w_gk, nsa_q_norm_g, nsa_k_norm_g, nsa_pe_cmp, nsa_w_cmp,
                       nsa_b_gate, w_out)
    table = rel_bias
    nblk_p = t_p // SEL_BLOCK
    nblk_s = (past + SEL_BLOCK) // SEL_BLOCK
    pos_p = [np.arange(i * LANES, (i + 1) * LANES) for i in range(t_p // LANES)]
    bk_p, sbase_p, useimp_p = _cmp_tables(pos_p, nblk_p)
    bias_p = _head_rows(_bias_table(table, bk_p, sub_far=False), LANES)
    nb_s = LANES // t_s
    qpos_s = past + np.arange(t_s)
    bk_s, _, _ = _cmp_tables([qpos_s], nblk_s)
    bias_s = _head_rows(_bias_table(table, bk_s, sub_far=False), t_s)
    _, sbase_1, useimp_1 = _cmp_tables([np.tile(qpos_s, nb_s)], nblk_s)
    tq = np.arange(TQ)
    every = np.ones((1, 1), bool)
    bk_dt = np.concatenate([_bucket_matrix(tq, tq, every).T, _bucket_matrix(TQ + tq, tq, every).T])
    dt = _bias_table(table, bk_dt, sub_far=True)
    dt = dt.reshape(G_KV, HPG, 2, TQ, TQ).transpose(2, 0, 3, 1, 4).reshape(2, G_KV, TQ, HPG * TQ)
    edge = jnp.asarray(np.tile(np.where(tq[:, None] > tq[None, :], 0.0, NEG).astype(np.float32), (1, HPG)))
    n_keys = past + LANES
    kpos = np.arange(n_keys)
    dsl = _head_rows(_bias_table(table, _bucket_matrix(qpos_s, kpos, (kpos < past + t_s)[None, :]),
                                 sub_far=True), t_s)[0]
    n_wkeys = w_past + LANES
    wk = np.arange(n_wkeys)
    wpos = past - w_past + wk
    wvalid = (wk < w_past + t_s)[None, :] & ((qpos_s[:, None] - wpos[None, :]) < WINDOW) & (wpos >= 0)[None, :]
    dwn = _head_rows(_bias_table(table, _bucket_matrix(qpos_s, wpos, wvalid), sub_far=True), t_s)[0]

    cache_cmp = cache_cmp_kv.transpose(0, 1, 3, 4, 5, 2)
    cache_slc = cache_slc_kv.transpose(0, 1, 3, 4, 5, 2)
    win_state = state_win_kv.transpose(0, 1, 3, 4, 5, 2)
    s0 = state_gla.transpose(0, 1, 2, 4, 3).reshape(depth, n_bs, W_GLA, DK_GLA)
    i256 = np.arange(W_GLA)
    i128 = np.arange(LANES)
    bdmask = jnp.asarray((i256[:, None] // DV_GLA == i128[None, :] // DK_GLA).astype(np.float32))
    s0 = jnp.tile(s0, (1, 1, 1, H_GLA)) * bdmask
    zero_s0 = jnp.zeros((n_bp, W_GLA, LANES), F32)
    zero_hist = jnp.zeros((n_bp, HIST, W_CONV), F32)

    def unpack_state(st):
        b = st.shape[0]
        blocks = [st[:, DV_GLA * h:DV_GLA * (h + 1), DK_GLA * h:DK_GLA * (h + 1)] for h in range(H_GLA)]
        return jnp.stack(blocks, axis=1).transpose(0, 1, 3, 2)

    xp = x_prompt.reshape(n_bp * t_p, D_MODEL)
    xs = x_sample.reshape(n_bs * t_s, D_MODEL)
    outs_p, outs_s = [], []
    chain_p, win_all = (), None
    for l in range(depth):
        final = l == depth - 1
        row = lambda a: a[l][None, :]
        common = dict(wdw=conv_w_dw[l], bdw=row(conv_b_dw), lng=row(conv_ln_g), lnb=row(conv_ln_b), wpw=wp["wpw"][l],
                      bpw=row(conv_b_pw))
        bgk = jnp.zeros((1, LANES), F32).at[0, :].set(gla_b_gk[l])
        gng = jnp.tile(gla_norm_g[l], (H_GLA,))[None, :]
        kg12 = wp["kg"][l, 1:3]
        kg0 = wp["kg"][l, 0:1]
        zc, zg, qn, cmp_n, _, win_n, ngate, ng, ksa, vs, kwa, vw, cmp_all, slc_all = _in_proj(
            xp, row(norm_g), wp["w_in"][l], wp["indq"], wp["indk"], wp["qg"][l], kg12, seq_len=t_p, prompt=True,
            layer=l, depth=depth, chain=chain_p)
        chain_p = (cmp_all, slc_all)
        yc, conv_st = _conv(zc, zero_hist, nb=1, t_len=t_p, **common)
        yg, gla_st = _gla(zg, zero_s0, wp["wgk"][l], bgk, gng, nb=1, t_len=t_p, chunk=16)
        kvc = _compress_prompt(cmp_n, wp["wbd"][l], wp["pe"][l], n_b=n_bp, t_len=t_p)
        *q_ts, ocmp = _cmpsel(qn, kvc, bias_p, sbase_p, useimp_p, wp["indk"], kg0, wp["cvec"], nb=1, tb=LANES,
                              n_blocks=nblk_p, q_t=True)
        yn = _flash(q_ts, ksa, vs, kwa, vw, dt, edge, ocmp, ng, ngate, wp["bg"][l], wp["expand"], n_b=n_bp, t_len=t_p)
        xp = _out_proj(xp, yc, yg, yn, wp["w_out"][l], norm_f[None, :], final=final)
        w_keep = min(WINDOW, t_p)
        outs_p.append((conv_st, unpack_state(gla_st), win_n.reshape(n_bp, t_p, 256)[:, t_p - w_keep:]))
        zc, zg, qn, cmp_n, slc_n, win_n, ngate, ng = _in_proj(
            xs, row(norm_g), wp["w_in"][l], wp["indq"], wp["indk"], wp["qg"][l], kg12, seq_len=t_s, prompt=False)
        yc, conv_st = _conv(zc, state_conv[l], nb=nb_s, t_len=t_s, **common)
        yg, gla_st = _gla(zg, s0[l], wp["wgk"][l], bgk, gng, nb=nb_s, t_len=t_s, chunk=t_s)
        kvc = _compress_paged(page_table, cache_cmp, l, cmp_n, wp["wtap"][l], wp["petap"][l], t_new=t_s)
        qa, ocmp = _cmpsel(qn, kvc, bias_s, sbase_1, useimp_1, wp["indk"], kg0, wp["cvec"], nb=nb_s, tb=t_s,
                           n_blocks=nblk_s, q_t=False)
        yn, win_all = _dec_attn(page_table, cache_slc, l, qa, slc_n, win_n, win_state, dsl, dwn, ocmp, ng, ngate,
                                wp["bg"][l], wp["expand"], win_all, t_new=t_s)
        xs = _out_proj(xs, yc, yg, yn, wp["w_out"][l], norm_f[None, :], final=final)
        outs_s.append((conv_st, unpack_state(gla_st), None, cmp_n, slc_n))

    def stack(outs, k, shape):
        return jnp.stack([o[k] for o in outs]).reshape(shape)

    kv = (2, G_KV, HD)
    return (xp.reshape(n_bp, t_p, D_MODEL), xs.reshape(n_bs, t_s, D_MODEL),
            stack(outs_p, 0, (depth, n_bp, HIST, W_CONV)), stack(outs_s, 0, (depth, n_bs, HIST, W_CONV)),
            stack(outs_p, 1, (depth, n_bp, H_GLA, DK_GLA, DV_GLA)), stack(outs_s, 1, (depth, n_bs, H_GLA, DK_GLA, DV_GLA)),
            stack(outs_p, 2, (depth, n_bp, min(WINDOW, t_p)) + kv), win_all.transpose(0, 1, 5, 2, 3, 4),
            chain_p[0].transpose(0, 1, 5, 2, 3, 4), stack(outs_s, 3, (depth, n_bs, t_s) + kv),
            chain_p[1].transpose(0, 1, 5, 2, 3, 4), stack(outs_s, 4, (depth, n_bs, t_s) + kv))
```

```python
import functools
import math

import numpy as np
import jax
import jax.numpy as jnp
from jax import lax
from jax.experimental import pallas as pl
from jax.experimental.pallas import tpu as pltpu

F32 = jnp.float32
BF16 = jnp.bfloat16

D_MODEL = 1024
W_CONV = 256
CONV_WIDTH = 31
HIST = CONV_WIDTH - 1
H_GLA = 4
DK_GLA = 32
DV_GLA = 64
W_GLA = H_GLA * DV_GLA
GATE_RANK = 16
GATE_NORMALIZER = 16.0
H_NSA = 8
HD = 64
G_KV = 2
HPG = H_NSA // G_KV
W_NSA = H_NSA * HD
CMP_BLOCK = 32
SEL_BLOCK = 64
TOP_N = 8
WINDOW = 512
N_BUCKETS = 32
MAX_EXACT = N_BUCKETS // 2
MAX_DISTANCE = 128
PAGE_SIZE = 128
SCALE = HD ** -0.5
EPS = 1e-6
NEG = -1e30
FORCE = 1e4

LANES = 128
SEL_LANE0 = 64
SEL_SLOTS = 40
CONST_LANE = 112
ODD_SLOT0 = 64
TQ = 256
V_ROWS = 128
KEY_CHUNK = 256
COL_CHUNK = 128
VMEM_LIMIT = 56 * 1024 * 1024

C_CONV = 0
C_GLA = 3 * W_CONV
N_GLA = 2 * H_GLA * DK_GLA + 2 * W_GLA + LANES
C_NSA = C_GLA + N_GLA
N_NSA = W_NSA + 3 * 256 + W_NSA + LANES
N_IN = C_NSA + N_NSA


def _rel_bucket_np(dist):
    n = np.maximum(dist, 0)
    nf = np.maximum(n, 1).astype(np.float32)
    large = MAX_EXACT + (np.log(nf / np.float32(MAX_EXACT)) / np.float32(math.log(MAX_DISTANCE / MAX_EXACT))
                         * np.float32(N_BUCKETS - MAX_EXACT)).astype(np.int32)
    large = np.minimum(large, N_BUCKETS - 1)
    return np.where(n < MAX_EXACT, n, large).astype(np.int32)


def _in_proj_columns():
    o = {}
    off = 0
    for name, w in (("c_a", 256), ("c_b", 256), ("c_gate", 256), ("l_q", 128), ("l_k", 128), ("l_v", 256),
                    ("l_gk", 16), ("l_gate", 256), ("n_q", 512), ("n_cmp", 256), ("n_slc", 256),
                    ("n_win", 256), ("n_g", 24), ("n_gate", 512)):
        o[name] = off
        off += w
    cols = -np.ones((N_IN,), np.int64)

    def put(dst, name, width):
        cols[dst:dst + width] = o[name] + np.arange(width)

    put(0, "c_a", 256); put(256, "c_b", 256); put(512, "c_gate", 256)
    g = C_GLA
    put(g, "l_q", 128); put(g + 128, "l_k", 128); put(g + 256, "l_v", 256); put(g + 512, "l_gate", 256)
    put(g + 768, "l_gk", 16)
    n = C_NSA
    put(n, "n_q", 512); put(n + 512, "n_cmp", 256); put(n + 768, "n_slc", 256); put(n + 1024, "n_win", 256)
    cols[n + 1280:n + 1792] = o["n_gate"] + _nsa_perm()
    put(n + 1792, "n_g", 24)
    return cols


def _nsa_perm():
    p = np.zeros((W_NSA,), np.int64)
    for j in range(HPG):
        for g in range(G_KV):
            p[128 * j + 64 * g:128 * j + 64 * g + 64] = 64 * (HPG * g + j) + np.arange(64)
    return p


def _blockdiag_ones(n, blk):
    i = np.arange(n)
    return (i[:, None] // blk == i[None, :] // blk).astype(np.float32)


def _split2(x):
    hi = x.astype(BF16)
    lo = (x - hi.astype(F32)).astype(BF16)
    return hi, lo


def _dot(a, b):
    return jnp.dot(a, b, preferred_element_type=F32)


def _dot_nt(a, b):
    return lax.dot_general(a, b, (((1,), (1,)), ((), ())), preferred_element_type=F32)


def _dot_tn(a, b):
    return lax.dot_general(a, b, (((0,), (0,)), ((), ())), preferred_element_type=F32)


def _dot_split(x, m_bf16):
    hi, lo = _split2(x)
    return _dot(hi, m_bf16) + _dot(lo, m_bf16)


def _seg_rms(x, ind_ref, seg):
    return lax.rsqrt(_dot_split(x * x, ind_ref[...]) * (1.0 / seg) + EPS)


def _sigmoid(x):
    return 1.0 / (1.0 + jnp.exp(-x))


def _silu(x):
    return x * _sigmoid(x)


def _lane(shape):
    return lax.broadcasted_iota(jnp.int32, shape, len(shape) - 1)


def _half_pad(x, odd):
    if odd:
        x = pltpu.roll(x, 64, axis=1)
    return jnp.where(_lane(x.shape) < 64, x, 0.0)


def _inproj_body(x_ref, ng_ref, w_ref, indq_ref, indk_ref, qg_ref, kg_ref, *outs, tm, seq_len, prompt, n_chained):
    outs = outs[n_chained:]
    zc_ref, zg_ref, qn_ref, cmp_ref, slc_ref, win_ref, gate_ref, g_ref = outs[:8]
    x = x_ref[...]
    ms = jnp.mean(x * x, axis=-1, keepdims=True)
    h = (x * lax.rsqrt(ms + EPS) * ng_ref[...]).astype(BF16)

    def mm(lo, width):
        return _dot(h, w_ref[:, lo:lo + width])

    zc_ref[...] = mm(C_CONV, 3 * W_CONV)
    zg_ref[...] = mm(C_GLA, N_GLA)
    q = mm(C_NSA, W_NSA)
    qn_ref[...] = (q * _seg_rms(q, indq_ref, HD) * qg_ref[...]).astype(BF16)
    cmp = mm(C_NSA + 512, 256)
    cmp_ref[...] = cmp
    slc = mm(C_NSA + 768, 256)
    win = mm(C_NSA + 1024, 256)
    ks = slc[:, 0:128]
    ks = ks * _seg_rms(ks, indk_ref, HD) * kg_ref[0:1, :]
    kw = win[:, 0:128]
    kw = kw * _seg_rms(kw, indk_ref, HD) * kg_ref[1:2, :]
    slc_ref[:, 0:128] = ks
    slc_ref[:, 128:256] = slc[:, 128:256]
    win_ref[:, 0:128] = kw
    win_ref[:, 128:256] = win[:, 128:256]
    gate_ref[...] = mm(C_NSA + 1280, W_NSA)
    g_ref[...] = mm(C_NSA + 1792, LANES)
    if prompt:
        ksa_ref, vs_ref, kwa_ref, vw_ref, cmpt_ref, slct_ref = outs[8:]
        cmpt_ref[...] = cmp.T.reshape(2, G_KV, HD, tm)
        slct_ref[...] = jnp.concatenate([ks, slc[:, 128:256]], axis=1).T.reshape(2, G_KV, HD, tm)
        lane = _lane((tm, LANES))
        row = lax.broadcasted_iota(jnp.int32, (tm, LANES), 0)
        t = lax.rem(pl.program_id(0) * tm + row, seq_len)
        ones = ((lane == CONST_LANE) | (lane == CONST_LANE + 1)).astype(F32)
        onehot = ((lane - SEL_LANE0) == lax.shift_right_logical(t, 6)).astype(F32)
        for g in range(G_KV):
            ksa_ref[g] = (_half_pad(ks, g == 1) + onehot + ones).astype(BF16)
            kwa_ref[g] = (_half_pad(kw, g == 1) + ones).astype(BF16)
        feat = lax.broadcasted_iota(jnp.int32, (LANES, tm), 0)
        for v_ref, src in ((vs_ref, slc), (vw_ref, win)):
            vt = src[:, 128:256].T
            for g in range(G_KV):
                own = vt if g == 0 else pltpu.roll(vt, HD, axis=0)
                own = jnp.where(feat < HD, own, (feat == HD).astype(F32))[0:V_ROWS].astype(BF16)
                for kt in range(tm // TQ):
                    v_ref[kt, g] = own[:, kt * TQ:(kt + 1) * TQ]


def _in_proj(x2d, norm_g, w, indq, indk, qg, kg, *, seq_len, prompt, layer=0, depth=1, chain=()):
    m = x2d.shape[0]
    tm = min(512, m)
    assert m % tm == 0 and (not prompt or (seq_len % tm == 0 and tm % TQ == 0))
    row = lambda n: pl.BlockSpec((tm, n), lambda i: (i, 0))
    full = lambda a: pl.BlockSpec(a.shape, lambda i: (0,) * a.ndim)
    out_shape = [jax.ShapeDtypeStruct((m, 3 * W_CONV), F32), jax.ShapeDtypeStruct((m, N_GLA), F32),
                 jax.ShapeDtypeStruct((m, W_NSA), BF16), jax.ShapeDtypeStruct((m, 256), F32),
                 jax.ShapeDtypeStruct((m, 256), F32), jax.ShapeDtypeStruct((m, 256), F32),
                 jax.ShapeDtypeStruct((m, W_NSA), F32), jax.ShapeDtypeStruct((m, LANES), F32)]
    out_specs = [row(3 * W_CONV), row(N_GLA), row(W_NSA), row(256), row(256), row(256), row(W_NSA), row(LANES)]
    if prompt:
        aug = pl.BlockSpec((G_KV, tm, LANES), lambda i: (0, i, 0))
        vt = pl.BlockSpec((tm // TQ, G_KV, V_ROWS, TQ), lambda i: (i, 0, 0, 0))
        vt_shape = jax.ShapeDtypeStruct((m // TQ, G_KV, V_ROWS, TQ), BF16)
        per_seq = seq_len // tm
        leaf = pl.BlockSpec((None, None, 2, G_KV, HD, tm), lambda i: (layer, i // per_seq, 0, 0, 0, i % per_seq))
        leaf_shape = jax.ShapeDtypeStruct((depth, m // seq_len, 2, G_KV, HD, seq_len), F32)
        out_shape += [jax.ShapeDtypeStruct((G_KV, m, LANES), BF16), vt_shape,
                      jax.ShapeDtypeStruct((G_KV, m, LANES), BF16), vt_shape, leaf_shape, leaf_shape]
        out_specs += [aug, vt, aug, vt, leaf, leaf]
    args = [x2d, norm_g, w, indq, indk, qg, kg]
    return pl.pallas_call(
        functools.partial(_inproj_body, tm=tm, seq_len=seq_len, prompt=prompt, n_chained=len(chain)),
        grid=(m // tm,),
        in_specs=[row(D_MODEL), full(norm_g), full(w), full(indq), full(indk), full(qg), full(kg)]
        + [pl.BlockSpec(memory_space=pl.ANY)] * len(chain),
        out_specs=out_specs, out_shape=out_shape,
        input_output_aliases={len(args) + k: 12 + k for k in range(len(chain))},
        compiler_params=pltpu.CompilerParams(dimension_semantics=("arbitrary",), vmem_limit_bytes=VMEM_LIMIT),
        name="in_proj",
    )(*args, *chain)


def _outproj_body(x_ref, yc_ref, yg_ref, yn_ref, w_ref, nf_ref, o_ref, *, final):
    x = x_ref[...]
    x = (x + _dot(yc_ref[...], w_ref[0:256, :]) + _dot(yg_ref[...], w_ref[256:512, :])
         + _dot(yn_ref[...].astype(BF16), w_ref[512:1024, :]))
    if final:
        ms = jnp.mean(x * x, axis=-1, keepdims=True)
        x = x * lax.rsqrt(ms + EPS) * nf_ref[...]
    o_ref[...] = x


def _out_proj(x2d, yc, yg, yn, w, norm_f, *, final):
    m = x2d.shape[0]
    tm = min(512, m)
    assert m % tm == 0
    row = lambda n: pl.BlockSpec((tm, n), lambda i: (i, 0))
    full = lambda a: pl.BlockSpec(a.shape, lambda i: (0,) * a.ndim)
    return pl.pallas_call(
        functools.partial(_outproj_body, final=final),
        grid=(m // tm,),
        in_specs=[row(D_MODEL), row(256), row(256), row(512), full(w), full(norm_f)],
        out_specs=row(D_MODEL), out_shape=jax.ShapeDtypeStruct((m, D_MODEL), F32),
        compiler_params=pltpu.CompilerParams(dimension_semantics=("arbitrary",), vmem_limit_bytes=VMEM_LIMIT),
        name="out_proj",
    )(x2d, yc, yg, yn, w, norm_f)


def _conv_body(zc_ref, hist_ref, wdw_ref, bdw_ref, lng_ref, lnb_ref, wpw_ref, bpw_ref, y_ref, st_ref,
               ext_scr, act_scr, *, nb, t_len, tc, tmm):
    n_chunks = t_len // tc

    def per_batch(bi, _):
        base = bi * t_len
        ext_scr[0:32, :] = jnp.zeros((32, W_CONV), F32)
        ext_scr[2:32, :] = hist_ref[bi]

        def chunk(c, _):
            r0 = pl.multiple_of(c * tc, tc)
            g0 = pl.multiple_of(base + r0, tc)
            a = zc_ref[pl.ds(g0, tc), 0:256]
            b = zc_ref[pl.ds(g0, tc), 256:512]
            ext_scr[pl.ds(32 + r0, tc), :] = a * _sigmoid(b)
            win = ext_scr[pl.ds(r0, tc + 32), :]
            acc = jnp.zeros((tc, W_CONV), F32) + bdw_ref[...]
            for r in range(8):
                wr = pltpu.roll(win, tc + 32 - (2 + r), axis=0)
                for mi, j in enumerate(range(r, CONV_WIDTH, 8)):
                    acc = acc + wr[8 * mi:8 * mi + tc] * wdw_ref[j:j + 1, :]
            mu = jnp.mean(acc, axis=-1, keepdims=True)
            d = acc - mu
            var = jnp.mean(d * d, axis=-1, keepdims=True)
            yn = d * lax.rsqrt(var + EPS) * lng_ref[...] + lnb_ref[...]
            act_scr[pl.ds(g0, tc), :] = _silu(yn)
            return 0

        lax.fori_loop(0, n_chunks, chunk, 0)
        st_ref[bi] = ext_scr[2 + t_len:32 + t_len, :]
        return 0

    lax.fori_loop(0, nb, per_batch, 0)

    def mm(c, _):
        r0 = pl.multiple_of(c * tmm, tmm)
        y = _dot(act_scr[pl.ds(r0, tmm), :].astype(BF16), wpw_ref[...]) + bpw_ref[...]
        y_ref[pl.ds(r0, tmm), :] = (y * _silu(zc_ref[pl.ds(r0, tmm), 512:768])).astype(BF16)
        return 0

    lax.fori_loop(0, nb * t_len // tmm, mm, 0)


def _conv(zc, hist, wdw, bdw, lng, lnb, wpw, bpw, *, nb, t_len):
    m = zc.shape[0]
    n_b = m // t_len
    tc = min(64, t_len)
    tmm = min(256, nb * t_len)
    full = lambda a: pl.BlockSpec(a.shape, lambda i: (0,) * a.ndim)
    return pl.pallas_call(
        functools.partial(_conv_body, nb=nb, t_len=t_len, tc=tc, tmm=tmm),
        grid=(n_b // nb,),
        in_specs=[pl.BlockSpec((nb * t_len, 3 * W_CONV), lambda i: (i, 0)),
                  pl.BlockSpec((nb, HIST, W_CONV), lambda i: (i, 0, 0)),
                  full(wdw), full(bdw), full(lng), full(lnb), full(wpw), full(bpw)],
        out_specs=[pl.BlockSpec((nb * t_len, W_CONV), lambda i: (i, 0)),
                   pl.BlockSpec((nb, HIST, W_CONV), lambda i: (i, 0, 0))],
        out_shape=[jax.ShapeDtypeStruct((m, W_CONV), BF16), jax.ShapeDtypeStruct((n_b, HIST, W_CONV), F32)],
        scratch_shapes=[pltpu.VMEM((32 + t_len, W_CONV), F32), pltpu.VMEM((nb * t_len, W_CONV), F32)],
        compiler_params=pltpu.CompilerParams(dimension_semantics=("arbitrary",), vmem_limit_bytes=VMEM_LIMIT),
        name="conv",
    )(zc, hist, wdw, bdw, lng, lnb, wpw, bpw)


def _gla_body(zg_ref, s0_ref, wgk_ref, bgk_ref, ltri_ref, lsum_ref, ind_ref, indv_ref, ng_ref, bd_ref,
              y_ref, st_ref, bc_scr, qe_scr, ke_scr, dec_scr, o_scr, *, nb, t_len, chunk, rb):
    rows = nb * t_len
    n_rb = rows // rb
    cpb = t_len // chunk

    def phase1(i, _):
        r0 = pl.multiple_of(i * rb, rb)
        gk = zg_ref[pl.ds(r0, rb), 768:896]
        pre = _dot_split(gk, wgk_ref[0]) + _dot(gk.astype(BF16), wgk_ref[1]) + bgk_ref[...]
        la = (jnp.minimum(pre, 0.0) - jnp.log(1.0 + jnp.exp(-jnp.abs(pre)))) * (1.0 / GATE_NORMALIZER)
        h1, l1 = _split2(la)
        l2 = (la - h1.astype(F32) - l1.astype(F32)).astype(BF16)
        bc = _dot(ltri_ref[...], h1) + _dot(ltri_ref[...], l1) + _dot(ltri_ref[...], l2)
        bt = _dot(lsum_ref[...], h1) + _dot(lsum_ref[...], l1) + _dot(lsum_ref[...], l2)
        bc_scr[pl.ds(r0, rb), :] = bc
        qe_scr[pl.ds(r0, rb), :] = zg_ref[pl.ds(r0, rb), 0:128] * (DK_GLA ** -0.5) * jnp.exp(bc)
        ke_scr[pl.ds(r0, rb), :] = zg_ref[pl.ds(r0, rb), 128:256] * jnp.exp(bt - bc)
        dec_scr[pl.ds(r0, rb), :] = jnp.exp(bt)
        return 0

    lax.fori_loop(0, n_rb, phase1, 0)

    ti = lax.broadcasted_iota(jnp.int32, (chunk, chunk, LANES), 0)
    si = lax.broadcasted_iota(jnp.int32, (chunk, chunk, LANES), 1)
    causal = si <= ti

    u2 = 2 if (rows // chunk) % 2 == 0 else 1
    u3 = 4 if cpb % 4 == 0 else 1

    def phase2(cg, _):
        for u in range(u2):
            r0 = pl.multiple_of((cg * u2 + u) * chunk, chunk)
            bc = bc_scr[pl.ds(r0, chunk), :]
            q = zg_ref[pl.ds(r0, chunk), 0:128] * (DK_GLA ** -0.5)
            k = zg_ref[pl.ds(r0, chunk), 128:256]
            v = zg_ref[pl.ds(r0, chunk), 256:512]
            e = jnp.exp(jnp.where(causal, bc[:, None, :] - bc[None, :, :], NEG))
            p = (q[:, None, :] * k[None, :, :] * e).reshape(chunk * chunk, LANES)
            att = _dot(p.astype(BF16), ind_ref[...]).reshape(chunk, chunk, W_GLA)
            o_scr[pl.ds(r0, chunk), :] = (att * v[None, :, :]).sum(axis=1)
        return 0

    lax.fori_loop(0, rows // chunk // u2, phase2, 0)

    def per_batch(bi, _):
        def group(cg, s):
            for u in range(u3):
                r0 = pl.multiple_of(bi * t_len + (cg * u3 + u) * chunk, chunk)
                o_scr[pl.ds(r0, chunk), :] += _dot_nt(qe_scr[pl.ds(r0, chunk), :].astype(BF16), s.astype(BF16))
                upd = _dot_tn(zg_ref[pl.ds(r0, chunk), 256:512].astype(BF16),
                              ke_scr[pl.ds(r0, chunk), :].astype(BF16))
                s = s * dec_scr[pl.ds(r0, 1), :] + upd * bd_ref[...]
            return s

        st_ref[bi] = lax.fori_loop(0, cpb // u3, group, s0_ref[bi])
        return 0

    lax.fori_loop(0, nb, per_batch, 0)

    def phase4(i, _):
        r0 = pl.multiple_of(i * rb, rb)
        o = o_scr[pl.ds(r0, rb), :]
        o = o * _seg_rms(o, indv_ref, DV_GLA) * ng_ref[...]
        y_ref[pl.ds(r0, rb), :] = (o * _silu(zg_ref[pl.ds(r0, rb), 512:768])).astype(BF16)
        return 0

    lax.fori_loop(0, n_rb, phase4, 0)


def _gla(zg, s0t, wgk, bgk, ng, *, nb, t_len, chunk):
    m = zg.shape[0]
    n_b = m // t_len
    rows = nb * t_len
    rb = min(128, rows)
    assert rows % rb == 0 and rb % chunk == 0 and t_len % chunk == 0
    ltri = jnp.asarray(_blockdiag_ones(rb, chunk) * np.tril(np.ones((rb, rb), np.float32)), BF16)
    lsum = jnp.asarray(_blockdiag_ones(rb, chunk), BF16)
    i128 = np.arange(LANES)
    i256 = np.arange(W_GLA)
    ind = jnp.asarray((i128[:, None] // DK_GLA == i256[None, :] // DV_GLA).astype(np.float32), BF16)
    indv = jnp.asarray(_blockdiag_ones(W_GLA, DV_GLA), BF16)
    bd = jnp.asarray((i256[:, None] // DV_GLA == i128[None, :] // DK_GLA).astype(np.float32))
    full = lambda a: pl.BlockSpec(a.shape, lambda i: (0,) * a.ndim)
    return pl.pallas_call(
        functools.partial(_gla_body, nb=nb, t_len=t_len, chunk=chunk, rb=rb),
        grid=(n_b // nb,),
        in_specs=[pl.BlockSpec((rows, N_GLA), lambda i: (i, 0)),
                  pl.BlockSpec((nb, W_GLA, LANES), lambda i: (i, 0, 0)),
                  full(wgk), full(bgk), full(ltri), full(lsum), full(ind), full(indv), full(ng), full(bd)],
        out_specs=[pl.BlockSpec((rows, W_GLA), lambda i: (i, 0)),
                   pl.BlockSpec((nb, W_GLA, LANES), lambda i: (i, 0, 0))],
        out_shape=[jax.ShapeDtypeStruct((m, W_GLA), BF16), jax.ShapeDtypeStruct((n_b, W_GLA, LANES), F32)],
        scratch_shapes=[pltpu.VMEM((rows, LANES), F32), pltpu.VMEM((rows, LANES), F32),
                        pltpu.VMEM((rows, LANES), F32), pltpu.VMEM((rows, LANES), F32),
                        pltpu.VMEM((rows, W_GLA), F32)],
        compiler_params=pltpu.CompilerParams(dimension_semantics=("arbitrary",), vmem_limit_bytes=VMEM_LIMIT),
        name="gla",
    )(zg, s0t, wgk, bgk, ltri, lsum, ind, indv, ng, bd)


KW = CMP_BLOCK * 256


def _compress_rows(x_ref, w_ref, pe_ref, kvc_ref, n_rows):
    kvc_ref[...] = jnp.zeros(kvc_ref.shape, F32)
    for par in range(2):
        x = (x_ref[:, par * KW:(par + 1) * KW] + pe_ref[...]).astype(BF16)
        kvc_ref[ODD_SLOT0 * par:ODD_SLOT0 * par + n_rows, :] = _dot(x, w_ref[...])


def _compress_prompt_body(x_ref, w_ref, pe_ref, kvc_ref, *, n_rows):
    _compress_rows(x_ref, w_ref, pe_ref, kvc_ref, n_rows)


def _compress_prompt(cmp_rows, w, pe, *, n_b, t_len):
    n_rows = t_len // SEL_BLOCK
    full = lambda a: pl.BlockSpec(a.shape, lambda i: (0,) * a.ndim)
    return pl.pallas_call(
        functools.partial(_compress_prompt_body, n_rows=n_rows),
        grid=(n_b,),
        in_specs=[pl.BlockSpec((n_rows, 2 * KW), lambda i: (i, 0)), full(w), full(pe)],
        out_specs=pl.BlockSpec((None, LANES, 256), lambda i: (i, 0, 0)),
        out_shape=jax.ShapeDtypeStruct((n_b, LANES, 256), F32),
        compiler_params=pltpu.CompilerParams(dimension_semantics=("arbitrary",), vmem_limit_bytes=VMEM_LIMIT),
        name="compress_prompt",
    )(cmp_rows.reshape(n_b * n_rows, 2 * KW), w, pe)


def _compress_paged_body(pt_ref, *refs, n_pages, t_new, n_rows):
    pages = refs[:DEC_NB * n_pages]
    new_ref, w_ref, pe_ref, kvc_ref, x_scr, acc_scr = refs[DEC_NB * n_pages:]
    per_page = PAGE_SIZE // CMP_BLOCK
    n_past = n_pages * per_page
    n_cmp = 2 * n_rows
    for bi in range(DEC_NB):
        new = new_ref[bi * t_new:(bi + 1) * t_new, :]
        for s in range(2):
            for p in range(n_pages):
                rows = pages[bi * n_pages + p][s].reshape(LANES, PAGE_SIZE).T
                for c in range(per_page):
                    r0 = X_PITCH * (per_page * p + c)
                    x_scr[bi, s, r0:r0 + CMP_BLOCK, :] = rows[CMP_BLOCK * c:CMP_BLOCK * (c + 1)]
            x_scr[bi, s, X_PITCH * n_past:, :] = jnp.zeros((X_PITCH * (n_cmp - n_past), LANES), F32)
            x_scr[bi, s, X_PITCH * n_past:X_PITCH * n_past + t_new, :] = new[:, LANES * s:LANES * (s + 1)]
    kvc_ref[...] = jnp.zeros(kvc_ref.shape, F32)
    for s in range(2):
        acc = jnp.zeros((DEC_NB * n_cmp, LANES), F32)
        for j in range(CMP_BLOCK):
            x = jnp.concatenate([x_scr[bi, s, pl.ds(j, n_cmp, stride=X_PITCH), :] for bi in range(DEC_NB)], axis=0)
            acc = acc + _dot((x + pe_ref[s, j:j + 1, :]).astype(BF16), w_ref[s, j])
        acc_scr[...] = acc
        for bi in range(DEC_NB):
            for par in range(2):
                kvc_ref[bi, ODD_SLOT0 * par:ODD_SLOT0 * par + n_rows, LANES * s:LANES * (s + 1)] = (
                    acc_scr[pl.ds(bi * n_cmp + par, n_rows, stride=2), :])


def _paged_specs(layer, n_pages, block):
    return [pl.BlockSpec((None, None) + block, functools.partial(
        lambda b, pt, bi, p: (layer, pt[DEC_NB * b + bi, p]) + (0,) * len(block), bi=bi, p=p))
        for bi in range(DEC_NB) for p in range(n_pages)]


PAGE_BLOCK = (2, G_KV, HD, PAGE_SIZE)
X_PITCH = CMP_BLOCK + 4
DEC_NB = 2


def _compress_paged(page_table, cache_t, layer, cmp_new, w, pe, *, t_new):
    n_b, n_pages = page_table.shape
    assert n_b % DEC_NB == 0
    n_rows = -(-((n_pages * PAGE_SIZE + SEL_BLOCK) // SEL_BLOCK) // 8) * 8
    full = lambda a: pl.BlockSpec(a.shape, lambda b, pt: (0,) * a.ndim)
    grid_spec = pltpu.PrefetchScalarGridSpec(
        num_scalar_prefetch=1, grid=(n_b // DEC_NB,),
        in_specs=_paged_specs(layer, n_pages, PAGE_BLOCK) + [
            pl.BlockSpec((DEC_NB * t_new, 256), lambda b, pt: (b, 0)), full(w), full(pe)],
        out_specs=pl.BlockSpec((DEC_NB, LANES, 256), lambda b, pt: (b, 0, 0)),
        scratch_shapes=[pltpu.VMEM((DEC_NB, 2, 2 * n_rows * X_PITCH, LANES), F32),
                        pltpu.VMEM((DEC_NB * 2 * n_rows, LANES), F32)])
    return pl.pallas_call(
        functools.partial(_compress_paged_body, n_pages=n_pages, t_new=t_new, n_rows=n_rows),
        grid_spec=grid_spec, out_shape=jax.ShapeDtypeStruct((n_b, LANES, 256), F32),
        compiler_params=pltpu.CompilerParams(dimension_semantics=("arbitrary",), vmem_limit_bytes=VMEM_LIMIT),
        name="compress_paged",
    )(page_table, *([cache_t] * (DEC_NB * n_pages)), cmp_new, w, pe)


def _cmpsel_body(qn_ref, kvc_ref, bias_ref, sbase_ref, useimp_ref, indk_ref, kg_ref, cvec_ref,
                 *outs, nb, tb, n_blocks, q_t):
    q_refs, ocmp_ref = outs[:-1], outs[-1]
    rows = nb * tb
    lane = _lane((rows, LANES))
    qn = qn_ref[...].astype(F32)
    qpad = [_half_pad(qn[:, 128 * (h // 2):128 * (h // 2) + 128], h % 2 == 1) for h in range(H_NSA)]
    imp_parts = []
    o_parts = [[None] * nb for _ in range(H_NSA)]
    kc_all = kvc_ref[:, :, 0:128].reshape(nb * LANES, LANES)
    kc_all = kc_all * _seg_rms(kc_all, indk_ref, HD) * kg_ref[...]
    for bi in range(nb):
        kc = kc_all[bi * LANES:(bi + 1) * LANES]
        vboth = kvc_ref[bi, :, 128:256].astype(BF16)
        imp_b = []
        for g in range(G_KV):
            kpad = _half_pad(kc, g == 1).astype(BF16)
            qs = jnp.concatenate([qpad[HPG * g + j][bi * tb:(bi + 1) * tb] for j in range(HPG)], axis=0).astype(BF16)
            bias = bias_ref[g]
            l = _dot_nt(qs, kpad) + bias
            vis = bias > 0.5 * NEG
            mx = jnp.max(l, axis=-1, keepdims=True)
            p = jnp.where(vis, jnp.exp(l - mx), 0.0)
            p = p / jnp.maximum(jnp.sum(p, axis=-1, keepdims=True), 1e-30)
            o = _dot(p.astype(BF16), vboth)
            ig = p[0:tb]
            for j in range(1, HPG):
                ig = ig + p[j * tb:(j + 1) * tb]
            imp_b.append(ig)
            for j in range(HPG):
                o_parts[HPG * g + j][bi] = o[j * tb:(j + 1) * tb]
        imp_parts.append(imp_b)
    for j in range(HPG):
        o0 = jnp.concatenate(o_parts[j], axis=0) if nb > 1 else o_parts[j][0]
        o1 = jnp.concatenate(o_parts[HPG + j], axis=0) if nb > 1 else o_parts[HPG + j][0]
        ocmp_ref[:, 128 * j:128 * j + 128] = jnp.where(lane < 64, o0, o1)
    for g in range(G_KV):
        ig = jnp.concatenate([imp_parts[bi][g] for bi in range(nb)], axis=0) if nb > 1 else imp_parts[0][g]
        imp = ig + pltpu.roll(ig, LANES - ODD_SLOT0, axis=1)
        score = jnp.where(useimp_ref[...] > 0.5, imp, sbase_ref[...])
        st = score.T[0:SEL_SLOTS]
        jj = lax.broadcasted_iota(jnp.int32, (SEL_SLOTS, rows), 0)
        rank = jnp.zeros((SEL_SLOTS, rows), F32)
        for jp in range(n_blocks):
            r = st[jp:jp + 1, :]
            before = (r > st) | ((r == st) & (jp < jj))
            rank = rank + before.astype(F32)
        selb = jnp.where(rank < TOP_N - 0.5, 0.0, NEG)
        full_t = jnp.concatenate([jnp.zeros((SEL_LANE0, rows), F32), selb,
                                  jnp.zeros((LANES - SEL_LANE0 - SEL_SLOTS, rows), F32)], axis=0)
        if q_t:
            for j in range(HPG):
                h = HPG * g + j
                q_refs[j][g] = ((qpad[h] + cvec_ref[h]).T + full_t).astype(BF16)
        else:
            extra = full_t.T
            for j in range(HPG):
                h = HPG * g + j
                q_refs[0][g, j] = (qpad[h] + extra + cvec_ref[h]).astype(q_refs[0].dtype)


def _cmpsel(qn, kvc, bias, sbase, useimp, indk, kg0, cvec, *, nb, tb, n_blocks, q_t):
    m = qn.shape[0]
    rows = nb * tb
    assert rows == LANES and m % rows == 0
    n_pos = bias.shape[0]
    full = lambda a: pl.BlockSpec(a.shape, lambda i: (0,) * a.ndim)
    if q_t:
        q_specs = [pl.BlockSpec((G_KV, LANES, rows), lambda i: (0, 0, i))] * HPG
        q_shapes = [jax.ShapeDtypeStruct((G_KV, LANES, m), BF16)] * HPG
    else:
        q_specs = [pl.BlockSpec((G_KV, HPG, rows, LANES), lambda i: (0, 0, i, 0))]
        q_shapes = [jax.ShapeDtypeStruct((G_KV, HPG, m, LANES), F32)]
    return pl.pallas_call(
        functools.partial(_cmpsel_body, nb=nb, tb=tb, n_blocks=n_blocks, q_t=q_t),
        grid=(m // rows,),
        in_specs=[pl.BlockSpec((rows, W_NSA), lambda i: (i, 0)),
                  pl.BlockSpec((nb, LANES, 256), (lambda i: (i, 0, 0)) if nb > 1 else (lambda i: (i // n_pos, 0, 0))),
                  pl.BlockSpec((None, G_KV, HPG * tb, LANES), lambda i: (i % n_pos, 0, 0, 0)),
                  pl.BlockSpec((None, rows, LANES), lambda i: (i % n_pos, 0, 0)),
                  pl.BlockSpec((None, rows, LANES), lambda i: (i % n_pos, 0, 0)),
                  full(indk), full(kg0), full(cvec)],
        out_specs=q_specs + [pl.BlockSpec((rows, W_NSA), lambda i: (i, 0))],
        out_shape=q_shapes + [jax.ShapeDtypeStruct((m, W_NSA), F32)],
        compiler_params=pltpu.CompilerParams(dimension_semantics=("arbitrary",), vmem_limit_bytes=VMEM_LIMIT),
        name="cmpsel",
    )(qn, kvc, bias, sbase, useimp, indk, kg0, cvec)


def _mix_gates(g_ref, bg_ref, expand_ref, gate_ref, ocmp_ref, o_slc, o_win, y_ref):
    gl = _sigmoid(g_ref[...] + bg_ref[...])
    gx = _dot_split(gl, expand_ref[...])
    for j in range(HPG):
        sl = slice(128 * j, 128 * j + 128)
        o = (gx[:, 128 * j:128 * j + 128] * ocmp_ref[:, sl]
             + gx[:, 128 * (HPG + j):128 * (HPG + j) + 128] * o_slc[j]
             + gx[:, 128 * (2 * HPG + j):128 * (2 * HPG + j) + 128] * o_win[j])
        y_ref[:, sl] = (o * _silu(gate_ref[:, sl])).astype(y_ref.dtype)


def _softmax_step(q_t, k, v_t, bias_t, m_scr, acc_scr):
    n_key, n_col = k.shape[0], q_t.shape[1]
    for k0 in range(0, n_key, KEY_CHUNK):
        ks = slice(k0, k0 + KEY_CHUNK)
        for c0 in range(0, n_col, COL_CHUNK):
            cs = slice(c0, c0 + COL_CHUNK)
            s = _dot(k[ks], q_t[:, cs])
            if bias_t is not None:
                s = s + bias_t(ks, cs)
            m_prev = m_scr[:, cs]
            m_new = jnp.maximum(m_prev, jnp.max(s, axis=0, keepdims=True))
            p = jnp.exp(s - m_new)
            acc_scr[:, cs] = jnp.exp(m_prev - m_new) * acc_scr[:, cs] + _dot(v_t[:, ks], p.astype(BF16))
            m_scr[:, cs] = m_new


def _flash_body(q0_ref, q1_ref, q2_ref, q3_ref, ksa_ref, vs_ref, kwa_ref, vw_ref, dt_ref, edge_ref, ocmp_ref,
                g_ref, gate_ref, bg_ref, expand_ref, y_ref, m_scr, acc_scr):
    i = pl.program_id(1)
    cols = HPG * TQ
    m_scr[...] = jnp.full(m_scr.shape, NEG, F32)
    acc_scr[...] = jnp.zeros(acc_scr.shape, F32)

    def run(n_near):
        for g in range(G_KV):
            q_t = jnp.concatenate([r[g] for r in (q0_ref, q1_ref, q2_ref, q3_ref)], axis=1)
            for br, (ka_ref, v_ref) in enumerate(((ksa_ref, vs_ref), (kwa_ref, vw_ref))):
                def step(kt, bias_t):
                    k0 = pl.multiple_of(kt * TQ, TQ)
                    _softmax_step(q_t, ka_ref[g, pl.ds(k0, TQ), :], v_ref[kt, g], bias_t, m_scr.at[br, g],
                                  acc_scr.at[br, g])

                if n_near == 2:
                    if br == 0:
                        def far_pair(kp, _):
                            step(2 * kp, None)
                            step(2 * kp + 1, None)
                            return 0

                        lax.fori_loop(0, lax.shift_right_logical(i - 1, 1), far_pair, 0)

                        @pl.when(lax.rem(i - 1, 2) == 1)
                        def _():
                            step(i - 2, None)
                    else:
                        step(i - 2, lambda ks, cs: edge_ref[ks, cs])
                if n_near >= 1:
                    step(i - 1, lambda ks, cs: dt_ref[1, g, ks, cs])
                step(i, lambda ks, cs: dt_ref[0, g, ks, cs])

    for n_near, cond in ((0, i == 0), (1, i == 1), (2, i >= 2)):
        pl.when(cond)(functools.partial(run, n_near))

    outs = []
    for br in range(2):
        o_t = jnp.concatenate([acc_scr[br, g, 0:HD, :] / acc_scr[br, g, HD:HD + 1, :] for g in range(G_KV)], axis=0)
        outs.append([o_t[:, j * TQ:(j + 1) * TQ].T for j in range(HPG)])
    _mix_gates(g_ref, bg_ref, expand_ref, gate_ref, ocmp_ref, outs[0], outs[1], y_ref)


def _flash(q_ts, ksa, vs, kwa, vw, dt, edge, ocmp, ng, ngate, bg, expand, *, n_b, t_len):
    m = n_b * t_len
    nq = t_len // TQ
    cols = HPG * TQ
    full = lambda a: pl.BlockSpec(a.shape, lambda b, i: (0,) * a.ndim)
    qspec = pl.BlockSpec((G_KV, LANES, TQ), lambda b, i: (0, 0, b * nq + i))
    kspec = pl.BlockSpec((G_KV, None, t_len, LANES), lambda b, i: (0, b, 0, 0))
    vspec = pl.BlockSpec((nq, G_KV, V_ROWS, TQ), lambda b, i: (b, 0, 0, 0))
    row = lambda n: pl.BlockSpec((TQ, n), lambda b, i: (b * nq + i, 0))
    return pl.pallas_call(
        _flash_body,
        grid=(n_b, nq),
        in_specs=[qspec] * HPG + [kspec, vspec, kspec, vspec, full(dt), full(edge), row(W_NSA), row(LANES),
                                  row(W_NSA), full(bg), full(expand)],
        out_specs=row(W_NSA), out_shape=jax.ShapeDtypeStruct((m, W_NSA), BF16),
        scratch_shapes=[pltpu.VMEM((2, G_KV, 1, cols), F32), pltpu.VMEM((2, G_KV, V_ROWS, cols), F32)],
        compiler_params=pltpu.CompilerParams(dimension_semantics=("arbitrary", "arbitrary"),
                                             vmem_limit_bytes=VMEM_LIMIT),
        name="flash",
    )(*q_ts, ksa.reshape(G_KV, n_b, t_len, LANES), vs, kwa.reshape(G_KV, n_b, t_len, LANES), vw,
      dt, edge, ocmp, ng, ngate, bg, expand)


def _dec_attn_body(pt_ref, *refs, n_pages, t_new, n_keys, n_wkeys, chained):
    pages = refs[:DEC_NB * n_pages]
    refs = list(refs[DEC_NB * n_pages:])
    if chained:
        del refs[11]
    (qa_ref, slc_ref, win_ref, wst_ref, dsl_ref, dwn_ref, ocmp_ref, g_ref, gate_ref, bg_ref, expand_ref,
     y_ref, wout_ref, ka_scr, va_scr, kw_scr, vw_scr) = refs
    past = n_pages * PAGE_SIZE
    w_past = wst_ref.shape[-1]
    rows = HPG * t_new

    @pl.when(pl.program_id(0) == 0)
    def _():
        for k_scr, n, with_blocks in ((ka_scr, n_keys, True), (kw_scr, n_wkeys, False)):
            feat = lax.broadcasted_iota(jnp.int32, (LANES, n), 0)
            key = lax.broadcasted_iota(jnp.int32, (LANES, n), 1)
            aug = (feat == CONST_LANE) | (feat == CONST_LANE + 1)
            if with_blocks:
                aug = aug | ((feat - SEL_LANE0) == lax.shift_right_logical(key, 6))
            for bi in range(DEC_NB):
                for g in range(G_KV):
                    k_scr[bi, g] = aug.astype(F32).astype(BF16)

    def new_keys_t(new):
        pad = jnp.concatenate([new, jnp.zeros((LANES - t_new, 256), F32)], axis=0)
        return pad[:, 0:128].T, pad[:, 128:256].T

    lane = _lane((HD, LANES))
    lane_o = _lane((t_new, LANES))
    outs = [[[None] * DEC_NB for _ in range(HPG)] for _ in range(2)]
    for bi in range(DEC_NB):
        tok = slice(bi * t_new, (bi + 1) * t_new)
        for p in range(n_pages):
            sl = slice(p * PAGE_SIZE, (p + 1) * PAGE_SIZE)
            for g in range(G_KV):
                ka_scr[bi, g, 0:HD, sl] = pages[bi * n_pages + p][0, g].astype(BF16)
                va_scr[bi, HD * g:HD * (g + 1), sl] = pages[bi * n_pages + p][1, g].astype(BF16)
        kn_t, vn_t = new_keys_t(slc_ref[tok, :])
        va_scr[bi, :, past:n_keys] = vn_t.astype(BF16)
        kwn_t, vwn_t = new_keys_t(win_ref[tok, :])
        vw_scr[bi, :, w_past:n_wkeys] = vwn_t.astype(BF16)
        for g in range(G_KV):
            ka_scr[bi, g, 0:HD, past:n_keys] = kn_t[HD * g:HD * (g + 1)].astype(BF16)
            kw_scr[bi, g, 0:HD, 0:w_past] = wst_ref[bi, 0, g].astype(BF16)
            kw_scr[bi, g, 0:HD, w_past:n_wkeys] = kwn_t[HD * g:HD * (g + 1)].astype(BF16)
            vw_scr[bi, HD * g:HD * (g + 1), 0:w_past] = wst_ref[bi, 1, g].astype(BF16)
            for s, new_t in ((0, kwn_t), (1, vwn_t)):
                sh = pltpu.roll(wst_ref[bi, s, g], w_past - t_new, axis=1)
                tail = pltpu.roll(new_t[HD * g:HD * (g + 1)], LANES - t_new, axis=1)
                wout_ref[bi, s, g, :, 0:w_past - LANES] = sh[:, 0:w_past - LANES]
                wout_ref[bi, s, g, :, w_past - LANES:w_past] = jnp.where(lane >= LANES - t_new, tail,
                                                                        sh[:, w_past - LANES:])
        res = [[None, None], [None, None]]
        for g in range(G_KV):
            q = qa_ref[g, :, tok, :].reshape(rows, LANES).astype(BF16)
            for br, (k_scr, v_scr, b_ref) in enumerate(((ka_scr, va_scr, dsl_ref), (kw_scr, vw_scr, dwn_ref))):
                s = _dot(q, k_scr[bi, g]) + b_ref[g]
                mx = jnp.max(s, axis=-1, keepdims=True)
                p = jnp.exp(s - mx)
                den = jnp.sum(p, axis=-1, keepdims=True)
                res[br][g] = _dot_nt(p.astype(BF16), v_scr[bi]) / den
        for br in range(2):
            for j in range(HPG):
                outs[br][j][bi] = jnp.where(lane_o < 64, res[br][0][j * t_new:(j + 1) * t_new],
                                            res[br][1][j * t_new:(j + 1) * t_new])
    outs = [[jnp.concatenate(outs[br][j], axis=0) for j in range(HPG)] for br in range(2)]
    _mix_gates(g_ref, bg_ref, expand_ref, gate_ref, ocmp_ref, outs[0], outs[1], y_ref)


def _dec_attn(page_table, cache, layer, qa, slc_new, win_new, win_state, dsl, dwn, ocmp, ng, ngate, bg, expand,
              win_all, *, t_new):
    n_b, n_pages = page_table.shape
    assert n_b % DEC_NB == 0
    m = n_b * t_new
    w_past = win_state.shape[-1]
    n_keys = dsl.shape[-1]
    n_wkeys = dwn.shape[-1]
    wblock = (2, G_KV, HD, w_past)
    full = lambda a: pl.BlockSpec(a.shape, lambda b, pt: (0,) * a.ndim)
    row = lambda n: pl.BlockSpec((DEC_NB * t_new, n), lambda b, pt: (b, 0))
    wspec = pl.BlockSpec((None, DEC_NB) + wblock, lambda b, pt: (layer, b, 0, 0, 0, 0))
    grid_spec = pltpu.PrefetchScalarGridSpec(
        num_scalar_prefetch=1, grid=(n_b // DEC_NB,),
        in_specs=_paged_specs(layer, n_pages, PAGE_BLOCK) + [
            pl.BlockSpec((G_KV, HPG, DEC_NB * t_new, LANES), lambda b, pt: (0, 0, b, 0)),
            row(256), row(256), wspec,
            full(dsl), full(dwn), row(W_NSA), row(LANES), row(W_NSA), full(bg), full(expand)]
        + ([] if win_all is None else [pl.BlockSpec(memory_space=pl.ANY)]),
        out_specs=[row(W_NSA), wspec],
        scratch_shapes=[pltpu.VMEM((DEC_NB, G_KV, LANES, n_keys), BF16), pltpu.VMEM((DEC_NB, LANES, n_keys), BF16),
                        pltpu.VMEM((DEC_NB, G_KV, LANES, n_wkeys), BF16), pltpu.VMEM((DEC_NB, LANES, n_wkeys), BF16)])
    args = [page_table, *([cache] * (DEC_NB * n_pages)), qa, slc_new, win_new, win_state, dsl, dwn, ocmp, ng, ngate,
            bg, expand]
    aliases = {}
    if win_all is not None:
        aliases = {len(args): 1}
        args.append(win_all)
    return pl.pallas_call(
        functools.partial(_dec_attn_body, n_pages=n_pages, t_new=t_new, n_keys=n_keys, n_wkeys=n_wkeys,
                          chained=win_all is not None),
        grid_spec=grid_spec,
        out_shape=[jax.ShapeDtypeStruct((m, W_NSA), F32), jax.ShapeDtypeStruct(win_state.shape[:2] + wblock, F32)],
        input_output_aliases=aliases,
        compiler_params=pltpu.CompilerParams(dimension_semantics=("arbitrary",), vmem_limit_bytes=VMEM_LIMIT),
        name="dec_attn",
    )(*args)


def _bucket_matrix(qpos, kpos, valid):
    dist = qpos[:, None] - kpos[None, :]
    return np.where(valid & (dist >= 0), _rel_bucket_np(dist), -1).astype(np.int32)


def _bias_body(tab_ref, bk_ref, o_ref, *, sub_far):
    bk = bk_ref[...]
    for h in range(H_NSA):
        far = tab_ref[N_BUCKETS - 1, h] if sub_far else 0.0
        acc = jnp.full(bk.shape, NEG, F32)
        for b in range(N_BUCKETS):
            acc = jnp.where(bk == b, tab_ref[b, h] - far, acc)
        o_ref[h] = acc


def _bias_table(table, bk, *, sub_far):
    r, c = bk.shape
    tr = 8 if r <= 8 else 64
    assert r % tr == 0
    return pl.pallas_call(
        functools.partial(_bias_body, sub_far=sub_far),
        grid=(r // tr,),
        in_specs=[pl.BlockSpec(memory_space=pltpu.SMEM), pl.BlockSpec((tr, c), lambda i: (i, 0))],
        out_specs=pl.BlockSpec((H_NSA, tr, c), lambda i: (0, i, 0)),
        out_shape=jax.ShapeDtypeStruct((H_NSA, r, c), F32),
        compiler_params=pltpu.CompilerParams(dimension_semantics=("arbitrary",)),
        name="bias_table",
    )(table, jnp.asarray(bk))


def _cmp_tables(qpos_blocks, n_blocks):
    slot = np.arange(LANES)
    blk = np.where(slot < ODD_SLOT0, 2 * slot, 2 * (slot - ODD_SLOT0) + 1)
    real = (blk < 2 * n_blocks) & ((slot % ODD_SLOT0) < n_blocks)
    blk_end = blk * CMP_BLOCK + CMP_BLOCK - 1
    buckets, sbases, useimps = [], [], []
    for qpos in qpos_blocks:
        buckets.append(_bucket_matrix(qpos, blk_end, real[None, :]))
        j = np.arange(LANES)[None, :]
        cur = (qpos // SEL_BLOCK)[:, None]
        forced = (j == 0) | (j == cur) | (j == cur - 1)
        valid = j * SEL_BLOCK <= qpos[:, None]
        inrange = j < n_blocks
        sbases.append(np.where(~inrange, -2.0, np.where(forced, FORCE, -1.0)).astype(np.float32))
        useimps.append((inrange & ~forced & valid).astype(np.float32))
    return np.concatenate(buckets), jnp.asarray(np.stack(sbases)), jnp.asarray(np.stack(useimps))


def _head_rows(b, t):
    k = b.shape[-1]
    n = b.shape[1] // t
    return b.reshape(G_KV, HPG, n, t, k).transpose(2, 0, 1, 3, 4).reshape(n, G_KV, HPG * t, k)


def _prep_weights(rel_bias, w_in, conv_w_pw, gla_w_gk, nsa_q_norm_g, nsa_k_norm_g, nsa_pe_cmp, nsa_w_cmp,
                  nsa_b_gate, w_out):
    depth = w_in.shape[0]
    cols = _in_proj_columns()
    pieces, start = [], 0
    for c in range(1, N_IN + 1):
        pad = cols[start] < 0
        if c == N_IN or (cols[c] >= 0 if pad else cols[c] != cols[c - 1] + 1):
            src = int(cols[start])
            pieces.append(jnp.zeros((depth, D_MODEL, c - start), w_in.dtype) if pad else w_in[:, :, src:src + c - start])
            start = c
    w_in_p = jnp.concatenate(pieces, axis=2).astype(BF16)
    perm = 512 + _nsa_perm()
    w_out_p = jnp.concatenate([w_out[:, 0:512]] + [w_out[:, int(perm[c]):int(perm[c]) + HD]
                                                   for c in range(0, W_NSA, HD)], axis=1).astype(BF16)
    qg = jnp.tile(nsa_q_norm_g, (1, H_NSA))[:, None, :] * SCALE
    kg = jnp.tile(nsa_k_norm_g, (1, 1, G_KV))
    wgk = jnp.zeros((depth, LANES, LANES), F32).at[:, :GATE_RANK, :].set(gla_w_gk)
    wgk_hi = wgk.astype(BF16)
    wgk2 = jnp.stack([wgk_hi, (wgk - wgk_hi.astype(F32)).astype(BF16)], axis=1)
    def blockdiag2(a, b):
        za = jnp.zeros(a.shape[:-1] + (b.shape[-1],), a.dtype)
        zb = jnp.zeros(b.shape[:-1] + (a.shape[-1],), b.dtype)
        return jnp.concatenate([jnp.concatenate([a, za], axis=-1), jnp.concatenate([zb, b], axis=-1)], axis=-2)

    wc = nsa_w_cmp.astype(BF16)
    wtap = blockdiag2(wc, wc)
    wbd = blockdiag2(wtap[:, 0], wtap[:, 1]).reshape(depth, KW, 256)
    pe = jnp.tile(nsa_pe_cmp.transpose(0, 2, 1, 3)[:, :, :, None, :], (1, 1, 1, G_KV, 1))
    petap = pe.transpose(0, 2, 1, 3, 4).reshape(depth, 2, CMP_BLOCK, LANES)
    pe = pe.reshape(depth, 1, KW)
    bg = jnp.zeros((depth, 1, LANES), F32).at[:, 0, :3 * H_NSA].set(nsa_b_gate)
    expand = np.zeros((LANES, 3 * HPG * LANES), np.float32)
    for br in range(3):
        for g in range(G_KV):
            for j in range(HPG):
                c = LANES * (HPG * br + j) + 64 * g
                expand[H_NSA * br + HPG * g + j, c:c + 64] = 1.0
    far = rel_bias[N_BUCKETS - 1]
    far_hi = far.astype(BF16).astype(F32)
    cvec = jnp.zeros((H_NSA, 1, LANES), F32).at[:, 0, CONST_LANE].set(far_hi).at[:, 0, CONST_LANE + 1].set(far - far_hi)
    return dict(w_in=w_in_p, w_out=w_out_p, qg=qg, kg=kg, wgk=wgk2, wbd=wbd, pe=pe, bg=bg,
                wtap=wtap, petap=petap,
                expand=jnp.asarray(expand, BF16), cvec=cvec, wpw=conv_w_pw.astype(BF16),
                indq=jnp.asarray(_blockdiag_ones(W_NSA, HD), BF16), indk=jnp.asarray(_blockdiag_ones(LANES, HD), BF16))


def kernel(x_prompt, x_sample, cache_cmp_kv, cache_slc_kv, page_table, state_win_kv, state_gla, state_conv, rel_bias, norm_g, w_in, conv_w_dw, conv_b_dw, conv_ln_g, conv_ln_b, conv_w_pw, conv_b_pw, gla_w_gk, gla_b_gk, gla_norm_g, nsa_q_norm_g, nsa_k_norm_g, nsa_pe_cmp, nsa_w_cmp, nsa_b_gate, w_out, norm_f):
    depth = w_in.shape[0]
    n_bp, t_p, _ = x_prompt.shape
    n_bs, t_s, _ = x_sample.shape
    n_pages = page_table.shape[1]
    past = n_pages * PAGE_SIZE
    w_past = state_win_kv.shape[2]
    assert t_p % TQ == 0 and WINDOW == 2 * TQ and t_s == 8 and w_past == WINDOW and past % SEL_BLOCK == 0
    wp = _prep_weights(rel_bias, w_in, conv_w_pw, gla_w_gk, nsa_q_norm_g, nsa_k_norm_g, nsa_pe_cmp, nsa_w_cmp,
                       nsa_b_gate, w_out)
    table = rel_bias
    nblk_p = t_p // SEL_BLOCK
    nblk_s = (past + SEL_BLOCK) // SEL_BLOCK
    pos_p = [np.arange(i * LANES, (i + 1) * LANES) for i in range(t_p // LANES)]
    bk_p, sbase_p, useimp_p = _cmp_tables(pos_p, nblk_p)
    bias_p = _head_rows(_bias_table(table, bk_p, sub_far=False), LANES)
    nb_s = LANES // t_s
    qpos_s = past + np.arange(t_s)
    bk_s, _, _ = _cmp_tables([qpos_s], nblk_s)
    bias_s = _head_rows(_bias_table(table, bk_s, sub_far=False), t_s)
    _, sbase_1, useimp_1 = _cmp_tables([np.tile(qpos_s, nb_s)], nblk_s)
    tq = np.arange(TQ)
    every = np.ones((1, 1), bool)
    bk_dt = np.concatenate([_bucket_matrix(tq, tq, every).T, _bucket_matrix(TQ + tq, tq, every).T])
    dt = _bias_table(table, bk_dt, sub_far=True)
    dt = dt.reshape(G_KV, HPG, 2, TQ, TQ).transpose(2, 0, 3, 1, 4).reshape(2, G_KV, TQ, HPG * TQ)
    edge = jnp.asarray(np.tile(np.where(tq[:, None] > tq[None, :], 0.0, NEG).astype(np.float32), (1, HPG)))
    n_keys = past + LANES
    kpos = np.arange(n_keys)
    dsl = _head_rows(_bias_table(table, _bucket_matrix(qpos_s, kpos, (kpos < past + t_s)[None, :]),
                                 sub_far=True), t_s)[0]
    n_wkeys = w_past + LANES
    wk = np.arange(n_wkeys)
    wpos = past - w_past + wk
    wvalid = (wk < w_past + t_s)[None, :] & ((qpos_s[:, None] - wpos[None, :]) < WINDOW) & (wpos >= 0)[None, :]
    dwn = _head_rows(_bias_table(table, _bucket_matrix(qpos_s, wpos, wvalid), sub_far=True), t_s)[0]

    cache_cmp = cache_cmp_kv.transpose(0, 1, 3, 4, 5, 2)
    cache_slc = cache_slc_kv.transpose(0, 1, 3, 4, 5, 2)
    win_state = state_win_kv.transpose(0, 1, 3, 4, 5, 2)
    s0 = state_gla.transpose(0, 1, 2, 4, 3).reshape(depth, n_bs, W_GLA, DK_GLA)
    i256 = np.arange(W_GLA)
    i128 = np.arange(LANES)
    bdmask = jnp.asarray((i256[:, None] // DV_GLA == i128[None, :] // DK_GLA).astype(np.float32))
    s0 = jnp.tile(s0, (1, 1, 1, H_GLA)) * bdmask
    zero_s0 = jnp.zeros((n_bp, W_GLA, LANES), F32)
    zero_hist = jnp.zeros((n_bp, HIST, W_CONV), F32)

    def unpack_state(st):
        b = st.shape[0]
        blocks = [st[:, DV_GLA * h:DV_GLA * (h + 1), DK_GLA * h:DK_GLA * (h + 1)] for h in range(H_GLA)]
        return jnp.stack(blocks, axis=1).transpose(0, 1, 3, 2)

    xp = x_prompt.reshape(n_bp * t_p, D_MODEL)
    xs = x_sample.reshape(n_bs * t_s, D_MODEL)
    outs_p, outs_s = [], []
    chain_p, win_all = (), None
    for l in range(depth):
        final = l == depth - 1
        row = lambda a: a[l][None, :]
        common = dict(wdw=conv_w_dw[l], bdw=row(conv_b_dw), lng=row(conv_ln_g), lnb=row(conv_ln_b), wpw=wp["wpw"][l],
                      bpw=row(conv_b_pw))
        bgk = jnp.zeros((1, LANES), F32).at[0, :].set(gla_b_gk[l])
        gng = jnp.tile(gla_norm_g[l], (H_GLA,))[None, :]
        kg12 = wp["kg"][l, 1:3]
        kg0 = wp["kg"][l, 0:1]
        zc, zg, qn, cmp_n, _, win_n, ngate, ng, ksa, vs, kwa, vw, cmp_all, slc_all = _in_proj(
            xp, row(norm_g), wp["w_in"][l], wp["indq"], wp["indk"], wp["qg"][l], kg12, seq_len=t_p, prompt=True,
            layer=l, depth=depth, chain=chain_p)
        chain_p = (cmp_all, slc_all)
        yc, conv_st = _conv(zc, zero_hist, nb=1, t_len=t_p, **common)
        yg, gla_st = _gla(zg, zero_s0, wp["wgk"][l], bgk, gng, nb=1, t_len=t_p, chunk=16)
        kvc = _compress_prompt(cmp_n, wp["wbd"][l], wp["pe"][l], n_b=n_bp, t_len=t_p)
        *q_ts, ocmp = _cmpsel(qn, kvc, bias_p, sbase_p, useimp_p, wp["indk"], kg0, wp["cvec"], nb=1, tb=LANES,
                              n_blocks=nblk_p, q_t=True)
        yn = _flash(q_ts, ksa, vs, kwa, vw, dt, edge, ocmp, ng, ngate, wp["bg"][l], wp["expand"], n_b=n_bp, t_len=t_p)
        xp = _out_proj(xp, yc, yg, yn, wp["w_out"][l], norm_f[None, :], final=final)
        w_keep = min(WINDOW, t_p)
        outs_p.append((conv_st, unpack_state(gla_st), win_n.reshape(n_bp, t_p, 256)[:, t_p - w_keep:]))
        zc, zg, qn, cmp_n, slc_n, win_n, ngate, ng = _in_proj(
            xs, row(norm_g), wp["w_in"][l], wp["indq"], wp["indk"], wp["qg"][l], kg12, seq_len=t_s, prompt=False)
        yc, conv_st = _conv(zc, state_conv[l], nb=nb_s, t_len=t_s, **common)
        yg, gla_st = _gla(zg, s0[l], wp["wgk"][l], bgk, gng, nb=nb_s, t_len=t_s, chunk=t_s)
        kvc = _compress_paged(page_table, cache_cmp, l, cmp_n, wp["wtap"][l], wp["petap"][l], t_new=t_s)
        qa, ocmp = _cmpsel(qn, kvc, bias_s, sbase_1, useimp_1, wp["indk"], kg0, wp["cvec"], nb=nb_s, tb=t_s,
                           n_blocks=nblk_s, q_t=False)
        yn, win_all = _dec_attn(page_table, cache_slc, l, qa, slc_n, win_n, win_state, dsl, dwn, ocmp, ng, ngate,
                                wp["bg"][l], wp["expand"], win_all, t_new=t_s)
        xs = _out_proj(xs, yc, yg, yn, wp["w_out"][l], norm_f[None, :], final=final)
        outs_s.append((conv_st, unpack_state(gla_st), None, cmp_n, slc_n))

    def stack(outs, k, shape):
        return jnp.stack([o[k] for o in outs]).reshape(shape)

    kv = (2, G_KV, HD)
    return (xp.reshape(n_bp, t_p, D_MODEL), xs.reshape(n_bs, t_s, D_MODEL),
            stack(outs_p, 0, (depth, n_bp, HIST, W_CONV)), stack(outs_s, 0, (depth, n_bs, HIST, W_CONV)),
            stack(outs_p, 1, (depth, n_bp, H_GLA, DK_GLA, DV_GLA)), stack(outs_s, 1, (depth, n_bs, H_GLA, DK_GLA, DV_GLA)),
            stack(outs_p, 2, (depth, n_bp, min(WINDOW, t_p)) + kv), win_all.transpose(0, 1, 5, 2, 3, 4),
            chain_p[0].transpose(0, 1, 5, 2, 3, 4), stack(outs_s, 3, (depth, n_bs, t_s) + kv),
            chain_p[1].transpose(0, 1, 5, 2, 3, 4), stack(outs_s, 4, (depth, n_bs, t_s) + kv))
```

```python
import functools
import math

import numpy as np
import jax
import jax.numpy as jnp
from jax import lax
from jax.experimental import pallas as pl
from jax.experimental.pallas import tpu as pltpu

F32 = jnp.float32
BF16 = jnp.bfloat16

D_MODEL = 1024
W_CONV = 256
CONV_WIDTH = 31
HIST = CONV_WIDTH - 1
H_GLA = 4
DK_GLA = 32
DV_GLA = 64
W_GLA = H_GLA * DV_GLA
GATE_RANK = 16
GATE_NORMALIZER = 16.0
H_NSA = 8
HD = 64
G_KV = 2
HPG = H_NSA // G_KV
W_NSA = H_NSA * HD
CMP_BLOCK = 32
SEL_BLOCK = 64
TOP_N = 8
WINDOW = 512
N_BUCKETS = 32
MAX_EXACT = N_BUCKETS // 2
MAX_DISTANCE = 128
PAGE_SIZE = 128
SCALE = HD ** -0.5
EPS = 1e-6
NEG = -1e30
FORCE = 1e4

LANES = 128
SEL_LANE0 = 64
SEL_SLOTS = 40
CONST_LANE = 112
ODD_SLOT0 = 64
TQ = 256
V_ROWS = 128
KEY_CHUNK = 256
COL_CHUNK = 128
VMEM_LIMIT = 56 * 1024 * 1024

C_CONV = 0
C_GLA = 3 * W_CONV
N_GLA = 2 * H_GLA * DK_GLA + 2 * W_GLA + LANES
C_NSA = C_GLA + N_GLA
N_NSA = W_NSA + 3 * 256 + W_NSA + LANES
N_IN = C_NSA + N_NSA


def _rel_bucket_np(dist):
    n = np.maximum(dist, 0)
    nf = np.maximum(n, 1).astype(np.float32)
    large = MAX_EXACT + (np.log(nf / np.float32(MAX_EXACT)) / np.float32(math.log(MAX_DISTANCE / MAX_EXACT))
                         * np.float32(N_BUCKETS - MAX_EXACT)).astype(np.int32)
    large = np.minimum(large, N_BUCKETS - 1)
    return np.where(n < MAX_EXACT, n, large).astype(np.int32)


def _in_proj_columns():
    o = {}
    off = 0
    for name, w in (("c_a", 256), ("c_b", 256), ("c_gate", 256), ("l_q", 128), ("l_k", 128), ("l_v", 256),
                    ("l_gk", 16), ("l_gate", 256), ("n_q", 512), ("n_cmp", 256), ("n_slc", 256),
                    ("n_win", 256), ("n_g", 24), ("n_gate", 512)):
        o[name] = off
        off += w
    cols = -np.ones((N_IN,), np.int64)

    def put(dst, name, width):
        cols[dst:dst + width] = o[name] + np.arange(width)

    put(0, "c_a", 256); put(256, "c_b", 256); put(512, "c_gate", 256)
    g = C_GLA
    put(g, "l_q", 128); put(g + 128, "l_k", 128); put(g + 256, "l_v", 256); put(g + 512, "l_gate", 256)
    put(g + 768, "l_gk", 16)
    n = C_NSA
    put(n, "n_q", 512); put(n + 512, "n_cmp", 256); put(n + 768, "n_slc", 256); put(n + 1024, "n_win", 256)
    cols[n + 1280:n + 1792] = o["n_gate"] + _nsa_perm()
    put(n + 1792, "n_g", 24)
    return cols


def _nsa_perm():
    p = np.zeros((W_NSA,), np.int64)
    for j in range(HPG):
        for g in range(G_KV):
            p[128 * j + 64 * g:128 * j + 64 * g + 64] = 64 * (HPG * g + j) + np.arange(64)
    return p


def _blockdiag_ones(n, blk):
    i = np.arange(n)
    return (i[:, None] // blk == i[None, :] // blk).astype(np.float32)


def _split2(x):
    hi = x.astype(BF16)
    lo = (x - hi.astype(F32)).astype(BF16)
    return hi, lo


def _dot(a, b):
    return jnp.dot(a, b, preferred_element_type=F32)


def _dot_nt(a, b):
    return lax.dot_general(a, b, (((1,), (1,)), ((), ())), preferred_element_type=F32)


def _dot_tn(a, b):
    return lax.dot_general(a, b, (((0,), (0,)), ((), ())), preferred_element_type=F32)


def _dot_split(x, m_bf16):
    hi, lo = _split2(x)
    return _dot(hi, m_bf16) + _dot(lo, m_bf16)


def _seg_rms(x, ind_ref, seg):
    return lax.rsqrt(_dot_split(x * x, ind_ref[...]) * (1.0 / seg) + EPS)


def _sigmoid(x):
    return 1.0 / (1.0 + jnp.exp(-x))


def _silu(x):
    return x * _sigmoid(x)


def _lane(shape):
    return lax.broadcasted_iota(jnp.int32, shape, len(shape) - 1)


def _half_pad(x, odd):
    if odd:
        x = pltpu.roll(x, 64, axis=1)
    return jnp.where(_lane(x.shape) < 64, x, 0.0)


def _inproj_body(x_ref, ng_ref, w_ref, indq_ref, indk_ref, qg_ref, kg_ref, *outs, tm, seq_len, prompt, n_chained):
    outs = outs[n_chained:]
    zc_ref, zg_ref, qn_ref, cmp_ref, slc_ref, win_ref, gate_ref, g_ref = outs[:8]
    x = x_ref[...]
    ms = jnp.mean(x * x, axis=-1, keepdims=True)
    h = (x * lax.rsqrt(ms + EPS) * ng_ref[...]).astype(BF16)

    def mm(lo, width):
        return _dot(h, w_ref[:, lo:lo + width])

    zc_ref[...] = mm(C_CONV, 3 * W_CONV)
    zg_ref[...] = mm(C_GLA, N_GLA)
    q = mm(C_NSA, W_NSA)
    qn_ref[...] = (q * _seg_rms(q, indq_ref, HD) * qg_ref[...]).astype(BF16)
    cmp = mm(C_NSA + 512, 256)
    cmp_ref[...] = cmp
    slc = mm(C_NSA + 768, 256)
    win = mm(C_NSA + 1024, 256)
    ks = slc[:, 0:128]
    ks = ks * _seg_rms(ks, indk_ref, HD) * kg_ref[0:1, :]
    kw = win[:, 0:128]
    kw = kw * _seg_rms(kw, indk_ref, HD) * kg_ref[1:2, :]
    slc_ref[:, 0:128] = ks
    slc_ref[:, 128:256] = slc[:, 128:256]
    win_ref[:, 0:128] = kw
    win_ref[:, 128:256] = win[:, 128:256]
    gate_ref[...] = mm(C_NSA + 1280, W_NSA)
    g_ref[...] = mm(C_NSA + 1792, LANES)
    if prompt:
        ksa_ref, vs_ref, kwa_ref, vw_ref, cmpt_ref, slct_ref = outs[8:]
        cmpt_ref[...] = cmp.T.reshape(2, G_KV, HD, tm)
        slct_ref[...] = jnp.concatenate([ks, slc[:, 128:256]], axis=1).T.reshape(2, G_KV, HD, tm)
        lane = _lane((tm, LANES))
        row = lax.broadcasted_iota(jnp.int32, (tm, LANES), 0)
        t = lax.rem(pl.program_id(0) * tm + row, seq_len)
        ones = ((lane == CONST_LANE) | (lane == CONST_LANE + 1)).astype(F32)
        onehot = ((lane - SEL_LANE0) == lax.shift_right_logical(t, 6)).astype(F32)
        for g in range(G_KV):
            ksa_ref[g] = (_half_pad(ks, g == 1) + onehot + ones).astype(BF16)
            kwa_ref[g] = (_half_pad(kw, g == 1) + ones).astype(BF16)
        feat = lax.broadcasted_iota(jnp.int32, (LANES, tm), 0)
        for v_ref, src in ((vs_ref, slc), (vw_ref, win)):
            vt = src[:, 128:256].T
            for g in range(G_KV):
                own = vt if g == 0 else pltpu.roll(vt, HD, axis=0)
                own = jnp.where(feat < HD, own, (feat == HD).astype(F32))[0:V_ROWS].astype(BF16)
                for kt in range(tm // TQ):
                    v_ref[kt, g] = own[:, kt * TQ:(kt + 1) * TQ]


def _in_proj(x2d, norm_g, w, indq, indk, qg, kg, *, seq_len, prompt, layer=0, depth=1, chain=()):
    m = x2d.shape[0]
    tm = min(512, m)
    assert m % tm == 0 and (not prompt or (seq_len % tm == 0 and tm % TQ == 0))
    row = lambda n: pl.BlockSpec((tm, n), lambda i: (i, 0))
    full = lambda a: pl.BlockSpec(a.shape, lambda i: (0,) * a.ndim)
    out_shape = [jax.ShapeDtypeStruct((m, 3 * W_CONV), F32), jax.ShapeDtypeStruct((m, N_GLA), F32),
                 jax.ShapeDtypeStruct((m, W_NSA), BF16), jax.ShapeDtypeStruct((m, 256), F32),
                 jax.ShapeDtypeStruct((m, 256), F32), jax.ShapeDtypeStruct((m, 256), F32),
                 jax.ShapeDtypeStruct((m, W_NSA), F32), jax.ShapeDtypeStruct((m, LANES), F32)]
    out_specs = [row(3 * W_CONV), row(N_GLA), row(W_NSA), row(256), row(256), row(256), row(W_NSA), row(LANES)]
    if prompt:
        aug = pl.BlockSpec((G_KV, tm, LANES), lambda i: (0, i, 0))
        vt = pl.BlockSpec((tm // TQ, G_KV, V_ROWS, TQ), lambda i: (i, 0, 0, 0))
        vt_shape = jax.ShapeDtypeStruct((m // TQ, G_KV, V_ROWS, TQ), BF16)
        per_seq = seq_len // tm
        leaf = pl.BlockSpec((None, None, 2, G_KV, HD, tm), lambda i: (layer, i // per_seq, 0, 0, 0, i % per_seq))
        leaf_shape = jax.ShapeDtypeStruct((depth, m // seq_len, 2, G_KV, HD, seq_len), F32)
        out_shape += [jax.ShapeDtypeStruct((G_KV, m, LANES), BF16), vt_shape,
                      jax.ShapeDtypeStruct((G_KV, m, LANES), BF16), vt_shape, leaf_shape, leaf_shape]
        out_specs += [aug, vt, aug, vt, leaf, leaf]
    args = [x2d, norm_g, w, indq, indk, qg, kg]
    return pl.pallas_call(
        functools.partial(_inproj_body, tm=tm, seq_len=seq_len, prompt=prompt, n_chained=len(chain)),
        grid=(m // tm,),
        in_specs=[row(D_MODEL), full(norm_g), full(w), full(indq), full(indk), full(qg), full(kg)]
        + [pl.BlockSpec(memory_space=pl.ANY)] * len(chain),
        out_specs=out_specs, out_shape=out_shape,
        input_output_aliases={len(args) + k: 12 + k for k in range(len(chain))},
        compiler_params=pltpu.CompilerParams(dimension_semantics=("arbitrary",), vmem_limit_bytes=VMEM_LIMIT),
        name="in_proj",
    )(*args, *chain)


def _outproj_body(x_ref, yc_ref, yg_ref, yn_ref, w_ref, nf_ref, o_ref, *, final):
    x = x_ref[...]
    x = (x + _dot(yc_ref[...], w_ref[0:256, :]) + _dot(yg_ref[...], w_ref[256:512, :])
         + _dot(yn_ref[...].astype(BF16), w_ref[512:1024, :]))
    if final:
        ms = jnp.mean(x * x, axis=-1, keepdims=True)
        x = x * lax.rsqrt(ms + EPS) * nf_ref[...]
    o_ref[...] = x


def _out_proj(x2d, yc, yg, yn, w, norm_f, *, final):
    m = x2d.shape[0]
    tm = min(512, m)
    assert m % tm == 0
    row = lambda n: pl.BlockSpec((tm, n), lambda i: (i, 0))
    full = lambda a: pl.BlockSpec(a.shape, lambda i: (0,) * a.ndim)
    return pl.pallas_call(
        functools.partial(_outproj_body, final=final),
        grid=(m // tm,),
        in_specs=[row(D_MODEL), row(256), row(256), row(512), full(w), full(norm_f)],
        out_specs=row(D_MODEL), out_shape=jax.ShapeDtypeStruct((m, D_MODEL), F32),
        compiler_params=pltpu.CompilerParams(dimension_semantics=("arbitrary",), vmem_limit_bytes=VMEM_LIMIT),
        name="out_proj",
    )(x2d, yc, yg, yn, w, norm_f)


def _conv_body(zc_ref, hist_ref, wdw_ref, bdw_ref, lng_ref, lnb_ref, wpw_ref, bpw_ref, y_ref, st_ref,
               ext_scr, act_scr, *, nb, t_len, tc, tmm):
    n_chunks = t_len // tc

    def per_batch(bi, _):
        base = bi * t_len
        ext_scr[0:32, :] = jnp.zeros((32, W_CONV), F32)
        ext_scr[2:32, :] = hist_ref[bi]

        def chunk(c, _):
            r0 = pl.multiple_of(c * tc, tc)
            g0 = pl.multiple_of(base + r0, tc)
            a = zc_ref[pl.ds(g0, tc), 0:256]
            b = zc_ref[pl.ds(g0, tc), 256:512]
            ext_scr[pl.ds(32 + r0, tc), :] = a * _sigmoid(b)
            win = ext_scr[pl.ds(r0, tc + 32), :]
            acc = jnp.zeros((tc, W_CONV), F32) + bdw_ref[...]
            for r in range(8):
                wr = pltpu.roll(win, tc + 32 - (2 + r), axis=0)
                for mi, j in enumerate(range(r, CONV_WIDTH, 8)):
                    acc = acc + wr[8 * mi:8 * mi + tc] * wdw_ref[j:j + 1, :]
            mu = jnp.mean(acc, axis=-1, keepdims=True)
            d = acc - mu
            var = jnp.mean(d * d, axis=-1, keepdims=True)
            yn = d * lax.rsqrt(var + EPS) * lng_ref[...] + lnb_ref[...]
            act_scr[pl.ds(g0, tc), :] = _silu(yn)
            return 0

        lax.fori_loop(0, n_chunks, chunk, 0)
        st_ref[bi] = ext_scr[2 + t_len:32 + t_len, :]
        return 0

    lax.fori_loop(0, nb, per_batch, 0)

    def mm(c, _):
        r0 = pl.multiple_of(c * tmm, tmm)
        y = _dot(act_scr[pl.ds(r0, tmm), :].astype(BF16), wpw_ref[...]) + bpw_ref[...]
        y_ref[pl.ds(r0, tmm), :] = (y * _silu(zc_ref[pl.ds(r0, tmm), 512:768])).astype(BF16)
        return 0

    lax.fori_loop(0, nb * t_len // tmm, mm, 0)


def _conv(zc, hist, wdw, bdw, lng, lnb, wpw, bpw, *, nb, t_len):
    m = zc.shape[0]
    n_b = m // t_len
    tc = min(64, t_len)
    tmm = min(256, nb * t_len)
    full = lambda a: pl.BlockSpec(a.shape, lambda i: (0,) * a.ndim)
    return pl.pallas_call(
        functools.partial(_conv_body, nb=nb, t_len=t_len, tc=tc, tmm=tmm),
        grid=(n_b // nb,),
        in_specs=[pl.BlockSpec((nb * t_len, 3 * W_CONV), lambda i: (i, 0)),
                  pl.BlockSpec((nb, HIST, W_CONV), lambda i: (i, 0, 0)),
                  full(wdw), full(bdw), full(lng), full(lnb), full(wpw), full(bpw)],
        out_specs=[pl.BlockSpec((nb * t_len, W_CONV), lambda i: (i, 0)),
                   pl.BlockSpec((nb, HIST, W_CONV), lambda i: (i, 0, 0))],
        out_shape=[jax.ShapeDtypeStruct((m, W_CONV), BF16), jax.ShapeDtypeStruct((n_b, HIST, W_CONV), F32)],
        scratch_shapes=[pltpu.VMEM((32 + t_len, W_CONV), F32), pltpu.VMEM((nb * t_len, W_CONV), F32)],
        compiler_params=pltpu.CompilerParams(dimension_semantics=("arbitrary",), vmem_limit_bytes=VMEM_LIMIT),
        name="conv",
    )(zc, hist, wdw, bdw, lng, lnb, wpw, bpw)


def _gla_body(zg_ref, s0_ref, wgk_ref, bgk_ref, ltri_ref, lsum_ref, ind_ref, indv_ref, ng_ref, bd_ref,
              y_ref, st_ref, bc_scr, qe_scr, ke_scr, dec_scr, o_scr, *, nb, t_len, chunk, rb):
    rows = nb * t_len
    n_rb = rows // rb
    cpb = t_len // chunk

    def phase1(i, _):
        r0 = pl.multiple_of(i * rb, rb)
        gk = zg_ref[pl.ds(r0, rb), 768:896]
        pre = _dot_split(gk, wgk_ref[0]) + _dot(gk.astype(BF16), wgk_ref[1]) + bgk_ref[...]
        la = (jnp.minimum(pre, 0.0) - jnp.log(1.0 + jnp.exp(-jnp.abs(pre)))) * (1.0 / GATE_NORMALIZER)
        h1, l1 = _split2(la)
        l2 = (la - h1.astype(F32) - l1.astype(F32)).astype(BF16)
        bc = _dot(ltri_ref[...], h1) + _dot(ltri_ref[...], l1) + _dot(ltri_ref[...], l2)
        bt = _dot(lsum_ref[...], h1) + _dot(lsum_ref[...], l1) + _dot(lsum_ref[...], l2)
        bc_scr[pl.ds(r0, rb), :] = bc
        qe_scr[pl.ds(r0, rb), :] = zg_ref[pl.ds(r0, rb), 0:128] * (DK_GLA ** -0.5) * jnp.exp(bc)
        ke_scr[pl.ds(r0, rb), :] = zg_ref[pl.ds(r0, rb), 128:256] * jnp.exp(bt - bc)
        dec_scr[pl.ds(r0, rb), :] = jnp.exp(bt)
        return 0

    lax.fori_loop(0, n_rb, phase1, 0)

    ti = lax.broadcasted_iota(jnp.int32, (chunk, chunk, LANES), 0)
    si = lax.broadcasted_iota(jnp.int32, (chunk, chunk, LANES), 1)
    causal = si <= ti

    u2 = 2 if (rows // chunk) % 2 == 0 else 1
    u3 = 4 if cpb % 4 == 0 else 1

    def phase2(cg, _):
        for u in range(u2):
            r0 = pl.multiple_of((cg * u2 + u) * chunk, chunk)
            bc = bc_scr[pl.ds(r0, chunk), :]
            q = zg_ref[pl.ds(r0, chunk), 0:128] * (DK_GLA ** -0.5)
            k = zg_ref[pl.ds(r0, chunk), 128:256]
            v = zg_ref[pl.ds(r0, chunk), 256:512]
            e = jnp.exp(jnp.where(causal, bc[:, None, :] - bc[None, :, :], NEG))
            p = (q[:, None, :] * k[None, :, :] * e).reshape(chunk * chunk, LANES)
            att = _dot(p.astype(BF16), ind_ref[...]).reshape(chunk, chunk, W_GLA)
            o_scr[pl.ds(r0, chunk), :] = (att * v[None, :, :]).sum(axis=1)
        return 0

    lax.fori_loop(0, rows // chunk // u2, phase2, 0)

    def per_batch(bi, _):
        def group(cg, s):
            for u in range(u3):
                r0 = pl.multiple_of(bi * t_len + (cg * u3 + u) * chunk, chunk)
                o_scr[pl.ds(r0, chunk), :] += _dot_nt(qe_scr[pl.ds(r0, chunk), :].astype(BF16), s.astype(BF16))
                upd = _dot_tn(zg_ref[pl.ds(r0, chunk), 256:512].astype(BF16),
                              ke_scr[pl.ds(r0, chunk), :].astype(BF16))
                s = s * dec_scr[pl.ds(r0, 1), :] + upd * bd_ref[...]
            return s

        st_ref[bi] = lax.fori_loop(0, cpb // u3, group, s0_ref[bi])
        return 0

    lax.fori_loop(0, nb, per_batch, 0)

    def phase4(i, _):
        r0 = pl.multiple_of(i * rb, rb)
        o = o_scr[pl.ds(r0, rb), :]
        o = o * _seg_rms(o, indv_ref, DV_GLA) * ng_ref[...]
        y_ref[pl.ds(r0, rb), :] = (o * _silu(zg_ref[pl.ds(r0, rb), 512:768])).astype(BF16)
        return 0

    lax.fori_loop(0, n_rb, phase4, 0)


def _gla(zg, s0t, wgk, bgk, ng, *, nb, t_len, chunk):
    m = zg.shape[0]
    n_b = m // t_len
    rows = nb * t_len
    rb = min(128, rows)
    assert rows % rb == 0 and rb % chunk == 0 and t_len % chunk == 0
    ltri = jnp.asarray(_blockdiag_ones(rb, chunk) * np.tril(np.ones((rb, rb), np.float32)), BF16)
    lsum = jnp.asarray(_blockdiag_ones(rb, chunk), BF16)
    i128 = np.arange(LANES)
    i256 = np.arange(W_GLA)
    ind = jnp.asarray((i128[:, None] // DK_GLA == i256[None, :] // DV_GLA).astype(np.float32), BF16)
    indv = jnp.asarray(_blockdiag_ones(W_GLA, DV_GLA), BF16)
    bd = jnp.asarray((i256[:, None] // DV_GLA == i128[None, :] // DK_GLA).astype(np.float32))
    full = lambda a: pl.BlockSpec(a.shape, lambda i: (0,) * a.ndim)
    return pl.pallas_call(
        functools.partial(_gla_body, nb=nb, t_len=t_len, chunk=chunk, rb=rb),
        grid=(n_b // nb,),
        in_specs=[pl.BlockSpec((rows, N_GLA), lambda i: (i, 0)),
                  pl.BlockSpec((nb, W_GLA, LANES), lambda i: (i, 0, 0)),
                  full(wgk), full(bgk), full(ltri), full(lsum), full(ind), full(indv), full(ng), full(bd)],
        out_specs=[pl.BlockSpec((rows, W_GLA), lambda i: (i, 0)),
                   pl.BlockSpec((nb, W_GLA, LANES), lambda i: (i, 0, 0))],
        out_shape=[jax.ShapeDtypeStruct((m, W_GLA), BF16), jax.ShapeDtypeStruct((n_b, W_GLA, LANES), F32)],
        scratch_shapes=[pltpu.VMEM((rows, LANES), F32), pltpu.VMEM((rows, LANES), F32),
                        pltpu.VMEM((rows, LANES), F32), pltpu.VMEM((rows, LANES), F32),
                        pltpu.VMEM((rows, W_GLA), F32)],
        compiler_params=pltpu.CompilerParams(dimension_semantics=("arbitrary",), vmem_limit_bytes=VMEM_LIMIT),
        name="gla",
    )(zg, s0t, wgk, bgk, ltri, lsum, ind, indv, ng, bd)


KW = CMP_BLOCK * 256


def _compress_rows(x_ref, w_ref, pe_ref, kvc_ref, n_rows):
    kvc_ref[...] = jnp.zeros(kvc_ref.shape, F32)
    for par in range(2):
        x = (x_ref[:, par * KW:(par + 1) * KW] + pe_ref[...]).astype(BF16)
        kvc_ref[ODD_SLOT0 * par:ODD_SLOT0 * par + n_rows, :] = _dot(x, w_ref[...])


def _compress_prompt_body(x_ref, w_ref, pe_ref, kvc_ref, *, n_rows):
    _compress_rows(x_ref, w_ref, pe_ref, kvc_ref, n_rows)


def _compress_prompt(cmp_rows, w, pe, *, n_b, t_len):
    n_rows = t_len // SEL_BLOCK
    full = lambda a: pl.BlockSpec(a.shape, lambda i: (0,) * a.ndim)
    return pl.pallas_call(
        functools.partial(_compress_prompt_body, n_rows=n_rows),
        grid=(n_b,),
        in_specs=[pl.BlockSpec((n_rows, 2 * KW), lambda i: (i, 0)), full(w), full(pe)],
        out_specs=pl.BlockSpec((None, LANES, 256), lambda i: (i, 0, 0)),
        out_shape=jax.ShapeDtypeStruct((n_b, LANES, 256), F32),
        compiler_params=pltpu.CompilerParams(dimension_semantics=("arbitrary",), vmem_limit_bytes=VMEM_LIMIT),
        name="compress_prompt",
    )(cmp_rows.reshape(n_b * n_rows, 2 * KW), w, pe)


def _compress_paged_body(pt_ref, *refs, n_pages, t_new, n_rows):
    pages = refs[:DEC_NB * n_pages]
    new_ref, w_ref, pe_ref, kvc_ref, x_scr, acc_scr = refs[DEC_NB * n_pages:]
    per_page = PAGE_SIZE // CMP_BLOCK
    n_past = n_pages * per_page
    n_cmp = 2 * n_rows
    for bi in range(DEC_NB):
        new = new_ref[bi * t_new:(bi + 1) * t_new, :]
        for s in range(2):
            for p in range(n_pages):
                rows = pages[bi * n_pages + p][s].reshape(LANES, PAGE_SIZE).T
                for c in range(per_page):
                    r0 = X_PITCH * (per_page * p + c)
                    x_scr[bi, s, r0:r0 + CMP_BLOCK, :] = rows[CMP_BLOCK * c:CMP_BLOCK * (c + 1)]
            x_scr[bi, s, X_PITCH * n_past:, :] = jnp.zeros((X_PITCH * (n_cmp - n_past), LANES), F32)
            x_scr[bi, s, X_PITCH * n_past:X_PITCH * n_past + t_new, :] = new[:, LANES * s:LANES * (s + 1)]
    kvc_ref[...] = jnp.zeros(kvc_ref.shape, F32)
    for s in range(2):
        acc = jnp.zeros((DEC_NB * n_cmp, LANES), F32)
        for j in range(CMP_BLOCK):
            x = jnp.concatenate([x_scr[bi, s, pl.ds(j, n_cmp, stride=X_PITCH), :] for bi in range(DEC_NB)], axis=0)
            acc = acc + _dot((x + pe_ref[s, j:j + 1, :]).astype(BF16), w_ref[s, j])
        acc_scr[...] = acc
        for bi in range(DEC_NB):
            for par in range(2):
                kvc_ref[bi, ODD_SLOT0 * par:ODD_SLOT0 * par + n_rows, LANES * s:LANES * (s + 1)] = (
                    acc_scr[pl.ds(bi * n_cmp + par, n_rows, stride=2), :])


def _paged_specs(layer, n_pages, block):
    return [pl.BlockSpec((None, None) + block, functools.partial(
        lambda b, pt, bi, p: (layer, pt[DEC_NB * b + bi, p]) + (0,) * len(block), bi=bi, p=p))
        for bi in range(DEC_NB) for p in range(n_pages)]


PAGE_BLOCK = (2, G_KV, HD, PAGE_SIZE)
X_PITCH = CMP_BLOCK + 4
DEC_NB = 2


def _compress_paged(page_table, cache_t, layer, cmp_new, w, pe, *, t_new):
    n_b, n_pages = page_table.shape
    assert n_b % DEC_NB == 0
    n_rows = -(-((n_pages * PAGE_SIZE + SEL_BLOCK) // SEL_BLOCK) // 8) * 8
    full = lambda a: pl.BlockSpec(a.shape, lambda b, pt: (0,) * a.ndim)
    grid_spec = pltpu.PrefetchScalarGridSpec(
        num_scalar_prefetch=1, grid=(n_b // DEC_NB,),
        in_specs=_paged_specs(layer, n_pages, PAGE_BLOCK) + [
            pl.BlockSpec((DEC_NB * t_new, 256), lambda b, pt: (b, 0)), full(w), full(pe)],
        out_specs=pl.BlockSpec((DEC_NB, LANES, 256), lambda b, pt: (b, 0, 0)),
        scratch_shapes=[pltpu.VMEM((DEC_NB, 2, 2 * n_rows * X_PITCH, LANES), F32),
                        pltpu.VMEM((DEC_NB * 2 * n_rows, LANES), F32)])
    return pl.pallas_call(
        functools.partial(_compress_paged_body, n_pages=n_pages, t_new=t_new, n_rows=n_rows),
        grid_spec=grid_spec, out_shape=jax.ShapeDtypeStruct((n_b, LANES, 256), F32),
        compiler_params=pltpu.CompilerParams(dimension_semantics=("arbitrary",), vmem_limit_bytes=VMEM_LIMIT),
        name="compress_paged",
    )(page_table, *([cache_t] * (DEC_NB * n_pages)), cmp_new, w, pe)


def _cmpsel_body(qn_ref, kvc_ref, bias_ref, sbase_ref, useimp_ref, indk_ref, kg_ref, cvec_ref,
                 *outs, nb, tb, n_blocks, q_t):
    q_refs, ocmp_ref = outs[:-1], outs[-1]
    rows = nb * tb
    lane = _lane((rows, LANES))
    qn = qn_ref[...].astype(F32)
    qpad = [_half_pad(qn[:, 128 * (h // 2):128 * (h // 2) + 128], h % 2 == 1) for h in range(H_NSA)]
    imp_parts = []
    o_parts = [[None] * nb for _ in range(H_NSA)]
    kc_all = kvc_ref[:, :, 0:128].reshape(nb * LANES, LANES)
    kc_all = kc_all * _seg_rms(kc_all, indk_ref, HD) * kg_ref[...]
    for bi in range(nb):
        kc = kc_all[bi * LANES:(bi + 1) * LANES]
        vboth = kvc_ref[bi, :, 128:256].astype(BF16)
        imp_b = []
        for g in range(G_KV):
            kpad = _half_pad(kc, g == 1).astype(BF16)
            qs = jnp.concatenate([qpad[HPG * g + j][bi * tb:(bi + 1) * tb] for j in range(HPG)], axis=0).astype(BF16)
            bias = bias_ref[g]
            l = _dot_nt(qs, kpad) + bias
            vis = bias > 0.5 * NEG
            mx = jnp.max(l, axis=-1, keepdims=True)
            p = jnp.where(vis, jnp.exp(l - mx), 0.0)
            p = p / jnp.maximum(jnp.sum(p, axis=-1, keepdims=True), 1e-30)
            o = _dot(p.astype(BF16), vboth)
            ig = p[0:tb]
            for j in range(1, HPG):
                ig = ig + p[j * tb:(j + 1) * tb]
            imp_b.append(ig)
            for j in range(HPG):
                o_parts[HPG * g + j][bi] = o[j * tb:(j + 1) * tb]
        imp_parts.append(imp_b)
    for j in range(HPG):
        o0 = jnp.concatenate(o_parts[j], axis=0) if nb > 1 else o_parts[j][0]
        o1 = jnp.concatenate(o_parts[HPG + j], axis=0) if nb > 1 else o_parts[HPG + j][0]
        ocmp_ref[:, 128 * j:128 * j + 128] = jnp.where(lane < 64, o0, o1)
    for g in range(G_KV):
        ig = jnp.concatenate([imp_parts[bi][g] for bi in range(nb)], axis=0) if nb > 1 else imp_parts[0][g]
        imp = ig + pltpu.roll(ig, LANES - ODD_SLOT0, axis=1)
        score = jnp.where(useimp_ref[...] > 0.5, imp, sbase_ref[...])
        st = score.T[0:SEL_SLOTS]
        jj = lax.broadcasted_iota(jnp.int32, (SEL_SLOTS, rows), 0)
        rank = jnp.zeros((SEL_SLOTS, rows), F32)
        for jp in range(n_blocks):
            r = st[jp:jp + 1, :]
            before = (r > st) | ((r == st) & (jp < jj))
            rank = rank + before.astype(F32)
        selb = jnp.where(rank < TOP_N - 0.5, 0.0, NEG)
        full_t = jnp.concatenate([jnp.zeros((SEL_LANE0, rows), F32), selb,
                                  jnp.zeros((LANES - SEL_LANE0 - SEL_SLOTS, rows), F32)], axis=0)
        if q_t:
            for j in range(HPG):
                h = HPG * g + j
                q_refs[j][g] = ((qpad[h] + cvec_ref[h]).T + full_t).astype(BF16)
        else:
            extra = full_t.T
            for j in range(HPG):
                h = HPG * g + j
                q_refs[0][g, j] = (qpad[h] + extra + cvec_ref[h]).astype(q_refs[0].dtype)


def _cmpsel(qn, kvc, bias, sbase, useimp, indk, kg0, cvec, *, nb, tb, n_blocks, q_t):
    m = qn.shape[0]
    rows = nb * tb
    assert rows == LANES and m % rows == 0
    n_pos = bias.shape[0]
    full = lambda a: pl.BlockSpec(a.shape, lambda i: (0,) * a.ndim)
    if q_t:
        q_specs = [pl.BlockSpec((G_KV, LANES, rows), lambda i: (0, 0, i))] * HPG
        q_shapes = [jax.ShapeDtypeStruct((G_KV, LANES, m), BF16)] * HPG
    else:
        q_specs = [pl.BlockSpec((G_KV, HPG, rows, LANES), lambda i: (0, 0, i, 0))]
        q_shapes = [jax.ShapeDtypeStruct((G_KV, HPG, m, LANES), F32)]
    return pl.pallas_call(
        functools.partial(_cmpsel_body, nb=nb, tb=tb, n_blocks=n_blocks, q_t=q_t),
        grid=(m // rows,),
        in_specs=[pl.BlockSpec((rows, W_NSA), lambda i: (i, 0)),
                  pl.BlockSpec((nb, LANES, 256), (lambda i: (i, 0, 0)) if nb > 1 else (lambda i: (i // n_pos, 0, 0))),
                  pl.BlockSpec((None, G_KV, HPG * tb, LANES), lambda i: (i % n_pos, 0, 0, 0)),
                  pl.BlockSpec((None, rows, LANES), lambda i: (i % n_pos, 0, 0)),
                  pl.BlockSpec((None, rows, LANES), lambda i: (i % n_pos, 0, 0)),
                  full(indk), full(kg0), full(cvec)],
        out_specs=q_specs + [pl.BlockSpec((rows, W_NSA), lambda i: (i, 0))],
        out_shape=q_shapes + [jax.ShapeDtypeStruct((m, W_NSA), F32)],
        compiler_params=pltpu.CompilerParams(dimension_semantics=("arbitrary",), vmem_limit_bytes=VMEM_LIMIT),
        name="cmpsel",
    )(qn, kvc, bias, sbase, useimp, indk, kg0, cvec)


def _mix_gates(g_ref, bg_ref, expand_ref, gate_ref, ocmp_ref, o_slc, o_win, y_ref):
    gl = _sigmoid(g_ref[...] + bg_ref[...])
    gx = _dot_split(gl, expand_ref[...])
    for j in range(HPG):
        sl = slice(128 * j, 128 * j + 128)
        o = (gx[:, 128 * j:128 * j + 128] * ocmp_ref[:, sl]
             + gx[:, 128 * (HPG + j):128 * (HPG + j) + 128] * o_slc[j]
             + gx[:, 128 * (2 * HPG + j):128 * (2 * HPG + j) + 128] * o_win[j])
        y_ref[:, sl] = (o * _silu(gate_ref[:, sl])).astype(y_ref.dtype)


def _softmax_step(q_t, k, v_t, bias_t, m_scr, acc_scr):
    n_key, n_col = k.shape[0], q_t.shape[1]
    for k0 in range(0, n_key, KEY_CHUNK):
        ks = slice(k0, k0 + KEY_CHUNK)
        for c0 in range(0, n_col, COL_CHUNK):
            cs = slice(c0, c0 + COL_CHUNK)
            s = _dot(k[ks], q_t[:, cs])
            if bias_t is not None:
                s = s + bias_t(ks, cs)
            m_prev = m_scr[:, cs]
            m_new = jnp.maximum(m_prev, jnp.max(s, axis=0, keepdims=True))
            p = jnp.exp(s - m_new)
            acc_scr[:, cs] = jnp.exp(m_prev - m_new) * acc_scr[:, cs] + _dot(v_t[:, ks], p.astype(BF16))
            m_scr[:, cs] = m_new


def _flash_body(q0_ref, q1_ref, q2_ref, q3_ref, ksa_ref, vs_ref, kwa_ref, vw_ref, dt_ref, edge_ref, ocmp_ref,
                g_ref, gate_ref, bg_ref, expand_ref, y_ref, m_scr, acc_scr):
    i = pl.program_id(1)
    cols = HPG * TQ
    m_scr[...] = jnp.full(m_scr.shape, NEG, F32)
    acc_scr[...] = jnp.zeros(acc_scr.shape, F32)

    def run(n_near):
        for g in range(G_KV):
            q_t = jnp.concatenate([r[g] for r in (q0_ref, q1_ref, q2_ref, q3_ref)], axis=1)
            for br, (ka_ref, v_ref) in enumerate(((ksa_ref, vs_ref), (kwa_ref, vw_ref))):
                def step(kt, bias_t):
                    k0 = pl.multiple_of(kt * TQ, TQ)
                    _softmax_step(q_t, ka_ref[g, pl.ds(k0, TQ), :], v_ref[kt, g], bias_t, m_scr.at[br, g],
                                  acc_scr.at[br, g])

                if n_near == 2:
                    if br == 0:
                        def far_pair(kp, _):
                            step(2 * kp, None)
                            step(2 * kp + 1, None)
                            return 0

                        lax.fori_loop(0, lax.shift_right_logical(i - 1, 1), far_pair, 0)

                        @pl.when(lax.rem(i - 1, 2) == 1)
                        def _():
                            step(i - 2, None)
                    else:
                        step(i - 2, lambda ks, cs: edge_ref[ks, cs])
                if n_near >= 1:
                    step(i - 1, lambda ks, cs: dt_ref[1, g, ks, cs])
                step(i, lambda ks, cs: dt_ref[0, g, ks, cs])

    for n_near, cond in ((0, i == 0), (1, i == 1), (2, i >= 2)):
        pl.when(cond)(functools.partial(run, n_near))

    outs = []
    for br in range(2):
        o_t = jnp.concatenate([acc_scr[br, g, 0:HD, :] / acc_scr[br, g, HD:HD + 1, :] for g in range(G_KV)], axis=0)
        outs.append([o_t[:, j * TQ:(j + 1) * TQ].T for j in range(HPG)])
    _mix_gates(g_ref, bg_ref, expand_ref, gate_ref, ocmp_ref, outs[0], outs[1], y_ref)


def _flash(q_ts, ksa, vs, kwa, vw, dt, edge, ocmp, ng, ngate, bg, expand, *, n_b, t_len):
    m = n_b * t_len
    nq = t_len // TQ
    cols = HPG * TQ
    full = lambda a: pl.BlockSpec(a.shape, lambda b, i: (0,) * a.ndim)
    qspec = pl.BlockSpec((G_KV, LANES, TQ), lambda b, i: (0, 0, b * nq + i))
    kspec = pl.BlockSpec((G_KV, None, t_len, LANES), lambda b, i: (0, b, 0, 0))
    vspec = pl.BlockSpec((nq, G_KV, V_ROWS, TQ), lambda b, i: (b, 0, 0, 0))
    row = lambda n: pl.BlockSpec((TQ, n), lambda b, i: (b * nq + i, 0))
    return pl.pallas_call(
        _flash_body,
        grid=(n_b, nq),
        in_specs=[qspec] * HPG + [kspec, vspec, kspec, vspec, full(dt), full(edge), row(W_NSA), row(LANES),
                                  row(W_NSA), full(bg), full(expand)],
        out_specs=row(W_NSA), out_shape=jax.ShapeDtypeStruct((m, W_NSA), BF16),
        scratch_shapes=[pltpu.VMEM((2, G_KV, 1, cols), F32), pltpu.VMEM((2, G_KV, V_ROWS, cols), F32)],
        compiler_params=pltpu.CompilerParams(dimension_semantics=("arbitrary", "arbitrary"),
                                             vmem_limit_bytes=VMEM_LIMIT),
        name="flash",
    )(*q_ts, ksa.reshape(G_KV, n_b, t_len, LANES), vs, kwa.reshape(G_KV, n_b, t_len, LANES), vw,
      dt, edge, ocmp, ng, ngate, bg, expand)


def _dec_attn_body(pt_ref, *refs, n_pages, t_new, n_keys, n_wkeys):
    pages = refs[:DEC_NB * n_pages]
    (qa_ref, slc_ref, win_ref, wst_ref, dsl_ref, dwn_ref, ocmp_ref, g_ref, gate_ref, bg_ref, expand_ref, _wall,
     y_ref, wout_ref, ka_scr, va_scr, kw_scr, vw_scr) = refs[DEC_NB * n_pages:]
    past = n_pages * PAGE_SIZE
    w_past = wst_ref.shape[-1]
    rows = HPG * t_new

    @pl.when(pl.program_id(0) == 0)
    def _():
        for k_scr, n, with_blocks in ((ka_scr, n_keys, True), (kw_scr, n_wkeys, False)):
            feat = lax.broadcasted_iota(jnp.int32, (LANES, n), 0)
            key = lax.broadcasted_iota(jnp.int32, (LANES, n), 1)
            aug = (feat == CONST_LANE) | (feat == CONST_LANE + 1)
            if with_blocks:
                aug = aug | ((feat - SEL_LANE0) == lax.shift_right_logical(key, 6))
            for bi in range(DEC_NB):
                for g in range(G_KV):
                    k_scr[bi, g] = aug.astype(F32).astype(BF16)

    def new_keys_t(new):
        pad = jnp.concatenate([new, jnp.zeros((LANES - t_new, 256), F32)], axis=0)
        return pad[:, 0:128].T, pad[:, 128:256].T

    lane = _lane((HD, LANES))
    lane_o = _lane((t_new, LANES))
    outs = [[[None] * DEC_NB for _ in range(HPG)] for _ in range(2)]
    for bi in range(DEC_NB):
        tok = slice(bi * t_new, (bi + 1) * t_new)
        for p in range(n_pages):
            sl = slice(p * PAGE_SIZE, (p + 1) * PAGE_SIZE)
            for g in range(G_KV):
                ka_scr[bi, g, 0:HD, sl] = pages[bi * n_pages + p][0, g].astype(BF16)
                va_scr[bi, HD * g:HD * (g + 1), sl] = pages[bi * n_pages + p][1, g].astype(BF16)
        kn_t, vn_t = new_keys_t(slc_ref[tok, :])
        va_scr[bi, :, past:n_keys] = vn_t.astype(BF16)
        kwn_t, vwn_t = new_keys_t(win_ref[tok, :])
        vw_scr[bi, :, w_past:n_wkeys] = vwn_t.astype(BF16)
        for g in range(G_KV):
            ka_scr[bi, g, 0:HD, past:n_keys] = kn_t[HD * g:HD * (g + 1)].astype(BF16)
            kw_scr[bi, g, 0:HD, 0:w_past] = wst_ref[bi, 0, g].astype(BF16)
            kw_scr[bi, g, 0:HD, w_past:n_wkeys] = kwn_t[HD * g:HD * (g + 1)].astype(BF16)
            vw_scr[bi, HD * g:HD * (g + 1), 0:w_past] = wst_ref[bi, 1, g].astype(BF16)
            for s, new_t in ((0, kwn_t), (1, vwn_t)):
                sh = pltpu.roll(wst_ref[bi, s, g], w_past - t_new, axis=1)
                tail = pltpu.roll(new_t[HD * g:HD * (g + 1)], LANES - t_new, axis=1)
                wout_ref[bi, s, g, :, 0:w_past - LANES] = sh[:, 0:w_past - LANES]
                wout_ref[bi, s, g, :, w_past - LANES:w_past] = jnp.where(lane >= LANES - t_new, tail,
                                                                        sh[:, w_past - LANES:])
        res = [[None, None], [None, None]]
        for g in range(G_KV):
            q = qa_ref[g, :, tok, :].reshape(rows, LANES).astype(BF16)
            for br, (k_scr, v_scr, b_ref) in enumerate(((ka_scr, va_scr, dsl_ref), (kw_scr, vw_scr, dwn_ref))):
                s = _dot(q, k_scr[bi, g]) + b_ref[g]
                mx = jnp.max(s, axis=-1, keepdims=True)
                p = jnp.exp(s - mx)
                den = jnp.sum(p, axis=-1, keepdims=True)
                res[br][g] = _dot_nt(p.astype(BF16), v_scr[bi]) / den
        for br in range(2):
            for j in range(HPG):
                outs[br][j][bi] = jnp.where(lane_o < 64, res[br][0][j * t_new:(j + 1) * t_new],
                                            res[br][1][j * t_new:(j + 1) * t_new])
    outs = [[jnp.concatenate(outs[br][j], axis=0) for j in range(HPG)] for br in range(2)]
    _mix_gates(g_ref, bg_ref, expand_ref, gate_ref, ocmp_ref, outs[0], outs[1], y_ref)


def _dec_attn(page_table, cache, layer, qa, slc_new, win_new, win_state, dsl, dwn, ocmp, ng, ngate, bg, expand,
              win_all, *, t_new):
    n_b, n_pages = page_table.shape
    assert n_b % DEC_NB == 0
    m = n_b * t_new
    w_past = win_state.shape[-1]
    n_keys = dsl.shape[-1]
    n_wkeys = dwn.shape[-1]
    wblock = (2, G_KV, HD, w_past)
    full = lambda a: pl.BlockSpec(a.shape, lambda b, pt: (0,) * a.ndim)
    row = lambda n: pl.BlockSpec((DEC_NB * t_new, n), lambda b, pt: (b, 0))
    wspec = pl.BlockSpec((None, DEC_NB) + wblock, lambda b, pt: (layer, b, 0, 0, 0, 0))
    grid_spec = pltpu.PrefetchScalarGridSpec(
        num_scalar_prefetch=1, grid=(n_b // DEC_NB,),
        in_specs=_paged_specs(layer, n_pages, PAGE_BLOCK) + [
            pl.BlockSpec((G_KV, HPG, DEC_NB * t_new, LANES), lambda b, pt: (0, 0, b, 0)),
            row(256), row(256), wspec,
            full(dsl), full(dwn), row(W_NSA), row(LANES), row(W_NSA), full(bg), full(expand),
            pl.BlockSpec(memory_space=pl.ANY)],
        out_specs=[row(W_NSA), wspec],
        scratch_shapes=[pltpu.VMEM((DEC_NB, G_KV, LANES, n_keys), BF16), pltpu.VMEM((DEC_NB, LANES, n_keys), BF16),
                        pltpu.VMEM((DEC_NB, G_KV, LANES, n_wkeys), BF16), pltpu.VMEM((DEC_NB, LANES, n_wkeys), BF16)])
    args = [page_table, *([cache] * (DEC_NB * n_pages)), qa, slc_new, win_new, win_state, dsl, dwn, ocmp, ng, ngate,
            bg, expand, win_all]
    return pl.pallas_call(
        functools.partial(_dec_attn_body, n_pages=n_pages, t_new=t_new, n_keys=n_keys, n_wkeys=n_wkeys),
        grid_spec=grid_spec,
        out_shape=[jax.ShapeDtypeStruct((m, W_NSA), F32), jax.ShapeDtypeStruct(win_all.shape, F32)],
        input_output_aliases={len(args) - 1: 1},
        compiler_params=pltpu.CompilerParams(dimension_semantics=("arbitrary",), vmem_limit_bytes=VMEM_LIMIT),
        name="dec_attn",
    )(*args)


def _bucket_matrix(qpos, kpos, valid):
    dist = qpos[:, None] - kpos[None, :]
    return np.where(valid & (dist >= 0), _rel_bucket_np(dist), -1).astype(np.int32)


def _bias_body(tab_ref, bk_ref, o_ref, *, sub_far):
    bk = bk_ref[...]
    for h in range(H_NSA):
        far = tab_ref[N_BUCKETS - 1, h] if sub_far else 0.0
        acc = jnp.full(bk.shape, NEG, F32)
        for b in range(N_BUCKETS):
            acc = jnp.where(bk == b, tab_ref[b, h] - far, acc)
        o_ref[h] = acc


def _bias_table(table, bk, *, sub_far):
    r, c = bk.shape
    tr = 8 if r <= 8 else 64
    assert r % tr == 0
    return pl.pallas_call(
        functools.partial(_bias_body, sub_far=sub_far),
        grid=(r // tr,),
        in_specs=[pl.BlockSpec(memory_space=pltpu.SMEM), pl.BlockSpec((tr, c), lambda i: (i, 0))],
        out_specs=pl.BlockSpec((H_NSA, tr, c), lambda i: (0, i, 0)),
        out_shape=jax.ShapeDtypeStruct((H_NSA, r, c), F32),
        compiler_params=pltpu.CompilerParams(dimension_semantics=("arbitrary",)),
        name="bias_table",
    )(table, jnp.asarray(bk))


def _cmp_tables(qpos_blocks, n_blocks):
    slot = np.arange(LANES)
    blk = np.where(slot < ODD_SLOT0, 2 * slot, 2 * (slot - ODD_SLOT0) + 1)
    real = (blk < 2 * n_blocks) & ((slot % ODD_SLOT0) < n_blocks)
    blk_end = blk * CMP_BLOCK + CMP_BLOCK - 1
    buckets, sbases, useimps = [], [], []
    for qpos in qpos_blocks:
        buckets.append(_bucket_matrix(qpos, blk_end, real[None, :]))
        j = np.arange(LANES)[None, :]
        cur = (qpos // SEL_BLOCK)[:, None]
        forced = (j == 0) | (j == cur) | (j == cur - 1)
        valid = j * SEL_BLOCK <= qpos[:, None]
        inrange = j < n_blocks
        sbases.append(np.where(~inrange, -2.0, np.where(forced, FORCE, -1.0)).astype(np.float32))
        useimps.append((inrange & ~forced & valid).astype(np.float32))
    return np.concatenate(buckets), jnp.asarray(np.stack(sbases)), jnp.asarray(np.stack(useimps))


def _head_rows(b, t):
    k = b.shape[-1]
    n = b.shape[1] // t
    return b.reshape(G_KV, HPG, n, t, k).transpose(2, 0, 1, 3, 4).reshape(n, G_KV, HPG * t, k)


def _prep_weights(rel_bias, w_in, conv_w_pw, gla_w_gk, nsa_q_norm_g, nsa_k_norm_g, nsa_pe_cmp, nsa_w_cmp,
                  nsa_b_gate, w_out):
    depth = w_in.shape[0]
    cols = _in_proj_columns()
    pieces, start = [], 0
    for c in range(1, N_IN + 1):
        pad = cols[start] < 0
        if c == N_IN or (cols[c] >= 0 if pad else cols[c] != cols[c - 1] + 1):
            src = int(cols[start])
            pieces.append(jnp.zeros((depth, D_MODEL, c - start), w_in.dtype) if pad else w_in[:, :, src:src + c - start])
            start = c
    w_in_p = jnp.concatenate(pieces, axis=2).astype(BF16)
    perm = 512 + _nsa_perm()
    w_out_p = jnp.concatenate([w_out[:, 0:512]] + [w_out[:, int(perm[c]):int(perm[c]) + HD]
                                                   for c in range(0, W_NSA, HD)], axis=1).astype(BF16)
    qg = jnp.tile(nsa_q_norm_g, (1, H_NSA))[:, None, :] * SCALE
    kg = jnp.tile(nsa_k_norm_g, (1, 1, G_KV))
    wgk = jnp.zeros((depth, LANES, LANES), F32).at[:, :GATE_RANK, :].set(gla_w_gk)
    wgk_hi = wgk.astype(BF16)
    wgk2 = jnp.stack([wgk_hi, (wgk - wgk_hi.astype(F32)).astype(BF16)], axis=1)
    def blockdiag2(a, b):
        za = jnp.zeros(a.shape[:-1] + (b.shape[-1],), a.dtype)
        zb = jnp.zeros(b.shape[:-1] + (a.shape[-1],), b.dtype)
        return jnp.concatenate([jnp.concatenate([a, za], axis=-1), jnp.concatenate([zb, b], axis=-1)], axis=-2)

    wc = nsa_w_cmp.astype(BF16)
    wtap = blockdiag2(wc, wc)
    wbd = blockdiag2(wtap[:, 0], wtap[:, 1]).reshape(depth, KW, 256)
    pe = jnp.tile(nsa_pe_cmp.transpose(0, 2, 1, 3)[:, :, :, None, :], (1, 1, 1, G_KV, 1))
    petap = pe.transpose(0, 2, 1, 3, 4).reshape(depth, 2, CMP_BLOCK, LANES)
    pe = pe.reshape(depth, 1, KW)
    bg = jnp.zeros((depth, 1, LANES), F32).at[:, 0, :3 * H_NSA].set(nsa_b_gate)
    expand = np.zeros((LANES, 3 * HPG * LANES), np.float32)
    for br in range(3):
        for g in range(G_KV):
            for j in range(HPG):
                c = LANES * (HPG * br + j) + 64 * g
                expand[H_NSA * br + HPG * g + j, c:c + 64] = 1.0
    far = rel_bias[N_BUCKETS - 1]
    far_hi = far.astype(BF16).astype(F32)
    cvec = jnp.zeros((H_NSA, 1, LANES), F32).at[:, 0, CONST_LANE].set(far_hi).at[:, 0, CONST_LANE + 1].set(far - far_hi)
    return dict(w_in=w_in_p, w_out=w_out_p, qg=qg, kg=kg, wgk=wgk2, wbd=wbd, pe=pe, bg=bg,
                wtap=wtap, petap=petap,
                expand=jnp.asarray(expand, BF16), cvec=cvec, wpw=conv_w_pw.astype(BF16),
                indq=jnp.asarray(_blockdiag_ones(W_NSA, HD), BF16), indk=jnp.asarray(_blockdiag_ones(LANES, HD), BF16))


def kernel(x_prompt, x_sample, cache_cmp_kv, cache_slc_kv, page_table, state_win_kv, state_gla, state_conv, rel_bias, norm_g, w_in, conv_w_dw, conv_b_dw, conv_ln_g, conv_ln_b, conv_w_pw, conv_b_pw, gla_w_gk, gla_b_gk, gla_norm_g, nsa_q_norm_g, nsa_k_norm_g, nsa_pe_cmp, nsa_w_cmp, nsa_b_gate, w_out, norm_f):
    depth = w_in.shape[0]
    n_bp, t_p, _ = x_prompt.shape
    n_bs, t_s, _ = x_sample.shape
    n_pages = page_table.shape[1]
    past = n_pages * PAGE_SIZE
    w_past = state_win_kv.shape[2]
    assert t_p % TQ == 0 and WINDOW == 2 * TQ and t_s == 8 and w_past == WINDOW and past % SEL_BLOCK == 0
    wp = _prep_weights(rel_bias, w_in, conv_w_pw, gla_w_gk, nsa_q_norm_g, nsa_k_norm_g, nsa_pe_cmp, nsa_w_cmp,
                       nsa_b_gate, w_out)
    table = rel_bias
    nblk_p = t_p // SEL_BLOCK
    nblk_s = (past + SEL_BLOCK) // SEL_BLOCK
    pos_p = [np.arange(i * LANES, (i + 1) * LANES) for i in range(t_p // LANES)]
    bk_p, sbase_p, useimp_p = _cmp_tables(pos_p, nblk_p)
    bias_p = _head_rows(_bias_table(table, bk_p, sub_far=False), LANES)
    nb_s = LANES // t_s
    qpos_s = past + np.arange(t_s)
    bk_s, _, _ = _cmp_tables([qpos_s], nblk_s)
    bias_s = _head_rows(_bias_table(table, bk_s, sub_far=False), t_s)
    _, sbase_1, useimp_1 = _cmp_tables([np.tile(qpos_s, nb_s)], nblk_s)
    tq = np.arange(TQ)
    every = np.ones((1, 1), bool)
    bk_dt = np.concatenate([_bucket_matrix(tq, tq, every).T, _bucket_matrix(TQ + tq, tq, every).T])
    dt = _bias_table(table, bk_dt, sub_far=True)
    dt = dt.reshape(G_KV, HPG, 2, TQ, TQ).transpose(2, 0, 3, 1, 4).reshape(2, G_KV, TQ, HPG * TQ)
    edge = jnp.asarray(np.tile(np.where(tq[:, None] > tq[None, :], 0.0, NEG).astype(np.float32), (1, HPG)))
    n_keys = past + LANES
    kpos = np.arange(n_keys)
    dsl = _head_rows(_bias_table(table, _bucket_matrix(qpos_s, kpos, (kpos < past + t_s)[None, :]),
                                 sub_far=True), t_s)[0]
    n_wkeys = w_past + LANES
    wk = np.arange(n_wkeys)
    wpos = past - w_past + wk
    wvalid = (wk < w_past + t_s)[None, :] & ((qpos_s[:, None] - wpos[None, :]) < WINDOW) & (wpos >= 0)[None, :]
    dwn = _head_rows(_bias_table(table, _bucket_matrix(qpos_s, wpos, wvalid), sub_far=True), t_s)[0]

    cache_cmp = cache_cmp_kv.transpose(0, 1, 3, 4, 5, 2)
    cache_slc = cache_slc_kv.transpose(0, 1, 3, 4, 5, 2)
    win_state = state_win_kv.transpose(0, 1, 3, 4, 5, 2)
    s0 = state_gla.transpose(0, 1, 2, 4, 3).reshape(depth, n_bs, W_GLA, DK_GLA)
    i256 = np.arange(W_GLA)
    i128 = np.arange(LANES)
    bdmask = jnp.asarray((i256[:, None] // DV_GLA == i128[None, :] // DK_GLA).astype(np.float32))
    s0 = jnp.tile(s0, (1, 1, 1, H_GLA)) * bdmask
    zero_s0 = jnp.zeros((n_bp, W_GLA, LANES), F32)
    zero_hist = jnp.zeros((n_bp, HIST, W_CONV), F32)

    def unpack_state(st):
        b = st.shape[0]
        blocks = [st[:, DV_GLA * h:DV_GLA * (h + 1), DK_GLA * h:DK_GLA * (h + 1)] for h in range(H_GLA)]
        return jnp.stack(blocks, axis=1).transpose(0, 1, 3, 2)

    xp = x_prompt.reshape(n_bp * t_p, D_MODEL)
    xs = x_sample.reshape(n_bs * t_s, D_MODEL)
    outs_p, outs_s = [], []
    leaf_shape = (depth, n_bp, 2, G_KV, HD, t_p)
    chain_p, win_all = (jnp.zeros(leaf_shape, F32), jnp.zeros(leaf_shape, F32)), jnp.zeros(win_state.shape, F32)
    for l in range(depth):
        final = l == depth - 1
        row = lambda a: a[l][None, :]
        common = dict(wdw=conv_w_dw[l], bdw=row(conv_b_dw), lng=row(conv_ln_g), lnb=row(conv_ln_b), wpw=wp["wpw"][l],
                      bpw=row(conv_b_pw))
        bgk = jnp.zeros((1, LANES), F32).at[0, :].set(gla_b_gk[l])
        gng = jnp.tile(gla_norm_g[l], (H_GLA,))[None, :]
        kg12 = wp["kg"][l, 1:3]
        kg0 = wp["kg"][l, 0:1]
        zc, zg, qn, cmp_n, _, win_n, ngate, ng, ksa, vs, kwa, vw, cmp_all, slc_all = _in_proj(
            xp, row(norm_g), wp["w_in"][l], wp["indq"], wp["indk"], wp["qg"][l], kg12, seq_len=t_p, prompt=True,
            layer=l, depth=depth, chain=chain_p)
        chain_p = (cmp_all, slc_all)
        yc, conv_st = _conv(zc, zero_hist, nb=1, t_len=t_p, **common)
        yg, gla_st = _gla(zg, zero_s0, wp["wgk"][l], bgk, gng, nb=1, t_len=t_p, chunk=16)
        kvc = _compress_prompt(cmp_n, wp["wbd"][l], wp["pe"][l], n_b=n_bp, t_len=t_p)
        *q_ts, ocmp = _cmpsel(qn, kvc, bias_p, sbase_p, useimp_p, wp["indk"], kg0, wp["cvec"], nb=1, tb=LANES,
                              n_blocks=nblk_p, q_t=True)
        yn = _flash(q_ts, ksa, vs, kwa, vw, dt, edge, ocmp, ng, ngate, wp["bg"][l], wp["expand"], n_b=n_bp, t_len=t_p)
        xp = _out_proj(xp, yc, yg, yn, wp["w_out"][l], norm_f[None, :], final=final)
        w_keep = min(WINDOW, t_p)
        outs_p.append((conv_st, unpack_state(gla_st), win_n.reshape(n_bp, t_p, 256)[:, t_p - w_keep:]))
        zc, zg, qn, cmp_n, slc_n, win_n, ngate, ng = _in_proj(
            xs, row(norm_g), wp["w_in"][l], wp["indq"], wp["indk"], wp["qg"][l], kg12, seq_len=t_s, prompt=False)
        yc, conv_st = _conv(zc, state_conv[l], nb=nb_s, t_len=t_s, **common)
        yg, gla_st = _gla(zg, s0[l], wp["wgk"][l], bgk, gng, nb=nb_s, t_len=t_s, chunk=t_s)
        kvc = _compress_paged(page_table, cache_cmp, l, cmp_n, wp["wtap"][l], wp["petap"][l], t_new=t_s)
        qa, ocmp = _cmpsel(qn, kvc, bias_s, sbase_1, useimp_1, wp["indk"], kg0, wp["cvec"], nb=nb_s, tb=t_s,
                           n_blocks=nblk_s, q_t=False)
        yn, win_all = _dec_attn(page_table, cache_slc, l, qa, slc_n, win_n, win_state, dsl, dwn, ocmp, ng, ngate,
                                wp["bg"][l], wp["expand"], win_all, t_new=t_s)
        xs = _out_proj(xs, yc, yg, yn, wp["w_out"][l], norm_f[None, :], final=final)
        outs_s.append((conv_st, unpack_state(gla_st), None, cmp_n, slc_n))

    def stack(outs, k, shape):
        return jnp.stack([o[k] for o in outs]).reshape(shape)

    kv = (2, G_KV, HD)
    return (xp.reshape(n_bp, t_p, D_MODEL), xs.reshape(n_bs, t_s, D_MODEL),
            stack(outs_p, 0, (depth, n_bp, HIST, W_CONV)), stack(outs_s, 0, (depth, n_bs, HIST, W_CONV)),
            stack(outs_p, 1, (depth, n_bp, H_GLA, DK_GLA, DV_GLA)), stack(outs_s, 1, (depth, n_bs, H_GLA, DK_GLA, DV_GLA)),
            stack(outs_p, 2, (depth, n_bp, min(WINDOW, t_p)) + kv), win_all.transpose(0, 1, 5, 2, 3, 4),
            chain_p[0].transpose(0, 1, 5, 2, 3, 4), stack(outs_s, 3, (depth, n_bs, t_s) + kv),
            chain_p[1].transpose(0, 1, 5, 2, 3, 4), stack(outs_s, 4, (depth, n_bs, t_s) + kv))
```

```python
import functools
import math

import numpy as np
import jax
import jax.numpy as jnp
from jax import lax
from jax.experimental import pallas as pl
from jax.experimental.pallas import tpu as pltpu

F32 = jnp.float32
BF16 = jnp.bfloat16

D_MODEL = 1024
W_CONV = 256
CONV_WIDTH = 31
HIST = CONV_WIDTH - 1
H_GLA = 4
DK_GLA = 32
DV_GLA = 64
W_GLA = H_GLA * DV_GLA
GATE_RANK = 16
GATE_NORMALIZER = 16.0
H_NSA = 8
HD = 64
G_KV = 2
HPG = H_NSA // G_KV
W_NSA = H_NSA * HD
CMP_BLOCK = 32
SEL_BLOCK = 64
TOP_N = 8
WINDOW = 512
N_BUCKETS = 32
MAX_EXACT = N_BUCKETS // 2
MAX_DISTANCE = 128
PAGE_SIZE = 128
SCALE = HD ** -0.5
EPS = 1e-6
NEG = -1e30
FORCE = 1e4

LANES = 128
SEL_LANE0 = 64
SEL_SLOTS = 40
CONST_LANE = 112
ODD_SLOT0 = 64
TQ = 256
V_ROWS = 128
KEY_CHUNK = 256
COL_CHUNK = 128
VMEM_LIMIT = 56 * 1024 * 1024

C_CONV = 0
C_GLA = 3 * W_CONV
N_GLA = 2 * H_GLA * DK_GLA + 2 * W_GLA + LANES
C_NSA = C_GLA + N_GLA
N_NSA = W_NSA + 3 * 256 + W_NSA + LANES
N_IN = C_NSA + N_NSA


def _rel_bucket_np(dist):
    n = np.maximum(dist, 0)
    nf = np.maximum(n, 1).astype(np.float32)
    large = MAX_EXACT + (np.log(nf / np.float32(MAX_EXACT)) / np.float32(math.log(MAX_DISTANCE / MAX_EXACT))
                         * np.float32(N_BUCKETS - MAX_EXACT)).astype(np.int32)
    large = np.minimum(large, N_BUCKETS - 1)
    return np.where(n < MAX_EXACT, n, large).astype(np.int32)


def _in_proj_columns():
    o = {}
    off = 0
    for name, w in (("c_a", 256), ("c_b", 256), ("c_gate", 256), ("l_q", 128), ("l_k", 128), ("l_v", 256),
                    ("l_gk", 16), ("l_gate", 256), ("n_q", 512), ("n_cmp", 256), ("n_slc", 256),
                    ("n_win", 256), ("n_g", 24), ("n_gate", 512)):
        o[name] = off
        off += w
    cols = -np.ones((N_IN,), np.int64)

    def put(dst, name, width):
        cols[dst:dst + width] = o[name] + np.arange(width)

    put(0, "c_a", 256); put(256, "c_b", 256); put(512, "c_gate", 256)
    g = C_GLA
    put(g, "l_q", 128); put(g + 128, "l_k", 128); put(g + 256, "l_v", 256); put(g + 512, "l_gate", 256)
    put(g + 768, "l_gk", 16)
    n = C_NSA
    put(n, "n_q", 512); put(n + 512, "n_cmp", 256); put(n + 768, "n_slc", 256); put(n + 1024, "n_win", 256)
    cols[n + 1280:n + 1792] = o["n_gate"] + _nsa_perm()
    put(n + 1792, "n_g", 24)
    return cols


def _nsa_perm():
    p = np.zeros((W_NSA,), np.int64)
    for j in range(HPG):
        for g in range(G_KV):
            p[128 * j + 64 * g:128 * j + 64 * g + 64] = 64 * (HPG * g + j) + np.arange(64)
    return p


def _blockdiag_ones(n, blk):
    i = np.arange(n)
    return (i[:, None] // blk == i[None, :] // blk).astype(np.float32)


def _split2(x):
    hi = x.astype(BF16)
    lo = (x - hi.astype(F32)).astype(BF16)
    return hi, lo


def _dot(a, b):
    return jnp.dot(a, b, preferred_element_type=F32)


def _dot_nt(a, b):
    return lax.dot_general(a, b, (((1,), (1,)), ((), ())), preferred_element_type=F32)


def _dot_tn(a, b):
    return lax.dot_general(a, b, (((0,), (0,)), ((), ())), preferred_element_type=F32)


def _dot_split(x, m_bf16):
    hi, lo = _split2(x)
    return _dot(hi, m_bf16) + _dot(lo, m_bf16)


def _seg_rms(x, ind_ref, seg):
    return lax.rsqrt(_dot_split(x * x, ind_ref[...]) * (1.0 / seg) + EPS)


def _sigmoid(x):
    return 1.0 / (1.0 + jnp.exp(-x))


def _silu(x):
    return x * _sigmoid(x)


def _lane(shape):
    return lax.broadcasted_iota(jnp.int32, shape, len(shape) - 1)


def _half_pad(x, odd):
    if odd:
        x = pltpu.roll(x, 64, axis=1)
    return jnp.where(_lane(x.shape) < 64, x, 0.0)


def _inproj_body(x_ref, ng_ref, w_ref, indq_ref, indk_ref, qg_ref, kg_ref, *outs, tm, seq_len, prompt, n_chained):
    outs = outs[n_chained:]
    zc_ref, zg_ref, qn_ref, cmp_ref, slc_ref, win_ref, gate_ref, g_ref = outs[:8]
    x = x_ref[...]
    ms = jnp.mean(x * x, axis=-1, keepdims=True)
    h = (x * lax.rsqrt(ms + EPS) * ng_ref[...]).astype(BF16)

    def mm(lo, width):
        return _dot(h, w_ref[:, lo:lo + width])

    zc_ref[...] = mm(C_CONV, 3 * W_CONV)
    zg_ref[...] = mm(C_GLA, N_GLA)
    q = mm(C_NSA, W_NSA)
    qn_ref[...] = (q * _seg_rms(q, indq_ref, HD) * qg_ref[...]).astype(BF16)
    cmp = mm(C_NSA + 512, 256)
    cmp_ref[...] = cmp
    slc = mm(C_NSA + 768, 256)
    win = mm(C_NSA + 1024, 256)
    ks = slc[:, 0:128]
    ks = ks * _seg_rms(ks, indk_ref, HD) * kg_ref[0:1, :]
    kw = win[:, 0:128]
    kw = kw * _seg_rms(kw, indk_ref, HD) * kg_ref[1:2, :]
    slc_ref[:, 0:128] = ks
    slc_ref[:, 128:256] = slc[:, 128:256]
    win_ref[:, 0:128] = kw
    win_ref[:, 128:256] = win[:, 128:256]
    gate_ref[...] = mm(C_NSA + 1280, W_NSA)
    g_ref[...] = mm(C_NSA + 1792, LANES)
    if prompt:
        ksa_ref, vs_ref, kwa_ref, vw_ref, cmpt_ref, slct_ref = outs[8:]
        cmpt_ref[...] = cmp.T.reshape(2, G_KV, HD, tm)
        slct_ref[...] = jnp.concatenate([ks, slc[:, 128:256]], axis=1).T.reshape(2, G_KV, HD, tm)
        lane = _lane((tm, LANES))
        row = lax.broadcasted_iota(jnp.int32, (tm, LANES), 0)
        t = lax.rem(pl.program_id(0) * tm + row, seq_len)
        ones = ((lane == CONST_LANE) | (lane == CONST_LANE + 1)).astype(F32)
        onehot = ((lane - SEL_LANE0) == lax.shift_right_logical(t, 6)).astype(F32)
        for g in range(G_KV):
            ksa_ref[g] = (_half_pad(ks, g == 1) + onehot + ones).astype(BF16)
            kwa_ref[g] = (_half_pad(kw, g == 1) + ones).astype(BF16)
        feat = lax.broadcasted_iota(jnp.int32, (LANES, tm), 0)
        for v_ref, src in ((vs_ref, slc), (vw_ref, win)):
            vt = src[:, 128:256].T
            for g in range(G_KV):
                own = vt if g == 0 else pltpu.roll(vt, HD, axis=0)
                own = jnp.where(feat < HD, own, (feat == HD).astype(F32))[0:V_ROWS].astype(BF16)
                for kt in range(tm // TQ):
                    v_ref[kt, g] = own[:, kt * TQ:(kt + 1) * TQ]


def _in_proj(x2d, norm_g, w, indq, indk, qg, kg, *, seq_len, prompt, layer=0, depth=1, chain=()):
    m = x2d.shape[0]
    tm = min(512, m)
    assert m % tm == 0 and (not prompt or (seq_len % tm == 0 and tm % TQ == 0))
    row = lambda n: pl.BlockSpec((tm, n), lambda i: (i, 0))
    full = lambda a: pl.BlockSpec(a.shape, lambda i: (0,) * a.ndim)
    out_shape = [jax.ShapeDtypeStruct((m, 3 * W_CONV), F32), jax.ShapeDtypeStruct((m, N_GLA), F32),
                 jax.ShapeDtypeStruct((m, W_NSA), BF16), jax.ShapeDtypeStruct((m, 256), F32),
                 jax.ShapeDtypeStruct((m, 256), F32), jax.ShapeDtypeStruct((m, 256), F32),
                 jax.ShapeDtypeStruct((m, W_NSA), F32), jax.ShapeDtypeStruct((m, LANES), F32)]
    out_specs = [row(3 * W_CONV), row(N_GLA), row(W_NSA), row(256), row(256), row(256), row(W_NSA), row(LANES)]
    if prompt:
        aug = pl.BlockSpec((G_KV, tm, LANES), lambda i: (0, i, 0))
        vt = pl.BlockSpec((tm // TQ, G_KV, V_ROWS, TQ), lambda i: (i, 0, 0, 0))
        vt_shape = jax.ShapeDtypeStruct((m // TQ, G_KV, V_ROWS, TQ), BF16)
        per_seq = seq_len // tm
        leaf = pl.BlockSpec((None, None, 2, G_KV, HD, tm), lambda i: (layer, i // per_seq, 0, 0, 0, i % per_seq))
        leaf_shape = jax.ShapeDtypeStruct((depth, m // seq_len, 2, G_KV, HD, seq_len), F32)
        out_shape += [jax.ShapeDtypeStruct((G_KV, m, LANES), BF16), vt_shape,
                      jax.ShapeDtypeStruct((G_KV, m, LANES), BF16), vt_shape, leaf_shape, leaf_shape]
        out_specs += [aug, vt, aug, vt, leaf, leaf]
    args = [x2d, norm_g, w, indq, indk, qg, kg]
    return pl.pallas_call(
        functools.partial(_inproj_body, tm=tm, seq_len=seq_len, prompt=prompt, n_chained=len(chain)),
        grid=(m // tm,),
        in_specs=[row(D_MODEL), full(norm_g), full(w), full(indq), full(indk), full(qg), full(kg)]
        + [pl.BlockSpec(memory_space=pl.ANY)] * len(chain),
        out_specs=out_specs, out_shape=out_shape,
        input_output_aliases={len(args) + k: 12 + k for k in range(len(chain))},
        compiler_params=pltpu.CompilerParams(dimension_semantics=("arbitrary",), vmem_limit_bytes=VMEM_LIMIT),
        name="in_proj",
    )(*args, *chain)


def _outproj_body(x_ref, yc_ref, yg_ref, yn_ref, w_ref, nf_ref, o_ref, *, final):
    x = x_ref[...]
    x = (x + _dot(yc_ref[...], w_ref[0:256, :]) + _dot(yg_ref[...], w_ref[256:512, :])
         + _dot(yn_ref[...].astype(BF16), w_ref[512:1024, :]))
    if final:
        ms = jnp.mean(x * x, axis=-1, keepdims=True)
        x = x * lax.rsqrt(ms + EPS) * nf_ref[...]
    o_ref[...] = x


def _out_proj(x2d, yc, yg, yn, w, norm_f, *, final):
    m = x2d.shape[0]
    tm = min(512, m)
    assert m % tm == 0
    row = lambda n: pl.BlockSpec((tm, n), lambda i: (i, 0))
    full = lambda a: pl.BlockSpec(a.shape, lambda i: (0,) * a.ndim)
    return pl.pallas_call(
        functools.partial(_outproj_body, final=final),
        grid=(m // tm,),
        in_specs=[row(D_MODEL), row(256), row(256), row(512), full(w), full(norm_f)],
        out_specs=row(D_MODEL), out_shape=jax.ShapeDtypeStruct((m, D_MODEL), F32),
        compiler_params=pltpu.CompilerParams(dimension_semantics=("arbitrary",), vmem_limit_bytes=VMEM_LIMIT),
        name="out_proj",
    )(x2d, yc, yg, yn, w, norm_f)


def _conv_body(zc_ref, hist_ref, wdw_ref, bdw_ref, lng_ref, lnb_ref, wpw_ref, bpw_ref, y_ref, st_ref,
               ext_scr, act_scr, *, nb, t_len, tc, tmm):
    n_chunks = t_len // tc

    def per_batch(bi, _):
        base = bi * t_len
        ext_scr[0:32, :] = jnp.zeros((32, W_CONV), F32)
        ext_scr[2:32, :] = hist_ref[bi]

        def chunk(c, _):
            r0 = pl.multiple_of(c * tc, tc)
            g0 = pl.multiple_of(base + r0, tc)
            a = zc_ref[pl.ds(g0, tc), 0:256]
            b = zc_ref[pl.ds(g0, tc), 256:512]
            ext_scr[pl.ds(32 + r0, tc), :] = a * _sigmoid(b)
            win = ext_scr[pl.ds(r0, tc + 32), :]
            acc = jnp.zeros((tc, W_CONV), F32) + bdw_ref[...]
            for r in range(8):
                wr = pltpu.roll(win, tc + 32 - (2 + r), axis=0)
                for mi, j in enumerate(range(r, CONV_WIDTH, 8)):
                    acc = acc + wr[8 * mi:8 * mi + tc] * wdw_ref[j:j + 1, :]
            mu = jnp.mean(acc, axis=-1, keepdims=True)
            d = acc - mu
            var = jnp.mean(d * d, axis=-1, keepdims=True)
            yn = d * lax.rsqrt(var + EPS) * lng_ref[...] + lnb_ref[...]
            act_scr[pl.ds(g0, tc), :] = _silu(yn)
            return 0

        lax.fori_loop(0, n_chunks, chunk, 0)
        st_ref[bi] = ext_scr[2 + t_len:32 + t_len, :]
        return 0

    lax.fori_loop(0, nb, per_batch, 0)

    def mm(c, _):
        r0 = pl.multiple_of(c * tmm, tmm)
        y = _dot(act_scr[pl.ds(r0, tmm), :].astype(BF16), wpw_ref[...]) + bpw_ref[...]
        y_ref[pl.ds(r0, tmm), :] = (y * _silu(zc_ref[pl.ds(r0, tmm), 512:768])).astype(BF16)
        return 0

    lax.fori_loop(0, nb * t_len // tmm, mm, 0)


def _conv(zc, hist, wdw, bdw, lng, lnb, wpw, bpw, *, nb, t_len):
    m = zc.shape[0]
    n_b = m // t_len
    tc = min(64, t_len)
    tmm = min(256, nb * t_len)
    full = lambda a: pl.BlockSpec(a.shape, lambda i: (0,) * a.ndim)
    return pl.pallas_call(
        functools.partial(_conv_body, nb=nb, t_len=t_len, tc=tc, tmm=tmm),
        grid=(n_b // nb,),
        in_specs=[pl.BlockSpec((nb * t_len, 3 * W_CONV), lambda i: (i, 0)),
                  pl.BlockSpec((nb, HIST, W_CONV), lambda i: (i, 0, 0)),
                  full(wdw), full(bdw), full(lng), full(lnb), full(wpw), full(bpw)],
        out_specs=[pl.BlockSpec((nb * t_len, W_CONV), lambda i: (i, 0)),
                   pl.BlockSpec((nb, HIST, W_CONV), lambda i: (i, 0, 0))],
        out_shape=[jax.ShapeDtypeStruct((m, W_CONV), BF16), jax.ShapeDtypeStruct((n_b, HIST, W_CONV), F32)],
        scratch_shapes=[pltpu.VMEM((32 + t_len, W_CONV), F32), pltpu.VMEM((nb * t_len, W_CONV), F32)],
        compiler_params=pltpu.CompilerParams(dimension_semantics=("arbitrary",), vmem_limit_bytes=VMEM_LIMIT),
        name="conv",
    )(zc, hist, wdw, bdw, lng, lnb, wpw, bpw)


def _gla_body(zg_ref, s0_ref, wgk_ref, bgk_ref, ltri_ref, lsum_ref, ind_ref, indv_ref, ng_ref, bd_ref,
              y_ref, st_ref, bc_scr, qe_scr, ke_scr, dec_scr, o_scr, *, nb, t_len, chunk, rb):
    rows = nb * t_len
    n_rb = rows // rb
    cpb = t_len // chunk

    def phase1(i, _):
        r0 = pl.multiple_of(i * rb, rb)
        gk = zg_ref[pl.ds(r0, rb), 768:896]
        pre = _dot_split(gk, wgk_ref[0]) + _dot(gk.astype(BF16), wgk_ref[1]) + bgk_ref[...]
        la = (jnp.minimum(pre, 0.0) - jnp.log(1.0 + jnp.exp(-jnp.abs(pre)))) * (1.0 / GATE_NORMALIZER)
        h1, l1 = _split2(la)
        l2 = (la - h1.astype(F32) - l1.astype(F32)).astype(BF16)
        bc = _dot(ltri_ref[...], h1) + _dot(ltri_ref[...], l1) + _dot(ltri_ref[...], l2)
        bt = _dot(lsum_ref[...], h1) + _dot(lsum_ref[...], l1) + _dot(lsum_ref[...], l2)
        bc_scr[pl.ds(r0, rb), :] = bc
        qe_scr[pl.ds(r0, rb), :] = zg_ref[pl.ds(r0, rb), 0:128] * (DK_GLA ** -0.5) * jnp.exp(bc)
        ke_scr[pl.ds(r0, rb), :] = zg_ref[pl.ds(r0, rb), 128:256] * jnp.exp(bt - bc)
        dec_scr[pl.ds(r0, rb), :] = jnp.exp(bt)
        return 0

    lax.fori_loop(0, n_rb, phase1, 0)

    ti = lax.broadcasted_iota(jnp.int32, (chunk, chunk, LANES), 0)
    si = lax.broadcasted_iota(jnp.int32, (chunk, chunk, LANES), 1)
    causal = si <= ti

    u2 = 4 if (rows // chunk) % 4 == 0 else 1
    u3 = 8 if cpb % 8 == 0 else 1

    def phase2(cg, _):
        for u in range(u2):
            r0 = pl.multiple_of((cg * u2 + u) * chunk, chunk)
            bc = bc_scr[pl.ds(r0, chunk), :]
            q = zg_ref[pl.ds(r0, chunk), 0:128] * (DK_GLA ** -0.5)
            k = zg_ref[pl.ds(r0, chunk), 128:256]
            v = zg_ref[pl.ds(r0, chunk), 256:512]
            e = jnp.exp(jnp.where(causal, bc[:, None, :] - bc[None, :, :], NEG))
            p = (q[:, None, :] * k[None, :, :] * e).reshape(chunk * chunk, LANES)
            att = _dot(p.astype(BF16), ind_ref[...]).reshape(chunk, chunk, W_GLA)
            o_scr[pl.ds(r0, chunk), :] = (att * v[None, :, :]).sum(axis=1)
        return 0

    lax.fori_loop(0, rows // chunk // u2, phase2, 0)

    def per_batch(bi, _):
        def group(cg, s):
            for u in range(u3):
                r0 = pl.multiple_of(bi * t_len + (cg * u3 + u) * chunk, chunk)
                o_scr[pl.ds(r0, chunk), :] += _dot_nt(qe_scr[pl.ds(r0, chunk), :].astype(BF16), s.astype(BF16))
                upd = _dot_tn(zg_ref[pl.ds(r0, chunk), 256:512].astype(BF16),
                              ke_scr[pl.ds(r0, chunk), :].astype(BF16))
                s = s * dec_scr[pl.ds(r0, 1), :] + upd * bd_ref[...]
            return s

        st_ref[bi] = lax.fori_loop(0, cpb // u3, group, s0_ref[bi])
        return 0

    lax.fori_loop(0, nb, per_batch, 0)

    def phase4(i, _):
        r0 = pl.multiple_of(i * rb, rb)
        o = o_scr[pl.ds(r0, rb), :]
        o = o * _seg_rms(o, indv_ref, DV_GLA) * ng_ref[...]
        y_ref[pl.ds(r0, rb), :] = (o * _silu(zg_ref[pl.ds(r0, rb), 512:768])).astype(BF16)
        return 0

    lax.fori_loop(0, n_rb, phase4, 0)


def _gla(zg, s0t, wgk, bgk, ng, *, nb, t_len, chunk):
    m = zg.shape[0]
    n_b = m // t_len
    rows = nb * t_len
    rb = min(128, rows)
    assert rows % rb == 0 and rb % chunk == 0 and t_len % chunk == 0
    ltri = jnp.asarray(_blockdiag_ones(rb, chunk) * np.tril(np.ones((rb, rb), np.float32)), BF16)
    lsum = jnp.asarray(_blockdiag_ones(rb, chunk), BF16)
    i128 = np.arange(LANES)
    i256 = np.arange(W_GLA)
    ind = jnp.asarray((i128[:, None] // DK_GLA == i256[None, :] // DV_GLA).astype(np.float32), BF16)
    indv = jnp.asarray(_blockdiag_ones(W_GLA, DV_GLA), BF16)
    bd = jnp.asarray((i256[:, None] // DV_GLA == i128[None, :] // DK_GLA).astype(np.float32))
    full = lambda a: pl.BlockSpec(a.shape, lambda i: (0,) * a.ndim)
    return pl.pallas_call(
        functools.partial(_gla_body, nb=nb, t_len=t_len, chunk=chunk, rb=rb),
        grid=(n_b // nb,),
        in_specs=[pl.BlockSpec((rows, N_GLA), lambda i: (i, 0)),
                  pl.BlockSpec((nb, W_GLA, LANES), lambda i: (i, 0, 0)),
                  full(wgk), full(bgk), full(ltri), full(lsum), full(ind), full(indv), full(ng), full(bd)],
        out_specs=[pl.BlockSpec((rows, W_GLA), lambda i: (i, 0)),
                   pl.BlockSpec((nb, W_GLA, LANES), lambda i: (i, 0, 0))],
        out_shape=[jax.ShapeDtypeStruct((m, W_GLA), BF16), jax.ShapeDtypeStruct((n_b, W_GLA, LANES), F32)],
        scratch_shapes=[pltpu.VMEM((rows, LANES), F32), pltpu.VMEM((rows, LANES), F32),
                        pltpu.VMEM((rows, LANES), F32), pltpu.VMEM((rows, LANES), F32),
                        pltpu.VMEM((rows, W_GLA), F32)],
        compiler_params=pltpu.CompilerParams(dimension_semantics=("arbitrary",), vmem_limit_bytes=VMEM_LIMIT),
        name="gla",
    )(zg, s0t, wgk, bgk, ltri, lsum, ind, indv, ng, bd)


KW = CMP_BLOCK * 256


def _compress_rows(x_ref, w_ref, pe_ref, kvc_ref, n_rows):
    kvc_ref[...] = jnp.zeros(kvc_ref.shape, F32)
    for par in range(2):
        x = (x_ref[:, par * KW:(par + 1) * KW] + pe_ref[...]).astype(BF16)
        kvc_ref[ODD_SLOT0 * par:ODD_SLOT0 * par + n_rows, :] = _dot(x, w_ref[...])


def _compress_prompt_body(x_ref, w_ref, pe_ref, kvc_ref, *, n_rows):
    _compress_rows(x_ref, w_ref, pe_ref, kvc_ref, n_rows)


def _compress_prompt(cmp_rows, w, pe, *, n_b, t_len):
    n_rows = t_len // SEL_BLOCK
    full = lambda a: pl.BlockSpec(a.shape, lambda i: (0,) * a.ndim)
    return pl.pallas_call(
        functools.partial(_compress_prompt_body, n_rows=n_rows),
        grid=(n_b,),
        in_specs=[pl.BlockSpec((n_rows, 2 * KW), lambda i: (i, 0)), full(w), full(pe)],
        out_specs=pl.BlockSpec((None, LANES, 256), lambda i: (i, 0, 0)),
        out_shape=jax.ShapeDtypeStruct((n_b, LANES, 256), F32),
        compiler_params=pltpu.CompilerParams(dimension_semantics=("arbitrary",), vmem_limit_bytes=VMEM_LIMIT),
        name="compress_prompt",
    )(cmp_rows.reshape(n_b * n_rows, 2 * KW), w, pe)


def _compress_paged_body(pt_ref, *refs, n_pages, t_new, n_rows):
    pages = refs[:DEC_NB * n_pages]
    new_ref, w_ref, pe_ref, kvc_ref, x_scr, acc_scr = refs[DEC_NB * n_pages:]
    per_page = PAGE_SIZE // CMP_BLOCK
    n_past = n_pages * per_page
    n_cmp = 2 * n_rows
    for bi in range(DEC_NB):
        new = new_ref[bi * t_new:(bi + 1) * t_new, :]
        for s in range(2):
            for p in range(n_pages):
                rows = pages[bi * n_pages + p][s].reshape(LANES, PAGE_SIZE).T
                for c in range(per_page):
                    r0 = X_PITCH * (per_page * p + c)
                    x_scr[bi, s, r0:r0 + CMP_BLOCK, :] = rows[CMP_BLOCK * c:CMP_BLOCK * (c + 1)]
            x_scr[bi, s, X_PITCH * n_past:, :] = jnp.zeros((X_PITCH * (n_cmp - n_past), LANES), F32)
            x_scr[bi, s, X_PITCH * n_past:X_PITCH * n_past + t_new, :] = new[:, LANES * s:LANES * (s + 1)]
    kvc_ref[...] = jnp.zeros(kvc_ref.shape, F32)
    for s in range(2):
        acc = jnp.zeros((DEC_NB * n_cmp, LANES), F32)
        for j in range(CMP_BLOCK):
            x = jnp.concatenate([x_scr[bi, s, pl.ds(j, n_cmp, stride=X_PITCH), :] for bi in range(DEC_NB)], axis=0)
            acc = acc + _dot((x + pe_ref[s, j:j + 1, :]).astype(BF16), w_ref[s, j])
        acc_scr[...] = acc
        for bi in range(DEC_NB):
            for par in range(2):
                kvc_ref[bi, ODD_SLOT0 * par:ODD_SLOT0 * par + n_rows, LANES * s:LANES * (s + 1)] = (
                    acc_scr[pl.ds(bi * n_cmp + par, n_rows, stride=2), :])


def _paged_specs(layer, n_pages, block):
    return [pl.BlockSpec((None, None) + block, functools.partial(
        lambda b, pt, bi, p: (layer, pt[DEC_NB * b + bi, p]) + (0,) * len(block), bi=bi, p=p))
        for bi in range(DEC_NB) for p in range(n_pages)]


PAGE_BLOCK = (2, G_KV, HD, PAGE_SIZE)
X_PITCH = CMP_BLOCK + 4
DEC_NB = 2


def _compress_paged(page_table, cache_t, layer, cmp_new, w, pe, *, t_new):
    n_b, n_pages = page_table.shape
    assert n_b % DEC_NB == 0
    n_rows = -(-((n_pages * PAGE_SIZE + SEL_BLOCK) // SEL_BLOCK) // 8) * 8
    full = lambda a: pl.BlockSpec(a.shape, lambda b, pt: (0,) * a.ndim)
    grid_spec = pltpu.PrefetchScalarGridSpec(
        num_scalar_prefetch=1, grid=(n_b // DEC_NB,),
        in_specs=_paged_specs(layer, n_pages, PAGE_BLOCK) + [
            pl.BlockSpec((DEC_NB * t_new, 256), lambda b, pt: (b, 0)), full(w), full(pe)],
        out_specs=pl.BlockSpec((DEC_NB, LANES, 256), lambda b, pt: (b, 0, 0)),
        scratch_shapes=[pltpu.VMEM((DEC_NB, 2, 2 * n_rows * X_PITCH, LANES), F32),
                        pltpu.VMEM((DEC_NB * 2 * n_rows, LANES), F32)])
    return pl.pallas_call(
        functools.partial(_compress_paged_body, n_pages=n_pages, t_new=t_new, n_rows=n_rows),
        grid_spec=grid_spec, out_shape=jax.ShapeDtypeStruct((n_b, LANES, 256), F32),
        compiler_params=pltpu.CompilerParams(dimension_semantics=("arbitrary",), vmem_limit_bytes=VMEM_LIMIT),
        name="compress_paged",
    )(page_table, *([cache_t] * (DEC_NB * n_pages)), cmp_new, w, pe)


def _cmpsel_body(qn_ref, kvc_ref, bias_ref, sbase_ref, useimp_ref, indk_ref, kg_ref, cvec_ref,
                 *outs, nb, tb, n_blocks, q_t):
    q_refs, ocmp_ref = outs[:-1], outs[-1]
    rows = nb * tb
    lane = _lane((rows, LANES))
    qn = qn_ref[...].astype(F32)
    qpad = [_half_pad(qn[:, 128 * (h // 2):128 * (h // 2) + 128], h % 2 == 1) for h in range(H_NSA)]
    imp_parts = []
    o_parts = [[None] * nb for _ in range(H_NSA)]
    kc_all = kvc_ref[:, :, 0:128].reshape(nb * LANES, LANES)
    kc_all = kc_all * _seg_rms(kc_all, indk_ref, HD) * kg_ref[...]
    for bi in range(nb):
        kc = kc_all[bi * LANES:(bi + 1) * LANES]
        vboth = kvc_ref[bi, :, 128:256].astype(BF16)
        imp_b = []
        for g in range(G_KV):
            kpad = _half_pad(kc, g == 1).astype(BF16)
            qs = jnp.concatenate([qpad[HPG * g + j][bi * tb:(bi + 1) * tb] for j in range(HPG)], axis=0).astype(BF16)
            bias = bias_ref[g]
            l = _dot_nt(qs, kpad) + bias
            vis = bias > 0.5 * NEG
            mx = jnp.max(l, axis=-1, keepdims=True)
            p = jnp.where(vis, jnp.exp(l - mx), 0.0)
            p = p / jnp.maximum(jnp.sum(p, axis=-1, keepdims=True), 1e-30)
            o = _dot(p.astype(BF16), vboth)
            ig = p[0:tb]
            for j in range(1, HPG):
                ig = ig + p[j * tb:(j + 1) * tb]
            imp_b.append(ig)
            for j in range(HPG):
                o_parts[HPG * g + j][bi] = o[j * tb:(j + 1) * tb]
        imp_parts.append(imp_b)
    for j in range(HPG):
        o0 = jnp.concatenate(o_parts[j], axis=0) if nb > 1 else o_parts[j][0]
        o1 = jnp.concatenate(o_parts[HPG + j], axis=0) if nb > 1 else o_parts[HPG + j][0]
        ocmp_ref[:, 128 * j:128 * j + 128] = jnp.where(lane < 64, o0, o1)
    for g in range(G_KV):
        ig = jnp.concatenate([imp_parts[bi][g] for bi in range(nb)], axis=0) if nb > 1 else imp_parts[0][g]
        imp = ig + pltpu.roll(ig, LANES - ODD_SLOT0, axis=1)
        score = jnp.where(useimp_ref[...] > 0.5, imp, sbase_ref[...])
        st = score.T[0:SEL_SLOTS]
        jj = lax.broadcasted_iota(jnp.int32, (SEL_SLOTS, rows), 0)
        rank = jnp.zeros((SEL_SLOTS, rows), F32)
        for jp in range(n_blocks):
            r = st[jp:jp + 1, :]
            before = (r > st) | ((r == st) & (jp < jj))
            rank = rank + before.astype(F32)
        selb = jnp.where(rank < TOP_N - 0.5, 0.0, NEG)
        full_t = jnp.concatenate([jnp.zeros((SEL_LANE0, rows), F32), selb,
                                  jnp.zeros((LANES - SEL_LANE0 - SEL_SLOTS, rows), F32)], axis=0)
        if q_t:
            for j in range(HPG):
                h = HPG * g + j
                q_refs[j][g] = ((qpad[h] + cvec_ref[h]).T + full_t).astype(BF16)
        else:
            extra = full_t.T
            for j in range(HPG):
                h = HPG * g + j
                q_refs[0][g, j] = (qpad[h] + extra + cvec_ref[h]).astype(q_refs[0].dtype)


def _cmpsel(qn, kvc, bias, sbase, useimp, indk, kg0, cvec, *, nb, tb, n_blocks, q_t):
    m = qn.shape[0]
    rows = nb * tb
    assert rows == LANES and m % rows == 0
    n_pos = bias.shape[0]
    full = lambda a: pl.BlockSpec(a.shape, lambda i: (0,) * a.ndim)
    if q_t:
        q_specs = [pl.BlockSpec((G_KV, LANES, rows), lambda i: (0, 0, i))] * HPG
        q_shapes = [jax.ShapeDtypeStruct((G_KV, LANES, m), BF16)] * HPG
    else:
        q_specs = [pl.BlockSpec((G_KV, HPG, rows, LANES), lambda i: (0, 0, i, 0))]
        q_shapes = [jax.ShapeDtypeStruct((G_KV, HPG, m, LANES), F32)]
    return pl.pallas_call(
        functools.partial(_cmpsel_body, nb=nb, tb=tb, n_blocks=n_blocks, q_t=q_t),
        grid=(m // rows,),
        in_specs=[pl.BlockSpec((rows, W_NSA), lambda i: (i, 0)),
                  pl.BlockSpec((nb, LANES, 256), (lambda i: (i, 0, 0)) if nb > 1 else (lambda i: (i // n_pos, 0, 0))),
                  pl.BlockSpec((None, G_KV, HPG * tb, LANES), lambda i: (i % n_pos, 0, 0, 0)),
                  pl.BlockSpec((None, rows, LANES), lambda i: (i % n_pos, 0, 0)),
                  pl.BlockSpec((None, rows, LANES), lambda i: (i % n_pos, 0, 0)),
                  full(indk), full(kg0), full(cvec)],
        out_specs=q_specs + [pl.BlockSpec((rows, W_NSA), lambda i: (i, 0))],
        out_shape=q_shapes + [jax.ShapeDtypeStruct((m, W_NSA), F32)],
        compiler_params=pltpu.CompilerParams(dimension_semantics=("arbitrary",), vmem_limit_bytes=VMEM_LIMIT),
        name="cmpsel",
    )(qn, kvc, bias, sbase, useimp, indk, kg0, cvec)


def _mix_gates(g_ref, bg_ref, expand_ref, gate_ref, ocmp_ref, o_slc, o_win, y_ref):
    gl = _sigmoid(g_ref[...] + bg_ref[...])
    gx = _dot_split(gl, expand_ref[...])
    for j in range(HPG):
        sl = slice(128 * j, 128 * j + 128)
        o = (gx[:, 128 * j:128 * j + 128] * ocmp_ref[:, sl]
             + gx[:, 128 * (HPG + j):128 * (HPG + j) + 128] * o_slc[j]
             + gx[:, 128 * (2 * HPG + j):128 * (2 * HPG + j) + 128] * o_win[j])
        y_ref[:, sl] = (o * _silu(gate_ref[:, sl])).astype(y_ref.dtype)


def _softmax_step(q_t, k, v_t, bias_t, m_scr, acc_scr):
    n_key, n_col = k.shape[0], q_t.shape[1]
    for k0 in range(0, n_key, KEY_CHUNK):
        ks = slice(k0, k0 + KEY_CHUNK)
        for c0 in range(0, n_col, COL_CHUNK):
            cs = slice(c0, c0 + COL_CHUNK)
            s = _dot(k[ks], q_t[:, cs])
            if bias_t is not None:
                s = s + bias_t(ks, cs)
            m_prev = m_scr[:, cs]
            m_new = jnp.maximum(m_prev, jnp.max(s, axis=0, keepdims=True))
            p = jnp.exp(s - m_new)
            acc_scr[:, cs] = jnp.exp(m_prev - m_new) * acc_scr[:, cs] + _dot(v_t[:, ks], p.astype(BF16))
            m_scr[:, cs] = m_new


def _flash_body(q0_ref, q1_ref, q2_ref, q3_ref, ksa_ref, vs_ref, kwa_ref, vw_ref, dt_ref, edge_ref, ocmp_ref,
                g_ref, gate_ref, bg_ref, expand_ref, y_ref, m_scr, acc_scr):
    i = pl.program_id(1)
    cols = HPG * TQ
    m_scr[...] = jnp.full(m_scr.shape, NEG, F32)
    acc_scr[...] = jnp.zeros(acc_scr.shape, F32)

    def run(n_near):
        for g in range(G_KV):
            q_t = jnp.concatenate([r[g] for r in (q0_ref, q1_ref, q2_ref, q3_ref)], axis=1)
            for br, (ka_ref, v_ref) in enumerate(((ksa_ref, vs_ref), (kwa_ref, vw_ref))):
                def step(kt, bias_t):
                    k0 = pl.multiple_of(kt * TQ, TQ)
                    _softmax_step(q_t, ka_ref[g, pl.ds(k0, TQ), :], v_ref[kt, g], bias_t, m_scr.at[br, g],
                                  acc_scr.at[br, g])

                if n_near == 2:
                    if br == 0:
                        def far_pair(kp, _):
                            step(2 * kp, None)
                            step(2 * kp + 1, None)
                            return 0

                        lax.fori_loop(0, lax.shift_right_logical(i - 1, 1), far_pair, 0)

                        @pl.when(lax.rem(i - 1, 2) == 1)
                        def _():
                            step(i - 2, None)
                    else:
                        step(i - 2, lambda ks, cs: edge_ref[ks, cs])
                if n_near >= 1:
                    step(i - 1, lambda ks, cs: dt_ref[1, g, ks, cs])
                step(i, lambda ks, cs: dt_ref[0, g, ks, cs])

    for n_near, cond in ((0, i == 0), (1, i == 1), (2, i >= 2)):
        pl.when(cond)(functools.partial(run, n_near))

    outs = []
    for br in range(2):
        o_t = jnp.concatenate([acc_scr[br, g, 0:HD, :] / acc_scr[br, g, HD:HD + 1, :] for g in range(G_KV)], axis=0)
        outs.append([o_t[:, j * TQ:(j + 1) * TQ].T for j in range(HPG)])
    _mix_gates(g_ref, bg_ref, expand_ref, gate_ref, ocmp_ref, outs[0], outs[1], y_ref)


def _flash(q_ts, ksa, vs, kwa, vw, dt, edge, ocmp, ng, ngate, bg, expand, *, n_b, t_len):
    m = n_b * t_len
    nq = t_len // TQ
    cols = HPG * TQ
    full = lambda a: pl.BlockSpec(a.shape, lambda b, i: (0,) * a.ndim)
    qspec = pl.BlockSpec((G_KV, LANES, TQ), lambda b, i: (0, 0, b * nq + i))
    kspec = pl.BlockSpec((G_KV, None, t_len, LANES), lambda b, i: (0, b, 0, 0))
    vspec = pl.BlockSpec((nq, G_KV, V_ROWS, TQ), lambda b, i: (b, 0, 0, 0))
    row = lambda n: pl.BlockSpec((TQ, n), lambda b, i: (b * nq + i, 0))
    return pl.pallas_call(
        _flash_body,
        grid=(n_b, nq),
        in_specs=[qspec] * HPG + [kspec, vspec, kspec, vspec, full(dt), full(edge), row(W_NSA), row(LANES),
                                  row(W_NSA), full(bg), full(expand)],
        out_specs=row(W_NSA), out_shape=jax.ShapeDtypeStruct((m, W_NSA), BF16),
        scratch_shapes=[pltpu.VMEM((2, G_KV, 1, cols), F32), pltpu.VMEM((2, G_KV, V_ROWS, cols), F32)],
        compiler_params=pltpu.CompilerParams(dimension_semantics=("arbitrary", "arbitrary"),
                                             vmem_limit_bytes=VMEM_LIMIT),
        name="flash",
    )(*q_ts, ksa.reshape(G_KV, n_b, t_len, LANES), vs, kwa.reshape(G_KV, n_b, t_len, LANES), vw,
      dt, edge, ocmp, ng, ngate, bg, expand)


def _dec_attn_body(pt_ref, *refs, n_pages, t_new, n_keys, n_wkeys):
    pages = refs[:DEC_NB * n_pages]
    (qa_ref, slc_ref, win_ref, wst_ref, dsl_ref, dwn_ref, ocmp_ref, g_ref, gate_ref, bg_ref, expand_ref, _wall,
     y_ref, wout_ref, ka_scr, va_scr, kw_scr, vw_scr) = refs[DEC_NB * n_pages:]
    past = n_pages * PAGE_SIZE
    w_past = wst_ref.shape[-1]
    rows = HPG * t_new

    @pl.when(pl.program_id(0) == 0)
    def _():
        for k_scr, n, with_blocks in ((ka_scr, n_keys, True), (kw_scr, n_wkeys, False)):
            feat = lax.broadcasted_iota(jnp.int32, (LANES, n), 0)
            key = lax.broadcasted_iota(jnp.int32, (LANES, n), 1)
            aug = (feat == CONST_LANE) | (feat == CONST_LANE + 1)
            if with_blocks:
                aug = aug | ((feat - SEL_LANE0) == lax.shift_right_logical(key, 6))
            for bi in range(DEC_NB):
                for g in range(G_KV):
                    k_scr[bi, g] = aug.astype(F32).astype(BF16)

    def new_keys_t(new):
        pad = jnp.concatenate([new, jnp.zeros((LANES - t_new, 256), F32)], axis=0)
        return pad[:, 0:128].T, pad[:, 128:256].T

    lane = _lane((HD, LANES))
    lane_o = _lane((t_new, LANES))
    outs = [[[None] * DEC_NB for _ in range(HPG)] for _ in range(2)]
    for bi in range(DEC_NB):
        tok = slice(bi * t_new, (bi + 1) * t_new)
        for p in range(n_pages):
            sl = slice(p * PAGE_SIZE, (p + 1) * PAGE_SIZE)
            for g in range(G_KV):
                ka_scr[bi, g, 0:HD, sl] = pages[bi * n_pages + p][0, g].astype(BF16)
                va_scr[bi, HD * g:HD * (g + 1), sl] = pages[bi * n_pages + p][1, g].astype(BF16)
        kn_t, vn_t = new_keys_t(slc_ref[tok, :])
        va_scr[bi, :, past:n_keys] = vn_t.astype(BF16)
        kwn_t, vwn_t = new_keys_t(win_ref[tok, :])
        vw_scr[bi, :, w_past:n_wkeys] = vwn_t.astype(BF16)
        for g in range(G_KV):
            ka_scr[bi, g, 0:HD, past:n_keys] = kn_t[HD * g:HD * (g + 1)].astype(BF16)
            kw_scr[bi, g, 0:HD, 0:w_past] = wst_ref[bi, 0, g].astype(BF16)
            kw_scr[bi, g, 0:HD, w_past:n_wkeys] = kwn_t[HD * g:HD * (g + 1)].astype(BF16)
            vw_scr[bi, HD * g:HD * (g + 1), 0:w_past] = wst_ref[bi, 1, g].astype(BF16)
            for s, new_t in ((0, kwn_t), (1, vwn_t)):
                sh = pltpu.roll(wst_ref[bi, s, g], w_past - t_new, axis=1)
                tail = pltpu.roll(new_t[HD * g:HD * (g + 1)], LANES - t_new, axis=1)
                wout_ref[bi, s, g, :, 0:w_past - LANES] = sh[:, 0:w_past - LANES]
                wout_ref[bi, s, g, :, w_past - LANES:w_past] = jnp.where(lane >= LANES - t_new, tail,
                                                                        sh[:, w_past - LANES:])
        res = [[None, None], [None, None]]
        for g in range(G_KV):
            q = qa_ref[g, :, tok, :].reshape(rows, LANES).astype(BF16)
            for br, (k_scr, v_scr, b_ref) in enumerate(((ka_scr, va_scr, dsl_ref), (kw_scr, vw_scr, dwn_ref))):
                s = _dot(q, k_scr[bi, g]) + b_ref[g]
                mx = jnp.max(s, axis=-1, keepdims=True)
                p = jnp.exp(s - mx)
                den = jnp.sum(p, axis=-1, keepdims=True)
                res[br][g] = _dot_nt(p.astype(BF16), v_scr[bi]) / den
        for br in range(2):
            for j in range(HPG):
                outs[br][j][bi] = jnp.where(lane_o < 64, res[br][0][j * t_new:(j + 1) * t_new],
                                            res[br][1][j * t_new:(j + 1) * t_new])
    outs = [[jnp.concatenate(outs[br][j], axis=0) for j in range(HPG)] for br in range(2)]
    _mix_gates(g_ref, bg_ref, expand_ref, gate_ref, ocmp_ref, outs[0], outs[1], y_ref)


def _dec_attn(page_table, cache, layer, qa, slc_new, win_new, win_state, dsl, dwn, ocmp, ng, ngate, bg, expand,
              win_all, *, t_new):
    n_b, n_pages = page_table.shape
    assert n_b % DEC_NB == 0
    m = n_b * t_new
    w_past = win_state.shape[-1]
    n_keys = dsl.shape[-1]
    n_wkeys = dwn.shape[-1]
    wblock = (2, G_KV, HD, w_past)
    full = lambda a: pl.BlockSpec(a.shape, lambda b, pt: (0,) * a.ndim)
    row = lambda n: pl.BlockSpec((DEC_NB * t_new, n), lambda b, pt: (b, 0))
    wspec = pl.BlockSpec((None, DEC_NB) + wblock, lambda b, pt: (layer, b, 0, 0, 0, 0))
    grid_spec = pltpu.PrefetchScalarGridSpec(
        num_scalar_prefetch=1, grid=(n_b // DEC_NB,),
        in_specs=_paged_specs(layer, n_pages, PAGE_BLOCK) + [
            pl.BlockSpec((G_KV, HPG, DEC_NB * t_new, LANES), lambda b, pt: (0, 0, b, 0)),
            row(256), row(256), wspec,
            full(dsl), full(dwn), row(W_NSA), row(LANES), row(W_NSA), full(bg), full(expand),
            pl.BlockSpec(memory_space=pl.ANY)],
        out_specs=[row(W_NSA), wspec],
        scratch_shapes=[pltpu.VMEM((DEC_NB, G_KV, LANES, n_keys), BF16), pltpu.VMEM((DEC_NB, LANES, n_keys), BF16),
                        pltpu.VMEM((DEC_NB, G_KV, LANES, n_wkeys), BF16), pltpu.VMEM((DEC_NB, LANES, n_wkeys), BF16)])
    args = [page_table, *([cache] * (DEC_NB * n_pages)), qa, slc_new, win_new, win_state, dsl, dwn, ocmp, ng, ngate,
            bg, expand, win_all]
    return pl.pallas_call(
        functools.partial(_dec_attn_body, n_pages=n_pages, t_new=t_new, n_keys=n_keys, n_wkeys=n_wkeys),
        grid_spec=grid_spec,
        out_shape=[jax.ShapeDtypeStruct((m, W_NSA), F32), jax.ShapeDtypeStruct(win_all.shape, F32)],
        input_output_aliases={len(args) - 1: 1},
        compiler_params=pltpu.CompilerParams(dimension_semantics=("arbitrary",), vmem_limit_bytes=VMEM_LIMIT),
        name="dec_attn",
    )(*args)


def _bucket_matrix(qpos, kpos, valid):
    dist = qpos[:, None] - kpos[None, :]
    return np.where(valid & (dist >= 0), _rel_bucket_np(dist), -1).astype(np.int32)


def _bias_body(tab_ref, bk_ref, o_ref, *, sub_far):
    bk = bk_ref[...]
    for h in range(H_NSA):
        far = tab_ref[N_BUCKETS - 1, h] if sub_far else 0.0
        acc = jnp.full(bk.shape, NEG, F32)
        for b in range(N_BUCKETS):
            acc = jnp.where(bk == b, tab_ref[b, h] - far, acc)
        o_ref[h] = acc


def _bias_table(table, bk, *, sub_far):
    r, c = bk.shape
    tr = 8 if r <= 8 else 64
    assert r % tr == 0
    return pl.pallas_call(
        functools.partial(_bias_body, sub_far=sub_far),
        grid=(r // tr,),
        in_specs=[pl.BlockSpec(memory_space=pltpu.SMEM), pl.BlockSpec((tr, c), lambda i: (i, 0))],
        out_specs=pl.BlockSpec((H_NSA, tr, c), lambda i: (0, i, 0)),
        out_shape=jax.ShapeDtypeStruct((H_NSA, r, c), F32),
        compiler_params=pltpu.CompilerParams(dimension_semantics=("arbitrary",)),
        name="bias_table",
    )(table, jnp.asarray(bk))


def _cmp_tables(qpos_blocks, n_blocks):
    slot = np.arange(LANES)
    blk = np.where(slot < ODD_SLOT0, 2 * slot, 2 * (slot - ODD_SLOT0) + 1)
    real = (blk < 2 * n_blocks) & ((slot % ODD_SLOT0) < n_blocks)
    blk_end = blk * CMP_BLOCK + CMP_BLOCK - 1
    buckets, sbases, useimps = [], [], []
    for qpos in qpos_blocks:
        buckets.append(_bucket_matrix(qpos, blk_end, real[None, :]))
        j = np.arange(LANES)[None, :]
        cur = (qpos // SEL_BLOCK)[:, None]
        forced = (j == 0) | (j == cur) | (j == cur - 1)
        valid = j * SEL_BLOCK <= qpos[:, None]
        inrange = j < n_blocks
        sbases.append(np.where(~inrange, -2.0, np.where(forced, FORCE, -1.0)).astype(np.float32))
        useimps.append((inrange & ~forced & valid).astype(np.float32))
    return np.concatenate(buckets), jnp.asarray(np.stack(sbases)), jnp.asarray(np.stack(useimps))


def _head_rows(b, t):
    k = b.shape[-1]
    n = b.shape[1] // t
    return b.reshape(G_KV, HPG, n, t, k).transpose(2, 0, 1, 3, 4).reshape(n, G_KV, HPG * t, k)


def _prep_weights(rel_bias, w_in, conv_w_pw, gla_w_gk, nsa_q_norm_g, nsa_k_norm_g, nsa_pe_cmp, nsa_w_cmp,
                  nsa_b_gate, w_out):
    depth = w_in.shape[0]
    cols = _in_proj_columns()
    pieces, start = [], 0
    for c in range(1, N_IN + 1):
        pad = cols[start] < 0
        if c == N_IN or (cols[c] >= 0 if pad else cols[c] != cols[c - 1] + 1):
            src = int(cols[start])
            pieces.append(jnp.zeros((depth, D_MODEL, c - start), w_in.dtype) if pad else w_in[:, :, src:src + c - start])
            start = c
    w_in_p = jnp.concatenate(pieces, axis=2).astype(BF16)
    perm = 512 + _nsa_perm()
    w_out_p = jnp.concatenate([w_out[:, 0:512]] + [w_out[:, int(perm[c]):int(perm[c]) + HD]
                                                   for c in range(0, W_NSA, HD)], axis=1).astype(BF16)
    qg = jnp.tile(nsa_q_norm_g, (1, H_NSA))[:, None, :] * SCALE
    kg = jnp.tile(nsa_k_norm_g, (1, 1, G_KV))
    wgk = jnp.zeros((depth, LANES, LANES), F32).at[:, :GATE_RANK, :].set(gla_w_gk)
    wgk_hi = wgk.astype(BF16)
    wgk2 = jnp.stack([wgk_hi, (wgk - wgk_hi.astype(F32)).astype(BF16)], axis=1)
    def blockdiag2(a, b):
        za = jnp.zeros(a.shape[:-1] + (b.shape[-1],), a.dtype)
        zb = jnp.zeros(b.shape[:-1] + (a.shape[-1],), b.dtype)
        return jnp.concatenate([jnp.concatenate([a, za], axis=-1), jnp.concatenate([zb, b], axis=-1)], axis=-2)

    wc = nsa_w_cmp.astype(BF16)
    wtap = blockdiag2(wc, wc)
    wbd = blockdiag2(wtap[:, 0], wtap[:, 1]).reshape(depth, KW, 256)
    pe = jnp.tile(nsa_pe_cmp.transpose(0, 2, 1, 3)[:, :, :, None, :], (1, 1, 1, G_KV, 1))
    petap = pe.transpose(0, 2, 1, 3, 4).reshape(depth, 2, CMP_BLOCK, LANES)
    pe = pe.reshape(depth, 1, KW)
    bg = jnp.zeros((depth, 1, LANES), F32).at[:, 0, :3 * H_NSA].set(nsa_b_gate)
    expand = np.zeros((LANES, 3 * HPG * LANES), np.float32)
    for br in range(3):
        for g in range(G_KV):
            for j in range(HPG):
                c = LANES * (HPG * br + j) + 64 * g
                expand[H_NSA * br + HPG * g + j, c:c + 64] = 1.0
    far = rel_bias[N_BUCKETS - 1]
    far_hi = far.astype(BF16).astype(F32)
    cvec = jnp.zeros((H_NSA, 1, LANES), F32).at[:, 0, CONST_LANE].set(far_hi).at[:, 0, CONST_LANE + 1].set(far - far_hi)
    return dict(w_in=w_in_p, w_out=w_out_p, qg=qg, kg=kg, wgk=wgk2, wbd=wbd, pe=pe, bg=bg,
                wtap=wtap, petap=petap,
                expand=jnp.asarray(expand, BF16), cvec=cvec, wpw=conv_w_pw.astype(BF16),
                indq=jnp.asarray(_blockdiag_ones(W_NSA, HD), BF16), indk=jnp.asarray(_blockdiag_ones(LANES, HD), BF16))


def kernel(x_prompt, x_sample, cache_cmp_kv, cache_slc_kv, page_table, state_win_kv, state_gla, state_conv, rel_bias, norm_g, w_in, conv_w_dw, conv_b_dw, conv_ln_g, conv_ln_b, conv_w_pw, conv_b_pw, gla_w_gk, gla_b_gk, gla_norm_g, nsa_q_norm_g, nsa_k_norm_g, nsa_pe_cmp, nsa_w_cmp, nsa_b_gate, w_out, norm_f):
    depth = w_in.shape[0]
    n_bp, t_p, _ = x_prompt.shape
    n_bs, t_s, _ = x_sample.shape
    n_pages = page_table.shape[1]
    past = n_pages * PAGE_SIZE
    w_past = state_win_kv.shape[2]
    assert t_p % TQ == 0 and WINDOW == 2 * TQ and t_s == 8 and w_past == WINDOW and past % SEL_BLOCK == 0
    wp = _prep_weights(rel_bias, w_in, conv_w_pw, gla_w_gk, nsa_q_norm_g, nsa_k_norm_g, nsa_pe_cmp, nsa_w_cmp,
                       nsa_b_gate, w_out)
    table = rel_bias
    nblk_p = t_p // SEL_BLOCK
    nblk_s = (past + SEL_BLOCK) // SEL_BLOCK
    pos_p = [np.arange(i * LANES, (i + 1) * LANES) for i in range(t_p // LANES)]
    bk_p, sbase_p, useimp_p = _cmp_tables(pos_p, nblk_p)
    bias_p = _head_rows(_bias_table(table, bk_p, sub_far=False), LANES)
    nb_s = LANES // t_s
    qpos_s = past + np.arange(t_s)
    bk_s, _, _ = _cmp_tables([qpos_s], nblk_s)
    bias_s = _head_rows(_bias_table(table, bk_s, sub_far=False), t_s)
    _, sbase_1, useimp_1 = _cmp_tables([np.tile(qpos_s, nb_s)], nblk_s)
    tq = np.arange(TQ)
    every = np.ones((1, 1), bool)
    bk_dt = np.concatenate([_bucket_matrix(tq, tq, every).T, _bucket_matrix(TQ + tq, tq, every).T])
    dt = _bias_table(table, bk_dt, sub_far=True)
    dt = dt.reshape(G_KV, HPG, 2, TQ, TQ).transpose(2, 0, 3, 1, 4).reshape(2, G_KV, TQ, HPG * TQ)
    edge = jnp.asarray(np.tile(np.where(tq[:, None] > tq[None, :], 0.0, NEG).astype(np.float32), (1, HPG)))
    n_keys = past + LANES
    kpos = np.arange(n_keys)
    dsl = _head_rows(_bias_table(table, _bucket_matrix(qpos_s, kpos, (kpos < past + t_s)[None, :]),
                                 sub_far=True), t_s)[0]
    n_wkeys = w_past + LANES
    wk = np.arange(n_wkeys)
    wpos = past - w_past + wk
    wvalid = (wk < w_past + t_s)[None, :] & ((qpos_s[:, None] - wpos[None, :]) < WINDOW) & (wpos >= 0)[None, :]
    dwn = _head_rows(_bias_table(table, _bucket_matrix(qpos_s, wpos, wvalid), sub_far=True), t_s)[0]

    cache_cmp = cache_cmp_kv.transpose(0, 1, 3, 4, 5, 2)
    cache_slc = cache_slc_kv.transpose(0, 1, 3, 4, 5, 2)
    win_state = state_win_kv.transpose(0, 1, 3, 4, 5, 2)
    s0 = state_gla.transpose(0, 1, 2, 4, 3).reshape(depth, n_bs, W_GLA, DK_GLA)
    i256 = np.arange(W_GLA)
    i128 = np.arange(LANES)
    bdmask = jnp.asarray((i256[:, None] // DV_GLA == i128[None, :] // DK_GLA).astype(np.float32))
    s0 = jnp.tile(s0, (1, 1, 1, H_GLA)) * bdmask
    zero_s0 = jnp.zeros((n_bp, W_GLA, LANES), F32)
    zero_hist = jnp.zeros((n_bp, HIST, W_CONV), F32)

    def unpack_state(st):
        b = st.shape[0]
        blocks = [st[:, DV_GLA * h:DV_GLA * (h + 1), DK_GLA * h:DK_GLA * (h + 1)] for h in range(H_GLA)]
        return jnp.stack(blocks, axis=1).transpose(0, 1, 3, 2)

    xp = x_prompt.reshape(n_bp * t_p, D_MODEL)
    xs = x_sample.reshape(n_bs * t_s, D_MODEL)
    outs_p, outs_s = [], []
    leaf_shape = (depth, n_bp, 2, G_KV, HD, t_p)
    chain_p, win_all = (jnp.zeros(leaf_shape, F32), jnp.zeros(leaf_shape, F32)), jnp.zeros(win_state.shape, F32)
    for l in range(depth):
        final = l == depth - 1
        row = lambda a: a[l][None, :]
        common = dict(wdw=conv_w_dw[l], bdw=row(conv_b_dw), lng=row(conv_ln_g), lnb=row(conv_ln_b), wpw=wp["wpw"][l],
                      bpw=row(conv_b_pw))
        bgk = jnp.zeros((1, LANES), F32).at[0, :].set(gla_b_gk[l])
        gng = jnp.tile(gla_norm_g[l], (H_GLA,))[None, :]
        kg12 = wp["kg"][l, 1:3]
        kg0 = wp["kg"][l, 0:1]
        zc, zg, qn, cmp_n, _, win_n, ngate, ng, ksa, vs, kwa, vw, cmp_all, slc_all = _in_proj(
            xp, row(norm_g), wp["w_in"][l], wp["indq"], wp["indk"], wp["qg"][l], kg12, seq_len=t_p, prompt=True,
            layer=l, depth=depth, chain=chain_p)
        chain_p = (cmp_all, slc_all)
        yc, conv_st = _conv(zc, zero_hist, nb=1, t_len=t_p, **common)
        yg, gla_st = _gla(zg, zero_s0, wp["wgk"][l], bgk, gng, nb=1, t_len=t_p, chunk=16)
        kvc = _compress_prompt(cmp_n, wp["wbd"][l], wp["pe"][l], n_b=n_bp, t_len=t_p)
        *q_ts, ocmp = _cmpsel(qn, kvc, bias_p, sbase_p, useimp_p, wp["indk"], kg0, wp["cvec"], nb=1, tb=LANES,
                              n_blocks=nblk_p, q_t=True)
        yn = _flash(q_ts, ksa, vs, kwa, vw, dt, edge, ocmp, ng, ngate, wp["bg"][l], wp["expand"], n_b=n_bp, t_len=t_p)
        xp = _out_proj(xp, yc, yg, yn, wp["w_out"][l], norm_f[None, :], final=final)
        w_keep = min(WINDOW, t_p)
        outs_p.append((conv_st, unpack_state(gla_st), win_n.reshape(n_bp, t_p, 256)[:, t_p - w_keep:]))
        zc, zg, qn, cmp_n, slc_n, win_n, ngate, ng = _in_proj(
            xs, row(norm_g), wp["w_in"][l], wp["indq"], wp["indk"], wp["qg"][l], kg12, seq_len=t_s, prompt=False)
        yc, conv_st = _conv(zc, state_conv[l], nb=nb_s, t_len=t_s, **common)
        yg, gla_st = _gla(zg, s0[l], wp["wgk"][l], bgk, gng, nb=nb_s, t_len=t_s, chunk=t_s)
        kvc = _compress_paged(page_table, cache_cmp, l, cmp_n, wp["wtap"][l], wp["petap"][l], t_new=t_s)
        qa, ocmp = _cmpsel(qn, kvc, bias_s, sbase_1, useimp_1, wp["indk"], kg0, wp["cvec"], nb=nb_s, tb=t_s,
                           n_blocks=nblk_s, q_t=False)
        yn, win_all = _dec_attn(page_table, cache_slc, l, qa, slc_n, win_n, win_state, dsl, dwn, ocmp, ng, ngate,
                                wp["bg"][l], wp["expand"], win_all, t_new=t_s)
        xs = _out_proj(xs, yc, yg, yn, wp["w_out"][l], norm_f[None, :], final=final)
        outs_s.append((conv_st, unpack_state(gla_st), None, cmp_n, slc_n))

    def stack(outs, k, shape):
        return jnp.stack([o[k] for o in outs]).reshape(shape)

    kv = (2, G_KV, HD)
    return (xp.reshape(n_bp, t_p, D_MODEL), xs.reshape(n_bs, t_s, D_MODEL),
            stack(outs_p, 0, (depth, n_bp, HIST, W_CONV)), stack(outs_s, 0, (depth, n_bs, HIST, W_CONV)),
            stack(outs_p, 1, (depth, n_bp, H_GLA, DK_GLA, DV_GLA)), stack(outs_s, 1, (depth, n_bs, H_GLA, DK_GLA, DV_GLA)),
            stack(outs_p, 2, (depth, n_bp, min(WINDOW, t_p)) + kv), win_all.transpose(0, 1, 5, 2, 3, 4),
            chain_p[0].transpose(0, 1, 5, 2, 3, 4), stack(outs_s, 3, (depth, n_bs, t_s) + kv),
            chain_p[1].transpose(0, 1, 5, 2, 3, 4), stack(outs_s, 4, (depth, n_bs, t_s) + kv))
```

```python
import functools
import math

import numpy as np
import jax
import jax.numpy as jnp
from jax import lax
from jax.experimental import pallas as pl
from jax.experimental.pallas import tpu as pltpu

F32 = jnp.float32
BF16 = jnp.bfloat16

D_MODEL = 1024
W_CONV = 256
CONV_WIDTH = 31
HIST = CONV_WIDTH - 1
H_GLA = 4
DK_GLA = 32
DV_GLA = 64
W_GLA = H_GLA * DV_GLA
GATE_RANK = 16
GATE_NORMALIZER = 16.0
H_NSA = 8
HD = 64
G_KV = 2
HPG = H_NSA // G_KV
W_NSA = H_NSA * HD
CMP_BLOCK = 32
SEL_BLOCK = 64
TOP_N = 8
WINDOW = 512
N_BUCKETS = 32
MAX_EXACT = N_BUCKETS // 2
MAX_DISTANCE = 128
PAGE_SIZE = 128
SCALE = HD ** -0.5
EPS = 1e-6
NEG = -1e30
FORCE = 1e4

LANES = 128
SEL_LANE0 = 64
SEL_SLOTS = 40
CONST_LANE = 112
ODD_SLOT0 = 64
TQ = 256
V_ROWS = 128
KEY_CHUNK = 256
COL_CHUNK = 128
VMEM_LIMIT = 56 * 1024 * 1024

C_CONV = 0
C_GLA = 3 * W_CONV
N_GLA = 2 * H_GLA * DK_GLA + 2 * W_GLA + LANES
C_NSA = C_GLA + N_GLA
N_NSA = W_NSA + 3 * 256 + W_NSA + LANES
N_IN = C_NSA + N_NSA


def _rel_bucket_np(dist):
    n = np.maximum(dist, 0)
    nf = np.maximum(n, 1).astype(np.float32)
    large = MAX_EXACT + (np.log(nf / np.float32(MAX_EXACT)) / np.float32(math.log(MAX_DISTANCE / MAX_EXACT))
                         * np.float32(N_BUCKETS - MAX_EXACT)).astype(np.int32)
    large = np.minimum(large, N_BUCKETS - 1)
    return np.where(n < MAX_EXACT, n, large).astype(np.int32)


def _in_proj_columns():
    o = {}
    off = 0
    for name, w in (("c_a", 256), ("c_b", 256), ("c_gate", 256), ("l_q", 128), ("l_k", 128), ("l_v", 256),
                    ("l_gk", 16), ("l_gate", 256), ("n_q", 512), ("n_cmp", 256), ("n_slc", 256),
                    ("n_win", 256), ("n_g", 24), ("n_gate", 512)):
        o[name] = off
        off += w
    cols = -np.ones((N_IN,), np.int64)

    def put(dst, name, width):
        cols[dst:dst + width] = o[name] + np.arange(width)

    put(0, "c_a", 256); put(256, "c_b", 256); put(512, "c_gate", 256)
    g = C_GLA
    put(g, "l_q", 128); put(g + 128, "l_k", 128); put(g + 256, "l_v", 256); put(g + 512, "l_gate", 256)
    put(g + 768, "l_gk", 16)
    n = C_NSA
    put(n, "n_q", 512); put(n + 512, "n_cmp", 256); put(n + 768, "n_slc", 256); put(n + 1024, "n_win", 256)
    cols[n + 1280:n + 1792] = o["n_gate"] + _nsa_perm()
    put(n + 1792, "n_g", 24)
    return cols


def _nsa_perm():
    p = np.zeros((W_NSA,), np.int64)
    for j in range(HPG):
        for g in range(G_KV):
            p[128 * j + 64 * g:128 * j + 64 * g + 64] = 64 * (HPG * g + j) + np.arange(64)
    return p


def _blockdiag_ones(n, blk):
    i = np.arange(n)
    return (i[:, None] // blk == i[None, :] // blk).astype(np.float32)


def _split2(x):
    hi = x.astype(BF16)
    lo = (x - hi.astype(F32)).astype(BF16)
    return hi, lo


def _dot(a, b):
    return jnp.dot(a, b, preferred_element_type=F32)


def _dot_nt(a, b):
    return lax.dot_general(a, b, (((1,), (1,)), ((), ())), preferred_element_type=F32)


def _dot_tn(a, b):
    return lax.dot_general(a, b, (((0,), (0,)), ((), ())), preferred_element_type=F32)


def _dot_split(x, m_bf16):
    hi, lo = _split2(x)
    return _dot(hi, m_bf16) + _dot(lo, m_bf16)


def _seg_rms(x, ind_ref, seg):
    return lax.rsqrt(_dot_split(x * x, ind_ref[...]) * (1.0 / seg) + EPS)


def _sigmoid(x):
    return 1.0 / (1.0 + jnp.exp(-x))


def _silu(x):
    return x * _sigmoid(x)


def _lane(shape):
    return lax.broadcasted_iota(jnp.int32, shape, len(shape) - 1)


def _half_pad(x, odd):
    if odd:
        x = pltpu.roll(x, 64, axis=1)
    return jnp.where(_lane(x.shape) < 64, x, 0.0)


def _inproj_body(x_ref, ng_ref, w_ref, indq_ref, indk_ref, qg_ref, kg_ref, *outs, tm, seq_len, prompt, n_chained):
    outs = outs[n_chained:]
    zc_ref, zg_ref, qn_ref, cmp_ref, slc_ref, win_ref, gate_ref, g_ref = outs[:8]
    x = x_ref[...]
    ms = jnp.mean(x * x, axis=-1, keepdims=True)
    h = (x * lax.rsqrt(ms + EPS) * ng_ref[...]).astype(BF16)

    def mm(lo, width):
        return _dot(h, w_ref[:, lo:lo + width])

    zc_ref[...] = mm(C_CONV, 3 * W_CONV)
    zg_ref[...] = mm(C_GLA, N_GLA)
    q = mm(C_NSA, W_NSA)
    qn_ref[...] = (q * _seg_rms(q, indq_ref, HD) * qg_ref[...]).astype(BF16)
    cmp = mm(C_NSA + 512, 256)
    cmp_ref[...] = cmp
    slc = mm(C_NSA + 768, 256)
    win = mm(C_NSA + 1024, 256)
    ks = slc[:, 0:128]
    ks = ks * _seg_rms(ks, indk_ref, HD) * kg_ref[0:1, :]
    kw = win[:, 0:128]
    kw = kw * _seg_rms(kw, indk_ref, HD) * kg_ref[1:2, :]
    slc_ref[:, 0:128] = ks
    slc_ref[:, 128:256] = slc[:, 128:256]
    win_ref[:, 0:128] = kw
    win_ref[:, 128:256] = win[:, 128:256]
    gate_ref[...] = mm(C_NSA + 1280, W_NSA)
    g_ref[...] = mm(C_NSA + 1792, LANES)
    if prompt:
        ksa_ref, vs_ref, kwa_ref, vw_ref, cmpt_ref, slct_ref = outs[8:]
        cmpt_ref[...] = cmp.T.reshape(2, G_KV, HD, tm)
        slct_ref[...] = jnp.concatenate([ks, slc[:, 128:256]], axis=1).T.reshape(2, G_KV, HD, tm)
        lane = _lane((tm, LANES))
        row = lax.broadcasted_iota(jnp.int32, (tm, LANES), 0)
        t = lax.rem(pl.program_id(0) * tm + row, seq_len)
        ones = ((lane == CONST_LANE) | (lane == CONST_LANE + 1)).astype(F32)
        onehot = ((lane - SEL_LANE0) == lax.shift_right_logical(t, 6)).astype(F32)
        for g in range(G_KV):
            ksa_ref[g] = (_half_pad(ks, g == 1) + onehot + ones).astype(BF16)
            kwa_ref[g] = (_half_pad(kw, g == 1) + ones).astype(BF16)
        feat = lax.broadcasted_iota(jnp.int32, (LANES, tm), 0)
        for v_ref, src in ((vs_ref, slc), (vw_ref, win)):
            vt = src[:, 128:256].T
            for g in range(G_KV):
                own = vt if g == 0 else pltpu.roll(vt, HD, axis=0)
                own = jnp.where(feat < HD, own, (feat == HD).astype(F32))[0:V_ROWS].astype(BF16)
                for kt in range(tm // TQ):
                    v_ref[kt, g] = own[:, kt * TQ:(kt + 1) * TQ]


def _in_proj(x2d, norm_g, w, indq, indk, qg, kg, *, seq_len, prompt, layer=0, depth=1, chain=()):
    m = x2d.shape[0]
    tm = min(512, m)
    assert m % tm == 0 and (not prompt or (seq_len % tm == 0 and tm % TQ == 0))
    row = lambda n: pl.BlockSpec((tm, n), lambda i: (i, 0))
    full = lambda a: pl.BlockSpec(a.shape, lambda i: (0,) * a.ndim)
    out_shape = [jax.ShapeDtypeStruct((m, 3 * W_CONV), F32), jax.ShapeDtypeStruct((m, N_GLA), F32),
                 jax.ShapeDtypeStruct((m, W_NSA), BF16), jax.ShapeDtypeStruct((m, 256), F32),
                 jax.ShapeDtypeStruct((m, 256), F32), jax.ShapeDtypeStruct((m, 256), F32),
                 jax.ShapeDtypeStruct((m, W_NSA), F32), jax.ShapeDtypeStruct((m, LANES), F32)]
    out_specs = [row(3 * W_CONV), row(N_GLA), row(W_NSA), row(256), row(256), row(256), row(W_NSA), row(LANES)]
    if prompt:
        aug = pl.BlockSpec((G_KV, tm, LANES), lambda i: (0, i, 0))
        vt = pl.BlockSpec((tm // TQ, G_KV, V_ROWS, TQ), lambda i: (i, 0, 0, 0))
        vt_shape = jax.ShapeDtypeStruct((m // TQ, G_KV, V_ROWS, TQ), BF16)
        per_seq = seq_len // tm
        leaf = pl.BlockSpec((None, None, 2, G_KV, HD, tm), lambda i: (layer, i // per_seq, 0, 0, 0, i % per_seq))
        leaf_shape = jax.ShapeDtypeStruct((depth, m // seq_len, 2, G_KV, HD, seq_len), F32)
        out_shape += [jax.ShapeDtypeStruct((G_KV, m, LANES), BF16), vt_shape,
                      jax.ShapeDtypeStruct((G_KV, m, LANES), BF16), vt_shape, leaf_shape, leaf_shape]
        out_specs += [aug, vt, aug, vt, leaf, leaf]
    args = [x2d, norm_g, w, indq, indk, qg, kg]
    return pl.pallas_call(
        functools.partial(_inproj_body, tm=tm, seq_len=seq_len, prompt=prompt, n_chained=len(chain)),
        grid=(m // tm,),
        in_specs=[row(D_MODEL), full(norm_g), full(w), full(indq), full(indk), full(qg), full(kg)]
        + [pl.BlockSpec(memory_space=pl.ANY)] * len(chain),
        out_specs=out_specs, out_shape=out_shape,
        input_output_aliases={len(args) + k: 12 + k for k in range(len(chain))},
        compiler_params=pltpu.CompilerParams(dimension_semantics=("arbitrary",), vmem_limit_bytes=VMEM_LIMIT),
        name="in_proj",
    )(*args, *chain)


def _outproj_body(x_ref, yc_ref, yg_ref, yn_ref, w_ref, nf_ref, o_ref, *, final):
    x = x_ref[...]
    x = (x + _dot(yc_ref[...], w_ref[0:256, :]) + _dot(yg_ref[...], w_ref[256:512, :])
         + _dot(yn_ref[...].astype(BF16), w_ref[512:1024, :]))
    if final:
        ms = jnp.mean(x * x, axis=-1, keepdims=True)
        x = x * lax.rsqrt(ms + EPS) * nf_ref[...]
    o_ref[...] = x


def _out_proj(x2d, yc, yg, yn, w, norm_f, *, final):
    m = x2d.shape[0]
    tm = min(512, m)
    assert m % tm == 0
    row = lambda n: pl.BlockSpec((tm, n), lambda i: (i, 0))
    full = lambda a: pl.BlockSpec(a.shape, lambda i: (0,) * a.ndim)
    return pl.pallas_call(
        functools.partial(_outproj_body, final=final),
        grid=(m // tm,),
        in_specs=[row(D_MODEL), row(256), row(256), row(512), full(w), full(norm_f)],
        out_specs=row(D_MODEL), out_shape=jax.ShapeDtypeStruct((m, D_MODEL), F32),
        compiler_params=pltpu.CompilerParams(dimension_semantics=("arbitrary",), vmem_limit_bytes=VMEM_LIMIT),
        name="out_proj",
    )(x2d, yc, yg, yn, w, norm_f)


def _conv_body(zc_ref, hist_ref, wdw_ref, bdw_ref, lng_ref, lnb_ref, wpw_ref, bpw_ref, y_ref, st_ref,
               ext_scr, act_scr, *, nb, t_len, tc, tmm):
    n_chunks = t_len // tc

    def per_batch(bi, _):
        base = bi * t_len
        ext_scr[0:32, :] = jnp.zeros((32, W_CONV), F32)
        ext_scr[2:32, :] = hist_ref[bi]

        def chunk(c, _):
            r0 = pl.multiple_of(c * tc, tc)
            g0 = pl.multiple_of(base + r0, tc)
            a = zc_ref[pl.ds(g0, tc), 0:256]
            b = zc_ref[pl.ds(g0, tc), 256:512]
            ext_scr[pl.ds(32 + r0, tc), :] = a * _sigmoid(b)
            win = ext_scr[pl.ds(r0, tc + 32), :]
            acc = jnp.zeros((tc, W_CONV), F32) + bdw_ref[...]
            for r in range(8):
                wr = pltpu.roll(win, tc + 32 - (2 + r), axis=0)
                for mi, j in enumerate(range(r, CONV_WIDTH, 8)):
                    acc = acc + wr[8 * mi:8 * mi + tc] * wdw_ref[j:j + 1, :]
            mu = jnp.mean(acc, axis=-1, keepdims=True)
            d = acc - mu
            var = jnp.mean(d * d, axis=-1, keepdims=True)
            yn = d * lax.rsqrt(var + EPS) * lng_ref[...] + lnb_ref[...]
            act_scr[pl.ds(g0, tc), :] = _silu(yn)
            return 0

        lax.fori_loop(0, n_chunks, chunk, 0)
        st_ref[bi] = ext_scr[2 + t_len:32 + t_len, :]
        return 0

    lax.fori_loop(0, nb, per_batch, 0)

    def mm(c, _):
        r0 = pl.multiple_of(c * tmm, tmm)
        y = _dot(act_scr[pl.ds(r0, tmm), :].astype(BF16), wpw_ref[...]) + bpw_ref[...]
        y_ref[pl.ds(r0, tmm), :] = (y * _silu(zc_ref[pl.ds(r0, tmm), 512:768])).astype(BF16)
        return 0

    lax.fori_loop(0, nb * t_len // tmm, mm, 0)


def _conv(zc, hist, wdw, bdw, lng, lnb, wpw, bpw, *, nb, t_len):
    m = zc.shape[0]
    n_b = m // t_len
    tc = min(64, t_len)
    tmm = min(256, nb * t_len)
    full = lambda a: pl.BlockSpec(a.shape, lambda i: (0,) * a.ndim)
    return pl.pallas_call(
        functools.partial(_conv_body, nb=nb, t_len=t_len, tc=tc, tmm=tmm),
        grid=(n_b // nb,),
        in_specs=[pl.BlockSpec((nb * t_len, 3 * W_CONV), lambda i: (i, 0)),
                  pl.BlockSpec((nb, HIST, W_CONV), lambda i: (i, 0, 0)),
                  full(wdw), full(bdw), full(lng), full(lnb), full(wpw), full(bpw)],
        out_specs=[pl.BlockSpec((nb * t_len, W_CONV), lambda i: (i, 0)),
                   pl.BlockSpec((nb, HIST, W_CONV), lambda i: (i, 0, 0))],
        out_shape=[jax.ShapeDtypeStruct((m, W_CONV), BF16), jax.ShapeDtypeStruct((n_b, HIST, W_CONV), F32)],
        scratch_shapes=[pltpu.VMEM((32 + t_len, W_CONV), F32), pltpu.VMEM((nb * t_len, W_CONV), F32)],
        compiler_params=pltpu.CompilerParams(dimension_semantics=("arbitrary",), vmem_limit_bytes=VMEM_LIMIT),
        name="conv",
    )(zc, hist, wdw, bdw, lng, lnb, wpw, bpw)


def _gla_body(zg_ref, s0_ref, wgk_ref, bgk_ref, ltri_ref, lsum_ref, ind_ref, indv_ref, ng_ref, bd_ref,
              y_ref, st_ref, bc_scr, qe_scr, ke_scr, dec_scr, o_scr, *, nb, t_len, chunk, rb):
    rows = nb * t_len
    n_rb = rows // rb
    cpb = t_len // chunk

    def phase1(i, _):
        r0 = pl.multiple_of(i * rb, rb)
        gk = zg_ref[pl.ds(r0, rb), 768:896]
        pre = _dot_split(gk, wgk_ref[0]) + _dot(gk.astype(BF16), wgk_ref[1]) + bgk_ref[...]
        la = (jnp.minimum(pre, 0.0) - jnp.log(1.0 + jnp.exp(-jnp.abs(pre)))) * (1.0 / GATE_NORMALIZER)
        h1, l1 = _split2(la)
        l2 = (la - h1.astype(F32) - l1.astype(F32)).astype(BF16)
        bc = _dot(ltri_ref[...], h1) + _dot(ltri_ref[...], l1) + _dot(ltri_ref[...], l2)
        bt = _dot(lsum_ref[...], h1) + _dot(lsum_ref[...], l1) + _dot(lsum_ref[...], l2)
        bc_scr[pl.ds(r0, rb), :] = bc
        qe_scr[pl.ds(r0, rb), :] = zg_ref[pl.ds(r0, rb), 0:128] * (DK_GLA ** -0.5) * jnp.exp(bc)
        ke_scr[pl.ds(r0, rb), :] = zg_ref[pl.ds(r0, rb), 128:256] * jnp.exp(bt - bc)
        dec_scr[pl.ds(r0, rb), :] = jnp.exp(bt)
        return 0

    lax.fori_loop(0, n_rb, phase1, 0)

    ti = lax.broadcasted_iota(jnp.int32, (chunk, chunk, LANES), 0)
    si = lax.broadcasted_iota(jnp.int32, (chunk, chunk, LANES), 1)
    causal = si <= ti

    u2 = 4 if (rows // chunk) % 4 == 0 else 1
    u3 = 16 if cpb % 16 == 0 else 1

    def phase2(cg, _):
        for u in range(u2):
            r0 = pl.multiple_of((cg * u2 + u) * chunk, chunk)
            bc = bc_scr[pl.ds(r0, chunk), :]
            q = zg_ref[pl.ds(r0, chunk), 0:128] * (DK_GLA ** -0.5)
            k = zg_ref[pl.ds(r0, chunk), 128:256]
            v = zg_ref[pl.ds(r0, chunk), 256:512]
            e = jnp.exp(jnp.where(causal, bc[:, None, :] - bc[None, :, :], NEG))
            p = (q[:, None, :] * k[None, :, :] * e).reshape(chunk * chunk, LANES)
            att = _dot(p.astype(BF16), ind_ref[...]).reshape(chunk, chunk, W_GLA)
            o_scr[pl.ds(r0, chunk), :] = (att * v[None, :, :]).sum(axis=1)
        return 0

    lax.fori_loop(0, rows // chunk // u2, phase2, 0)

    def per_batch(bi, _):
        def group(cg, s):
            for u in range(u3):
                r0 = pl.multiple_of(bi * t_len + (cg * u3 + u) * chunk, chunk)
                o_scr[pl.ds(r0, chunk), :] += _dot_nt(qe_scr[pl.ds(r0, chunk), :].astype(BF16), s.astype(BF16))
                upd = _dot_tn(zg_ref[pl.ds(r0, chunk), 256:512].astype(BF16),
                              ke_scr[pl.ds(r0, chunk), :].astype(BF16))
                s = s * dec_scr[pl.ds(r0, 1), :] + upd * bd_ref[...]
            return s

        st_ref[bi] = lax.fori_loop(0, cpb // u3, group, s0_ref[bi])
        return 0

    lax.fori_loop(0, nb, per_batch, 0)

    def phase4(i, _):
        r0 = pl.multiple_of(i * rb, rb)
        o = o_scr[pl.ds(r0, rb), :]
        o = o * _seg_rms(o, indv_ref, DV_GLA) * ng_ref[...]
        y_ref[pl.ds(r0, rb), :] = (o * _silu(zg_ref[pl.ds(r0, rb), 512:768])).astype(BF16)
        return 0

    lax.fori_loop(0, n_rb, phase4, 0)


def _gla(zg, s0t, wgk, bgk, ng, *, nb, t_len, chunk):
    m = zg.shape[0]
    n_b = m // t_len
    rows = nb * t_len
    rb = min(128, rows)
    assert rows % rb == 0 and rb % chunk == 0 and t_len % chunk == 0
    ltri = jnp.asarray(_blockdiag_ones(rb, chunk) * np.tril(np.ones((rb, rb), np.float32)), BF16)
    lsum = jnp.asarray(_blockdiag_ones(rb, chunk), BF16)
    i128 = np.arange(LANES)
    i256 = np.arange(W_GLA)
    ind = jnp.asarray((i128[:, None] // DK_GLA == i256[None, :] // DV_GLA).astype(np.float32), BF16)
    indv = jnp.asarray(_blockdiag_ones(W_GLA, DV_GLA), BF16)
    bd = jnp.asarray((i256[:, None] // DV_GLA == i128[None, :] // DK_GLA).astype(np.float32))
    full = lambda a: pl.BlockSpec(a.shape, lambda i: (0,) * a.ndim)
    return pl.pallas_call(
        functools.partial(_gla_body, nb=nb, t_len=t_len, chunk=chunk, rb=rb),
        grid=(n_b // nb,),
        in_specs=[pl.BlockSpec((rows, N_GLA), lambda i: (i, 0)),
                  pl.BlockSpec((nb, W_GLA, LANES), lambda i: (i, 0, 0)),
                  full(wgk), full(bgk), full(ltri), full(lsum), full(ind), full(indv), full(ng), full(bd)],
        out_specs=[pl.BlockSpec((rows, W_GLA), lambda i: (i, 0)),
                   pl.BlockSpec((nb, W_GLA, LANES), lambda i: (i, 0, 0))],
        out_shape=[jax.ShapeDtypeStruct((m, W_GLA), BF16), jax.ShapeDtypeStruct((n_b, W_GLA, LANES), F32)],
        scratch_shapes=[pltpu.VMEM((rows, LANES), F32), pltpu.VMEM((rows, LANES), F32),
                        pltpu.VMEM((rows, LANES), F32), pltpu.VMEM((rows, LANES), F32),
                        pltpu.VMEM((rows, W_GLA), F32)],
        compiler_params=pltpu.CompilerParams(dimension_semantics=("arbitrary",), vmem_limit_bytes=VMEM_LIMIT),
        name="gla",
    )(zg, s0t, wgk, bgk, ltri, lsum, ind, indv, ng, bd)


KW = CMP_BLOCK * 256


def _compress_rows(x_ref, w_ref, pe_ref, kvc_ref, n_rows):
    kvc_ref[...] = jnp.zeros(kvc_ref.shape, F32)
    for par in range(2):
        x = (x_ref[:, par * KW:(par + 1) * KW] + pe_ref[...]).astype(BF16)
        kvc_ref[ODD_SLOT0 * par:ODD_SLOT0 * par + n_rows, :] = _dot(x, w_ref[...])


def _compress_prompt_body(x_ref, w_ref, pe_ref, kvc_ref, *, n_rows):
    _compress_rows(x_ref, w_ref, pe_ref, kvc_ref, n_rows)


def _compress_prompt(cmp_rows, w, pe, *, n_b, t_len):
    n_rows = t_len // SEL_BLOCK
    full = lambda a: pl.BlockSpec(a.shape, lambda i: (0,) * a.ndim)
    return pl.pallas_call(
        functools.partial(_compress_prompt_body, n_rows=n_rows),
        grid=(n_b,),
        in_specs=[pl.BlockSpec((n_rows, 2 * KW), lambda i: (i, 0)), full(w), full(pe)],
        out_specs=pl.BlockSpec((None, LANES, 256), lambda i: (i, 0, 0)),
        out_shape=jax.ShapeDtypeStruct((n_b, LANES, 256), F32),
        compiler_params=pltpu.CompilerParams(dimension_semantics=("arbitrary",), vmem_limit_bytes=VMEM_LIMIT),
        name="compress_prompt",
    )(cmp_rows.reshape(n_b * n_rows, 2 * KW), w, pe)


def _compress_paged_body(pt_ref, *refs, n_pages, t_new, n_rows):
    pages = refs[:DEC_NB * n_pages]
    new_ref, w_ref, pe_ref, kvc_ref, x_scr, acc_scr = refs[DEC_NB * n_pages:]
    per_page = PAGE_SIZE // CMP_BLOCK
    n_past = n_pages * per_page
    n_cmp = 2 * n_rows
    for bi in range(DEC_NB):
        new = new_ref[bi * t_new:(bi + 1) * t_new, :]
        for s in range(2):
            for p in range(n_pages):
                rows = pages[bi * n_pages + p][s].reshape(LANES, PAGE_SIZE).T
                for c in range(per_page):
                    r0 = X_PITCH * (per_page * p + c)
                    x_scr[bi, s, r0:r0 + CMP_BLOCK, :] = rows[CMP_BLOCK * c:CMP_BLOCK * (c + 1)]
            x_scr[bi, s, X_PITCH * n_past:, :] = jnp.zeros((X_PITCH * (n_cmp - n_past), LANES), F32)
            x_scr[bi, s, X_PITCH * n_past:X_PITCH * n_past + t_new, :] = new[:, LANES * s:LANES * (s + 1)]
    kvc_ref[...] = jnp.zeros(kvc_ref.shape, F32)
    for s in range(2):
        acc = jnp.zeros((DEC_NB * n_cmp, LANES), F32)
        for j in range(CMP_BLOCK):
            x = jnp.concatenate([x_scr[bi, s, pl.ds(j, n_cmp, stride=X_PITCH), :] for bi in range(DEC_NB)], axis=0)
            acc = acc + _dot((x + pe_ref[s, j:j + 1, :]).astype(BF16), w_ref[s, j])
        acc_scr[...] = acc
        for bi in range(DEC_NB):
            for par in range(2):
                kvc_ref[bi, ODD_SLOT0 * par:ODD_SLOT0 * par + n_rows, LANES * s:LANES * (s + 1)] = (
                    acc_scr[pl.ds(bi * n_cmp + par, n_rows, stride=2), :])


def _paged_specs(layer, n_pages, block):
    return [pl.BlockSpec((None, None) + block, functools.partial(
        lambda b, pt, bi, p: (layer, pt[DEC_NB * b + bi, p]) + (0,) * len(block), bi=bi, p=p))
        for bi in range(DEC_NB) for p in range(n_pages)]


PAGE_BLOCK = (2, G_KV, HD, PAGE_SIZE)
X_PITCH = CMP_BLOCK + 4
DEC_NB = 2


def _compress_paged(page_table, cache_t, layer, cmp_new, w, pe, *, t_new):
    n_b, n_pages = page_table.shape
    assert n_b % DEC_NB == 0
    n_rows = -(-((n_pages * PAGE_SIZE + SEL_BLOCK) // SEL_BLOCK) // 8) * 8
    full = lambda a: pl.BlockSpec(a.shape, lambda b, pt: (0,) * a.ndim)
    grid_spec = pltpu.PrefetchScalarGridSpec(
        num_scalar_prefetch=1, grid=(n_b // DEC_NB,),
        in_specs=_paged_specs(layer, n_pages, PAGE_BLOCK) + [
            pl.BlockSpec((DEC_NB * t_new, 256), lambda b, pt: (b, 0)), full(w), full(pe)],
        out_specs=pl.BlockSpec((DEC_NB, LANES, 256), lambda b, pt: (b, 0, 0)),
        scratch_shapes=[pltpu.VMEM((DEC_NB, 2, 2 * n_rows * X_PITCH, LANES), F32),
                        pltpu.VMEM((DEC_NB * 2 * n_rows, LANES), F32)])
    return pl.pallas_call(
        functools.partial(_compress_paged_body, n_pages=n_pages, t_new=t_new, n_rows=n_rows),
        grid_spec=grid_spec, out_shape=jax.ShapeDtypeStruct((n_b, LANES, 256), F32),
        compiler_params=pltpu.CompilerParams(dimension_semantics=("arbitrary",), vmem_limit_bytes=VMEM_LIMIT),
        name="compress_paged",
    )(page_table, *([cache_t] * (DEC_NB * n_pages)), cmp_new, w, pe)


def _cmpsel_body(qn_ref, kvc_ref, bias_ref, sbase_ref, useimp_ref, indk_ref, kg_ref, cvec_ref,
                 *outs, nb, tb, n_blocks, q_t):
    q_refs, ocmp_ref = outs[:-1], outs[-1]
    rows = nb * tb
    lane = _lane((rows, LANES))
    qn = qn_ref[...].astype(F32)
    qpad = [_half_pad(qn[:, 128 * (h // 2):128 * (h // 2) + 128], h % 2 == 1) for h in range(H_NSA)]
    imp_parts = []
    o_parts = [[None] * nb for _ in range(H_NSA)]
    kc_all = kvc_ref[:, :, 0:128].reshape(nb * LANES, LANES)
    kc_all = kc_all * _seg_rms(kc_all, indk_ref, HD) * kg_ref[...]
    for bi in range(nb):
        kc = kc_all[bi * LANES:(bi + 1) * LANES]
        vboth = kvc_ref[bi, :, 128:256].astype(BF16)
        imp_b = []
        for g in range(G_KV):
            kpad = _half_pad(kc, g == 1).astype(BF16)
            qs = jnp.concatenate([qpad[HPG * g + j][bi * tb:(bi + 1) * tb] for j in range(HPG)], axis=0).astype(BF16)
            bias = bias_ref[g]
            l = _dot_nt(qs, kpad) + bias
            vis = bias > 0.5 * NEG
            mx = jnp.max(l, axis=-1, keepdims=True)
            p = jnp.where(vis, jnp.exp(l - mx), 0.0)
            p = p / jnp.maximum(jnp.sum(p, axis=-1, keepdims=True), 1e-30)
            o = _dot(p.astype(BF16), vboth)
            ig = p[0:tb]
            for j in range(1, HPG):
                ig = ig + p[j * tb:(j + 1) * tb]
            imp_b.append(ig)
            for j in range(HPG):
                o_parts[HPG * g + j][bi] = o[j * tb:(j + 1) * tb]
        imp_parts.append(imp_b)
    for j in range(HPG):
        o0 = jnp.concatenate(o_parts[j], axis=0) if nb > 1 else o_parts[j][0]
        o1 = jnp.concatenate(o_parts[HPG + j], axis=0) if nb > 1 else o_parts[HPG + j][0]
        ocmp_ref[:, 128 * j:128 * j + 128] = jnp.where(lane < 64, o0, o1)
    for g in range(G_KV):
        ig = jnp.concatenate([imp_parts[bi][g] for bi in range(nb)], axis=0) if nb > 1 else imp_parts[0][g]
        imp = ig + pltpu.roll(ig, LANES - ODD_SLOT0, axis=1)
        score = jnp.where(useimp_ref[...] > 0.5, imp, sbase_ref[...])
        st = score.T[0:SEL_SLOTS]
        jj = lax.broadcasted_iota(jnp.int32, (SEL_SLOTS, rows), 0)
        rank = jnp.zeros((SEL_SLOTS, rows), F32)
        for jp in range(n_blocks):
            r = st[jp:jp + 1, :]
            before = (r > st) | ((r == st) & (jp < jj))
            rank = rank + before.astype(F32)
        selb = jnp.where(rank < TOP_N - 0.5, 0.0, NEG)
        full_t = jnp.concatenate([jnp.zeros((SEL_LANE0, rows), F32), selb,
                                  jnp.zeros((LANES - SEL_LANE0 - SEL_SLOTS, rows), F32)], axis=0)
        if q_t:
            for j in range(HPG):
                h = HPG * g + j
                q_refs[j][g] = ((qpad[h] + cvec_ref[h]).T + full_t).astype(BF16)
        else:
            extra = full_t.T
            for j in range(HPG):
                h = HPG * g + j
                q_refs[0][g, j] = (qpad[h] + extra + cvec_ref[h]).astype(q_refs[0].dtype)


def _cmpsel(qn, kvc, bias, sbase, useimp, indk, kg0, cvec, *, nb, tb, n_blocks, q_t):
    m = qn.shape[0]
    rows = nb * tb
    assert rows == LANES and m % rows == 0
    n_pos = bias.shape[0]
    full = lambda a: pl.BlockSpec(a.shape, lambda i: (0,) * a.ndim)
    if q_t:
        q_specs = [pl.BlockSpec((G_KV, LANES, rows), lambda i: (0, 0, i))] * HPG
        q_shapes = [jax.ShapeDtypeStruct((G_KV, LANES, m), BF16)] * HPG
    else:
        q_specs = [pl.BlockSpec((G_KV, HPG, rows, LANES), lambda i: (0, 0, i, 0))]
        q_shapes = [jax.ShapeDtypeStruct((G_KV, HPG, m, LANES), F32)]
    return pl.pallas_call(
        functools.partial(_cmpsel_body, nb=nb, tb=tb, n_blocks=n_blocks, q_t=q_t),
        grid=(m // rows,),
        in_specs=[pl.BlockSpec((rows, W_NSA), lambda i: (i, 0)),
                  pl.BlockSpec((nb, LANES, 256), (lambda i: (i, 0, 0)) if nb > 1 else (lambda i: (i // n_pos, 0, 0))),
                  pl.BlockSpec((None, G_KV, HPG * tb, LANES), lambda i: (i % n_pos, 0, 0, 0)),
                  pl.BlockSpec((None, rows, LANES), lambda i: (i % n_pos, 0, 0)),
                  pl.BlockSpec((None, rows, LANES), lambda i: (i % n_pos, 0, 0)),
                  full(indk), full(kg0), full(cvec)],
        out_specs=q_specs + [pl.BlockSpec((rows, W_NSA), lambda i: (i, 0))],
        out_shape=q_shapes + [jax.ShapeDtypeStruct((m, W_NSA), F32)],
        compiler_params=pltpu.CompilerParams(dimension_semantics=("arbitrary",), vmem_limit_bytes=VMEM_LIMIT),
        name="cmpsel",
    )(qn, kvc, bias, sbase, useimp, indk, kg0, cvec)


def _mix_gates(g_ref, bg_ref, expand_ref, gate_ref, ocmp_ref, o_slc, o_win, y_ref):
    gl = _sigmoid(g_ref[...] + bg_ref[...])
    gx = _dot_split(gl, expand_ref[...])
    for j in range(HPG):
        sl = slice(128 * j, 128 * j + 128)
        o = (gx[:, 128 * j:128 * j + 128] * ocmp_ref[:, sl]
             + gx[:, 128 * (HPG + j):128 * (HPG + j) + 128] * o_slc[j]
             + gx[:, 128 * (2 * HPG + j):128 * (2 * HPG + j) + 128] * o_win[j])
        y_ref[:, sl] = (o * _silu(gate_ref[:, sl])).astype(y_ref.dtype)


def _softmax_step(q_t, k, v_t, bias_t, m_scr, acc_scr):
    n_key, n_col = k.shape[0], q_t.shape[1]
    for k0 in range(0, n_key, KEY_CHUNK):
        ks = slice(k0, k0 + KEY_CHUNK)
        for c0 in range(0, n_col, COL_CHUNK):
            cs = slice(c0, c0 + COL_CHUNK)
            s = _dot(k[ks], q_t[:, cs])
            if bias_t is not None:
                s = s + bias_t(ks, cs)
            m_prev = m_scr[:, cs]
            m_new = jnp.maximum(m_prev, jnp.max(s, axis=0, keepdims=True))
            p = jnp.exp(s - m_new)
            acc_scr[:, cs] = jnp.exp(m_prev - m_new) * acc_scr[:, cs] + _dot(v_t[:, ks], p.astype(BF16))
            m_scr[:, cs] = m_new


def _flash_body(q0_ref, q1_ref, q2_ref, q3_ref, ksa_ref, vs_ref, kwa_ref, vw_ref, dt_ref, edge_ref, ocmp_ref,
                g_ref, gate_ref, bg_ref, expand_ref, y_ref, m_scr, acc_scr):
    i = pl.program_id(1)
    cols = HPG * TQ
    m_scr[...] = jnp.full(m_scr.shape, NEG, F32)
    acc_scr[...] = jnp.zeros(acc_scr.shape, F32)

    def run(n_near):
        for g in range(G_KV):
            q_t = jnp.concatenate([r[g] for r in (q0_ref, q1_ref, q2_ref, q3_ref)], axis=1)
            for br, (ka_ref, v_ref) in enumerate(((ksa_ref, vs_ref), (kwa_ref, vw_ref))):
                def step(kt, bias_t):
                    k0 = pl.multiple_of(kt * TQ, TQ)
                    _softmax_step(q_t, ka_ref[g, pl.ds(k0, TQ), :], v_ref[kt, g], bias_t, m_scr.at[br, g],
                                  acc_scr.at[br, g])

                if n_near == 2:
                    if br == 0:
                        def far_pair(kp, _):
                            step(2 * kp, None)
                            step(2 * kp + 1, None)
                            return 0

                        lax.fori_loop(0, lax.shift_right_logical(i - 1, 1), far_pair, 0)

                        @pl.when(lax.rem(i - 1, 2) == 1)
                        def _():
                            step(i - 2, None)
                    else:
                        step(i - 2, lambda ks, cs: edge_ref[ks, cs])
                if n_near >= 1:
                    step(i - 1, lambda ks, cs: dt_ref[1, g, ks, cs])
                step(i, lambda ks, cs: dt_ref[0, g, ks, cs])

    for n_near, cond in ((0, i == 0), (1, i == 1), (2, i >= 2)):
        pl.when(cond)(functools.partial(run, n_near))

    outs = []
    for br in range(2):
        o_t = jnp.concatenate([acc_scr[br, g, 0:HD, :] / acc_scr[br, g, HD:HD + 1, :] for g in range(G_KV)], axis=0)
        outs.append([o_t[:, j * TQ:(j + 1) * TQ].T for j in range(HPG)])
    _mix_gates(g_ref, bg_ref, expand_ref, gate_ref, ocmp_ref, outs[0], outs[1], y_ref)


def _flash(q_ts, ksa, vs, kwa, vw, dt, edge, ocmp, ng, ngate, bg, expand, *, n_b, t_len):
    m = n_b * t_len
    nq = t_len // TQ
    cols = HPG * TQ
    full = lambda a: pl.BlockSpec(a.shape, lambda b, i: (0,) * a.ndim)
    qspec = pl.BlockSpec((G_KV, LANES, TQ), lambda b, i: (0, 0, b * nq + i))
    kspec = pl.BlockSpec((G_KV, None, t_len, LANES), lambda b, i: (0, b, 0, 0))
    vspec = pl.BlockSpec((nq, G_KV, V_ROWS, TQ), lambda b, i: (b, 0, 0, 0))
    row = lambda n: pl.BlockSpec((TQ, n), lambda b, i: (b * nq + i, 0))
    return pl.pallas_call(
        _flash_body,
        grid=(n_b, nq),
        in_specs=[qspec] * HPG + [kspec, vspec, kspec, vspec, full(dt), full(edge), row(W_NSA), row(LANES),
                                  row(W_NSA), full(bg), full(expand)],
        out_specs=row(W_NSA), out_shape=jax.ShapeDtypeStruct((m, W_NSA), BF16),
        scratch_shapes=[pltpu.VMEM((2, G_KV, 1, cols), F32), pltpu.VMEM((2, G_KV, V_ROWS, cols), F32)],
        compiler_params=pltpu.CompilerParams(dimension_semantics=("arbitrary", "arbitrary"),
                                             vmem_limit_bytes=VMEM_LIMIT),
        name="flash",
    )(*q_ts, ksa.reshape(G_KV, n_b, t_len, LANES), vs, kwa.reshape(G_KV, n_b, t_len, LANES), vw,
      dt, edge, ocmp, ng, ngate, bg, expand)


def _dec_attn_body(pt_ref, *refs, n_pages, t_new, n_keys, n_wkeys):
    pages = refs[:DEC_NB * n_pages]
    (qa_ref, slc_ref, win_ref, wst_ref, dsl_ref, dwn_ref, ocmp_ref, g_ref, gate_ref, bg_ref, expand_ref, _wall,
     y_ref, wout_ref, ka_scr, va_scr, kw_scr, vw_scr) = refs[DEC_NB * n_pages:]
    past = n_pages * PAGE_SIZE
    w_past = wst_ref.shape[-1]
    rows = HPG * t_new

    @pl.when(pl.program_id(0) == 0)
    def _():
        for k_scr, n, with_blocks in ((ka_scr, n_keys, True), (kw_scr, n_wkeys, False)):
            feat = lax.broadcasted_iota(jnp.int32, (LANES, n), 0)
            key = lax.broadcasted_iota(jnp.int32, (LANES, n), 1)
            aug = (feat == CONST_LANE) | (feat == CONST_LANE + 1)
            if with_blocks:
                aug = aug | ((feat - SEL_LANE0) == lax.shift_right_logical(key, 6))
            for bi in range(DEC_NB):
                for g in range(G_KV):
                    k_scr[bi, g] = aug.astype(F32).astype(BF16)

    def new_keys_t(new):
        pad = jnp.concatenate([new, jnp.zeros((LANES - t_new, 256), F32)], axis=0)
        return pad[:, 0:128].T, pad[:, 128:256].T

    lane = _lane((HD, LANES))
    lane_o = _lane((t_new, LANES))
    outs = [[[None] * DEC_NB for _ in range(HPG)] for _ in range(2)]
    for bi in range(DEC_NB):
        tok = slice(bi * t_new, (bi + 1) * t_new)
        for p in range(n_pages):
            sl = slice(p * PAGE_SIZE, (p + 1) * PAGE_SIZE)
            for g in range(G_KV):
                ka_scr[bi, g, 0:HD, sl] = pages[bi * n_pages + p][0, g].astype(BF16)
                va_scr[bi, HD * g:HD * (g + 1), sl] = pages[bi * n_pages + p][1, g].astype(BF16)
        kn_t, vn_t = new_keys_t(slc_ref[tok, :])
        va_scr[bi, :, past:n_keys] = vn_t.astype(BF16)
        kwn_t, vwn_t = new_keys_t(win_ref[tok, :])
        vw_scr[bi, :, w_past:n_wkeys] = vwn_t.astype(BF16)
        for g in range(G_KV):
            ka_scr[bi, g, 0:HD, past:n_keys] = kn_t[HD * g:HD * (g + 1)].astype(BF16)
            kw_scr[bi, g, 0:HD, 0:w_past] = wst_ref[bi, 0, g].astype(BF16)
            kw_scr[bi, g, 0:HD, w_past:n_wkeys] = kwn_t[HD * g:HD * (g + 1)].astype(BF16)
            vw_scr[bi, HD * g:HD * (g + 1), 0:w_past] = wst_ref[bi, 1, g].astype(BF16)
            for s, new_t in ((0, kwn_t), (1, vwn_t)):
                sh = pltpu.roll(wst_ref[bi, s, g], w_past - t_new, axis=1)
                tail = pltpu.roll(new_t[HD * g:HD * (g + 1)], LANES - t_new, axis=1)
                wout_ref[bi, s, g, :, 0:w_past - LANES] = sh[:, 0:w_past - LANES]
                wout_ref[bi, s, g, :, w_past - LANES:w_past] = jnp.where(lane >= LANES - t_new, tail,
                                                                        sh[:, w_past - LANES:])
        res = [[None, None], [None, None]]
        for g in range(G_KV):
            q = qa_ref[g, :, tok, :].reshape(rows, LANES).astype(BF16)
            for br, (k_scr, v_scr, b_ref) in enumerate(((ka_scr, va_scr, dsl_ref), (kw_scr, vw_scr, dwn_ref))):
                s = _dot(q, k_scr[bi, g]) + b_ref[g]
                mx = jnp.max(s, axis=-1, keepdims=True)
                p = jnp.exp(s - mx)
                den = jnp.sum(p, axis=-1, keepdims=True)
                res[br][g] = _dot_nt(p.astype(BF16), v_scr[bi]) / den
        for br in range(2):
            for j in range(HPG):
                outs[br][j][bi] = jnp.where(lane_o < 64, res[br][0][j * t_new:(j + 1) * t_new],
                                            res[br][1][j * t_new:(j + 1) * t_new])
    outs = [[jnp.concatenate(outs[br][j], axis=0) for j in range(HPG)] for br in range(2)]
    _mix_gates(g_ref, bg_ref, expand_ref, gate_ref, ocmp_ref, outs[0], outs[1], y_ref)


def _dec_attn(page_table, cache, layer, qa, slc_new, win_new, win_state, dsl, dwn, ocmp, ng, ngate, bg, expand,
              win_all, *, t_new):
    n_b, n_pages = page_table.shape
    assert n_b % DEC_NB == 0
    m = n_b * t_new
    w_past = win_state.shape[-1]
    n_keys = dsl.shape[-1]
    n_wkeys = dwn.shape[-1]
    wblock = (2, G_KV, HD, w_past)
    full = lambda a: pl.BlockSpec(a.shape, lambda b, pt: (0,) * a.ndim)
    row = lambda n: pl.BlockSpec((DEC_NB * t_new, n), lambda b, pt: (b, 0))
    wspec = pl.BlockSpec((None, DEC_NB) + wblock, lambda b, pt: (layer, b, 0, 0, 0, 0))
    grid_spec = pltpu.PrefetchScalarGridSpec(
        num_scalar_prefetch=1, grid=(n_b // DEC_NB,),
        in_specs=_paged_specs(layer, n_pages, PAGE_BLOCK) + [
            pl.BlockSpec((G_KV, HPG, DEC_NB * t_new, LANES), lambda b, pt: (0, 0, b, 0)),
            row(256), row(256), wspec,
            full(dsl), full(dwn), row(W_NSA), row(LANES), row(W_NSA), full(bg), full(expand),
            pl.BlockSpec(memory_space=pl.ANY)],
        out_specs=[row(W_NSA), wspec],
        scratch_shapes=[pltpu.VMEM((DEC_NB, G_KV, LANES, n_keys), BF16), pltpu.VMEM((DEC_NB, LANES, n_keys), BF16),
                        pltpu.VMEM((DEC_NB, G_KV, LANES, n_wkeys), BF16), pltpu.VMEM((DEC_NB, LANES, n_wkeys), BF16)])
    args = [page_table, *([cache] * (DEC_NB * n_pages)), qa, slc_new, win_new, win_state, dsl, dwn, ocmp, ng, ngate,
            bg, expand, win_all]
    return pl.pallas_call(
        functools.partial(_dec_attn_body, n_pages=n_pages, t_new=t_new, n_keys=n_keys, n_wkeys=n_wkeys),
        grid_spec=grid_spec,
        out_shape=[jax.ShapeDtypeStruct((m, W_NSA), F32), jax.ShapeDtypeStruct(win_all.shape, F32)],
        input_output_aliases={len(args) - 1: 1},
        compiler_params=pltpu.CompilerParams(dimension_semantics=("arbitrary",), vmem_limit_bytes=VMEM_LIMIT),
        name="dec_attn",
    )(*args)


def _bucket_matrix(qpos, kpos, valid):
    dist = qpos[:, None] - kpos[None, :]
    return np.where(valid & (dist >= 0), _rel_bucket_np(dist), -1).astype(np.int32)


def _bias_body(tab_ref, bk_ref, o_ref, *, sub_far):
    bk = bk_ref[...]
    for h in range(H_NSA):
        far = tab_ref[N_BUCKETS - 1, h] if sub_far else 0.0
        acc = jnp.full(bk.shape, NEG, F32)
        for b in range(N_BUCKETS):
            acc = jnp.where(bk == b, tab_ref[b, h] - far, acc)
        o_ref[h] = acc


def _bias_table(table, bk, *, sub_far):
    r, c = bk.shape
    tr = 8 if r <= 8 else 64
    assert r % tr == 0
    return pl.pallas_call(
        functools.partial(_bias_body, sub_far=sub_far),
        grid=(r // tr,),
        in_specs=[pl.BlockSpec(memory_space=pltpu.SMEM), pl.BlockSpec((tr, c), lambda i: (i, 0))],
        out_specs=pl.BlockSpec((H_NSA, tr, c), lambda i: (0, i, 0)),
        out_shape=jax.ShapeDtypeStruct((H_NSA, r, c), F32),
        compiler_params=pltpu.CompilerParams(dimension_semantics=("arbitrary",)),
        name="bias_table",
    )(table, jnp.asarray(bk))


def _cmp_tables(qpos_blocks, n_blocks):
    slot = np.arange(LANES)
    blk = np.where(slot < ODD_SLOT0, 2 * slot, 2 * (slot - ODD_SLOT0) + 1)
    real = (blk < 2 * n_blocks) & ((slot % ODD_SLOT0) < n_blocks)
    blk_end = blk * CMP_BLOCK + CMP_BLOCK - 1
    buckets, sbases, useimps = [], [], []
    for qpos in qpos_blocks:
        buckets.append(_bucket_matrix(qpos, blk_end, real[None, :]))
        j = np.arange(LANES)[None, :]
        cur = (qpos // SEL_BLOCK)[:, None]
        forced = (j == 0) | (j == cur) | (j == cur - 1)
        valid = j * SEL_BLOCK <= qpos[:, None]
        inrange = j < n_blocks
        sbases.append(np.where(~inrange, -2.0, np.where(forced, FORCE, -1.0)).astype(np.float32))
        useimps.append((inrange & ~forced & valid).astype(np.float32))
    return np.concatenate(buckets), jnp.asarray(np.stack(sbases)), jnp.asarray(np.stack(useimps))


def _head_rows(b, t):
    k = b.shape[-1]
    n = b.shape[1] // t
    return b.reshape(G_KV, HPG, n, t, k).transpose(2, 0, 1, 3, 4).reshape(n, G_KV, HPG * t, k)


def _prep_weights(rel_bias, w_in, conv_w_pw, gla_w_gk, nsa_q_norm_g, nsa_k_norm_g, nsa_pe_cmp, nsa_w_cmp,
                  nsa_b_gate, w_out):
    depth = w_in.shape[0]
    cols = _in_proj_columns()
    pieces, start = [], 0
    for c in range(1, N_IN + 1):
        pad = cols[start] < 0
        if c == N_IN or (cols[c] >= 0 if pad else cols[c] != cols[c - 1] + 1):
            src = int(cols[start])
            pieces.append(jnp.zeros((depth, D_MODEL, c - start), w_in.dtype) if pad else w_in[:, :, src:src + c - start])
            start = c
    w_in_p = jnp.concatenate(pieces, axis=2).astype(BF16)
    perm = 512 + _nsa_perm()
    w_out_p = jnp.concatenate([w_out[:, 0:512]] + [w_out[:, int(perm[c]):int(perm[c]) + HD]
                                                   for c in range(0, W_NSA, HD)], axis=1).astype(BF16)
    qg = jnp.tile(nsa_q_norm_g, (1, H_NSA))[:, None, :] * SCALE
    kg = jnp.tile(nsa_k_norm_g, (1, 1, G_KV))
    wgk = jnp.zeros((depth, LANES, LANES), F32).at[:, :GATE_RANK, :].set(gla_w_gk)
    wgk_hi = wgk.astype(BF16)
    wgk2 = jnp.stack([wgk_hi, (wgk - wgk_hi.astype(F32)).astype(BF16)], axis=1)
    def blockdiag2(a, b):
        za = jnp.zeros(a.shape[:-1] + (b.shape[-1],), a.dtype)
        zb = jnp.zeros(b.shape[:-1] + (a.shape[-1],), b.dtype)
        return jnp.concatenate([jnp.concatenate([a, za], axis=-1), jnp.concatenate([zb, b], axis=-1)], axis=-2)

    wc = nsa_w_cmp.astype(BF16)
    wtap = blockdiag2(wc, wc)
    wbd = blockdiag2(wtap[:, 0], wtap[:, 1]).reshape(depth, KW, 256)
    pe = jnp.tile(nsa_pe_cmp.transpose(0, 2, 1, 3)[:, :, :, None, :], (1, 1, 1, G_KV, 1))
    petap = pe.transpose(0, 2, 1, 3, 4).reshape(depth, 2, CMP_BLOCK, LANES)
    pe = pe.reshape(depth, 1, KW)
    bg = jnp.zeros((depth, 1, LANES), F32).at[:, 0, :3 * H_NSA].set(nsa_b_gate)
    expand = np.zeros((LANES, 3 * HPG * LANES), np.float32)
    for br in range(3):
        for g in range(G_KV):
            for j in range(HPG):
                c = LANES * (HPG * br + j) + 64 * g
                expand[H_NSA * br + HPG * g + j, c:c + 64] = 1.0
    far = rel_bias[N_BUCKETS - 1]
    far_hi = far.astype(BF16).astype(F32)
    cvec = jnp.zeros((H_NSA, 1, LANES), F32).at[:, 0, CONST_LANE].set(far_hi).at[:, 0, CONST_LANE + 1].set(far - far_hi)
    return dict(w_in=w_in_p, w_out=w_out_p, qg=qg, kg=kg, wgk=wgk2, wbd=wbd, pe=pe, bg=bg,
                wtap=wtap, petap=petap,
                expand=jnp.asarray(expand, BF16), cvec=cvec, wpw=conv_w_pw.astype(BF16),
                indq=jnp.asarray(_blockdiag_ones(W_NSA, HD), BF16), indk=jnp.asarray(_blockdiag_ones(LANES, HD), BF16))


def kernel(x_prompt, x_sample, cache_cmp_kv, cache_slc_kv, page_table, state_win_kv, state_gla, state_conv, rel_bias, norm_g, w_in, conv_w_dw, conv_b_dw, conv_ln_g, conv_ln_b, conv_w_pw, conv_b_pw, gla_w_gk, gla_b_gk, gla_norm_g, nsa_q_norm_g, nsa_k_norm_g, nsa_pe_cmp, nsa_w_cmp, nsa_b_gate, w_out, norm_f):
    depth = w_in.shape[0]
    n_bp, t_p, _ = x_prompt.shape
    n_bs, t_s, _ = x_sample.shape
    n_pages = page_table.shape[1]
    past = n_pages * PAGE_SIZE
    w_past = state_win_kv.shape[2]
    assert t_p % TQ == 0 and WINDOW == 2 * TQ and t_s == 8 and w_past == WINDOW and past % SEL_BLOCK == 0
    wp = _prep_weights(rel_bias, w_in, conv_w_pw, gla_w_gk, nsa_q_norm_g, nsa_k_norm_g, nsa_pe_cmp, nsa_w_cmp,
                       nsa_b_gate, w_out)
    table = rel_bias
    nblk_p = t_p // SEL_BLOCK
    nblk_s = (past + SEL_BLOCK) // SEL_BLOCK
    pos_p = [np.arange(i * LANES, (i + 1) * LANES) for i in range(t_p // LANES)]
    bk_p, sbase_p, useimp_p = _cmp_tables(pos_p, nblk_p)
    bias_p = _head_rows(_bias_table(table, bk_p, sub_far=False), LANES)
    nb_s = LANES // t_s
    qpos_s = past + np.arange(t_s)
    bk_s, _, _ = _cmp_tables([qpos_s], nblk_s)
    bias_s = _head_rows(_bias_table(table, bk_s, sub_far=False), t_s)
    _, sbase_1, useimp_1 = _cmp_tables([np.tile(qpos_s, nb_s)], nblk_s)
    tq = np.arange(TQ)
    every = np.ones((1, 1), bool)
    bk_dt = np.concatenate([_bucket_matrix(tq, tq, every).T, _bucket_matrix(TQ + tq, tq, every).T])
    dt = _bias_table(table, bk_dt, sub_far=True)
    dt = dt.reshape(G_KV, HPG, 2, TQ, TQ).transpose(2, 0, 3, 1, 4).reshape(2, G_KV, TQ, HPG * TQ)
    edge = jnp.asarray(np.tile(np.where(tq[:, None] > tq[None, :], 0.0, NEG).astype(np.float32), (1, HPG)))
    n_keys = past + LANES
    kpos = np.arange(n_keys)
    dsl = _head_rows(_bias_table(table, _bucket_matrix(qpos_s, kpos, (kpos < past + t_s)[None, :]),
                                 sub_far=True), t_s)[0]
    n_wkeys = w_past + LANES
    wk = np.arange(n_wkeys)
    wpos = past - w_past + wk
    wvalid = (wk < w_past + t_s)[None, :] & ((qpos_s[:, None] - wpos[None, :]) < WINDOW) & (wpos >= 0)[None, :]
    dwn = _head_rows(_bias_table(table, _bucket_matrix(qpos_s, wpos, wvalid), sub_far=True), t_s)[0]

    cache_cmp = cache_cmp_kv.transpose(0, 1, 3, 4, 5, 2)
    cache_slc = cache_slc_kv.transpose(0, 1, 3, 4, 5, 2)
    win_state = state_win_kv.transpose(0, 1, 3, 4, 5, 2)
    s0 = state_gla.transpose(0, 1, 2, 4, 3).reshape(depth, n_bs, W_GLA, DK_GLA)
    i256 = np.arange(W_GLA)
    i128 = np.arange(LANES)
    bdmask = jnp.asarray((i256[:, None] // DV_GLA == i128[None, :] // DK_GLA).astype(np.float32))
    s0 = jnp.tile(s0, (1, 1, 1, H_GLA)) * bdmask
    zero_s0 = jnp.zeros((n_bp, W_GLA, LANES), F32)
    zero_hist = jnp.zeros((n_bp, HIST, W_CONV), F32)

    def unpack_state(st):
        b = st.shape[0]
        blocks = [st[:, DV_GLA * h:DV_GLA * (h + 1), DK_GLA * h:DK_GLA * (h + 1)] for h in range(H_GLA)]
        return jnp.stack(blocks, axis=1).transpose(0, 1, 3, 2)

    xp = x_prompt.reshape(n_bp * t_p, D_MODEL)
    xs = x_sample.reshape(n_bs * t_s, D_MODEL)
    outs_p, outs_s = [], []
    leaf_shape = (depth, n_bp, 2, G_KV, HD, t_p)
    chain_p, win_all = (jnp.zeros(leaf_shape, F32), jnp.zeros(leaf_shape, F32)), jnp.zeros(win_state.shape, F32)
    for l in range(depth):
        final = l == depth - 1
        row = lambda a: a[l][None, :]
        common = dict(wdw=conv_w_dw[l], bdw=row(conv_b_dw), lng=row(conv_ln_g), lnb=row(conv_ln_b), wpw=wp["wpw"][l],
                      bpw=row(conv_b_pw))
        bgk = jnp.zeros((1, LANES), F32).at[0, :].set(gla_b_gk[l])
        gng = jnp.tile(gla_norm_g[l], (H_GLA,))[None, :]
        kg12 = wp["kg"][l, 1:3]
        kg0 = wp["kg"][l, 0:1]
        zc, zg, qn, cmp_n, _, win_n, ngate, ng, ksa, vs, kwa, vw, cmp_all, slc_all = _in_proj(
            xp, row(norm_g), wp["w_in"][l], wp["indq"], wp["indk"], wp["qg"][l], kg12, seq_len=t_p, prompt=True,
            layer=l, depth=depth, chain=chain_p)
        chain_p = (cmp_all, slc_all)
        yc, conv_st = _conv(zc, zero_hist, nb=1, t_len=t_p, **common)
        yg, gla_st = _gla(zg, zero_s0, wp["wgk"][l], bgk, gng, nb=1, t_len=t_p, chunk=16)
        kvc = _compress_prompt(cmp_n, wp["wbd"][l], wp["pe"][l], n_b=n_bp, t_len=t_p)
        *q_ts, ocmp = _cmpsel(qn, kvc, bias_p, sbase_p, useimp_p, wp["indk"], kg0, wp["cvec"], nb=1, tb=LANES,
                              n_blocks=nblk_p, q_t=True)
        yn = _flash(q_ts, ksa, vs, kwa, vw, dt, edge, ocmp, ng, ngate, wp["bg"][l], wp["expand"], n_b=n_bp, t_len=t_p)
        xp = _out_proj(xp, yc, yg, yn, wp["w_out"][l], norm_f[None, :], final=final)
        w_keep = min(WINDOW, t_p)
        outs_p.append((conv_st, unpack_state(gla_st), win_n.reshape(n_bp, t_p, 256)[:, t_p - w_keep:]))
        zc, zg, qn, cmp_n, slc_n, win_n, ngate, ng = _in_proj(
            xs, row(norm_g), wp["w_in"][l], wp["indq"], wp["indk"], wp["qg"][l], kg12, seq_len=t_s, prompt=False)
        yc, conv_st = _conv(zc, state_conv[l], nb=nb_s, t_len=t_s, **common)
        yg, gla_st = _gla(zg, s0[l], wp["wgk"][l], bgk, gng, nb=nb_s, t_len=t_s, chunk=t_s)
        kvc = _compress_paged(page_table, cache_cmp, l, cmp_n, wp["wtap"][l], wp["petap"][l], t_new=t_s)
        qa, ocmp = _cmpsel(qn, kvc, bias_s, sbase_1, useimp_1, wp["indk"], kg0, wp["cvec"], nb=nb_s, tb=t_s,
                           n_blocks=nblk_s, q_t=False)
        yn, win_all = _dec_attn(page_table, cache_slc, l, qa, slc_n, win_n, win_state, dsl, dwn, ocmp, ng, ngate,
                                wp["bg"][l], wp["expand"], win_all, t_new=t_s)
        xs = _out_proj(xs, yc, yg, yn, wp["w_out"][l], norm_f[None, :], final=final)
        outs_s.append((conv_st, unpack_state(gla_st), None, cmp_n, slc_n))

    def stack(outs, k, shape):
        return jnp.stack([o[k] for o in outs]).reshape(shape)

    kv = (2, G_KV, HD)
    return (xp.reshape(n_bp, t_p, D_MODEL), xs.reshape(n_bs, t_s, D_MODEL),
            stack(outs_p, 0, (depth, n_bp, HIST, W_CONV)), stack(outs_s, 0, (depth, n_bs, HIST, W_CONV)),
            stack(outs_p, 1, (depth, n_bp, H_GLA, DK_GLA, DV_GLA)), stack(outs_s, 1, (depth, n_bs, H_GLA, DK_GLA, DV_GLA)),
            stack(outs_p, 2, (depth, n_bp, min(WINDOW, t_p)) + kv), win_all.transpose(0, 1, 5, 2, 3, 4),
            chain_p[0].transpose(0, 1, 5, 2, 3, 4), stack(outs_s, 3, (depth, n_bs, t_s) + kv),
            chain_p[1].transpose(0, 1, 5, 2, 3, 4), stack(outs_s, 4, (depth, n_bs, t_s) + kv))
```
